```python
import jax, jax.numpy as jnp
from jax import lax
import numpy as np

D_MODEL = 1024
BATCH = 8
SEQ = 4096
DEPTH = 2

HEAD_DIM = 64
D_MIX = D_MODEL
RET_HEADS = D_MIX // 4 // HEAD_DIM
RET_W = RET_HEADS * HEAD_DIM
SWA_HEADS = D_MIX // 2 // HEAD_DIM
SWA_KV_HEADS = 2
SWA_GROUP = SWA_HEADS // SWA_KV_HEADS
SWA_W = SWA_HEADS * HEAD_DIM
SWA_KV_W = SWA_KV_HEADS * HEAD_DIM
CONV_CH = D_MIX - RET_W - SWA_W
CONV_GROUPS = 4
CONV_GROUP_DIM = CONV_CH // CONV_GROUPS
CONV_WIDTH = 31
WINDOW = 128
BLOCK = 128
RET_CHUNK = 128
ROPE_THETA = 500000.0
ROPE_DIM = HEAD_DIM // 4
RET_ROPE_THETA = 10000.0
D_FF = 256 * ((8 * D_MODEL // 3 + 255) // 256)
N_EXPERTS = 8
TOP_K = 2
D_FF_EXPERT = 7 * D_MODEL // 2
EPS = 1e-6
NEG_INF = -1e30
N_DENSE = (DEPTH + 1) // 2
N_MOE = DEPTH // 2
D_IN = 4 * RET_W + SWA_W + 2 * SWA_KV_W + 2 * CONV_CH
SPLITS = [RET_W, 2 * RET_W, 3 * RET_W, 4 * RET_W, 4 * RET_W + SWA_W,
          4 * RET_W + SWA_W + SWA_KV_W, 4 * RET_W + SWA_W + 2 * SWA_KV_W]

kernel_name = "hymba_style_retention_swa_conformer_moe"


def rmsnorm(x, g):
    xf = x.astype(jnp.float32)
    y = xf * lax.rsqrt(jnp.mean(xf * xf, axis=-1, keepdims=True) + EPS)
    return (y * g.astype(jnp.float32)).astype(x.dtype)


def rope(x, theta, rot_dim):
    S = x.shape[1]
    half = rot_dim // 2
    inv = 1.0 / (theta ** (jnp.arange(half, dtype=jnp.float32) / half))
    ang = jnp.arange(S, dtype=jnp.float32)[:, None] * inv[None, :]
    cos = jnp.cos(ang)[None, :, None, :]
    sin = jnp.sin(ang)[None, :, None, :]
    xr = x[..., :rot_dim].astype(jnp.float32)
    x1, x2 = xr[..., :half], xr[..., half:]
    rot = jnp.concatenate([x1 * cos - x2 * sin, x2 * cos + x1 * sin], axis=-1).astype(x.dtype)
    return jnp.concatenate([rot, x[..., rot_dim:]], axis=-1)


def retention(q, k, v, g, gn_g):
    B, S, H, D = q.shape
    C = RET_CHUNK
    N = S // C
    dt = q.dtype
    qf = rope(q, RET_ROPE_THETA, D).astype(jnp.float32).reshape(B, N, C, H, D)
    kf = (rope(k, RET_ROPE_THETA, D).astype(jnp.float32) * D ** -0.5).reshape(B, N, C, H, D)
    vf = v.astype(jnp.float32).reshape(B, N, C, H, D)
    lg = jnp.log(1.0 - 2.0 ** (-5.0 - jnp.arange(H, dtype=jnp.float32)))
    idx = jnp.arange(C)
    rel = idx[:, None] - idx[None, :]
    dmask = jnp.where(rel[None] >= 0, jnp.exp(jnp.maximum(rel, 0)[None].astype(jnp.float32) * lg[:, None, None]), 0.0)
    scores = jnp.einsum('bnihd,bnjhd->bnhij', qf, kf) * dmask[None, None]
    intra = jnp.einsum('bnhij,bnjhd->bnihd', scores, vf)
    k_decay = jnp.exp((C - 1 - idx)[:, None].astype(jnp.float32) * lg[None, :])
    kv = jnp.einsum('bnjhd,jh,bnjhe->bnhde', kf, k_decay, vf)
    chunk_decay = jnp.exp(C * lg)[None, :, None, None]

    def step(state, kv_n):
        return state * chunk_decay + kv_n, state

    _, prev = lax.scan(step, jnp.zeros((B, H, D, D), jnp.float32), jnp.moveaxis(kv, 1, 0))
    prev = jnp.moveaxis(prev, 0, 1)
    q_decay = jnp.exp((idx + 1)[:, None].astype(jnp.float32) * lg[None, :])
    cross = jnp.einsum('bnihd,bnhde,ih->bnihe', qf, prev, q_decay)
    o = (intra + cross).reshape(B, S, H, D)
    mu = jnp.mean(o, axis=-1, keepdims=True)
    var = jnp.mean(jnp.square(o - mu), axis=-1, keepdims=True)
    o = ((o - mu) * lax.rsqrt(var + EPS)).reshape(B, S, H * D) * gn_g.astype(jnp.float32)
    return (jax.nn.silu(g.astype(jnp.float32)) * o).astype(dt)


def sliding_window_attention(q, k, v, sinks):
    B, S, _, D = q.shape
    NB = S // BLOCK
    q = rope(q, ROPE_THETA, ROPE_DIM)
    k = rope(k, ROPE_THETA, ROPE_DIM)
    qb = q.reshape(B, NB, BLOCK, SWA_KV_HEADS, SWA_GROUP, D)
    kb = k.reshape(B, NB, BLOCK, SWA_KV_HEADS, D)
    vb = v.reshape(B, NB, BLOCK, SWA_KV_HEADS, D)
    pad = ((0, 0), (1, 0), (0, 0), (0, 0), (0, 0))
    kband = jnp.concatenate([jnp.pad(kb, pad)[:, :-1], kb], axis=2)
    vband = jnp.concatenate([jnp.pad(vb, pad)[:, :-1], vb], axis=2)
    s = jnp.einsum('bnqhgd,bnkhd->bnhgqk', qb, kband).astype(jnp.float32) * D ** -0.5
    qi = jnp.arange(BLOCK)[:, None] + BLOCK
    kj = jnp.arange(2 * BLOCK)[None, :]
    rel = qi - kj
    allowed = (rel >= 0) & (rel < WINDOW)
    nb = jnp.arange(NB)[:, None, None]
    valid = allowed[None] & ((nb > 0) | (kj[None] >= BLOCK))
    s = jnp.where(valid[None, :, None, None], s, NEG_INF)
    sink = sinks.astype(jnp.float32).reshape(1, 1, SWA_KV_HEADS, SWA_GROUP, 1, 1)
    m = jnp.maximum(jnp.max(s, axis=-1, keepdims=True), sink)
    p = jnp.exp(s - m)
    p = p / (jnp.sum(p, axis=-1, keepdims=True) + jnp.exp(sink - m))
    o = jnp.einsum('bnhgqk,bnkhd->bnqhgd', p.astype(v.dtype), vband)
    return o.reshape(B, S, SWA_W)


def conformer_conv(u, dw_w, dw_b, ln_g, ln_b, pw_w):
    B, S, _ = u.shape
    a, gate = u[..., :CONV_CH], u[..., CONV_CH:]
    h = a * jax.nn.sigmoid(gate)
    h = lax.conv_general_dilated(h, dw_w[:, None, :], window_strides=(1,),
                                 padding=[(CONV_WIDTH - 1, 0)],
                                 dimension_numbers=('NWC', 'WIO', 'NWC'),
                                 feature_group_count=CONV_CH) + dw_b
    hf = h.astype(jnp.float32)
    mu = jnp.mean(hf, axis=-1, keepdims=True)
    var = jnp.mean(jnp.square(hf - mu), axis=-1, keepdims=True)
    hf = (hf - mu) * lax.rsqrt(var + EPS) * ln_g.astype(jnp.float32) + ln_b.astype(jnp.float32)
    h = jax.nn.silu(hf).astype(u.dtype).reshape(B, S, CONV_GROUPS, CONV_GROUP_DIM)
    return jnp.einsum('bsgc,gcd->bsgd', h, pw_w).reshape(B, S, CONV_CH)


def mixer(h, w_in, ret_gn_g, sinks, dw_w, dw_b, ln_g, ln_b, pw_w, w_out):
    B, S, _ = h.shape
    z = h @ w_in
    rq, rk, rv, rg, sq, sk, sv, cu = jnp.split(z, SPLITS, axis=-1)
    hd = (B, S, -1, HEAD_DIM)
    y_ret = retention(rq.reshape(hd), rk.reshape(hd), rv.reshape(hd), rg, ret_gn_g)
    y_swa = sliding_window_attention(sq.reshape(hd), sk.reshape(hd), sv.reshape(hd), sinks)
    y_conv = conformer_conv(cu, dw_w, dw_b, ln_g, ln_b, pw_w)
    return jnp.concatenate([y_ret, y_swa, y_conv], axis=-1) @ w_out


def swiglu(h, w_gate, w_up, w_down):
    return (jax.nn.silu(h @ w_gate) * (h @ w_up)) @ w_down


def moe(h, w_router, w_gate, w_up, w_down):
    B, S, D = h.shape
    t = h.reshape(B * S, D)
    logits = (t @ w_router).astype(jnp.float32)
    top_vals, top_idx = lax.top_k(logits, TOP_K)
    weights = jax.nn.softmax(top_vals, axis=-1)
    gates = jnp.sum(jax.nn.one_hot(top_idx, N_EXPERTS, dtype=jnp.float32) * weights[..., None], axis=1)
    y = jnp.zeros_like(t)
    for e in range(N_EXPERTS):
        y = y + gates[:, e:e + 1].astype(t.dtype) * swiglu(t, w_gate[e], w_up[e], w_down[e])
    return y.reshape(B, S, D)


def setup_inputs(seed: int = 0) -> dict:
    key = jax.random.key(seed)
    ks = jax.random.split(key, 24)
    f32 = jnp.float32

    def nrm(k, shape, scale):
        return jax.random.normal(k, shape, f32) * scale

    return {
        "x": nrm(ks[0], (BATCH, SEQ, D_MODEL), 1.0),
        "norm_mix_g": 1.0 + nrm(ks[1], (DEPTH, D_MODEL), 0.02),
        "w_in": nrm(ks[2], (DEPTH, D_MODEL, D_IN), D_MODEL ** -0.5),
        "ret_gn_g": 1.0 + nrm(ks[3], (DEPTH, RET_W), 0.02),
        "attn_sinks": nrm(ks[4], (DEPTH, SWA_HEADS), 0.5),
        "conv_dw_w": nrm(ks[5], (DEPTH, CONV_WIDTH, CONV_CH), CONV_WIDTH ** -0.5),
        "conv_dw_b": nrm(ks[6], (DEPTH, CONV_CH), 0.02),
        "conv_ln_g": 1.0 + nrm(ks[7], (DEPTH, CONV_CH), 0.02),
        "conv_ln_b": nrm(ks[8], (DEPTH, CONV_CH), 0.02),
        "conv_pw_w": nrm(ks[9], (DEPTH, CONV_GROUPS, CONV_GROUP_DIM, CONV_GROUP_DIM), CONV_GROUP_DIM ** -0.5),
        "w_out": nrm(ks[10], (DEPTH, D_MIX, D_MODEL), D_MIX ** -0.5),
        "norm_ffn_g": 1.0 + nrm(ks[11], (DEPTH, D_MODEL), 0.02),
        "ffn_w_gate": nrm(ks[12], (N_DENSE, D_MODEL, D_FF), D_MODEL ** -0.5),
        "ffn_w_up": nrm(ks[13], (N_DENSE, D_MODEL, D_FF), D_MODEL ** -0.5),
        "ffn_w_down": nrm(ks[14], (N_DENSE, D_FF, D_MODEL), D_FF ** -0.5),
        "moe_router": nrm(ks[15], (N_MOE, D_MODEL, N_EXPERTS), D_MODEL ** -0.5),
        "moe_w_gate": nrm(ks[16], (N_MOE, N_EXPERTS, D_MODEL, D_FF_EXPERT), D_MODEL ** -0.5),
        "moe_w_up": nrm(ks[17], (N_MOE, N_EXPERTS, D_MODEL, D_FF_EXPERT), D_MODEL ** -0.5),
        "moe_w_down": nrm(ks[18], (N_MOE, N_EXPERTS, D_FF_EXPERT, D_MODEL), D_FF_EXPERT ** -0.5),
        "final_norm_g": 1.0 + nrm(ks[19], (D_MODEL,), 0.02),
    }


def reference(x, norm_mix_g, w_in, ret_gn_g, attn_sinks, conv_dw_w, conv_dw_b, conv_ln_g, conv_ln_b,
              conv_pw_w, w_out, norm_ffn_g, ffn_w_gate, ffn_w_up, ffn_w_down, moe_router, moe_w_gate,
              moe_w_up, moe_w_down, final_norm_g):
    for l in range(DEPTH):
        h = rmsnorm(x, norm_mix_g[l])
        x = x + mixer(h, w_in[l], ret_gn_g[l], attn_sinks[l], conv_dw_w[l], conv_dw_b[l],
                      conv_ln_g[l], conv_ln_b[l], conv_pw_w[l], w_out[l])
        h = rmsnorm(x, norm_ffn_g[l])
        if l % 2 == 0:
            j = l // 2
            x = x + swiglu(h, ffn_w_gate[j], ffn_w_up[j], ffn_w_down[j])
        else:
            j = l // 2
            x = x + moe(h, moe_router[j], moe_w_gate[j], moe_w_up[j], moe_w_down[j])
    return rmsnorm(x, final_norm_g)
```

```python
import functools

import jax
import jax.numpy as jnp
from jax import lax
from jax.experimental import pallas as pl
from jax.experimental.pallas import tpu as pltpu

F32 = jnp.float32
BF16 = jnp.bfloat16

HEAD_DIM = 64
RET_HEADS = 4
RET_W = RET_HEADS * HEAD_DIM
SWA_HEADS = 8
SWA_KV_HEADS = 2
SWA_GROUP = SWA_HEADS // SWA_KV_HEADS
SWA_W = SWA_HEADS * HEAD_DIM
SWA_KV_W = SWA_KV_HEADS * HEAD_DIM
CONV_CH = 256
CONV_GROUPS = 4
CONV_WIDTH = 31
WINDOW = 128
BLOCK = 128
RET_CHUNK = 128
ROPE_THETA = 500000.0
ROPE_DIM = HEAD_DIM // 4
RET_ROPE_THETA = 10000.0
N_EXPERTS = 8
TOP_K = 2
EPS = 1e-6
NEG_INF = -1e30

LANES = 128
CONV_HALO = 32
VMEM_LIMIT = 56 * 1024 * 1024

O_RQ, O_RK, O_RV, O_RG = 0, RET_W, 2 * RET_W, 3 * RET_W
O_SQ = 4 * RET_W
O_SK = O_SQ + SWA_W
O_SV = O_SK + SWA_KV_W
O_CA = O_SV + SWA_KV_W
O_CG = O_CA + CONV_CH
D_IN = O_CG + CONV_CH


def _params(*sem):
    return pltpu.CompilerParams(dimension_semantics=sem, vmem_limit_bytes=VMEM_LIMIT)


def _rms(x, g):
    return x * lax.rsqrt(jnp.mean(x * x, axis=-1, keepdims=True) + EPS) * g


def _silu(x):
    return x * jax.nn.sigmoid(x)


def _dot(a, b):
    return jnp.dot(a, b, preferred_element_type=F32)


def _dot_nt(a, b):
    return lax.dot_general(a, b, (((1,), (1,)), ((), ())), preferred_element_type=F32)


def _dot_tn(a, b):
    return lax.dot_general(a, b, (((0,), (0,)), ((), ())), preferred_element_type=F32)


def _rope_tables(seq, theta, rot_dim):
    half = rot_dim // 2
    inv = 1.0 / (theta ** (jnp.arange(half, dtype=F32) / half))
    ang = jnp.arange(seq, dtype=F32)[:, None] * inv[None, :]
    cos, sin = jnp.cos(ang), jnp.sin(ang)
    rest = HEAD_DIM - rot_dim
    c = jnp.concatenate([cos, cos, jnp.ones((seq, rest), F32)], axis=1)
    s = jnp.concatenate([-sin, sin, jnp.zeros((seq, rest), F32)], axis=1)
    reps = LANES // HEAD_DIM
    return jnp.tile(c, (1, reps)), jnp.tile(s, (1, reps))


def _inproj_body(x_ref, g_ref, w_ref, rc_ref, rs_ref, sc_ref, ss_ref,
                 rq_ref, rk_ref, rv_ref, rg_ref, sq_ref, sk_ref, sv_ref, ch_ref):
    h = _rms(x_ref[0], g_ref[...]).astype(BF16)
    lane = lax.broadcasted_iota(jnp.int32, (1, LANES), 1) % HEAD_DIM

    def seg(a, b):
        return _dot(h, w_ref[:, a:b])

    def rope(v, c, s, half):
        up = pltpu.roll(v, LANES - half, axis=1)
        dn = pltpu.roll(v, half, axis=1)
        return v * c + jnp.where(lane < half, up, dn) * s

    def rope_store(z, o_ref, c, s, half, scale):
        for i in range(z.shape[1] // LANES):
            sl = slice(i * LANES, (i + 1) * LANES)
            r = rope(z[:, sl], c, s, half)
            if scale != 1.0:
                r = r * scale
            o_ref[0, :, sl] = r.astype(o_ref.dtype)

    rc, rs = rc_ref[...], rs_ref[...]
    sc, ss = sc_ref[...], ss_ref[...]
    scale = HEAD_DIM ** -0.5
    rope_store(seg(O_RQ, O_RK), rq_ref, rc, rs, HEAD_DIM // 2, 1.0)
    rope_store(seg(O_RK, O_RV), rk_ref, rc, rs, HEAD_DIM // 2, scale)
    rv_ref[0] = seg(O_RV, O_RG).astype(BF16)
    rg_ref[0] = seg(O_RG, O_SQ)
    rope_store(seg(O_SQ, O_SK), sq_ref, sc, ss, ROPE_DIM // 2, scale)
    rope_store(seg(O_SK, O_SV), sk_ref, sc, ss, ROPE_DIM // 2, 1.0)
    sv_ref[0] = seg(O_SV, O_CA).astype(BF16)
    ch_ref[0] = seg(O_CA, O_CG) * jax.nn.sigmoid(seg(O_CG, D_IN))


def _inproj(x, g, w_bf, tm):
    B, S, D = x.shape
    tm = min(tm, S)
    rc, rs = _rope_tables(S, RET_ROPE_THETA, HEAD_DIM)
    sc, ss = _rope_tables(S, ROPE_THETA, ROPE_DIM)
    tab = pl.BlockSpec((tm, LANES), lambda s, b: (s, 0))

    def out(width, dtype):
        return (jax.ShapeDtypeStruct((B, S, width), dtype),
                pl.BlockSpec((1, tm, width), lambda s, b: (b, s, 0)))

    outs = [out(RET_W, BF16), out(RET_W, BF16), out(RET_W, BF16), out(RET_W, F32),
            out(SWA_W, BF16), out(SWA_KV_W, BF16), out(SWA_KV_W, BF16), out(CONV_CH, F32)]
    return pl.pallas_call(
        _inproj_body,
        grid=(S // tm, B),
        in_specs=[pl.BlockSpec((1, tm, D), lambda s, b: (b, s, 0)),
                  pl.BlockSpec((1, D), lambda s, b: (0, 0)),
                  pl.BlockSpec((D, D_IN), lambda s, b: (0, 0)),
                  tab, tab, tab, tab],
        out_specs=[o[1] for o in outs],
        out_shape=[o[0] for o in outs],
        compiler_params=_params("arbitrary", "arbitrary"),
        name="inproj",
    )(x, g.reshape(1, D), w_bf, rc, rs, sc, ss)


def _ret_body(q_ref, k_ref, v_ref, g_ref, dm_ref, kd_ref, qd_ref, cd_ref, gn_ref, o_ref, st_ref, *, ts):
    @pl.when(pl.program_id(1) == 0)
    def _():
        st_ref[...] = jnp.zeros_like(st_ref)

    C = RET_CHUNK
    for c in range(ts // C):
        rows = slice(c * C, (c + 1) * C)
        qa, ka, va, ga = q_ref[0, rows, :], k_ref[0, rows, :], v_ref[0, rows, :], g_ref[0, rows, :]
        outs = []
        for h in range(RET_HEADS):
            cols = slice(h * HEAD_DIM, (h + 1) * HEAD_DIM)
            q, k, v = qa[:, cols], ka[:, cols], va[:, cols]
            st = st_ref[h]
            scores = _dot_nt(q, k) * dm_ref[h]
            intra = _dot(scores.astype(BF16), v)
            cross = _dot(q, st.astype(BF16)) * qd_ref[h]
            kdec = (k.astype(F32) * kd_ref[h]).astype(BF16)
            st_ref[h] = st * cd_ref[h] + _dot_tn(kdec, v)
            o = intra + cross
            mu = jnp.mean(o, axis=-1, keepdims=True)
            d = o - mu
            var = jnp.mean(d * d, axis=-1, keepdims=True)
            outs.append(d * lax.rsqrt(var + EPS))
        on = jnp.concatenate(outs, axis=1) * gn_ref[...]
        o_ref[0, rows, :] = (_silu(ga) * on).astype(BF16)


def _retention(rq, rk, rv, rg, gn_g, ts):
    B, S, W = rq.shape
    ts = min(ts, S)
    C, H, D = RET_CHUNK, RET_HEADS, HEAD_DIM
    lg = jnp.log(1.0 - 2.0 ** (-5.0 - jnp.arange(H, dtype=F32)))
    idx = jnp.arange(C)
    rel = idx[:, None] - idx[None, :]
    dmask = jnp.where(rel[None] >= 0,
                      jnp.exp(jnp.maximum(rel, 0)[None].astype(F32) * lg[:, None, None]), 0.0)
    k_decay = jnp.exp((C - 1 - idx)[:, None].astype(F32) * lg[None, :])
    q_decay = jnp.exp((idx + 1)[:, None].astype(F32) * lg[None, :])
    chunk_decay = jnp.exp(C * lg)
    kd = jnp.broadcast_to(k_decay.T[:, :, None], (H, C, D))
    qd = jnp.broadcast_to(q_decay.T[:, :, None], (H, C, D))
    cd = jnp.broadcast_to(chunk_decay[:, None, None], (H, D, D))
    act = pl.BlockSpec((1, ts, W), lambda b, s: (b, s, 0))

    def const(shape):
        return pl.BlockSpec(shape, lambda b, s: (0,) * len(shape))

    return pl.pallas_call(
        functools.partial(_ret_body, ts=ts),
        grid=(B, S // ts),
        in_specs=[act, act, act, act, const((H, C, C)), const((H, C, D)), const((H, C, D)),
                  const((H, D, D)), const((1, W))],
        out_specs=act,
        out_shape=jax.ShapeDtypeStruct((B, S, W), BF16),
        scratch_shapes=[pltpu.VMEM((H, D, D), F32)],
        compiler_params=_params("arbitrary", "arbitrary"),
        name="retention",
    )(rq, rk, rv, rg, dmask, kd, qd, cd, gn_g.reshape(1, W))


def _swa_body(sink_ref, q_ref, kc_ref, kp_ref, vc_ref, vp_ref, mask_ref, o_ref, *, tq):
    n = pl.program_id(1)
    kcat = jnp.concatenate([kp_ref[0], kc_ref[0]], axis=0)
    vcat = jnp.concatenate([vp_ref[0], vc_ref[0]], axis=0)
    for j in range(tq // BLOCK):
        qb = q_ref[0, j * BLOCK:(j + 1) * BLOCK, :]
        kb = kcat[j * BLOCK:(j + 2) * BLOCK]
        vb = vcat[j * BLOCK:(j + 2) * BLOCK]
        if j == 0:
            mask = jnp.where(n == 0, mask_ref[1], mask_ref[0])
        else:
            mask = mask_ref[0]
        keep = mask > 0.0
        outs = []
        for hq in range(SWA_HEADS):
            hk = hq // SWA_GROUP
            kh = kb[:, hk * HEAD_DIM:(hk + 1) * HEAD_DIM]
            vh = vb[:, hk * HEAD_DIM:(hk + 1) * HEAD_DIM]
            qh = qb[:, hq * HEAD_DIM:(hq + 1) * HEAD_DIM]
            s = jnp.where(keep, _dot_nt(qh, kh), NEG_INF)
            sink = sink_ref[hq]
            m = jnp.maximum(jnp.max(s, axis=-1, keepdims=True), sink)
            p = jnp.exp(s - m)
            den = jnp.sum(p, axis=-1, keepdims=True) + jnp.exp(sink - m)
            outs.append(_dot(p.astype(BF16), vh) / den)
        o_ref[0, j * BLOCK:(j + 1) * BLOCK, :] = jnp.concatenate(outs, axis=1).astype(BF16)


def _swa(sq, sk, sv, sinks, tq):
    B, S, W = sq.shape
    tq = min(tq, S)
    r = tq // BLOCK
    qi = jnp.arange(BLOCK)[:, None] + BLOCK
    kj = jnp.arange(2 * BLOCK)[None, :]
    rel = qi - kj
    allowed = (rel >= 0) & (rel < WINDOW)
    mask = jnp.stack([allowed, allowed & (kj >= BLOCK)]).astype(F32)
    cur = pl.BlockSpec((1, tq, SWA_KV_W), lambda b, n: (b, n, 0))
    prev = pl.BlockSpec((1, BLOCK, SWA_KV_W), lambda b, n: (b, jnp.maximum(n * r - 1, 0), 0))
    return pl.pallas_call(
        functools.partial(_swa_body, tq=tq),
        grid=(B, S // tq),
        in_specs=[pl.BlockSpec(memory_space=pltpu.SMEM),
                  pl.BlockSpec((1, tq, W), lambda b, n: (b, n, 0)),
                  cur, prev, cur, prev,
                  pl.BlockSpec((2, BLOCK, 2 * BLOCK), lambda b, n: (0, 0, 0))],
        out_specs=pl.BlockSpec((1, tq, W), lambda b, n: (b, n, 0)),
        out_shape=jax.ShapeDtypeStruct((B, S, W), BF16),
        compiler_params=_params("arbitrary", "arbitrary"),
        name="swa",
    )(sinks.astype(F32), sq, sk, sk, sv, sv, mask)


def _conv_body(h_ref, dw_ref, db_ref, lg_ref, lb_ref, pw_ref, o_ref, hbuf, *, ts, rb):
    s = pl.program_id(1)

    @pl.when(s == 0)
    def _():
        hbuf[0:CONV_HALO, :] = jnp.zeros((CONV_HALO, CONV_CH), F32)

    @pl.when(s > 0)
    def _():
        hbuf[0:CONV_HALO, :] = hbuf[ts:ts + CONV_HALO, :]

    hbuf[CONV_HALO:CONV_HALO + ts, :] = h_ref[0]
    off = CONV_HALO - (CONV_WIDTH - 1)
    for r in range(ts // rb):
        base = r * rb
        acc = jnp.broadcast_to(db_ref[...], (rb, CONV_CH))
        for w in range(CONV_WIDTH):
            acc = acc + hbuf[base + off + w:base + off + w + rb, :] * dw_ref[w:w + 1, :]
        mu = jnp.mean(acc, axis=-1, keepdims=True)
        d = acc - mu
        var = jnp.mean(d * d, axis=-1, keepdims=True)
        hn = d * lax.rsqrt(var + EPS) * lg_ref[...] + lb_ref[...]
        o_ref[0, base:base + rb, :] = _dot(_silu(hn).astype(BF16), pw_ref[...]).astype(BF16)


def _conv(ch, dw_w, dw_b, ln_g, ln_b, pw_w, ts, rb):
    B, S, W = ch.shape
    ts = min(ts, S)
    rb = min(rb, ts)
    pw = jnp.zeros((W, W), F32)
    gd = W // CONV_GROUPS
    for g in range(CONV_GROUPS):
        pw = pw.at[g * gd:(g + 1) * gd, g * gd:(g + 1) * gd].set(pw_w[g])
    act = pl.BlockSpec((1, ts, W), lambda b, s: (b, s, 0))

    def const(shape):
        return pl.BlockSpec(shape, lambda b, s: (0,) * len(shape))

    return pl.pallas_call(
        functools.partial(_conv_body, ts=ts, rb=rb),
        grid=(B, S // ts),
        in_specs=[act, const((CONV_WIDTH, W)), const((1, W)), const((1, W)), const((1, W)), const((W, W))],
        out_specs=act,
        out_shape=jax.ShapeDtypeStruct((B, S, W), BF16),
        scratch_shapes=[pltpu.VMEM((CONV_HALO + ts, W), F32)],
        compiler_params=_params("arbitrary", "arbitrary"),
        name="conv",
    )(ch, dw_w, dw_b.reshape(1, W), ln_g.reshape(1, W), ln_b.reshape(1, W), pw.astype(BF16))


def _outproj_body(yr_ref, ys_ref, yc_ref, w_ref, x_ref, g_ref, *rest, route):
    if route:
        wr_ref, xo_ref, hn_ref, rt_ref = rest
    else:
        xo_ref, hn_ref = rest
    a, b = RET_W, RET_W + SWA_W
    x = x_ref[...] + (_dot(yr_ref[...], w_ref[0:a, :]) + _dot(ys_ref[...], w_ref[a:b, :])
                      + _dot(yc_ref[...], w_ref[b:, :]))
    xo_ref[...] = x
    h = _rms(x, g_ref[...])
    if not route:
        hn_ref[...] = h.astype(BF16)
        return
    hn_ref[...] = h
    logits = _dot(h.astype(BF16), wr_ref[...])
    lane = lax.broadcasted_iota(jnp.int32, logits.shape, 1)
    lg = jnp.where(lane < N_EXPERTS, logits, -jnp.inf)
    m1 = jnp.max(lg, axis=-1, keepdims=True)
    i1 = jnp.min(jnp.where(lg == m1, lane, LANES), axis=-1, keepdims=True)
    lg2 = jnp.where(lane == i1, -jnp.inf, lg)
    m2 = jnp.max(lg2, axis=-1, keepdims=True)
    i2 = jnp.min(jnp.where(lg2 == m2, lane, LANES), axis=-1, keepdims=True)
    e = jnp.exp(m2 - m1)
    w1 = 1.0 / (1.0 + e)
    w2 = e / (1.0 + e)
    rt_ref[...] = jnp.where(lane == 0, i1.astype(F32),
                            jnp.where(lane == 1, i2.astype(F32),
                                      jnp.where(lane == 2, w1, jnp.where(lane == 3, w2, 0.0))))


def _outproj(yr, ys, yc, w_bf, x, g, w_router, tm):
    T, D = x.shape
    tm = min(tm, T)
    route = w_router is not None

    def row(width):
        return pl.BlockSpec((tm, width), lambda i: (i, 0))

    def const(shape):
        return pl.BlockSpec(shape, lambda i: (0,) * len(shape))

    in_specs = [row(RET_W), row(SWA_W), row(CONV_CH), const((D, D)), row(D), const((1, D))]
    args = [yr, ys, yc, w_bf, x, g.reshape(1, D)]
    out_shape = [jax.ShapeDtypeStruct((T, D), F32), jax.ShapeDtypeStruct((T, D), F32 if route else BF16)]
    out_specs = [row(D), row(D)]
    if route:
        wr = jnp.zeros((D, LANES), F32).at[:, :N_EXPERTS].set(w_router).astype(BF16)
        in_specs.append(const((D, LANES)))
        args.append(wr)
        out_shape.append(jax.ShapeDtypeStruct((T, LANES), F32))
        out_specs.append(row(LANES))
    return pl.pallas_call(
        functools.partial(_outproj_body, route=route),
        grid=(T // tm,),
        in_specs=in_specs, out_specs=out_specs, out_shape=out_shape,
        compiler_params=_params("arbitrary"),
        name="outproj_route" if route else "outproj",
    )(*args)


def _dense_ffn_body(h_ref, x_ref, wg_ref, wu_ref, wd_ref, o_ref, *, chunks):
    h = h_ref[...]
    acc = x_ref[...]
    for a, b in chunks:
        g = _dot(h, wg_ref[:, a:b])
        u = _dot(h, wu_ref[:, a:b])
        acc = acc + _dot((_silu(g) * u).astype(BF16), wd_ref[a:b, :])
    o_ref[...] = acc


def _dense_ffn(hn, x, wg, wu, wd, tm, fc):
    T, D = x.shape
    FF = wg.shape[1]
    tm = min(tm, T)
    chunks = tuple((a, min(a + fc, FF)) for a in range(0, FF, fc))

    def row(dtype):
        return pl.BlockSpec((tm, D), lambda i: (i, 0))

    return pl.pallas_call(
        functools.partial(_dense_ffn_body, chunks=chunks),
        grid=(T // tm,),
        in_specs=[row(BF16), row(F32),
                  pl.BlockSpec((D, FF), lambda i: (0, 0)),
                  pl.BlockSpec((D, FF), lambda i: (0, 0)),
                  pl.BlockSpec((FF, D), lambda i: (0, 0))],
        out_specs=row(F32),
        out_shape=jax.ShapeDtypeStruct((T, D), F32),
        compiler_params=_params("arbitrary"),
        name="dense_ffn",
    )(hn, x, wg.astype(BF16), wu.astype(BF16), wd.astype(BF16))


def _route_meta(ids, tm):
    flat = ids.reshape(-1)
    oh = (flat[:, None] == jnp.arange(N_EXPERTS, dtype=jnp.int32)[None, :]).astype(jnp.int32)
    csum = jnp.cumsum(oh, axis=0)
    counts = csum[-1]
    rank = jnp.sum((csum - oh) * oh, axis=1)
    ptiles = (counts + tm - 1) // tm
    tile_end = jnp.cumsum(ptiles)
    tile_start = tile_end - ptiles
    dest = (jnp.sum(oh * tile_start[None, :], axis=1) * tm + rank).astype(jnp.int32)
    n_tiles = flat.shape[0] // tm + N_EXPERTS
    tile_expert = jnp.sum(jnp.arange(n_tiles, dtype=jnp.int32)[:, None] >= tile_end[None, :], axis=1)
    tile_expert = jnp.minimum(tile_expert, N_EXPERTS - 1).astype(jnp.int32)
    return dest, tile_expert, tile_end[-1:].astype(jnp.int32)


def _dispatch_body(dest_ref, h_hbm, xs_hbm, sem, *, tt):
    base = pl.program_id(0) * tt

    def copy(t, slot):
        return pltpu.make_async_copy(h_hbm.at[pl.ds(base + t, 1)],
                                     xs_hbm.at[pl.ds(dest_ref[0, 0, TOP_K * t + slot], 1)], sem)

    def start(t, c):
        for slot in range(TOP_K):
            copy(t, slot).start()
        return c

    def wait(t, c):
        for slot in range(TOP_K):
            copy(t, slot).wait()
        return c

    lax.fori_loop(0, tt, start, 0)
    lax.fori_loop(0, tt, wait, 0)


def _dispatch(hn, dest, n_rows, tt):
    T, D = hn.shape
    tt = min(tt, T)
    return pl.pallas_call(
        lambda dest_ref, h_hbm, zero_hbm, xs_hbm, sem: _dispatch_body(dest_ref, h_hbm, xs_hbm, sem, tt=tt),
        grid=(T // tt,),
        in_specs=[pl.BlockSpec((1, 1, TOP_K * tt), lambda i: (i, 0, 0), memory_space=pltpu.SMEM),
                  pl.BlockSpec(memory_space=pl.ANY),
                  pl.BlockSpec(memory_space=pl.ANY)],
        out_specs=pl.BlockSpec(memory_space=pl.ANY),
        out_shape=jax.ShapeDtypeStruct((n_rows, D), hn.dtype),
        scratch_shapes=[pltpu.SemaphoreType.DMA],
        input_output_aliases={2: 0},
        compiler_params=pltpu.CompilerParams(dimension_semantics=("arbitrary",), has_side_effects=True),
        name="moe_dispatch",
    )(dest.reshape(T // tt, 1, TOP_K * tt), hn, jnp.zeros((n_rows, D), hn.dtype))


def _moe_ffn_body(te_ref, nu_ref, x_ref, wg_ref, wu_ref, wd_ref, o_ref):
    i, f = pl.program_id(0), pl.program_id(1)

    @pl.when(i < nu_ref[0])
    def _():
        x = x_ref[...].astype(BF16)
        g = _dot(x, wg_ref[0].astype(BF16))
        u = _dot(x, wu_ref[0].astype(BF16))
        y = _dot((_silu(g) * u).astype(BF16), wd_ref[0].astype(BF16))

        @pl.when(f == 0)
        def _():
            o_ref[...] = y

        @pl.when(f > 0)
        def _():
            o_ref[...] += y

    @pl.when(jnp.logical_and(i >= nu_ref[0], f == 0))
    def _():
        o_ref[...] = jnp.zeros_like(o_ref)


def _moe_ffn(xs, tile_expert, n_used, wg, wu, wd, tm, fc):
    R, D = xs.shape
    E, _, FF = wg.shape
    nf = FF // fc

    def tile(i, nu):
        return jnp.minimum(i, nu[0] - 1)

    def fidx(i, f, nu):
        return jnp.where(i < nu[0], f, nf - 1)

    grid_spec = pltpu.PrefetchScalarGridSpec(
        num_scalar_prefetch=2,
        grid=(R // tm, nf),
        in_specs=[pl.BlockSpec((tm, D), lambda i, f, te, nu: (tile(i, nu), 0)),
                  pl.BlockSpec((1, D, fc), lambda i, f, te, nu: (te[tile(i, nu)], 0, fidx(i, f, nu))),
                  pl.BlockSpec((1, D, fc), lambda i, f, te, nu: (te[tile(i, nu)], 0, fidx(i, f, nu))),
                  pl.BlockSpec((1, fc, D), lambda i, f, te, nu: (te[tile(i, nu)], fidx(i, f, nu), 0))],
        out_specs=pl.BlockSpec((tm, D), lambda i, f, te, nu: (i, 0)),
    )
    return pl.pallas_call(
        _moe_ffn_body,
        grid_spec=grid_spec,
        out_shape=jax.ShapeDtypeStruct((R, D), F32),
        compiler_params=_params("arbitrary", "arbitrary"),
        name="moe_ffn",
    )(tile_expert, n_used, xs, wg, wu, wd)


def _combine_body(dest_ref, ys_hbm, x_ref, rt_ref, g_ref, o_ref, buf, sem, *, tc):
    def copy(t, slot):
        return pltpu.make_async_copy(ys_hbm.at[pl.ds(dest_ref[0, 0, TOP_K * t + slot], 1)],
                                     buf.at[slot, pl.ds(t, 1)], sem)

    def start(t, c):
        for slot in range(TOP_K):
            copy(t, slot).start()
        return c

    def wait(t, c):
        for slot in range(TOP_K):
            copy(t, slot).wait()
        return c

    lax.fori_loop(0, tc, start, 0)
    lax.fori_loop(0, tc, wait, 0)
    rt = rt_ref[...]
    y = x_ref[...] + (rt[:, 2:3] * buf[0] + rt[:, 3:4] * buf[1])
    o_ref[...] = _rms(y, g_ref[...])


def _combine(ys, dest, x, rt, g, tc):
    T, D = x.shape
    tc = min(tc, T)
    return pl.pallas_call(
        functools.partial(_combine_body, tc=tc),
        grid=(T // tc,),
        in_specs=[pl.BlockSpec((1, 1, TOP_K * tc), lambda i: (i, 0, 0), memory_space=pltpu.SMEM),
                  pl.BlockSpec(memory_space=pl.ANY),
                  pl.BlockSpec((tc, D), lambda i: (i, 0)),
                  pl.BlockSpec((tc, LANES), lambda i: (i, 0)),
                  pl.BlockSpec((1, D), lambda i: (0, 0))],
        out_specs=pl.BlockSpec((tc, D), lambda i: (i, 0)),
        out_shape=jax.ShapeDtypeStruct((T, D), F32),
        scratch_shapes=[pltpu.VMEM((TOP_K, tc, D), F32), pltpu.SemaphoreType.DMA],
        compiler_params=_params("arbitrary"),
        name="moe_combine",
    )(dest.reshape(T // tc, 1, TOP_K * tc), ys, x, rt, g.reshape(1, D))


def kernel(x, norm_mix_g, w_in, ret_gn_g, attn_sinks, conv_dw_w, conv_dw_b, conv_ln_g, conv_ln_b,
           conv_pw_w, w_out, norm_ffn_g, ffn_w_gate, ffn_w_up, ffn_w_down, moe_router, moe_w_gate,
           moe_w_up, moe_w_down, final_norm_g):
    B, S, D = x.shape
    T = B * S
    depth = w_in.shape[0]
    assert depth == 2 and ffn_w_gate.shape[0] == 1 and moe_router.shape[0] == 1, "dense layer then MoE layer"
    moe_tm = min(1024, T)

    for l in range(depth):
        rq, rk, rv, rg, sq, sk, sv, ch = _inproj(x, norm_mix_g[l], w_in[l].astype(BF16), tm=512)
        y_ret = _retention(rq, rk, rv, rg, ret_gn_g[l], ts=512)
        y_swa = _swa(sq, sk, sv, attn_sinks[l], tq=512)
        y_conv = _conv(ch, conv_dw_w[l], conv_dw_b[l], conv_ln_g[l], conv_ln_b[l], conv_pw_w[l], ts=512, rb=64)
        flat = lambda a: a.reshape(T, a.shape[-1])
        if l % 2 == 0:
            x2, hn = _outproj(flat(y_ret), flat(y_swa), flat(y_conv), w_out[l].astype(BF16), flat(x),
                              norm_ffn_g[l], None, tm=512)
            j = l // 2
            x = _dense_ffn(hn, x2, ffn_w_gate[j], ffn_w_up[j], ffn_w_down[j], tm=512, fc=512).reshape(B, S, D)
        else:
            j = l // 2
            x2, hn, rt = _outproj(flat(y_ret), flat(y_swa), flat(y_conv), w_out[l].astype(BF16), flat(x),
                                  norm_ffn_g[l], moe_router[j], tm=512)
            ids = rt[:, :TOP_K].astype(jnp.int32)
            dest, tile_expert, n_used = _route_meta(ids, moe_tm)
            n_rows = TOP_K * T + N_EXPERTS * moe_tm
            xs = _dispatch(hn, dest, n_rows, tt=1024)
            ys = _moe_ffn(xs, tile_expert, n_used, moe_w_gate[j], moe_w_up[j], moe_w_down[j], tm=moe_tm, fc=512)
            x = _combine(ys, dest, x2, rt, final_norm_g, tc=256).reshape(B, S, D)
    return x
```

```python
import functools

import jax
import jax.numpy as jnp
from jax import lax
from jax.experimental import pallas as pl
from jax.experimental.pallas import tpu as pltpu

F32 = jnp.float32
BF16 = jnp.bfloat16

HEAD_DIM = 64
RET_HEADS = 4
RET_W = RET_HEADS * HEAD_DIM
SWA_HEADS = 8
SWA_KV_HEADS = 2
SWA_GROUP = SWA_HEADS // SWA_KV_HEADS
SWA_W = SWA_HEADS * HEAD_DIM
SWA_KV_W = SWA_KV_HEADS * HEAD_DIM
CONV_CH = 256
CONV_GROUPS = 4
CONV_WIDTH = 31
WINDOW = 128
BLOCK = 128
RET_CHUNK = 128
ROPE_THETA = 500000.0
ROPE_DIM = HEAD_DIM // 4
RET_ROPE_THETA = 10000.0
N_EXPERTS = 8
TOP_K = 2
EPS = 1e-6
NEG_INF = -1e30

LANES = 128
CONV_HALO = 32
VMEM_LIMIT = 56 * 1024 * 1024

O_RQ, O_RK, O_RV, O_RG = 0, RET_W, 2 * RET_W, 3 * RET_W
O_SQ = 4 * RET_W
O_SK = O_SQ + SWA_W
O_SV = O_SK + SWA_KV_W
O_CA = O_SV + SWA_KV_W
O_CG = O_CA + CONV_CH
D_IN = O_CG + CONV_CH


def _params(*sem):
    return pltpu.CompilerParams(dimension_semantics=sem, vmem_limit_bytes=VMEM_LIMIT)


def _rms(x, g):
    return x * lax.rsqrt(jnp.mean(x * x, axis=-1, keepdims=True) + EPS) * g


def _silu(x):
    return x * jax.nn.sigmoid(x)


def _dot(a, b):
    return jnp.dot(a, b, preferred_element_type=F32)


def _dot_nt(a, b):
    return lax.dot_general(a, b, (((1,), (1,)), ((), ())), preferred_element_type=F32)


def _dot_tn(a, b):
    return lax.dot_general(a, b, (((0,), (0,)), ((), ())), preferred_element_type=F32)


def _rope_tables(seq, theta, rot_dim):
    half = rot_dim // 2
    inv = 1.0 / (theta ** (jnp.arange(half, dtype=F32) / half))
    ang = jnp.arange(seq, dtype=F32)[:, None] * inv[None, :]
    cos, sin = jnp.cos(ang), jnp.sin(ang)
    rest = HEAD_DIM - rot_dim
    c = jnp.concatenate([cos, cos, jnp.ones((seq, rest), F32)], axis=1)
    s = jnp.concatenate([-sin, sin, jnp.zeros((seq, rest), F32)], axis=1)
    reps = LANES // HEAD_DIM
    return jnp.tile(c, (1, reps)), jnp.tile(s, (1, reps))


def _inproj_body(x_ref, g_ref, w_ref, rc_ref, rs_ref, sc_ref, ss_ref,
                 rq_ref, rk_ref, rv_ref, rg_ref, sq_ref, sk_ref, sv_ref, ch_ref):
    h = _rms(x_ref[0], g_ref[...]).astype(BF16)
    lane = lax.broadcasted_iota(jnp.int32, (1, LANES), 1) % HEAD_DIM

    def seg(a, b):
        return _dot(h, w_ref[:, a:b])

    def rope(v, c, s, half):
        up = pltpu.roll(v, LANES - half, axis=1)
        dn = pltpu.roll(v, half, axis=1)
        return v * c + jnp.where(lane < half, up, dn) * s

    def rope_store(z, o_ref, c, s, half, scale):
        for i in range(z.shape[1] // LANES):
            sl = slice(i * LANES, (i + 1) * LANES)
            r = rope(z[:, sl], c, s, half)
            if scale != 1.0:
                r = r * scale
            o_ref[0, :, sl] = r.astype(o_ref.dtype)

    rc, rs = rc_ref[...], rs_ref[...]
    sc, ss = sc_ref[...], ss_ref[...]
    scale = HEAD_DIM ** -0.5
    rope_store(seg(O_RQ, O_RK), rq_ref, rc, rs, HEAD_DIM // 2, 1.0)
    rope_store(seg(O_RK, O_RV), rk_ref, rc, rs, HEAD_DIM // 2, scale)
    rv_ref[0] = seg(O_RV, O_RG).astype(BF16)
    rg_ref[0] = seg(O_RG, O_SQ)
    rope_store(seg(O_SQ, O_SK), sq_ref, sc, ss, ROPE_DIM // 2, scale)
    rope_store(seg(O_SK, O_SV), sk_ref, sc, ss, ROPE_DIM // 2, 1.0)
    sv_ref[0] = seg(O_SV, O_CA).astype(BF16)
    ch_ref[0] = seg(O_CA, O_CG) * jax.nn.sigmoid(seg(O_CG, D_IN))


def _inproj(x, g, w_bf, tm):
    B, S, D = x.shape
    tm = min(tm, S)
    rc, rs = _rope_tables(S, RET_ROPE_THETA, HEAD_DIM)
    sc, ss = _rope_tables(S, ROPE_THETA, ROPE_DIM)
    tab = pl.BlockSpec((tm, LANES), lambda s, b: (s, 0))

    def out(width, dtype):
        return (jax.ShapeDtypeStruct((B, S, width), dtype),
                pl.BlockSpec((1, tm, width), lambda s, b: (b, s, 0)))

    outs = [out(RET_W, BF16), out(RET_W, BF16), out(RET_W, BF16), out(RET_W, F32),
            out(SWA_W, BF16), out(SWA_KV_W, BF16), out(SWA_KV_W, BF16), out(CONV_CH, F32)]
    return pl.pallas_call(
        _inproj_body,
        grid=(S // tm, B),
        in_specs=[pl.BlockSpec((1, tm, D), lambda s, b: (b, s, 0)),
                  pl.BlockSpec((1, D), lambda s, b: (0, 0)),
                  pl.BlockSpec((D, D_IN), lambda s, b: (0, 0)),
                  tab, tab, tab, tab],
        out_specs=[o[1] for o in outs],
        out_shape=[o[0] for o in outs],
        compiler_params=_params("arbitrary", "arbitrary"),
        name="inproj",
    )(x, g.reshape(1, D), w_bf, rc, rs, sc, ss)


def _ret_body(q_ref, k_ref, v_ref, g_ref, dm_ref, kd_ref, qd_ref, cd_ref, gn_ref, o_ref, st_ref, *, ts):
    @pl.when(pl.program_id(1) == 0)
    def _():
        st_ref[...] = jnp.zeros_like(st_ref)

    C = RET_CHUNK
    for c in range(ts // C):
        rows = slice(c * C, (c + 1) * C)
        qa, ka, va, ga = q_ref[0, rows, :], k_ref[0, rows, :], v_ref[0, rows, :], g_ref[0, rows, :]
        outs = []
        for h in range(RET_HEADS):
            cols = slice(h * HEAD_DIM, (h + 1) * HEAD_DIM)
            q, k, v = qa[:, cols], ka[:, cols], va[:, cols]
            st = st_ref[h]
            scores = _dot_nt(q, k) * dm_ref[h]
            intra = _dot(scores.astype(BF16), v)
            cross = _dot(q, st.astype(BF16)) * qd_ref[h]
            kdec = (k.astype(F32) * kd_ref[h]).astype(BF16)
            st_ref[h] = st * cd_ref[h] + _dot_tn(kdec, v)
            o = intra + cross
            mu = jnp.mean(o, axis=-1, keepdims=True)
            d = o - mu
            var = jnp.mean(d * d, axis=-1, keepdims=True)
            outs.append(d * lax.rsqrt(var + EPS))
        on = jnp.concatenate(outs, axis=1) * gn_ref[...]
        o_ref[0, rows, :] = (_silu(ga) * on).astype(BF16)


def _retention(rq, rk, rv, rg, gn_g, ts):
    B, S, W = rq.shape
    ts = min(ts, S)
    C, H, D = RET_CHUNK, RET_HEADS, HEAD_DIM
    lg = jnp.log(1.0 - 2.0 ** (-5.0 - jnp.arange(H, dtype=F32)))
    idx = jnp.arange(C)
    rel = idx[:, None] - idx[None, :]
    dmask = jnp.where(rel[None] >= 0,
                      jnp.exp(jnp.maximum(rel, 0)[None].astype(F32) * lg[:, None, None]), 0.0)
    k_decay = jnp.exp((C - 1 - idx)[:, None].astype(F32) * lg[None, :])
    q_decay = jnp.exp((idx + 1)[:, None].astype(F32) * lg[None, :])
    chunk_decay = jnp.exp(C * lg)
    kd = jnp.broadcast_to(k_decay.T[:, :, None], (H, C, D))
    qd = jnp.broadcast_to(q_decay.T[:, :, None], (H, C, D))
    cd = jnp.broadcast_to(chunk_decay[:, None, None], (H, D, D))
    act = pl.BlockSpec((1, ts, W), lambda b, s: (b, s, 0))

    def const(shape):
        return pl.BlockSpec(shape, lambda b, s: (0,) * len(shape))

    return pl.pallas_call(
        functools.partial(_ret_body, ts=ts),
        grid=(B, S // ts),
        in_specs=[act, act, act, act, const((H, C, C)), const((H, C, D)), const((H, C, D)),
                  const((H, D, D)), const((1, W))],
        out_specs=act,
        out_shape=jax.ShapeDtypeStruct((B, S, W), BF16),
        scratch_shapes=[pltpu.VMEM((H, D, D), F32)],
        compiler_params=_params("arbitrary", "arbitrary"),
        name="retention",
    )(rq, rk, rv, rg, dmask, kd, qd, cd, gn_g.reshape(1, W))


def _swa_body(sink_ref, q_ref, kc_ref, kp_ref, vc_ref, vp_ref, mask_ref, o_ref, *, tq):
    n = pl.program_id(1)
    kcat = jnp.concatenate([kp_ref[0], kc_ref[0]], axis=0)
    vcat = jnp.concatenate([vp_ref[0], vc_ref[0]], axis=0)
    for j in range(tq // BLOCK):
        qb = q_ref[0, j * BLOCK:(j + 1) * BLOCK, :]
        kb = kcat[j * BLOCK:(j + 2) * BLOCK]
        vb = vcat[j * BLOCK:(j + 2) * BLOCK]
        if j == 0:
            mask = jnp.where(n == 0, mask_ref[1], mask_ref[0])
        else:
            mask = mask_ref[0]
        keep = mask > 0.0
        outs = []
        for hq in range(SWA_HEADS):
            hk = hq // SWA_GROUP
            kh = kb[:, hk * HEAD_DIM:(hk + 1) * HEAD_DIM]
            vh = vb[:, hk * HEAD_DIM:(hk + 1) * HEAD_DIM]
            qh = qb[:, hq * HEAD_DIM:(hq + 1) * HEAD_DIM]
            s = jnp.where(keep, _dot_nt(qh, kh), NEG_INF)
            sink = sink_ref[hq]
            m = jnp.maximum(jnp.max(s, axis=-1, keepdims=True), sink)
            p = jnp.exp(s - m)
            den = jnp.sum(p, axis=-1, keepdims=True) + jnp.exp(sink - m)
            outs.append(_dot(p.astype(BF16), vh) / den)
        o_ref[0, j * BLOCK:(j + 1) * BLOCK, :] = jnp.concatenate(outs, axis=1).astype(BF16)


def _swa(sq, sk, sv, sinks, tq):
    B, S, W = sq.shape
    tq = min(tq, S)
    r = tq // BLOCK
    qi = jnp.arange(BLOCK)[:, None] + BLOCK
    kj = jnp.arange(2 * BLOCK)[None, :]
    rel = qi - kj
    allowed = (rel >= 0) & (rel < WINDOW)
    mask = jnp.stack([allowed, allowed & (kj >= BLOCK)]).astype(F32)
    cur = pl.BlockSpec((1, tq, SWA_KV_W), lambda b, n: (b, n, 0))
    prev = pl.BlockSpec((1, BLOCK, SWA_KV_W), lambda b, n: (b, jnp.maximum(n * r - 1, 0), 0))
    return pl.pallas_call(
        functools.partial(_swa_body, tq=tq),
        grid=(B, S // tq),
        in_specs=[pl.BlockSpec(memory_space=pltpu.SMEM),
                  pl.BlockSpec((1, tq, W), lambda b, n: (b, n, 0)),
                  cur, prev, cur, prev,
                  pl.BlockSpec((2, BLOCK, 2 * BLOCK), lambda b, n: (0, 0, 0))],
        out_specs=pl.BlockSpec((1, tq, W), lambda b, n: (b, n, 0)),
        out_shape=jax.ShapeDtypeStruct((B, S, W), BF16),
        compiler_params=_params("arbitrary", "arbitrary"),
        name="swa",
    )(sinks.astype(F32), sq, sk, sk, sv, sv, mask)


def _conv_body(h_ref, dw_ref, db_ref, lg_ref, lb_ref, pw_ref, o_ref, hbuf, *, ts, rb):
    s = pl.program_id(1)

    @pl.when(s == 0)
    def _():
        hbuf[0:CONV_HALO, :] = jnp.zeros((CONV_HALO, CONV_CH), F32)

    @pl.when(s > 0)
    def _():
        hbuf[0:CONV_HALO, :] = hbuf[ts:ts + CONV_HALO, :]

    hbuf[CONV_HALO:CONV_HALO + ts, :] = h_ref[0]
    off = CONV_HALO - (CONV_WIDTH - 1)
    for r in range(ts // rb):
        base = r * rb
        acc = jnp.broadcast_to(db_ref[...], (rb, CONV_CH))
        for w in range(CONV_WIDTH):
            acc = acc + hbuf[base + off + w:base + off + w + rb, :] * dw_ref[w:w + 1, :]
        mu = jnp.mean(acc, axis=-1, keepdims=True)
        d = acc - mu
        var = jnp.mean(d * d, axis=-1, keepdims=True)
        hn = d * lax.rsqrt(var + EPS) * lg_ref[...] + lb_ref[...]
        o_ref[0, base:base + rb, :] = _dot(_silu(hn).astype(BF16), pw_ref[...]).astype(BF16)


def _conv(ch, dw_w, dw_b, ln_g, ln_b, pw_w, ts, rb):
    B, S, W = ch.shape
    ts = min(ts, S)
    rb = min(rb, ts)
    pw = jnp.zeros((W, W), F32)
    gd = W // CONV_GROUPS
    for g in range(CONV_GROUPS):
        pw = pw.at[g * gd:(g + 1) * gd, g * gd:(g + 1) * gd].set(pw_w[g])
    act = pl.BlockSpec((1, ts, W), lambda b, s: (b, s, 0))

    def const(shape):
        return pl.BlockSpec(shape, lambda b, s: (0,) * len(shape))

    return pl.pallas_call(
        functools.partial(_conv_body, ts=ts, rb=rb),
        grid=(B, S // ts),
        in_specs=[act, const((CONV_WIDTH, W)), const((1, W)), const((1, W)), const((1, W)), const((W, W))],
        out_specs=act,
        out_shape=jax.ShapeDtypeStruct((B, S, W), BF16),
        scratch_shapes=[pltpu.VMEM((CONV_HALO + ts, W), F32)],
        compiler_params=_params("arbitrary", "arbitrary"),
        name="conv",
    )(ch, dw_w, dw_b.reshape(1, W), ln_g.reshape(1, W), ln_b.reshape(1, W), pw.astype(BF16))


def _outproj_body(yr_ref, ys_ref, yc_ref, w_ref, x_ref, g_ref, *rest, route):
    if route:
        wr_ref, xo_ref, hn_ref, rt_ref = rest
    else:
        xo_ref, hn_ref = rest
    a, b = RET_W, RET_W + SWA_W
    x = x_ref[...] + (_dot(yr_ref[...], w_ref[0:a, :]) + _dot(ys_ref[...], w_ref[a:b, :])
                      + _dot(yc_ref[...], w_ref[b:, :]))
    xo_ref[...] = x
    h = _rms(x, g_ref[...])
    if not route:
        hn_ref[...] = h.astype(BF16)
        return
    hn_ref[...] = h
    logits = _dot(h.astype(BF16), wr_ref[...])
    lane = lax.broadcasted_iota(jnp.int32, logits.shape, 1)
    lg = jnp.where(lane < N_EXPERTS, logits, -jnp.inf)
    m1 = jnp.max(lg, axis=-1, keepdims=True)
    i1 = jnp.min(jnp.where(lg == m1, lane, LANES), axis=-1, keepdims=True)
    lg2 = jnp.where(lane == i1, -jnp.inf, lg)
    m2 = jnp.max(lg2, axis=-1, keepdims=True)
    i2 = jnp.min(jnp.where(lg2 == m2, lane, LANES), axis=-1, keepdims=True)
    e = jnp.exp(m2 - m1)
    w1 = 1.0 / (1.0 + e)
    w2 = e / (1.0 + e)
    rt_ref[...] = jnp.where(lane == 0, i1.astype(F32),
                            jnp.where(lane == 1, i2.astype(F32),
                                      jnp.where(lane == 2, w1, jnp.where(lane == 3, w2, 0.0))))


def _outproj(yr, ys, yc, w_bf, x, g, w_router, tm):
    T, D = x.shape
    tm = min(tm, T)
    route = w_router is not None

    def row(width):
        return pl.BlockSpec((tm, width), lambda i: (i, 0))

    def const(shape):
        return pl.BlockSpec(shape, lambda i: (0,) * len(shape))

    in_specs = [row(RET_W), row(SWA_W), row(CONV_CH), const((D, D)), row(D), const((1, D))]
    args = [yr, ys, yc, w_bf, x, g.reshape(1, D)]
    out_shape = [jax.ShapeDtypeStruct((T, D), F32), jax.ShapeDtypeStruct((T, D), F32 if route else BF16)]
    out_specs = [row(D), row(D)]
    if route:
        wr = jnp.zeros((D, LANES), F32).at[:, :N_EXPERTS].set(w_router).astype(BF16)
        in_specs.append(const((D, LANES)))
        args.append(wr)
        out_shape.append(jax.ShapeDtypeStruct((T, LANES), F32))
        out_specs.append(row(LANES))
    return pl.pallas_call(
        functools.partial(_outproj_body, route=route),
        grid=(T // tm,),
        in_specs=in_specs, out_specs=out_specs, out_shape=out_shape,
        compiler_params=_params("arbitrary"),
        name="outproj_route" if route else "outproj",
    )(*args)


def _dense_ffn_body(h_ref, x_ref, wg_ref, wu_ref, wd_ref, o_ref, *, chunks):
    h = h_ref[...]
    acc = x_ref[...]
    for a, b in chunks:
        g = _dot(h, wg_ref[:, a:b])
        u = _dot(h, wu_ref[:, a:b])
        acc = acc + _dot((_silu(g) * u).astype(BF16), wd_ref[a:b, :])
    o_ref[...] = acc


def _dense_ffn(hn, x, wg, wu, wd, tm, fc):
    T, D = x.shape
    FF = wg.shape[1]
    tm = min(tm, T)
    chunks = tuple((a, min(a + fc, FF)) for a in range(0, FF, fc))

    def row(dtype):
        return pl.BlockSpec((tm, D), lambda i: (i, 0))

    return pl.pallas_call(
        functools.partial(_dense_ffn_body, chunks=chunks),
        grid=(T // tm,),
        in_specs=[row(BF16), row(F32),
                  pl.BlockSpec((D, FF), lambda i: (0, 0)),
                  pl.BlockSpec((D, FF), lambda i: (0, 0)),
                  pl.BlockSpec((FF, D), lambda i: (0, 0))],
        out_specs=row(F32),
        out_shape=jax.ShapeDtypeStruct((T, D), F32),
        compiler_params=_params("arbitrary"),
        name="dense_ffn",
    )(hn, x, wg.astype(BF16), wu.astype(BF16), wd.astype(BF16))


def _route_meta(ids, tm):
    flat = ids.reshape(-1)
    oh = (flat[:, None] == jnp.arange(N_EXPERTS, dtype=jnp.int32)[None, :]).astype(jnp.int32)
    csum = jnp.cumsum(oh, axis=0)
    counts = csum[-1]
    rank = jnp.sum((csum - oh) * oh, axis=1)
    ptiles = (counts + tm - 1) // tm
    tile_end = jnp.cumsum(ptiles)
    tile_start = tile_end - ptiles
    dest = (jnp.sum(oh * tile_start[None, :], axis=1) * tm + rank).astype(jnp.int32)
    n_tiles = flat.shape[0] // tm + N_EXPERTS
    tile_expert = jnp.sum(jnp.arange(n_tiles, dtype=jnp.int32)[:, None] >= tile_end[None, :], axis=1)
    tile_expert = jnp.minimum(tile_expert, N_EXPERTS - 1).astype(jnp.int32)
    return dest, tile_expert, tile_end[-1:].astype(jnp.int32)


def _dispatch_body(dest_ref, h_ref, xs_hbm, sem, *, tt):
    def copy(t, slot):
        return pltpu.make_async_copy(h_ref.at[pl.ds(t, 1)],
                                     xs_hbm.at[pl.ds(dest_ref[0, 0, TOP_K * t + slot], 1)], sem)

    def start(t, c):
        for slot in range(TOP_K):
            copy(t, slot).start()
        return c

    def wait(t, c):
        for slot in range(TOP_K):
            copy(t, slot).wait()
        return c

    lax.fori_loop(0, tt, start, 0, unroll=8)
    lax.fori_loop(0, tt, wait, 0, unroll=8)


def _dispatch(hn, dest, n_rows, tt):
    T, D = hn.shape
    tt = min(tt, T)
    return pl.pallas_call(
        lambda dest_ref, h_hbm, zero_hbm, xs_hbm, sem: _dispatch_body(dest_ref, h_hbm, xs_hbm, sem, tt=tt),
        grid=(T // tt,),
        in_specs=[pl.BlockSpec((1, 1, TOP_K * tt), lambda i: (i, 0, 0), memory_space=pltpu.SMEM),
                  pl.BlockSpec((tt, D), lambda i: (i, 0)),
                  pl.BlockSpec(memory_space=pl.ANY)],
        out_specs=pl.BlockSpec(memory_space=pl.ANY),
        out_shape=jax.ShapeDtypeStruct((n_rows, D), hn.dtype),
        scratch_shapes=[pltpu.SemaphoreType.DMA],
        input_output_aliases={2: 0},
        compiler_params=pltpu.CompilerParams(dimension_semantics=("arbitrary",), has_side_effects=True),
        name="moe_dispatch",
    )(dest.reshape(T // tt, 1, TOP_K * tt), hn, jnp.zeros((n_rows, D), hn.dtype))


def _moe_ffn_body(te_ref, nu_ref, x_ref, wg_ref, wu_ref, wd_ref, o_ref):
    i, f = pl.program_id(0), pl.program_id(1)

    @pl.when(i < nu_ref[0])
    def _():
        x = x_ref[...].astype(BF16)
        g = _dot(x, wg_ref[0].astype(BF16))
        u = _dot(x, wu_ref[0].astype(BF16))
        y = _dot((_silu(g) * u).astype(BF16), wd_ref[0].astype(BF16))

        @pl.when(f == 0)
        def _():
            o_ref[...] = y

        @pl.when(f > 0)
        def _():
            o_ref[...] += y

    @pl.when(jnp.logical_and(i >= nu_ref[0], f == 0))
    def _():
        o_ref[...] = jnp.zeros_like(o_ref)


def _moe_ffn(xs, tile_expert, n_used, wg, wu, wd, tm, fc):
    R, D = xs.shape
    E, _, FF = wg.shape
    nf = FF // fc

    def tile(i, nu):
        return jnp.minimum(i, nu[0] - 1)

    def fidx(i, f, nu):
        return jnp.where(i < nu[0], f, nf - 1)

    grid_spec = pltpu.PrefetchScalarGridSpec(
        num_scalar_prefetch=2,
        grid=(R // tm, nf),
        in_specs=[pl.BlockSpec((tm, D), lambda i, f, te, nu: (tile(i, nu), 0)),
                  pl.BlockSpec((1, D, fc), lambda i, f, te, nu: (te[tile(i, nu)], 0, fidx(i, f, nu))),
                  pl.BlockSpec((1, D, fc), lambda i, f, te, nu: (te[tile(i, nu)], 0, fidx(i, f, nu))),
                  pl.BlockSpec((1, fc, D), lambda i, f, te, nu: (te[tile(i, nu)], fidx(i, f, nu), 0))],
        out_specs=pl.BlockSpec((tm, D), lambda i, f, te, nu: (i, 0)),
    )
    return pl.pallas_call(
        _moe_ffn_body,
        grid_spec=grid_spec,
        out_shape=jax.ShapeDtypeStruct((R, D), F32),
        compiler_params=_params("arbitrary", "arbitrary"),
        name="moe_ffn",
    )(tile_expert, n_used, xs, wg, wu, wd)


def _combine_body(dest_ref, ys_hbm, x_ref, rt_ref, g_ref, o_ref, buf, sem, *, tc):
    def copy(t, slot):
        return pltpu.make_async_copy(ys_hbm.at[pl.ds(dest_ref[0, 0, TOP_K * t + slot], 1)],
                                     buf.at[slot, pl.ds(t, 1)], sem)

    def start(t, c):
        for slot in range(TOP_K):
            copy(t, slot).start()
        return c

    def wait(t, c):
        for slot in range(TOP_K):
            copy(t, slot).wait()
        return c

    lax.fori_loop(0, tc, start, 0, unroll=8)
    lax.fori_loop(0, tc, wait, 0, unroll=8)
    rt = rt_ref[...]
    y = x_ref[...] + (rt[:, 2:3] * buf[0] + rt[:, 3:4] * buf[1])
    o_ref[...] = _rms(y, g_ref[...])


def _combine(ys, dest, x, rt, g, tc):
    T, D = x.shape
    tc = min(tc, T)
    return pl.pallas_call(
        functools.partial(_combine_body, tc=tc),
        grid=(T // tc,),
        in_specs=[pl.BlockSpec((1, 1, TOP_K * tc), lambda i: (i, 0, 0), memory_space=pltpu.SMEM),
                  pl.BlockSpec(memory_space=pl.ANY),
                  pl.BlockSpec((tc, D), lambda i: (i, 0)),
                  pl.BlockSpec((tc, LANES), lambda i: (i, 0)),
                  pl.BlockSpec((1, D), lambda i: (0, 0))],
        out_specs=pl.BlockSpec((tc, D), lambda i: (i, 0)),
        out_shape=jax.ShapeDtypeStruct((T, D), F32),
        scratch_shapes=[pltpu.VMEM((TOP_K, tc, D), F32), pltpu.SemaphoreType.DMA],
        compiler_params=_params("arbitrary"),
        name="moe_combine",
    )(dest.reshape(T // tc, 1, TOP_K * tc), ys, x, rt, g.reshape(1, D))


def kernel(x, norm_mix_g, w_in, ret_gn_g, attn_sinks, conv_dw_w, conv_dw_b, conv_ln_g, conv_ln_b,
           conv_pw_w, w_out, norm_ffn_g, ffn_w_gate, ffn_w_up, ffn_w_down, moe_router, moe_w_gate,
           moe_w_up, moe_w_down, final_norm_g):
    B, S, D = x.shape
    T = B * S
    depth = w_in.shape[0]
    assert depth == 2 and ffn_w_gate.shape[0] == 1 and moe_router.shape[0] == 1, "dense layer then MoE layer"
    moe_tm = min(1024, T)

    for l in range(depth):
        rq, rk, rv, rg, sq, sk, sv, ch = _inproj(x, norm_mix_g[l], w_in[l].astype(BF16), tm=512)
        y_ret = _retention(rq, rk, rv, rg, ret_gn_g[l], ts=512)
        y_swa = _swa(sq, sk, sv, attn_sinks[l], tq=512)
        y_conv = _conv(ch, conv_dw_w[l], conv_dw_b[l], conv_ln_g[l], conv_ln_b[l], conv_pw_w[l], ts=512, rb=64)
        flat = lambda a: a.reshape(T, a.shape[-1])
        if l % 2 == 0:
            x2, hn = _outproj(flat(y_ret), flat(y_swa), flat(y_conv), w_out[l].astype(BF16), flat(x),
                              norm_ffn_g[l], None, tm=512)
            j = l // 2
            x = _dense_ffn(hn, x2, ffn_w_gate[j], ffn_w_up[j], ffn_w_down[j], tm=512, fc=512).reshape(B, S, D)
        else:
            j = l // 2
            x2, hn, rt = _outproj(flat(y_ret), flat(y_swa), flat(y_conv), w_out[l].astype(BF16), flat(x),
                                  norm_ffn_g[l], moe_router[j], tm=512)
            ids = rt[:, :TOP_K].astype(jnp.int32)
            dest, tile_expert, n_used = _route_meta(ids, moe_tm)
            n_rows = TOP_K * T + N_EXPERTS * moe_tm
            xs = _dispatch(hn, dest, n_rows, tt=1024)
            ys = _moe_ffn(xs, tile_expert, n_used, moe_w_gate[j], moe_w_up[j], moe_w_down[j], tm=moe_tm, fc=512)
            x = _combine(ys, dest, x2, rt, final_norm_g, tc=256).reshape(B, S, D)
    return x
```

```python
import functools

import jax
import jax.numpy as jnp
from jax import lax
from jax.experimental import pallas as pl
from jax.experimental.pallas import tpu as pltpu

F32 = jnp.float32
BF16 = jnp.bfloat16

HEAD_DIM = 64
RET_HEADS = 4
RET_W = RET_HEADS * HEAD_DIM
SWA_HEADS = 8
SWA_KV_HEADS = 2
SWA_GROUP = SWA_HEADS // SWA_KV_HEADS
SWA_W = SWA_HEADS * HEAD_DIM
SWA_KV_W = SWA_KV_HEADS * HEAD_DIM
CONV_CH = 256
CONV_GROUPS = 4
CONV_WIDTH = 31
WINDOW = 128
BLOCK = 128
RET_CHUNK = 128
ROPE_THETA = 500000.0
ROPE_DIM = HEAD_DIM // 4
RET_ROPE_THETA = 10000.0
N_EXPERTS = 8
TOP_K = 2
EPS = 1e-6
NEG_INF = -1e30

LANES = 128
CONV_HALO = 32
VMEM_LIMIT = 56 * 1024 * 1024

O_RQ, O_RK, O_RV, O_RG = 0, RET_W, 2 * RET_W, 3 * RET_W
O_SQ = 4 * RET_W
O_SK = O_SQ + SWA_W
O_SV = O_SK + SWA_KV_W
O_CA = O_SV + SWA_KV_W
O_CG = O_CA + CONV_CH
D_IN = O_CG + CONV_CH


def _params(*sem):
    return pltpu.CompilerParams(dimension_semantics=sem, vmem_limit_bytes=VMEM_LIMIT)


def _rms(x, g):
    return x * lax.rsqrt(jnp.mean(x * x, axis=-1, keepdims=True) + EPS) * g


def _silu(x):
    return x * jax.nn.sigmoid(x)


def _dot(a, b):
    return jnp.dot(a, b, preferred_element_type=F32)


def _dot_nt(a, b):
    return lax.dot_general(a, b, (((1,), (1,)), ((), ())), preferred_element_type=F32)


def _dot_tn(a, b):
    return lax.dot_general(a, b, (((0,), (0,)), ((), ())), preferred_element_type=F32)


def _rope_tables(seq, theta, rot_dim):
    half = rot_dim // 2
    inv = 1.0 / (theta ** (jnp.arange(half, dtype=F32) / half))
    ang = jnp.arange(seq, dtype=F32)[:, None] * inv[None, :]
    cos, sin = jnp.cos(ang), jnp.sin(ang)
    rest = HEAD_DIM - rot_dim
    c = jnp.concatenate([cos, cos, jnp.ones((seq, rest), F32)], axis=1)
    s = jnp.concatenate([-sin, sin, jnp.zeros((seq, rest), F32)], axis=1)
    reps = LANES // HEAD_DIM
    return jnp.tile(c, (1, reps)), jnp.tile(s, (1, reps))


def _inproj_body(x_ref, g_ref, w_ref, rc_ref, rs_ref, sc_ref, ss_ref,
                 rq_ref, rk_ref, rv_ref, rg_ref, sq_ref, sk_ref, sv_ref, ch_ref):
    h = _rms(x_ref[0], g_ref[...]).astype(BF16)
    lane = lax.broadcasted_iota(jnp.int32, (1, LANES), 1) % HEAD_DIM

    def seg(a, b):
        return _dot(h, w_ref[:, a:b])

    def rope(v, c, s, half):
        up = pltpu.roll(v, LANES - half, axis=1)
        dn = pltpu.roll(v, half, axis=1)
        return v * c + jnp.where(lane < half, up, dn) * s

    def rope_store(z, o_ref, c, s, half, scale):
        for i in range(z.shape[1] // LANES):
            sl = slice(i * LANES, (i + 1) * LANES)
            r = rope(z[:, sl], c, s, half)
            if scale != 1.0:
                r = r * scale
            o_ref[0, :, sl] = r.astype(o_ref.dtype)

    rc, rs = rc_ref[...], rs_ref[...]
    sc, ss = sc_ref[...], ss_ref[...]
    scale = HEAD_DIM ** -0.5
    rope_store(seg(O_RQ, O_RK), rq_ref, rc, rs, HEAD_DIM // 2, 1.0)
    rope_store(seg(O_RK, O_RV), rk_ref, rc, rs, HEAD_DIM // 2, scale)
    rv_ref[0] = seg(O_RV, O_RG).astype(BF16)
    rg_ref[0] = seg(O_RG, O_SQ)
    rope_store(seg(O_SQ, O_SK), sq_ref, sc, ss, ROPE_DIM // 2, scale)
    rope_store(seg(O_SK, O_SV), sk_ref, sc, ss, ROPE_DIM // 2, 1.0)
    sv_ref[0] = seg(O_SV, O_CA).astype(BF16)
    ch_ref[0] = seg(O_CA, O_CG) * jax.nn.sigmoid(seg(O_CG, D_IN))


def _inproj(x, g, w_bf, tm):
    B, S, D = x.shape
    tm = min(tm, S)
    rc, rs = _rope_tables(S, RET_ROPE_THETA, HEAD_DIM)
    sc, ss = _rope_tables(S, ROPE_THETA, ROPE_DIM)
    tab = pl.BlockSpec((tm, LANES), lambda s, b: (s, 0))

    def out(width, dtype):
        return (jax.ShapeDtypeStruct((B, S, width), dtype),
                pl.BlockSpec((1, tm, width), lambda s, b: (b, s, 0)))

    outs = [out(RET_W, BF16), out(RET_W, BF16), out(RET_W, BF16), out(RET_W, F32),
            out(SWA_W, BF16), out(SWA_KV_W, BF16), out(SWA_KV_W, BF16), out(CONV_CH, F32)]
    return pl.pallas_call(
        _inproj_body,
        grid=(S // tm, B),
        in_specs=[pl.BlockSpec((1, tm, D), lambda s, b: (b, s, 0)),
                  pl.BlockSpec((1, D), lambda s, b: (0, 0)),
                  pl.BlockSpec((D, D_IN), lambda s, b: (0, 0)),
                  tab, tab, tab, tab],
        out_specs=[o[1] for o in outs],
        out_shape=[o[0] for o in outs],
        compiler_params=_params("arbitrary", "arbitrary"),
        name="inproj",
    )(x, g.reshape(1, D), w_bf, rc, rs, sc, ss)


def _ret_body(q_ref, k_ref, v_ref, g_ref, dm_ref, kd_ref, qd_ref, cd_ref, gn_ref, o_ref, st_ref, *, ts):
    @pl.when(pl.program_id(1) == 0)
    def _():
        st_ref[...] = jnp.zeros_like(st_ref)

    C = RET_CHUNK
    for c in range(ts // C):
        rows = slice(c * C, (c + 1) * C)
        qa, ka, va, ga = q_ref[0, rows, :], k_ref[0, rows, :], v_ref[0, rows, :], g_ref[0, rows, :]
        outs = []
        for h in range(RET_HEADS):
            cols = slice(h * HEAD_DIM, (h + 1) * HEAD_DIM)
            q, k, v = qa[:, cols], ka[:, cols], va[:, cols]
            st = st_ref[h]
            scores = _dot_nt(q, k) * dm_ref[h]
            intra = _dot(scores.astype(BF16), v)
            cross = _dot(q, st.astype(BF16)) * qd_ref[h]
            kdec = (k.astype(F32) * kd_ref[h]).astype(BF16)
            st_ref[h] = st * cd_ref[h] + _dot_tn(kdec, v)
            o = intra + cross
            mu = jnp.mean(o, axis=-1, keepdims=True)
            d = o - mu
            var = jnp.mean(d * d, axis=-1, keepdims=True)
            outs.append(d * lax.rsqrt(var + EPS))
        on = jnp.concatenate(outs, axis=1) * gn_ref[...]
        o_ref[0, rows, :] = (_silu(ga) * on).astype(BF16)


def _retention(rq, rk, rv, rg, gn_g, ts):
    B, S, W = rq.shape
    ts = min(ts, S)
    C, H, D = RET_CHUNK, RET_HEADS, HEAD_DIM
    lg = jnp.log(1.0 - 2.0 ** (-5.0 - jnp.arange(H, dtype=F32)))
    idx = jnp.arange(C)
    rel = idx[:, None] - idx[None, :]
    dmask = jnp.where(rel[None] >= 0,
                      jnp.exp(jnp.maximum(rel, 0)[None].astype(F32) * lg[:, None, None]), 0.0)
    k_decay = jnp.exp((C - 1 - idx)[:, None].astype(F32) * lg[None, :])
    q_decay = jnp.exp((idx + 1)[:, None].astype(F32) * lg[None, :])
    chunk_decay = jnp.exp(C * lg)
    kd = jnp.broadcast_to(k_decay.T[:, :, None], (H, C, D))
    qd = jnp.broadcast_to(q_decay.T[:, :, None], (H, C, D))
    cd = jnp.broadcast_to(chunk_decay[:, None, None], (H, D, D))
    act = pl.BlockSpec((1, ts, W), lambda b, s: (b, s, 0))

    def const(shape):
        return pl.BlockSpec(shape, lambda b, s: (0,) * len(shape))

    return pl.pallas_call(
        functools.partial(_ret_body, ts=ts),
        grid=(B, S // ts),
        in_specs=[act, act, act, act, const((H, C, C)), const((H, C, D)), const((H, C, D)),
                  const((H, D, D)), const((1, W))],
        out_specs=act,
        out_shape=jax.ShapeDtypeStruct((B, S, W), BF16),
        scratch_shapes=[pltpu.VMEM((H, D, D), F32)],
        compiler_params=_params("arbitrary", "arbitrary"),
        name="retention",
    )(rq, rk, rv, rg, dmask, kd, qd, cd, gn_g.reshape(1, W))


def _swa_body(sink_ref, q_ref, kc_ref, kp_ref, vc_ref, vp_ref, mask_ref, o_ref, *, tq):
    n = pl.program_id(1)
    kcat = jnp.concatenate([kp_ref[0], kc_ref[0]], axis=0)
    vcat = jnp.concatenate([vp_ref[0], vc_ref[0]], axis=0)
    for j in range(tq // BLOCK):
        qb = q_ref[0, j * BLOCK:(j + 1) * BLOCK, :]
        kb = kcat[j * BLOCK:(j + 2) * BLOCK]
        vb = vcat[j * BLOCK:(j + 2) * BLOCK]
        if j == 0:
            mask = jnp.where(n == 0, mask_ref[1], mask_ref[0])
        else:
            mask = mask_ref[0]
        keep = mask > 0.0
        outs = []
        for hq in range(SWA_HEADS):
            hk = hq // SWA_GROUP
            kh = kb[:, hk * HEAD_DIM:(hk + 1) * HEAD_DIM]
            vh = vb[:, hk * HEAD_DIM:(hk + 1) * HEAD_DIM]
            qh = qb[:, hq * HEAD_DIM:(hq + 1) * HEAD_DIM]
            s = jnp.where(keep, _dot_nt(qh, kh), NEG_INF)
            sink = sink_ref[hq]
            m = jnp.maximum(jnp.max(s, axis=-1, keepdims=True), sink)
            p = jnp.exp(s - m)
            den = jnp.sum(p, axis=-1, keepdims=True) + jnp.exp(sink - m)
            outs.append(_dot(p.astype(BF16), vh) / den)
        o_ref[0, j * BLOCK:(j + 1) * BLOCK, :] = jnp.concatenate(outs, axis=1).astype(BF16)


def _swa(sq, sk, sv, sinks, tq):
    B, S, W = sq.shape
    tq = min(tq, S)
    r = tq // BLOCK
    qi = jnp.arange(BLOCK)[:, None] + BLOCK
    kj = jnp.arange(2 * BLOCK)[None, :]
    rel = qi - kj
    allowed = (rel >= 0) & (rel < WINDOW)
    mask = jnp.stack([allowed, allowed & (kj >= BLOCK)]).astype(F32)
    cur = pl.BlockSpec((1, tq, SWA_KV_W), lambda b, n: (b, n, 0))
    prev = pl.BlockSpec((1, BLOCK, SWA_KV_W), lambda b, n: (b, jnp.maximum(n * r - 1, 0), 0))
    return pl.pallas_call(
        functools.partial(_swa_body, tq=tq),
        grid=(B, S // tq),
        in_specs=[pl.BlockSpec(memory_space=pltpu.SMEM),
                  pl.BlockSpec((1, tq, W), lambda b, n: (b, n, 0)),
                  cur, prev, cur, prev,
                  pl.BlockSpec((2, BLOCK, 2 * BLOCK), lambda b, n: (0, 0, 0))],
        out_specs=pl.BlockSpec((1, tq, W), lambda b, n: (b, n, 0)),
        out_shape=jax.ShapeDtypeStruct((B, S, W), BF16),
        compiler_params=_params("arbitrary", "arbitrary"),
        name="swa",
    )(sinks.astype(F32), sq, sk, sk, sv, sv, mask)


def _conv_body(h_ref, dw_ref, db_ref, lg_ref, lb_ref, pw_ref, o_ref, hbuf, *, ts, rb):
    s = pl.program_id(1)

    @pl.when(s == 0)
    def _():
        hbuf[0:CONV_HALO, :] = jnp.zeros((CONV_HALO, CONV_CH), F32)

    @pl.when(s > 0)
    def _():
        hbuf[0:CONV_HALO, :] = hbuf[ts:ts + CONV_HALO, :]

    hbuf[CONV_HALO:CONV_HALO + ts, :] = h_ref[0]
    off = CONV_HALO - (CONV_WIDTH - 1)
    for r in range(ts // rb):
        base = r * rb
        acc = jnp.broadcast_to(db_ref[...], (rb, CONV_CH))
        for w in range(CONV_WIDTH):
            acc = acc + hbuf[base + off + w:base + off + w + rb, :] * dw_ref[w:w + 1, :]
        mu = jnp.mean(acc, axis=-1, keepdims=True)
        d = acc - mu
        var = jnp.mean(d * d, axis=-1, keepdims=True)
        hn = d * lax.rsqrt(var + EPS) * lg_ref[...] + lb_ref[...]
        o_ref[0, base:base + rb, :] = _dot(_silu(hn).astype(BF16), pw_ref[...]).astype(BF16)


def _conv(ch, dw_w, dw_b, ln_g, ln_b, pw_w, ts, rb):
    B, S, W = ch.shape
    ts = min(ts, S)
    rb = min(rb, ts)
    pw = jnp.zeros((W, W), F32)
    gd = W // CONV_GROUPS
    for g in range(CONV_GROUPS):
        pw = pw.at[g * gd:(g + 1) * gd, g * gd:(g + 1) * gd].set(pw_w[g])
    act = pl.BlockSpec((1, ts, W), lambda b, s: (b, s, 0))

    def const(shape):
        return pl.BlockSpec(shape, lambda b, s: (0,) * len(shape))

    return pl.pallas_call(
        functools.partial(_conv_body, ts=ts, rb=rb),
        grid=(B, S // ts),
        in_specs=[act, const((CONV_WIDTH, W)), const((1, W)), const((1, W)), const((1, W)), const((W, W))],
        out_specs=act,
        out_shape=jax.ShapeDtypeStruct((B, S, W), BF16),
        scratch_shapes=[pltpu.VMEM((CONV_HALO + ts, W), F32)],
        compiler_params=_params("arbitrary", "arbitrary"),
        name="conv",
    )(ch, dw_w, dw_b.reshape(1, W), ln_g.reshape(1, W), ln_b.reshape(1, W), pw.astype(BF16))


def _outproj_body(yr_ref, ys_ref, yc_ref, w_ref, x_ref, g_ref, *rest, route):
    if route:
        wr_ref, xo_ref, hn_ref, rt_ref = rest
    else:
        xo_ref, hn_ref = rest
    a, b = RET_W, RET_W + SWA_W
    x = x_ref[...] + (_dot(yr_ref[...], w_ref[0:a, :]) + _dot(ys_ref[...], w_ref[a:b, :])
                      + _dot(yc_ref[...], w_ref[b:, :]))
    xo_ref[...] = x
    h = _rms(x, g_ref[...]).astype(BF16)
    hn_ref[...] = h
    if not route:
        return
    logits = _dot(h, wr_ref[...])
    lane = lax.broadcasted_iota(jnp.int32, logits.shape, 1)
    lg = jnp.where(lane < N_EXPERTS, logits, -jnp.inf)
    m1 = jnp.max(lg, axis=-1, keepdims=True)
    i1 = jnp.min(jnp.where(lg == m1, lane, LANES), axis=-1, keepdims=True)
    lg2 = jnp.where(lane == i1, -jnp.inf, lg)
    m2 = jnp.max(lg2, axis=-1, keepdims=True)
    i2 = jnp.min(jnp.where(lg2 == m2, lane, LANES), axis=-1, keepdims=True)
    e = jnp.exp(m2 - m1)
    w1 = 1.0 / (1.0 + e)
    w2 = e / (1.0 + e)
    rt_ref[...] = jnp.where(lane == 0, i1.astype(F32),
                            jnp.where(lane == 1, i2.astype(F32),
                                      jnp.where(lane == 2, w1, jnp.where(lane == 3, w2, 0.0))))


def _outproj(yr, ys, yc, w_bf, x, g, w_router, tm):
    T, D = x.shape
    tm = min(tm, T)
    route = w_router is not None

    def row(width):
        return pl.BlockSpec((tm, width), lambda i: (i, 0))

    def const(shape):
        return pl.BlockSpec(shape, lambda i: (0,) * len(shape))

    in_specs = [row(RET_W), row(SWA_W), row(CONV_CH), const((D, D)), row(D), const((1, D))]
    args = [yr, ys, yc, w_bf, x, g.reshape(1, D)]
    out_shape = [jax.ShapeDtypeStruct((T, D), F32), jax.ShapeDtypeStruct((T, D), BF16)]
    out_specs = [row(D), row(D)]
    if route:
        wr = jnp.zeros((D, LANES), F32).at[:, :N_EXPERTS].set(w_router).astype(BF16)
        in_specs.append(const((D, LANES)))
        args.append(wr)
        out_shape.append(jax.ShapeDtypeStruct((T, LANES), F32))
        out_specs.append(row(LANES))
    return pl.pallas_call(
        functools.partial(_outproj_body, route=route),
        grid=(T // tm,),
        in_specs=in_specs, out_specs=out_specs, out_shape=out_shape,
        compiler_params=_params("arbitrary"),
        name="outproj_route" if route else "outproj",
    )(*args)


def _dense_ffn_body(h_ref, x_ref, wg_ref, wu_ref, wd_ref, o_ref, *, chunks):
    h = h_ref[...]
    acc = x_ref[...]
    for a, b in chunks:
        g = _dot(h, wg_ref[:, a:b])
        u = _dot(h, wu_ref[:, a:b])
        acc = acc + _dot((_silu(g) * u).astype(BF16), wd_ref[a:b, :])
    o_ref[...] = acc


def _dense_ffn(hn, x, wg, wu, wd, tm, fc):
    T, D = x.shape
    FF = wg.shape[1]
    tm = min(tm, T)
    chunks = tuple((a, min(a + fc, FF)) for a in range(0, FF, fc))

    def row(dtype):
        return pl.BlockSpec((tm, D), lambda i: (i, 0))

    return pl.pallas_call(
        functools.partial(_dense_ffn_body, chunks=chunks),
        grid=(T // tm,),
        in_specs=[row(BF16), row(F32),
                  pl.BlockSpec((D, FF), lambda i: (0, 0)),
                  pl.BlockSpec((D, FF), lambda i: (0, 0)),
                  pl.BlockSpec((FF, D), lambda i: (0, 0))],
        out_specs=row(F32),
        out_shape=jax.ShapeDtypeStruct((T, D), F32),
        compiler_params=_params("arbitrary"),
        name="dense_ffn",
    )(hn, x, wg.astype(BF16), wu.astype(BF16), wd.astype(BF16))


MOE_TB = 512
MOE_TM = 512
RUN_ALIGN = 8
RUN_SIZES = (512, 256, 128, 64, 32, 16, 8)
BLOCK_ROWS = TOP_K * MOE_TB + N_EXPERTS * RUN_ALIGN
BLOCK_LANES = 1152
MOE_FC = 512


def _route_meta(ids, n_tiles):
    T = ids.shape[0]
    nb = T // MOE_TB
    e = ids.reshape(nb, MOE_TB * TOP_K)
    oh = (e[..., None] == jnp.arange(N_EXPERTS, dtype=jnp.int32)).astype(jnp.int32)
    csum = jnp.cumsum(oh, axis=1)
    rank = jnp.sum((csum - oh) * oh, axis=-1)
    run_len = (csum[:, -1, :] + RUN_ALIGN - 1) // RUN_ALIGN * RUN_ALIGN
    loc_off = jnp.cumsum(run_len, axis=1) - run_len
    pos = jnp.sum(oh * loc_off[:, None, :], axis=-1) + rank
    group = jnp.sum(run_len, axis=0)
    ptiles = (group + MOE_TM - 1) // MOE_TM
    tile_end = jnp.cumsum(ptiles)
    gstart = (tile_end - ptiles) * MOE_TM
    run_start = gstart[None, :] + jnp.cumsum(run_len, axis=0) - run_len
    tile_expert = jnp.sum(jnp.arange(n_tiles, dtype=jnp.int32)[:, None] >= tile_end[None, :], axis=1)
    i32 = lambda a: a.astype(jnp.int32)
    pos3 = pos.reshape(nb, MOE_TB, TOP_K)
    return dict(
        pos_l=i32(jnp.swapaxes(pos3, 1, 2)),
        pos_c=i32(pos3.reshape(T, TOP_K)),
        loc_off=i32(loc_off.reshape(-1)), run_start=i32(run_start.reshape(-1)), run_len=i32(run_len.reshape(-1)),
        pad_start=i32(gstart + group), pad_len=i32(ptiles * MOE_TM - group),
        tile_expert=i32(jnp.minimum(tile_expert, N_EXPERTS - 1)), n_used=i32(tile_end[-1:]))


def _run_dma(src, dst, src_off, dst_off, length, sem, wait):
    off = 0
    for k in RUN_SIZES:
        part = length & k

        @pl.when(part != 0)
        def _(off=off, k=k):
            cp = pltpu.make_async_copy(src.at[pl.ds(pl.multiple_of(src_off + off, RUN_ALIGN), k)],
                                       dst.at[pl.ds(pl.multiple_of(dst_off + off, RUN_ALIGN), k)], sem)
            if wait:
                cp.wait()
            else:
                cp.start()

        off = off + part


def _dispatch_body(lo_ref, rs_ref, rl_ref, ps_ref, pn_ref, nu_ref, pos_ref, h_ref, xs_hbm, sbuf, zbuf, sem):
    b, nb = pl.program_id(0), pl.num_programs(0)
    slot = b % 2
    n_tiles = xs_hbm.shape[0] // MOE_TM

    def zero_tile(i, wait):
        cp = pltpu.make_async_copy(zbuf, xs_hbm.at[pl.ds(pl.multiple_of(i * MOE_TM, MOE_TM), MOE_TM)], sem.at[2])
        if wait:
            cp.wait()
        else:
            cp.start()

    def runs(blk, s, wait):
        for e in range(N_EXPERTS):
            j = blk * N_EXPERTS + e
            _run_dma(sbuf.at[s], xs_hbm, lo_ref[j], rs_ref[j], rl_ref[j], sem.at[s], wait)

    @pl.when(b >= 2)
    def _():
        runs(b - 2, slot, True)

    pos = pos_ref[0]
    r = lax.broadcasted_iota(jnp.int32, (BLOCK_ROWS, MOE_TB), 0)
    onehot = jnp.logical_or(r == pos[0:1, :], r == pos[1:2, :])
    sbuf[slot] = _dot(jnp.where(onehot, 1.0, 0.0).astype(BF16), h_ref[...])
    runs(b, slot, False)

    @pl.when(b == nb - 1)
    def _():
        zbuf[...] = jnp.zeros_like(zbuf)
        for e in range(N_EXPERTS):
            _run_dma(zbuf, xs_hbm, 0, ps_ref[e], pn_ref[e], sem.at[2], False)
        lax.fori_loop(nu_ref[0], n_tiles, lambda i, c: zero_tile(i, False), None)
        for e in range(N_EXPERTS):
            _run_dma(zbuf, xs_hbm, 0, ps_ref[e], pn_ref[e], sem.at[2], True)
        lax.fori_loop(nu_ref[0], n_tiles, lambda i, c: zero_tile(i, True), None)

        @pl.when(b >= 1)
        def _():
            runs(b - 1, 1 - slot, True)

        runs(b, slot, True)


def _dispatch(hn, meta, n_rows):
    T, D = hn.shape
    nb = T // MOE_TB
    grid_spec = pltpu.PrefetchScalarGridSpec(
        num_scalar_prefetch=6,
        grid=(nb,),
        in_specs=[pl.BlockSpec((1, TOP_K, MOE_TB), lambda b, *_: (b, 0, 0)),
                  pl.BlockSpec((MOE_TB, D), lambda b, *_: (b, 0))],
        out_specs=pl.BlockSpec(memory_space=pl.ANY),
        scratch_shapes=[pltpu.VMEM((2, BLOCK_ROWS, D), F32), pltpu.VMEM((MOE_TM, D), F32),
                        pltpu.SemaphoreType.DMA((3,))],
    )
    return pl.pallas_call(
        _dispatch_body,
        grid_spec=grid_spec,
        out_shape=jax.ShapeDtypeStruct((n_rows, D), F32),
        compiler_params=pltpu.CompilerParams(dimension_semantics=("arbitrary",), vmem_limit_bytes=VMEM_LIMIT,
                                             has_side_effects=True),
        name="moe_dispatch",
    )(meta["loc_off"], meta["run_start"], meta["run_len"], meta["pad_start"], meta["pad_len"], meta["n_used"],
      meta["pos_l"], hn)


def _moe_ffn_body(te_ref, nu_ref, x_ref, wg_hbm, wu_hbm, wd_hbm, o_ref, wg_s, wu_s, wd_s, st_a, st_d, sem):
    i = pl.program_id(0)
    e = te_ref[i]
    active = i < nu_ref[0]
    nf = wg_s.shape[1] // MOE_FC
    fsl = [slice(f * MOE_FC, (f + 1) * MOE_FC) for f in range(nf)]

    @pl.when(jnp.logical_and(active, jnp.logical_or(i == 0, e != te_ref[jnp.maximum(i - 1, 0)])))
    def _():
        jobs = ([(wg_hbm, wg_s, True, f) for f in range(nf)] + [(wu_hbm, wu_s, True, f) for f in range(nf)]
                + [(wd_hbm, wd_s, False, f) for f in range(nf)])

        def copy(j):
            w_hbm, _, cols, f = jobs[j]
            if cols:
                return pltpu.make_async_copy(w_hbm.at[e, :, fsl[f]], st_a.at[j % 2], sem.at[j % 2])
            return pltpu.make_async_copy(w_hbm.at[e, fsl[f], :], st_d.at[j % 2], sem.at[j % 2])

        copy(0).start()
        for j, (_, w_s, cols, f) in enumerate(jobs):
            if j + 1 < len(jobs):
                copy(j + 1).start()
            copy(j).wait()
            if cols:
                w_s[:, fsl[f]] = st_a[j % 2].astype(BF16)
            else:
                w_s[fsl[f], :] = st_d[j % 2].astype(BF16)

    @pl.when(active)
    def _():
        x = x_ref[...].astype(BF16)
        acc = jnp.zeros(o_ref.shape, F32)
        for sl in fsl:
            g = _dot(x, wg_s[:, sl])
            u = _dot(x, wu_s[:, sl])
            acc = acc + _dot((_silu(g) * u).astype(BF16), wd_s[sl, :])
        o_ref[...] = acc

    @pl.when(jnp.logical_not(active))
    def _():
        o_ref[...] = jnp.zeros_like(o_ref)


def _moe_ffn(xs, meta, wg, wu, wd):
    R, D = xs.shape
    E, _, FF = wg.shape
    grid_spec = pltpu.PrefetchScalarGridSpec(
        num_scalar_prefetch=2,
        grid=(R // MOE_TM,),
        in_specs=[pl.BlockSpec((MOE_TM, D), lambda i, te, nu: (jnp.minimum(i, nu[0] - 1), 0)),
                  pl.BlockSpec(memory_space=pl.ANY), pl.BlockSpec(memory_space=pl.ANY),
                  pl.BlockSpec(memory_space=pl.ANY)],
        out_specs=pl.BlockSpec((MOE_TM, D), lambda i, te, nu: (i, 0)),
        scratch_shapes=[pltpu.VMEM((D, FF), BF16), pltpu.VMEM((D, FF), BF16), pltpu.VMEM((FF, D), BF16),
                        pltpu.VMEM((2, D, MOE_FC), F32), pltpu.VMEM((2, MOE_FC, D), F32),
                        pltpu.SemaphoreType.DMA((2,))],
    )
    return pl.pallas_call(
        _moe_ffn_body,
        grid_spec=grid_spec,
        out_shape=jax.ShapeDtypeStruct((R, D), F32),
        compiler_params=_params("arbitrary"),
        name="moe_ffn",
    )(meta["tile_expert"], meta["n_used"], xs, wg, wu, wd)


def _combine_body(lo_ref, rs_ref, rl_ref, ys_hbm, x_ref, rt_ref, pc_ref, g_ref, o_ref, ybuf, sem):
    b, nb = pl.program_id(0), pl.num_programs(0)
    slot = b % 2

    def runs(blk, s, wait):
        for e in range(N_EXPERTS):
            j = blk * N_EXPERTS + e
            _run_dma(ys_hbm, ybuf.at[s], rs_ref[j], lo_ref[j], rl_ref[j], sem.at[s], wait)

    @pl.when(b == 0)
    def _():
        ybuf[...] = jnp.zeros_like(ybuf)
        runs(0, 0, False)

    @pl.when(b + 1 < nb)
    def _():
        runs(b + 1, 1 - slot, False)

    runs(b, slot, True)
    y = ybuf[slot].astype(BF16)
    pc, rt = pc_ref[...], rt_ref[...]
    lane = lax.broadcasted_iota(jnp.int32, (MOE_TB, BLOCK_LANES), 1)
    q0 = jnp.where(lane == pc[:, 0:1], 1.0, 0.0).astype(BF16)
    q1 = jnp.where(lane == pc[:, 1:2], 1.0, 0.0).astype(BF16)
    moe = rt[:, 2:3] * _dot(q0, y) + rt[:, 3:4] * _dot(q1, y)
    o_ref[...] = _rms(x_ref[...] + moe, g_ref[...])


def _combine(ys, meta, x, rt, g):
    T, D = x.shape
    grid_spec = pltpu.PrefetchScalarGridSpec(
        num_scalar_prefetch=3,
        grid=(T // MOE_TB,),
        in_specs=[pl.BlockSpec(memory_space=pl.ANY),
                  pl.BlockSpec((MOE_TB, D), lambda b, *_: (b, 0)),
                  pl.BlockSpec((MOE_TB, LANES), lambda b, *_: (b, 0)),
                  pl.BlockSpec((MOE_TB, TOP_K), lambda b, *_: (b, 0)),
                  pl.BlockSpec((1, D), lambda b, *_: (0, 0))],
        out_specs=pl.BlockSpec((MOE_TB, D), lambda b, *_: (b, 0)),
        scratch_shapes=[pltpu.VMEM((2, BLOCK_LANES, D), F32), pltpu.SemaphoreType.DMA((2,))],
    )
    return pl.pallas_call(
        _combine_body,
        grid_spec=grid_spec,
        out_shape=jax.ShapeDtypeStruct((T, D), F32),
        compiler_params=_params("arbitrary"),
        name="moe_combine",
    )(meta["loc_off"], meta["run_start"], meta["run_len"], ys, x, rt, meta["pos_c"], g.reshape(1, D))


def kernel(x, norm_mix_g, w_in, ret_gn_g, attn_sinks, conv_dw_w, conv_dw_b, conv_ln_g, conv_ln_b,
           conv_pw_w, w_out, norm_ffn_g, ffn_w_gate, ffn_w_up, ffn_w_down, moe_router, moe_w_gate,
           moe_w_up, moe_w_down, final_norm_g):
    B, S, D = x.shape
    T = B * S
    depth = w_in.shape[0]
    assert depth == 2 and ffn_w_gate.shape[0] == 1 and moe_router.shape[0] == 1, "dense layer then MoE layer"
    assert T % MOE_TB == 0 and moe_w_gate.shape[-1] % MOE_FC == 0

    for l in range(depth):
        rq, rk, rv, rg, sq, sk, sv, ch = _inproj(x, norm_mix_g[l], w_in[l].astype(BF16), tm=512)
        y_ret = _retention(rq, rk, rv, rg, ret_gn_g[l], ts=512)
        y_swa = _swa(sq, sk, sv, attn_sinks[l], tq=512)
        y_conv = _conv(ch, conv_dw_w[l], conv_dw_b[l], conv_ln_g[l], conv_ln_b[l], conv_pw_w[l], ts=512, rb=64)
        flat = lambda a: a.reshape(T, a.shape[-1])
        j = l // 2
        if l % 2 == 0:
            x2, hn = _outproj(flat(y_ret), flat(y_swa), flat(y_conv), w_out[l].astype(BF16), flat(x),
                              norm_ffn_g[l], None, tm=512)
            x = _dense_ffn(hn, x2, ffn_w_gate[j], ffn_w_up[j], ffn_w_down[j], tm=512, fc=512).reshape(B, S, D)
        else:
            x2, hn, rt = _outproj(flat(y_ret), flat(y_swa), flat(y_conv), w_out[l].astype(BF16), flat(x),
                                  norm_ffn_g[l], moe_router[j], tm=512)
            n_rows = TOP_K * T + (T // MOE_TB) * N_EXPERTS * RUN_ALIGN + N_EXPERTS * MOE_TM
            meta = _route_meta(rt[:, :TOP_K].astype(jnp.int32), n_rows // MOE_TM)
            xs = _dispatch(hn, meta, n_rows)
            ys = _moe_ffn(xs, meta, moe_w_gate[j], moe_w_up[j], moe_w_down[j])
            x = _combine(ys, meta, x2, rt, final_norm_g).reshape(B, S, D)
    return x
```

```python
import functools

import jax
import jax.numpy as jnp
from jax import lax
from jax.experimental import pallas as pl
from jax.experimental.pallas import tpu as pltpu

F32 = jnp.float32
BF16 = jnp.bfloat16

HEAD_DIM = 64
RET_HEADS = 4
RET_W = RET_HEADS * HEAD_DIM
SWA_HEADS = 8
SWA_KV_HEADS = 2
SWA_GROUP = SWA_HEADS // SWA_KV_HEADS
SWA_W = SWA_HEADS * HEAD_DIM
SWA_KV_W = SWA_KV_HEADS * HEAD_DIM
CONV_CH = 256
CONV_GROUPS = 4
CONV_WIDTH = 31
WINDOW = 128
BLOCK = 128
RET_CHUNK = 128
ROPE_THETA = 500000.0
ROPE_DIM = HEAD_DIM // 4
RET_ROPE_THETA = 10000.0
N_EXPERTS = 8
TOP_K = 2
EPS = 1e-6
NEG_INF = -1e30

LANES = 128
CONV_HALO = 32
VMEM_LIMIT = 56 * 1024 * 1024

O_RQ, O_RK, O_RV, O_RG = 0, RET_W, 2 * RET_W, 3 * RET_W
O_SQ = 4 * RET_W
O_SK = O_SQ + SWA_W
O_SV = O_SK + SWA_KV_W
O_CA = O_SV + SWA_KV_W
O_CG = O_CA + CONV_CH
D_IN = O_CG + CONV_CH


def _params(*sem):
    return pltpu.CompilerParams(dimension_semantics=sem, vmem_limit_bytes=VMEM_LIMIT)


def _rms(x, g):
    return x * lax.rsqrt(jnp.mean(x * x, axis=-1, keepdims=True) + EPS) * g


def _silu(x):
    return x * jax.nn.sigmoid(x)


def _dot(a, b):
    return jnp.dot(a, b, preferred_element_type=F32)


def _dot_nt(a, b):
    return lax.dot_general(a, b, (((1,), (1,)), ((), ())), preferred_element_type=F32)


def _dot_tn(a, b):
    return lax.dot_general(a, b, (((0,), (0,)), ((), ())), preferred_element_type=F32)


def _rope_tables(seq, theta, rot_dim):
    half = rot_dim // 2
    inv = 1.0 / (theta ** (jnp.arange(half, dtype=F32) / half))
    ang = jnp.arange(seq, dtype=F32)[:, None] * inv[None, :]
    cos, sin = jnp.cos(ang), jnp.sin(ang)
    rest = HEAD_DIM - rot_dim
    c = jnp.concatenate([cos, cos, jnp.ones((seq, rest), F32)], axis=1)
    s = jnp.concatenate([-sin, sin, jnp.zeros((seq, rest), F32)], axis=1)
    reps = LANES // HEAD_DIM
    return jnp.tile(c, (1, reps)), jnp.tile(s, (1, reps))


def _inproj_body(x_ref, g_ref, w_ref, rc_ref, rs_ref, sc_ref, ss_ref,
                 rq_ref, rk_ref, rv_ref, rg_ref, sq_ref, sk_ref, sv_ref, ch_ref):
    h = _rms(x_ref[0], g_ref[...]).astype(BF16)
    lane = lax.broadcasted_iota(jnp.int32, (1, LANES), 1) % HEAD_DIM

    def seg(a, b):
        return _dot(h, w_ref[:, a:b])

    def rope(v, c, s, half):
        up = pltpu.roll(v, LANES - half, axis=1)
        dn = pltpu.roll(v, half, axis=1)
        return v * c + jnp.where(lane < half, up, dn) * s

    def rope_store(z, o_ref, c, s, half, scale):
        for i in range(z.shape[1] // LANES):
            sl = slice(i * LANES, (i + 1) * LANES)
            r = rope(z[:, sl], c, s, half)
            if scale != 1.0:
                r = r * scale
            o_ref[0, :, sl] = r.astype(o_ref.dtype)

    rc, rs = rc_ref[...], rs_ref[...]
    sc, ss = sc_ref[...], ss_ref[...]
    scale = HEAD_DIM ** -0.5
    rope_store(seg(O_RQ, O_RK), rq_ref, rc, rs, HEAD_DIM // 2, 1.0)
    rope_store(seg(O_RK, O_RV), rk_ref, rc, rs, HEAD_DIM // 2, scale)
    rv_ref[0] = seg(O_RV, O_RG).astype(BF16)
    rg_ref[0] = seg(O_RG, O_SQ)
    rope_store(seg(O_SQ, O_SK), sq_ref, sc, ss, ROPE_DIM // 2, scale)
    rope_store(seg(O_SK, O_SV), sk_ref, sc, ss, ROPE_DIM // 2, 1.0)
    sv_ref[0] = seg(O_SV, O_CA).astype(BF16)
    ch_ref[0] = seg(O_CA, O_CG) * jax.nn.sigmoid(seg(O_CG, D_IN))


def _inproj(x, g, w_bf, tm):
    B, S, D = x.shape
    tm = min(tm, S)
    rc, rs = _rope_tables(S, RET_ROPE_THETA, HEAD_DIM)
    sc, ss = _rope_tables(S, ROPE_THETA, ROPE_DIM)
    tab = pl.BlockSpec((tm, LANES), lambda s, b: (s, 0))

    def out(width, dtype):
        return (jax.ShapeDtypeStruct((B, S, width), dtype),
                pl.BlockSpec((1, tm, width), lambda s, b: (b, s, 0)))

    outs = [out(RET_W, BF16), out(RET_W, BF16), out(RET_W, BF16), out(RET_W, F32),
            out(SWA_W, BF16), out(SWA_KV_W, BF16), out(SWA_KV_W, BF16), out(CONV_CH, F32)]
    return pl.pallas_call(
        _inproj_body,
        grid=(S // tm, B),
        in_specs=[pl.BlockSpec((1, tm, D), lambda s, b: (b, s, 0)),
                  pl.BlockSpec((1, D), lambda s, b: (0, 0)),
                  pl.BlockSpec((D, D_IN), lambda s, b: (0, 0)),
                  tab, tab, tab, tab],
        out_specs=[o[1] for o in outs],
        out_shape=[o[0] for o in outs],
        compiler_params=_params("arbitrary", "arbitrary"),
        name="inproj",
    )(x, g.reshape(1, D), w_bf, rc, rs, sc, ss)


def _ret_body(q_ref, k_ref, v_ref, g_ref, dm_ref, kd_ref, qd_ref, cd_ref, gn_ref, o_ref, st_ref, *, ts):
    @pl.when(pl.program_id(1) == 0)
    def _():
        st_ref[...] = jnp.zeros_like(st_ref)

    C = RET_CHUNK
    for c in range(ts // C):
        rows = slice(c * C, (c + 1) * C)
        qa, ka, va, ga = q_ref[0, rows, :], k_ref[0, rows, :], v_ref[0, rows, :], g_ref[0, rows, :]
        outs = []
        for h in range(RET_HEADS):
            cols = slice(h * HEAD_DIM, (h + 1) * HEAD_DIM)
            q, k, v = qa[:, cols], ka[:, cols], va[:, cols]
            st = st_ref[h]
            scores = _dot_nt(q, k) * dm_ref[h]
            intra = _dot(scores.astype(BF16), v)
            cross = _dot(q, st.astype(BF16)) * qd_ref[h]
            kdec = (k.astype(F32) * kd_ref[h]).astype(BF16)
            st_ref[h] = st * cd_ref[h] + _dot_tn(kdec, v)
            o = intra + cross
            mu = jnp.mean(o, axis=-1, keepdims=True)
            d = o - mu
            var = jnp.mean(d * d, axis=-1, keepdims=True)
            outs.append(d * lax.rsqrt(var + EPS))
        on = jnp.concatenate(outs, axis=1) * gn_ref[...]
        o_ref[0, rows, :] = (_silu(ga) * on).astype(BF16)


def _retention(rq, rk, rv, rg, gn_g, ts):
    B, S, W = rq.shape
    ts = min(ts, S)
    C, H, D = RET_CHUNK, RET_HEADS, HEAD_DIM
    lg = jnp.log(1.0 - 2.0 ** (-5.0 - jnp.arange(H, dtype=F32)))
    idx = jnp.arange(C)
    rel = idx[:, None] - idx[None, :]
    dmask = jnp.where(rel[None] >= 0,
                      jnp.exp(jnp.maximum(rel, 0)[None].astype(F32) * lg[:, None, None]), 0.0)
    k_decay = jnp.exp((C - 1 - idx)[:, None].astype(F32) * lg[None, :])
    q_decay = jnp.exp((idx + 1)[:, None].astype(F32) * lg[None, :])
    chunk_decay = jnp.exp(C * lg)
    kd = jnp.broadcast_to(k_decay.T[:, :, None], (H, C, D))
    qd = jnp.broadcast_to(q_decay.T[:, :, None], (H, C, D))
    cd = jnp.broadcast_to(chunk_decay[:, None, None], (H, D, D))
    act = pl.BlockSpec((1, ts, W), lambda b, s: (b, s, 0))

    def const(shape):
        return pl.BlockSpec(shape, lambda b, s: (0,) * len(shape))

    return pl.pallas_call(
        functools.partial(_ret_body, ts=ts),
        grid=(B, S // ts),
        in_specs=[act, act, act, act, const((H, C, C)), const((H, C, D)), const((H, C, D)),
                  const((H, D, D)), const((1, W))],
        out_specs=act,
        out_shape=jax.ShapeDtypeStruct((B, S, W), BF16),
        scratch_shapes=[pltpu.VMEM((H, D, D), F32)],
        compiler_params=_params("arbitrary", "arbitrary"),
        name="retention",
    )(rq, rk, rv, rg, dmask, kd, qd, cd, gn_g.reshape(1, W))


def _swa_body(sink_ref, q_ref, kc_ref, kp_ref, vc_ref, vp_ref, mask_ref, o_ref, *, tq):
    n = pl.program_id(1)
    kcat = jnp.concatenate([kp_ref[0], kc_ref[0]], axis=0)
    vcat = jnp.concatenate([vp_ref[0], vc_ref[0]], axis=0)
    for j in range(tq // BLOCK):
        qb = q_ref[0, j * BLOCK:(j + 1) * BLOCK, :]
        kb = kcat[j * BLOCK:(j + 2) * BLOCK]
        vb = vcat[j * BLOCK:(j + 2) * BLOCK]
        if j == 0:
            mask = jnp.where(n == 0, mask_ref[1], mask_ref[0])
        else:
            mask = mask_ref[0]
        keep = mask > 0.0
        outs = []
        for hq in range(SWA_HEADS):
            hk = hq // SWA_GROUP
            kh = kb[:, hk * HEAD_DIM:(hk + 1) * HEAD_DIM]
            vh = vb[:, hk * HEAD_DIM:(hk + 1) * HEAD_DIM]
            qh = qb[:, hq * HEAD_DIM:(hq + 1) * HEAD_DIM]
            s = jnp.where(keep, _dot_nt(qh, kh), NEG_INF)
            sink = sink_ref[hq]
            m = jnp.maximum(jnp.max(s, axis=-1, keepdims=True), sink)
            p = jnp.exp(s - m)
            den = jnp.sum(p, axis=-1, keepdims=True) + jnp.exp(sink - m)
            outs.append(_dot(p.astype(BF16), vh) / den)
        o_ref[0, j * BLOCK:(j + 1) * BLOCK, :] = jnp.concatenate(outs, axis=1).astype(BF16)


def _swa(sq, sk, sv, sinks, tq):
    B, S, W = sq.shape
    tq = min(tq, S)
    r = tq // BLOCK
    qi = jnp.arange(BLOCK)[:, None] + BLOCK
    kj = jnp.arange(2 * BLOCK)[None, :]
    rel = qi - kj
    allowed = (rel >= 0) & (rel < WINDOW)
    mask = jnp.stack([allowed, allowed & (kj >= BLOCK)]).astype(F32)
    cur = pl.BlockSpec((1, tq, SWA_KV_W), lambda b, n: (b, n, 0))
    prev = pl.BlockSpec((1, BLOCK, SWA_KV_W), lambda b, n: (b, jnp.maximum(n * r - 1, 0), 0))
    return pl.pallas_call(
        functools.partial(_swa_body, tq=tq),
        grid=(B, S // tq),
        in_specs=[pl.BlockSpec(memory_space=pltpu.SMEM),
                  pl.BlockSpec((1, tq, W), lambda b, n: (b, n, 0)),
                  cur, prev, cur, prev,
                  pl.BlockSpec((2, BLOCK, 2 * BLOCK), lambda b, n: (0, 0, 0))],
        out_specs=pl.BlockSpec((1, tq, W), lambda b, n: (b, n, 0)),
        out_shape=jax.ShapeDtypeStruct((B, S, W), BF16),
        compiler_params=_params("arbitrary", "arbitrary"),
        name="swa",
    )(sinks.astype(F32), sq, sk, sk, sv, sv, mask)


def _conv_body(h_ref, dw_ref, db_ref, lg_ref, lb_ref, pw_ref, o_ref, hbuf, hsh, *, ts, rb):
    s = pl.program_id(1)
    sub = 8
    span = ts + CONV_HALO - sub

    @pl.when(s == 0)
    def _():
        hbuf[0:CONV_HALO, :] = jnp.zeros((CONV_HALO, CONV_CH), F32)

    @pl.when(s > 0)
    def _():
        hbuf[0:CONV_HALO, :] = hbuf[ts:ts + CONV_HALO, :]

    hbuf[CONV_HALO:CONV_HALO + ts, :] = h_ref[0]
    for p in range(1, sub):
        hsh[p - 1] = hbuf[p:p + span, :]
    off = CONV_HALO - (CONV_WIDTH - 1)
    for r in range(ts // rb):
        base = r * rb
        acc = jnp.broadcast_to(db_ref[...], (rb, CONV_CH))
        for w in range(CONV_WIDTH):
            q, p = divmod(off + w, sub)
            lo = base + q * sub
            tap = hbuf[lo:lo + rb, :] if p == 0 else hsh[p - 1, lo:lo + rb, :]
            acc = acc + tap * dw_ref[w:w + 1, :]
        mu = jnp.mean(acc, axis=-1, keepdims=True)
        d = acc - mu
        var = jnp.mean(d * d, axis=-1, keepdims=True)
        hn = d * lax.rsqrt(var + EPS) * lg_ref[...] + lb_ref[...]
        o_ref[0, base:base + rb, :] = _dot(_silu(hn).astype(BF16), pw_ref[...]).astype(BF16)


def _conv(ch, dw_w, dw_b, ln_g, ln_b, pw_w, ts, rb):
    B, S, W = ch.shape
    ts = min(ts, S)
    rb = min(rb, ts)
    pw = jnp.zeros((W, W), F32)
    gd = W // CONV_GROUPS
    for g in range(CONV_GROUPS):
        pw = pw.at[g * gd:(g + 1) * gd, g * gd:(g + 1) * gd].set(pw_w[g])
    act = pl.BlockSpec((1, ts, W), lambda b, s: (b, s, 0))

    def const(shape):
        return pl.BlockSpec(shape, lambda b, s: (0,) * len(shape))

    return pl.pallas_call(
        functools.partial(_conv_body, ts=ts, rb=rb),
        grid=(B, S // ts),
        in_specs=[act, const((CONV_WIDTH, W)), const((1, W)), const((1, W)), const((1, W)), const((W, W))],
        out_specs=act,
        out_shape=jax.ShapeDtypeStruct((B, S, W), BF16),
        scratch_shapes=[pltpu.VMEM((CONV_HALO + ts, W), F32), pltpu.VMEM((7, CONV_HALO + ts - 8, W), F32)],
        compiler_params=_params("arbitrary", "arbitrary"),
        name="conv",
    )(ch, dw_w, dw_b.reshape(1, W), ln_g.reshape(1, W), ln_b.reshape(1, W), pw.astype(BF16))


def _outproj_body(yr_ref, ys_ref, yc_ref, w_ref, x_ref, g_ref, *rest, route):
    if route:
        wr_ref, xo_ref, hn_ref, rt_ref, rl_ref = rest
    else:
        xo_ref, hn_ref = rest
    a, b = RET_W, RET_W + SWA_W
    x = x_ref[...] + (_dot(yr_ref[...], w_ref[0:a, :]) + _dot(ys_ref[...], w_ref[a:b, :])
                      + _dot(yc_ref[...], w_ref[b:, :]))
    xo_ref[...] = x
    h = _rms(x, g_ref[...]).astype(BF16)
    hn_ref[...] = h
    if not route:
        return
    logits = _dot(h, wr_ref[...])
    lane = lax.broadcasted_iota(jnp.int32, logits.shape, 1)
    lg = jnp.where(lane < N_EXPERTS, logits, -jnp.inf)
    m1 = jnp.max(lg, axis=-1, keepdims=True)
    i1 = jnp.min(jnp.where(lg == m1, lane, LANES), axis=-1, keepdims=True)
    lg2 = jnp.where(lane == i1, -jnp.inf, lg)
    m2 = jnp.max(lg2, axis=-1, keepdims=True)
    i2 = jnp.min(jnp.where(lg2 == m2, lane, LANES), axis=-1, keepdims=True)
    e = jnp.exp(m2 - m1)
    w1 = 1.0 / (1.0 + e)
    w2 = e / (1.0 + e)
    tm = logits.shape[0]
    hit1, hit2 = lane == i1, lane == i2
    onehot = jnp.where(jnp.logical_or(hit1, hit2), 1.0, 0.0)
    before = (lax.broadcasted_iota(jnp.int32, (tm, tm), 0) > lax.broadcasted_iota(jnp.int32, (tm, tm), 1))
    rank = _dot(jnp.where(before, 1.0, 0.0).astype(BF16), onehot.astype(BF16))
    count = jnp.sum(onehot, axis=0, keepdims=True).astype(jnp.int32)
    run_len = (count + (RUN_ALIGN - 1)) & ~(RUN_ALIGN - 1)
    lower = (lax.broadcasted_iota(jnp.int32, (LANES, LANES), 0) < lax.broadcasted_iota(jnp.int32, (LANES, LANES), 1))
    run_rows = jnp.broadcast_to(run_len.astype(F32), (8, LANES)).astype(BF16)
    loc_off = _dot(run_rows, jnp.where(lower, 1.0, 0.0).astype(BF16))[0:1, :]
    pos = rank + loc_off
    p1 = jnp.sum(jnp.where(hit1, pos, 0.0), axis=-1, keepdims=True)
    p2 = jnp.sum(jnp.where(hit2, pos, 0.0), axis=-1, keepdims=True)
    cols = (i1.astype(F32), i2.astype(F32), w1, w2, p1, p2)
    rt = jnp.zeros_like(logits)
    for c, v in enumerate(cols):
        rt = jnp.where(lane == c, v, rt)
    rt_ref[...] = rt
    rl_ref[0] = jnp.broadcast_to(run_len, (8, LANES))


def _outproj(yr, ys, yc, w_bf, x, g, w_router, tm):
    T, D = x.shape
    tm = min(tm, T)
    route = w_router is not None

    def row(width):
        return pl.BlockSpec((tm, width), lambda i: (i, 0))

    def const(shape):
        return pl.BlockSpec(shape, lambda i: (0,) * len(shape))

    in_specs = [row(RET_W), row(SWA_W), row(CONV_CH), const((D, D)), row(D), const((1, D))]
    args = [yr, ys, yc, w_bf, x, g.reshape(1, D)]
    out_shape = [jax.ShapeDtypeStruct((T, D), F32), jax.ShapeDtypeStruct((T, D), BF16)]
    out_specs = [row(D), row(D)]
    if route:
        wr = jnp.zeros((D, LANES), F32).at[:, :N_EXPERTS].set(w_router).astype(BF16)
        in_specs.append(const((D, LANES)))
        args.append(wr)
        assert tm == MOE_TB, "the router kernel sorts one dispatch block per tile"
        out_shape += [jax.ShapeDtypeStruct((T, LANES), F32), jax.ShapeDtypeStruct((T // tm, 8, LANES), jnp.int32)]
        out_specs += [row(LANES), pl.BlockSpec((1, 8, LANES), lambda i: (i, 0, 0))]
    return pl.pallas_call(
        functools.partial(_outproj_body, route=route),
        grid=(T // tm,),
        in_specs=in_specs, out_specs=out_specs, out_shape=out_shape,
        compiler_params=_params("arbitrary"),
        name="outproj_route" if route else "outproj",
    )(*args)


def _dense_ffn_body(yr_ref, ys_ref, yc_ref, wo_ref, x_ref, g_ref, wg_ref, wu_ref, wd_ref, o_ref, *, chunks):
    a, b = RET_W, RET_W + SWA_W
    x = x_ref[...] + (_dot(yr_ref[...], wo_ref[0:a, :]) + _dot(ys_ref[...], wo_ref[a:b, :])
                      + _dot(yc_ref[...], wo_ref[b:, :]))
    h = _rms(x, g_ref[...]).astype(BF16)
    acc = x
    for a, b in chunks:
        g = _dot(h, wg_ref[:, a:b])
        u = _dot(h, wu_ref[:, a:b])
        acc = acc + _dot((_silu(g) * u).astype(BF16), wd_ref[a:b, :])
    o_ref[...] = acc


def _outproj_dense_ffn(yr, ys, yc, wo_bf, x, g, wg, wu, wd, tm, fc):
    T, D = x.shape
    FF = wg.shape[1]
    tm = min(tm, T)
    chunks = tuple((a, min(a + fc, FF)) for a in range(0, FF, fc))

    def row(width):
        return pl.BlockSpec((tm, width), lambda i: (i, 0))

    def const(shape):
        return pl.BlockSpec(shape, lambda i: (0,) * len(shape), pipeline_mode=pl.Buffered(1))

    return pl.pallas_call(
        functools.partial(_dense_ffn_body, chunks=chunks),
        grid=(T // tm,),
        in_specs=[row(RET_W), row(SWA_W), row(CONV_CH), const((D, D)), row(D), const((1, D)),
                  const((D, FF)), const((D, FF)), const((FF, D))],
        out_specs=row(D),
        out_shape=jax.ShapeDtypeStruct((T, D), F32),
        compiler_params=_params("arbitrary"),
        name="outproj_dense_ffn",
    )(yr, ys, yc, wo_bf, x, g.reshape(1, D), wg.astype(BF16), wu.astype(BF16), wd.astype(BF16))


MOE_TB = 512
MOE_TM = 512
RUN_ALIGN = 8
RUN_SIZES = (512, 256, 128, 64, 32, 16, 8)
BLOCK_ROWS = TOP_K * MOE_TB + N_EXPERTS * RUN_ALIGN
BLOCK_LANES = 1152
MOE_FC = 512


def _route_meta(rt, rl, n_tiles):
    T = rt.shape[0]
    nb = T // MOE_TB
    pos = rt[:, 4:4 + TOP_K].astype(jnp.int32)
    run_len = rl[:, 0, :N_EXPERTS]
    loc_off = jnp.cumsum(run_len, axis=1) - run_len
    group = jnp.sum(run_len, axis=0)
    ptiles = (group + MOE_TM - 1) // MOE_TM
    tile_end = jnp.cumsum(ptiles)
    gstart = (tile_end - ptiles) * MOE_TM
    run_start = gstart[None, :] + jnp.cumsum(run_len, axis=0) - run_len
    tile_expert = jnp.sum(jnp.arange(n_tiles, dtype=jnp.int32)[:, None] >= tile_end[None, :], axis=1)
    i32 = lambda a: a.astype(jnp.int32)
    pos3 = pos.reshape(nb, MOE_TB, TOP_K)
    return dict(
        pos_l=i32(jnp.swapaxes(pos3, 1, 2)),
        pos_c=i32(pos3.reshape(T, TOP_K)),
        loc_off=i32(loc_off.reshape(-1)), run_start=i32(run_start.reshape(-1)), run_len=i32(run_len.reshape(-1)),
        pad_start=i32(gstart + group), pad_len=i32(ptiles * MOE_TM - group),
        tile_expert=i32(jnp.minimum(tile_expert, N_EXPERTS - 1)), n_used=i32(tile_end[-1:]))


def _run_dma(src, dst, src_off, dst_off, length, sem, wait):
    off = 0
    for k in RUN_SIZES:
        part = length & k

        @pl.when(part != 0)
        def _(off=off, k=k):
            cp = pltpu.make_async_copy(src.at[pl.ds(pl.multiple_of(src_off + off, RUN_ALIGN), k)],
                                       dst.at[pl.ds(pl.multiple_of(dst_off + off, RUN_ALIGN), k)], sem)
            if wait:
                cp.wait()
            else:
                cp.start()

        off = off + part


def _dispatch_body(lo_ref, rs_ref, rl_ref, ps_ref, pn_ref, nu_ref, pos_ref, h_ref, xs_hbm, sbuf, zbuf, sem):
    b, nb = pl.program_id(0), pl.num_programs(0)
    slot = b % 2
    n_tiles = xs_hbm.shape[0] // MOE_TM

    def zero_tile(i, wait):
        cp = pltpu.make_async_copy(zbuf, xs_hbm.at[pl.ds(pl.multiple_of(i * MOE_TM, MOE_TM), MOE_TM)], sem.at[2])
        if wait:
            cp.wait()
        else:
            cp.start()

    def runs(blk, s, wait):
        for e in range(N_EXPERTS):
            j = blk * N_EXPERTS + e
            _run_dma(sbuf.at[s], xs_hbm, lo_ref[j], rs_ref[j], rl_ref[j], sem.at[s], wait)

    @pl.when(b >= 2)
    def _():
        runs(b - 2, slot, True)

    pos = pos_ref[0]
    r = lax.broadcasted_iota(jnp.int32, (BLOCK_ROWS, MOE_TB), 0)
    onehot = jnp.logical_or(r == pos[0:1, :], r == pos[1:2, :])
    sbuf[slot] = _dot(jnp.where(onehot, 1.0, 0.0).astype(BF16), h_ref[...])
    runs(b, slot, False)

    @pl.when(b == nb - 1)
    def _():
        zbuf[...] = jnp.zeros_like(zbuf)
        for e in range(N_EXPERTS):
            _run_dma(zbuf, xs_hbm, 0, ps_ref[e], pn_ref[e], sem.at[2], False)
        lax.fori_loop(nu_ref[0], n_tiles, lambda i, c: zero_tile(i, False), None)
        for e in range(N_EXPERTS):
            _run_dma(zbuf, xs_hbm, 0, ps_ref[e], pn_ref[e], sem.at[2], True)
        lax.fori_loop(nu_ref[0], n_tiles, lambda i, c: zero_tile(i, True), None)

        @pl.when(b >= 1)
        def _():
            runs(b - 1, 1 - slot, True)

        runs(b, slot, True)


def _dispatch(hn, meta, n_rows):
    T, D = hn.shape
    nb = T // MOE_TB
    grid_spec = pltpu.PrefetchScalarGridSpec(
        num_scalar_prefetch=6,
        grid=(nb,),
        in_specs=[pl.BlockSpec((1, TOP_K, MOE_TB), lambda b, *_: (b, 0, 0)),
                  pl.BlockSpec((MOE_TB, D), lambda b, *_: (b, 0))],
        out_specs=pl.BlockSpec(memory_space=pl.ANY),
        scratch_shapes=[pltpu.VMEM((2, BLOCK_ROWS, D), F32), pltpu.VMEM((MOE_TM, D), F32),
                        pltpu.SemaphoreType.DMA((3,))],
    )
    return pl.pallas_call(
        _dispatch_body,
        grid_spec=grid_spec,
        out_shape=jax.ShapeDtypeStruct((n_rows, D), F32),
        compiler_params=pltpu.CompilerParams(dimension_semantics=("arbitrary",), vmem_limit_bytes=VMEM_LIMIT,
                                             has_side_effects=True),
        name="moe_dispatch",
    )(meta["loc_off"], meta["run_start"], meta["run_len"], meta["pad_start"], meta["pad_len"], meta["n_used"],
      meta["pos_l"], hn)


def _moe_ffn_body(te_ref, nu_ref, x_ref, wg_hbm, wu_hbm, wd_hbm, o_ref, wg_s, wu_s, wd_s, st_a, st_d, sem):
    i = pl.program_id(0)
    e = te_ref[i]
    active = i < nu_ref[0]
    nf = wg_s.shape[1] // MOE_FC
    fsl = [slice(f * MOE_FC, (f + 1) * MOE_FC) for f in range(nf)]

    @pl.when(jnp.logical_and(active, jnp.logical_or(i == 0, e != te_ref[jnp.maximum(i - 1, 0)])))
    def _():
        jobs = ([(wg_hbm, wg_s, True, f) for f in range(nf)] + [(wu_hbm, wu_s, True, f) for f in range(nf)]
                + [(wd_hbm, wd_s, False, f) for f in range(nf)])

        def copy(j):
            w_hbm, _, cols, f = jobs[j]
            if cols:
                return pltpu.make_async_copy(w_hbm.at[e, :, fsl[f]], st_a.at[j % 2], sem.at[j % 2])
            return pltpu.make_async_copy(w_hbm.at[e, fsl[f], :], st_d.at[j % 2], sem.at[j % 2])

        copy(0).start()
        for j, (_, w_s, cols, f) in enumerate(jobs):
            if j + 1 < len(jobs):
                copy(j + 1).start()
            copy(j).wait()
            if cols:
                w_s[:, fsl[f]] = st_a[j % 2].astype(BF16)
            else:
                w_s[fsl[f], :] = st_d[j % 2].astype(BF16)

    @pl.when(active)
    def _():
        x = x_ref[...].astype(BF16)
        acc = jnp.zeros(o_ref.shape, F32)
        for sl in fsl:
            g = _dot(x, wg_s[:, sl])
            u = _dot(x, wu_s[:, sl])
            acc = acc + _dot((_silu(g) * u).astype(BF16), wd_s[sl, :])
        o_ref[...] = acc

    @pl.when(jnp.logical_not(active))
    def _():
        o_ref[...] = jnp.zeros_like(o_ref)


def _moe_ffn(xs, meta, wg, wu, wd):
    R, D = xs.shape
    E, _, FF = wg.shape
    grid_spec = pltpu.PrefetchScalarGridSpec(
        num_scalar_prefetch=2,
        grid=(R // MOE_TM,),
        in_specs=[pl.BlockSpec((MOE_TM, D), lambda i, te, nu: (jnp.minimum(i, nu[0] - 1), 0)),
                  pl.BlockSpec(memory_space=pl.ANY), pl.BlockSpec(memory_space=pl.ANY),
                  pl.BlockSpec(memory_space=pl.ANY)],
        out_specs=pl.BlockSpec((MOE_TM, D), lambda i, te, nu: (i, 0)),
        scratch_shapes=[pltpu.VMEM((D, FF), BF16), pltpu.VMEM((D, FF), BF16), pltpu.VMEM((FF, D), BF16),
                        pltpu.VMEM((2, D, MOE_FC), F32), pltpu.VMEM((2, MOE_FC, D), F32),
                        pltpu.SemaphoreType.DMA((2,))],
    )
    return pl.pallas_call(
        _moe_ffn_body,
        grid_spec=grid_spec,
        out_shape=jax.ShapeDtypeStruct((R, D), F32),
        compiler_params=_params("arbitrary"),
        name="moe_ffn",
    )(meta["tile_expert"], meta["n_used"], xs, wg, wu, wd)


def _combine_body(lo_ref, rs_ref, rl_ref, ys_hbm, x_ref, rt_ref, pc_ref, g_ref, o_ref, ybuf, sem):
    b, nb = pl.program_id(0), pl.num_programs(0)
    slot = b % 2

    def runs(blk, s, wait):
        for e in range(N_EXPERTS):
            j = blk * N_EXPERTS + e
            _run_dma(ys_hbm, ybuf.at[s], rs_ref[j], lo_ref[j], rl_ref[j], sem.at[s], wait)

    @pl.when(b == 0)
    def _():
        ybuf[...] = jnp.zeros_like(ybuf)
        runs(0, 0, False)

    @pl.when(b + 1 < nb)
    def _():
        runs(b + 1, 1 - slot, False)

    runs(b, slot, True)
    y = ybuf[slot].astype(BF16)
    pc, rt = pc_ref[...], rt_ref[...]
    lane = lax.broadcasted_iota(jnp.int32, (MOE_TB, BLOCK_LANES), 1)
    q0 = jnp.where(lane == pc[:, 0:1], 1.0, 0.0).astype(BF16)
    q1 = jnp.where(lane == pc[:, 1:2], 1.0, 0.0).astype(BF16)
    moe = rt[:, 2:3] * _dot(q0, y) + rt[:, 3:4] * _dot(q1, y)
    o_ref[...] = _rms(x_ref[...] + moe, g_ref[...])


def _combine(ys, meta, x, rt, g):
    T, D = x.shape
    grid_spec = pltpu.PrefetchScalarGridSpec(
        num_scalar_prefetch=3,
        grid=(T // MOE_TB,),
        in_specs=[pl.BlockSpec(memory_space=pl.ANY),
                  pl.BlockSpec((MOE_TB, D), lambda b, *_: (b, 0)),
                  pl.BlockSpec((MOE_TB, LANES), lambda b, *_: (b, 0)),
                  pl.BlockSpec((MOE_TB, TOP_K), lambda b, *_: (b, 0)),
                  pl.BlockSpec((1, D), lambda b, *_: (0, 0))],
        out_specs=pl.BlockSpec((MOE_TB, D), lambda b, *_: (b, 0)),
        scratch_shapes=[pltpu.VMEM((2, BLOCK_LANES, D), F32), pltpu.SemaphoreType.DMA((2,))],
    )
    return pl.pallas_call(
        _combine_body,
        grid_spec=grid_spec,
        out_shape=jax.ShapeDtypeStruct((T, D), F32),
        compiler_params=_params("arbitrary"),
        name="moe_combine",
    )(meta["loc_off"], meta["run_start"], meta["run_len"], ys, x, rt, meta["pos_c"], g.reshape(1, D))


def kernel(x, norm_mix_g, w_in, ret_gn_g, attn_sinks, conv_dw_w, conv_dw_b, conv_ln_g, conv_ln_b,
           conv_pw_w, w_out, norm_ffn_g, ffn_w_gate, ffn_w_up, ffn_w_down, moe_router, moe_w_gate,
           moe_w_up, moe_w_down, final_norm_g):
    B, S, D = x.shape
    T = B * S
    depth = w_in.shape[0]
    assert depth == 2 and ffn_w_gate.shape[0] == 1 and moe_router.shape[0] == 1, "dense layer then MoE layer"
    assert T % MOE_TB == 0 and moe_w_gate.shape[-1] % MOE_FC == 0

    for l in range(depth):
        rq, rk, rv, rg, sq, sk, sv, ch = _inproj(x, norm_mix_g[l], w_in[l].astype(BF16), tm=512)
        y_ret = _retention(rq, rk, rv, rg, ret_gn_g[l], ts=512)
        y_swa = _swa(sq, sk, sv, attn_sinks[l], tq=512)
        y_conv = _conv(ch, conv_dw_w[l], conv_dw_b[l], conv_ln_g[l], conv_ln_b[l], conv_pw_w[l], ts=512, rb=64)
        flat = lambda a: a.reshape(T, a.shape[-1])
        j = l // 2
        if l % 2 == 0:
            x = _outproj_dense_ffn(flat(y_ret), flat(y_swa), flat(y_conv), w_out[l].astype(BF16), flat(x),
                                   norm_ffn_g[l], ffn_w_gate[j], ffn_w_up[j], ffn_w_down[j],
                                   tm=512, fc=512).reshape(B, S, D)
        else:
            x2, hn, rt, rl = _outproj(flat(y_ret), flat(y_swa), flat(y_conv), w_out[l].astype(BF16), flat(x),
                                      norm_ffn_g[l], moe_router[j], tm=MOE_TB)
            n_rows = TOP_K * T + (T // MOE_TB) * N_EXPERTS * RUN_ALIGN + N_EXPERTS * MOE_TM
            meta = _route_meta(rt, rl, n_rows // MOE_TM)
            xs = _dispatch(hn, meta, n_rows)
            ys = _moe_ffn(xs, meta, moe_w_gate[j], moe_w_up[j], moe_w_down[j])
            x = _combine(ys, meta, x2, rt, final_norm_g).reshape(B, S, D)
    return x
```

```python
import functools

import jax
import jax.numpy as jnp
from jax import lax
from jax.experimental import pallas as pl
from jax.experimental.pallas import tpu as pltpu

F32 = jnp.float32
BF16 = jnp.bfloat16

HEAD_DIM = 64
RET_HEADS = 4
RET_W = RET_HEADS * HEAD_DIM
SWA_HEADS = 8
SWA_KV_HEADS = 2
SWA_GROUP = SWA_HEADS // SWA_KV_HEADS
SWA_W = SWA_HEADS * HEAD_DIM
SWA_KV_W = SWA_KV_HEADS * HEAD_DIM
CONV_CH = 256
CONV_GROUPS = 4
CONV_WIDTH = 31
WINDOW = 128
BLOCK = 128
RET_CHUNK = 128
ROPE_THETA = 500000.0
ROPE_DIM = HEAD_DIM // 4
RET_ROPE_THETA = 10000.0
N_EXPERTS = 8
TOP_K = 2
EPS = 1e-6
NEG_INF = -1e30

LANES = 128
CONV_HALO = 32
VMEM_LIMIT = 56 * 1024 * 1024

O_RQ, O_RK, O_RV, O_RG = 0, RET_W, 2 * RET_W, 3 * RET_W
O_SQ = 4 * RET_W
O_SK = O_SQ + SWA_W
O_SV = O_SK + SWA_KV_W
O_CA = O_SV + SWA_KV_W
O_CG = O_CA + CONV_CH
D_IN = O_CG + CONV_CH


def _params(*sem):
    return pltpu.CompilerParams(dimension_semantics=sem, vmem_limit_bytes=VMEM_LIMIT)


def _rms(x, g):
    return x * lax.rsqrt(jnp.mean(x * x, axis=-1, keepdims=True) + EPS) * g


def _silu(x):
    return x * jax.nn.sigmoid(x)


def _dot(a, b):
    return jnp.dot(a, b, preferred_element_type=F32)


def _dot_nt(a, b):
    return lax.dot_general(a, b, (((1,), (1,)), ((), ())), preferred_element_type=F32)


def _dot_tn(a, b):
    return lax.dot_general(a, b, (((0,), (0,)), ((), ())), preferred_element_type=F32)


def _rope_tables(seq, theta, rot_dim):
    half = rot_dim // 2
    inv = 1.0 / (theta ** (jnp.arange(half, dtype=F32) / half))
    ang = jnp.arange(seq, dtype=F32)[:, None] * inv[None, :]
    cos, sin = jnp.cos(ang), jnp.sin(ang)
    rest = HEAD_DIM - rot_dim
    c = jnp.concatenate([cos, cos, jnp.ones((seq, rest), F32)], axis=1)
    s = jnp.concatenate([-sin, sin, jnp.zeros((seq, rest), F32)], axis=1)
    reps = LANES // HEAD_DIM
    return jnp.tile(c, (1, reps)), jnp.tile(s, (1, reps))


def _inproj_body(x_ref, g_ref, w_ref, wgt_ref, rc_ref, rs_ref, sc_ref, ss_ref,
                 rq_ref, rk_ref, rv_ref, rg_ref, sq_ref, sk_ref, sv_ref, ch_ref):
    h = _rms(x_ref[0], g_ref[...]).astype(BF16)
    lane = lax.broadcasted_iota(jnp.int32, (1, LANES), 1) % HEAD_DIM

    def seg(a, b):
        return _dot(h, w_ref[:, a:b])

    def rope(v, c, s, half):
        up = pltpu.roll(v, LANES - half, axis=1)
        dn = pltpu.roll(v, half, axis=1)
        return v * c + jnp.where(lane < half, up, dn) * s

    def store_heads(z, o_ref, first=0):
        for j in range(z.shape[1] // HEAD_DIM):
            o_ref[0, first + j] = z[:, j * HEAD_DIM:(j + 1) * HEAD_DIM]

    def rope_store(z, o_ref, c, s, half, scale):
        for i in range(z.shape[1] // LANES):
            r = rope(z[:, i * LANES:(i + 1) * LANES], c, s, half)
            if scale != 1.0:
                r = r * scale
            store_heads(r.astype(BF16), o_ref, first=i * (LANES // HEAD_DIM))

    rc, rs = rc_ref[...], rs_ref[...]
    sc, ss = sc_ref[...], ss_ref[...]
    scale = HEAD_DIM ** -0.5
    rope_store(seg(O_RQ, O_RK), rq_ref, rc, rs, HEAD_DIM // 2, 1.0)
    rope_store(seg(O_RK, O_RV), rk_ref, rc, rs, HEAD_DIM // 2, scale)
    store_heads(seg(O_RV, O_RG).astype(BF16), rv_ref)
    rg_ref[0] = _dot_nt(wgt_ref[...], h)
    rope_store(seg(O_SQ, O_SK), sq_ref, sc, ss, ROPE_DIM // 2, scale)
    rope_store(seg(O_SK, O_SV), sk_ref, sc, ss, ROPE_DIM // 2, 1.0)
    store_heads(seg(O_SV, O_CA).astype(BF16), sv_ref)
    ch_ref[0] = seg(O_CA, O_CG) * jax.nn.sigmoid(seg(O_CG, D_IN))


def _inproj(x, g, w_bf, tm):
    B, S, D = x.shape
    tm = min(tm, S)
    rc, rs = _rope_tables(S, RET_ROPE_THETA, HEAD_DIM)
    sc, ss = _rope_tables(S, ROPE_THETA, ROPE_DIM)
    tab = pl.BlockSpec((tm, LANES), lambda s, b: (s, 0))

    def out(width, dtype):
        return (jax.ShapeDtypeStruct((B, S, width), dtype),
                pl.BlockSpec((1, tm, width), lambda s, b: (b, s, 0)))

    def out_heads(heads):
        return (jax.ShapeDtypeStruct((B, heads, S, HEAD_DIM), BF16),
                pl.BlockSpec((1, heads, tm, HEAD_DIM), lambda s, b: (b, 0, s, 0)))

    gate_t = (jax.ShapeDtypeStruct((B, RET_W, S), F32), pl.BlockSpec((1, RET_W, tm), lambda s, b: (b, 0, s)))
    outs = [out_heads(RET_HEADS), out_heads(RET_HEADS), out_heads(RET_HEADS), gate_t,
            out_heads(SWA_HEADS), out_heads(SWA_KV_HEADS), out_heads(SWA_KV_HEADS), out(CONV_CH, F32)]
    return pl.pallas_call(
        _inproj_body,
        grid=(S // tm, B),
        in_specs=[pl.BlockSpec((1, tm, D), lambda s, b: (b, s, 0)),
                  pl.BlockSpec((1, D), lambda s, b: (0, 0)),
                  pl.BlockSpec((D, D_IN), lambda s, b: (0, 0)),
                  pl.BlockSpec((RET_W, D), lambda s, b: (0, 0)),
                  tab, tab, tab, tab],
        out_specs=[o[1] for o in outs],
        out_shape=[o[0] for o in outs],
        compiler_params=_params("arbitrary", "arbitrary"),
        name="inproj",
    )(x, g.reshape(1, D), w_bf, w_bf[:, O_RG:O_SQ].T, rc, rs, sc, ss)


def _ret_body(q_ref, k_ref, v_ref, g_ref, dm_ref, kd_ref, qd_ref, cd_ref, gn_ref, o_ref, st_ref, *, ts):
    @pl.when(pl.program_id(2) == 0)
    def _():
        st_ref[...] = jnp.zeros_like(st_ref)

    C = RET_CHUNK
    for c in range(ts // C):
        rows = slice(c * C, (c + 1) * C)
        q, k, v = q_ref[0, 0, rows, :], k_ref[0, 0, rows, :], v_ref[0, 0, rows, :]
        st = st_ref[...]
        scores = _dot_nt(k, q) * dm_ref[0]
        intra = _dot_tn(v, scores.astype(BF16))
        cross = _dot_nt(st.astype(BF16), q) * qd_ref[0]
        kdec = (k.astype(F32) * kd_ref[0]).astype(BF16)
        st_ref[...] = st * cd_ref[0] + _dot_tn(v, kdec)
        o = intra + cross
        mu = jnp.mean(o, axis=0, keepdims=True)
        d = o - mu
        var = jnp.mean(d * d, axis=0, keepdims=True)
        on = d * lax.rsqrt(var + EPS) * gn_ref[0]
        o_ref[0, :, rows] = (_silu(g_ref[0, :, rows]) * on).astype(BF16)


def _retention(rq, rk, rv, rg_t, gn_g, ts):
    B, H, S, D = rq.shape
    ts = min(ts, S)
    C, W = RET_CHUNK, RET_W
    lg = jnp.log(1.0 - 2.0 ** (-5.0 - jnp.arange(H, dtype=F32)))
    idx = jnp.arange(C)
    rel = idx[:, None] - idx[None, :]
    dmask = jnp.where(rel[None] >= 0,
                      jnp.exp(jnp.maximum(rel, 0)[None].astype(F32) * lg[:, None, None]), 0.0)
    k_decay = jnp.exp((C - 1 - idx)[:, None].astype(F32) * lg[None, :])
    q_decay = jnp.exp((idx + 1)[:, None].astype(F32) * lg[None, :])
    chunk_decay = jnp.exp(C * lg)
    dm_t = jnp.swapaxes(dmask, 1, 2)
    kd = jnp.broadcast_to(k_decay.T[:, :, None], (H, C, D))
    qd = jnp.broadcast_to(q_decay.T[:, None, :], (H, D, C))
    cd = jnp.broadcast_to(chunk_decay[:, None, None], (H, D, D))
    gn = jnp.broadcast_to(gn_g.reshape(H, D, 1), (H, D, C))
    heads = pl.BlockSpec((1, 1, ts, D), lambda b, h, n: (b, h, n, 0))
    chan = pl.BlockSpec((1, D, ts), lambda b, h, n: (b, h, n))

    def per_head(r, c):
        return pl.BlockSpec((1, r, c), lambda b, h, n: (h, 0, 0))

    return pl.pallas_call(
        functools.partial(_ret_body, ts=ts),
        grid=(B, H, S // ts),
        in_specs=[heads, heads, heads, chan, per_head(C, C), per_head(C, D), per_head(D, C),
                  per_head(D, D), per_head(D, C)],
        out_specs=chan,
        out_shape=jax.ShapeDtypeStruct((B, W, S), BF16),
        scratch_shapes=[pltpu.VMEM((D, D), F32)],
        compiler_params=_params("arbitrary", "arbitrary", "arbitrary"),
        name="retention",
    )(rq, rk, rv, rg_t, dm_t, kd, qd, cd, gn)


def _swa_body(sink_ref, q_ref, kc_ref, kp_ref, vc_ref, vp_ref, bias_ref, o_ref, kcat, vcat, *, tq):
    hk, n = pl.program_id(1), pl.program_id(2)
    kcat[0:BLOCK], kcat[BLOCK:] = kp_ref[0, 0], kc_ref[0, 0]
    vcat[0:BLOCK], vcat[BLOCK:] = vp_ref[0, 0], vc_ref[0, 0]
    group = lax.broadcasted_iota(jnp.int32, (1, SWA_GROUP * BLOCK), 1) // BLOCK
    sink = jnp.zeros((1, SWA_GROUP * BLOCK), F32)
    for g in range(SWA_GROUP):
        sink = jnp.where(group == g, sink_ref[hk * SWA_GROUP + g], sink)
    for j in range(tq // BLOCK):
        q4 = q_ref[0, :, j * BLOCK:(j + 1) * BLOCK, :].reshape(SWA_GROUP * BLOCK, HEAD_DIM)
        kb = kcat[j * BLOCK:(j + 2) * BLOCK]
        vb = vcat[j * BLOCK:(j + 2) * BLOCK]
        bias = bias_ref[0] if j > 0 else bias_ref[jnp.where(n == 0, 1, 0)]
        s = _dot_nt(kb, q4) + bias
        m = jnp.maximum(jnp.max(s, axis=0, keepdims=True), sink)
        p = jnp.exp(s - m)
        den = jnp.sum(p, axis=0, keepdims=True) + jnp.exp(sink - m)
        o = _dot_tn(vb, p.astype(BF16)) * (1.0 / den)
        for g in range(SWA_GROUP):
            o_ref[0, g * HEAD_DIM:(g + 1) * HEAD_DIM, j * BLOCK:(j + 1) * BLOCK] = (
                o[:, g * BLOCK:(g + 1) * BLOCK].astype(BF16))


def _swa(sq, sk, sv, sinks, tq):
    B, _, S, D = sq.shape
    tq = min(tq, S)
    r = tq // BLOCK
    qi = jnp.arange(BLOCK)[None, :] + BLOCK
    kj = jnp.arange(2 * BLOCK)[:, None]
    rel = qi - kj
    allowed = (rel >= 0) & (rel < WINDOW)
    allowed = jnp.stack([allowed, allowed & (kj >= BLOCK)])
    bias = jnp.tile(jnp.where(allowed, 0.0, NEG_INF).astype(F32), (1, 1, SWA_GROUP))
    cur = pl.BlockSpec((1, 1, tq, D), lambda b, h, n: (b, h, n, 0))
    prev = pl.BlockSpec((1, 1, BLOCK, D), lambda b, h, n: (b, h, jnp.maximum(n * r - 1, 0), 0))
    return pl.pallas_call(
        functools.partial(_swa_body, tq=tq),
        grid=(B, SWA_KV_HEADS, S // tq),
        in_specs=[pl.BlockSpec(memory_space=pltpu.SMEM),
                  pl.BlockSpec((1, SWA_GROUP, tq, D), lambda b, h, n: (b, h, n, 0)),
                  cur, prev, cur, prev,
                  pl.BlockSpec((2, 2 * BLOCK, SWA_GROUP * BLOCK), lambda b, h, n: (0, 0, 0))],
        out_specs=pl.BlockSpec((1, SWA_GROUP * D, tq), lambda b, h, n: (b, h, n)),
        out_shape=jax.ShapeDtypeStruct((B, SWA_W, S), BF16),
        scratch_shapes=[pltpu.VMEM((BLOCK + tq, D), BF16), pltpu.VMEM((BLOCK + tq, D), BF16)],
        compiler_params=_params("arbitrary", "arbitrary", "arbitrary"),
        name="swa",
    )(sinks.astype(F32), sq, sk, sk, sv, sv, bias)


def _conv_body(h_ref, dw_ref, db_ref, lg_ref, lb_ref, pw_ref, o_ref, hbuf, hsh, *, ts, rb):
    s = pl.program_id(1)
    sub = 8
    span = ts + CONV_HALO - sub

    @pl.when(s == 0)
    def _():
        hbuf[0:CONV_HALO, :] = jnp.zeros((CONV_HALO, CONV_CH), F32)

    @pl.when(s > 0)
    def _():
        hbuf[0:CONV_HALO, :] = hbuf[ts:ts + CONV_HALO, :]

    hbuf[CONV_HALO:CONV_HALO + ts, :] = h_ref[0]
    for p in range(1, sub):
        hsh[p - 1] = hbuf[p:p + span, :]
    off = CONV_HALO - (CONV_WIDTH - 1)
    for r in range(ts // rb):
        base = r * rb
        acc = jnp.broadcast_to(db_ref[...], (rb, CONV_CH))
        for w in range(CONV_WIDTH):
            q, p = divmod(off + w, sub)
            lo = base + q * sub
            tap = hbuf[lo:lo + rb, :] if p == 0 else hsh[p - 1, lo:lo + rb, :]
            acc = acc + tap * dw_ref[w:w + 1, :]
        mu = jnp.mean(acc, axis=-1, keepdims=True)
        d = acc - mu
        var = jnp.mean(d * d, axis=-1, keepdims=True)
        hn = d * lax.rsqrt(var + EPS) * lg_ref[...] + lb_ref[...]
        o_ref[0, base:base + rb, :] = _dot(_silu(hn).astype(BF16), pw_ref[...]).astype(BF16)


def _conv(ch, dw_w, dw_b, ln_g, ln_b, pw_w, ts, rb):
    B, S, W = ch.shape
    ts = min(ts, S)
    rb = min(rb, ts)
    pw = jnp.zeros((W, W), F32)
    gd = W // CONV_GROUPS
    for g in range(CONV_GROUPS):
        pw = pw.at[g * gd:(g + 1) * gd, g * gd:(g + 1) * gd].set(pw_w[g])
    act = pl.BlockSpec((1, ts, W), lambda b, s: (b, s, 0))

    def const(shape):
        return pl.BlockSpec(shape, lambda b, s: (0,) * len(shape))

    return pl.pallas_call(
        functools.partial(_conv_body, ts=ts, rb=rb),
        grid=(B, S // ts),
        in_specs=[act, const((CONV_WIDTH, W)), const((1, W)), const((1, W)), const((1, W)), const((W, W))],
        out_specs=act,
        out_shape=jax.ShapeDtypeStruct((B, S, W), BF16),
        scratch_shapes=[pltpu.VMEM((CONV_HALO + ts, W), F32), pltpu.VMEM((7, CONV_HALO + ts - 8, W), F32)],
        compiler_params=_params("arbitrary", "arbitrary"),
        name="conv",
    )(ch, dw_w, dw_b.reshape(1, W), ln_g.reshape(1, W), ln_b.reshape(1, W), pw.astype(BF16))


def _mixer_residual(yr_ref, ys_ref, yc_ref, w_ref, x_ref):
    a, b = RET_W, RET_W + SWA_W
    return x_ref[...] + (_dot_tn(yr_ref[0], w_ref[0:a, :]) + _dot_tn(ys_ref[0], w_ref[a:b, :])
                         + _dot(yc_ref[...], w_ref[b:, :]))


def _mixer_specs(tm, tiles_per_seq):
    def chan(width):
        return pl.BlockSpec((1, width, tm), lambda i: (i // tiles_per_seq, 0, i % tiles_per_seq))

    return [chan(RET_W), chan(SWA_W), pl.BlockSpec((tm, CONV_CH), lambda i: (i, 0))]


def _outproj_body(yr_ref, ys_ref, yc_ref, w_ref, x_ref, g_ref, wr_ref, xo_ref, hn_ref, rt_ref, rl_ref):
    x = _mixer_residual(yr_ref, ys_ref, yc_ref, w_ref, x_ref)
    xo_ref[...] = x
    h = _rms(x, g_ref[...]).astype(BF16)
    hn_ref[...] = h
    logits = _dot(h, wr_ref[...])
    lane = lax.broadcasted_iota(jnp.int32, logits.shape, 1)
    lg = jnp.where(lane < N_EXPERTS, logits, -jnp.inf)
    m1 = jnp.max(lg, axis=-1, keepdims=True)
    i1 = jnp.min(jnp.where(lg == m1, lane, LANES), axis=-1, keepdims=True)
    lg2 = jnp.where(lane == i1, -jnp.inf, lg)
    m2 = jnp.max(lg2, axis=-1, keepdims=True)
    i2 = jnp.min(jnp.where(lg2 == m2, lane, LANES), axis=-1, keepdims=True)
    e = jnp.exp(m2 - m1)
    w1 = 1.0 / (1.0 + e)
    w2 = e / (1.0 + e)
    tm = logits.shape[0]
    hit1, hit2 = lane == i1, lane == i2
    onehot = jnp.where(jnp.logical_or(hit1, hit2), 1.0, 0.0)
    before = (lax.broadcasted_iota(jnp.int32, (tm, tm), 0) > lax.broadcasted_iota(jnp.int32, (tm, tm), 1))
    rank = _dot(jnp.where(before, 1.0, 0.0).astype(BF16), onehot.astype(BF16))
    count = jnp.sum(onehot, axis=0, keepdims=True).astype(jnp.int32)
    run_len = (count + (RUN_ALIGN - 1)) & ~(RUN_ALIGN - 1)
    lower = (lax.broadcasted_iota(jnp.int32, (LANES, LANES), 0) < lax.broadcasted_iota(jnp.int32, (LANES, LANES), 1))
    run_rows = jnp.broadcast_to(run_len.astype(F32), (8, LANES)).astype(BF16)
    loc_off = _dot(run_rows, jnp.where(lower, 1.0, 0.0).astype(BF16))[0:1, :]
    pos = rank + loc_off
    p1 = jnp.sum(jnp.where(hit1, pos, 0.0), axis=-1, keepdims=True)
    p2 = jnp.sum(jnp.where(hit2, pos, 0.0), axis=-1, keepdims=True)
    cols = (i1.astype(F32), i2.astype(F32), w1, w2, p1, p2)
    rt = jnp.zeros_like(logits)
    for c, v in enumerate(cols):
        rt = jnp.where(lane == c, v, rt)
    rt_ref[...] = rt
    rl_ref[0] = jnp.broadcast_to(run_len, (8, LANES))


def _outproj_route(yr, ys, yc, w_bf, x, g, w_router):
    T, D = x.shape
    tm = MOE_TB
    S = ys.shape[2]

    def row(width):
        return pl.BlockSpec((tm, width), lambda i: (i, 0))

    def const(shape):
        return pl.BlockSpec(shape, lambda i: (0,) * len(shape))

    wr = jnp.zeros((D, LANES), F32).at[:, :N_EXPERTS].set(w_router).astype(BF16)
    return pl.pallas_call(
        _outproj_body,
        grid=(T // tm,),
        in_specs=_mixer_specs(tm, S // tm) + [const((D, D)), row(D), const((1, D)), const((D, LANES))],
        out_specs=[row(D), row(D), row(LANES), pl.BlockSpec((1, 8, LANES), lambda i: (i, 0, 0))],
        out_shape=[jax.ShapeDtypeStruct((T, D), F32), jax.ShapeDtypeStruct((T, D), BF16),
                   jax.ShapeDtypeStruct((T, LANES), F32), jax.ShapeDtypeStruct((T // tm, 8, LANES), jnp.int32)],
        compiler_params=_params("arbitrary"),
        name="outproj_route",
    )(yr, ys, yc, w_bf, x, g.reshape(1, D), wr)


def _dense_ffn_body(yr_ref, ys_ref, yc_ref, wo_ref, x_ref, g_ref, wg_ref, wu_ref, wd_ref, o_ref, *, chunks):
    x = _mixer_residual(yr_ref, ys_ref, yc_ref, wo_ref, x_ref)
    h = _rms(x, g_ref[...]).astype(BF16)
    acc = x
    for a, b in chunks:
        g = _dot(h, wg_ref[:, a:b])
        u = _dot(h, wu_ref[:, a:b])
        acc = acc + _dot((_silu(g) * u).astype(BF16), wd_ref[a:b, :])
    o_ref[...] = acc


def _outproj_dense_ffn(yr, ys, yc, wo_bf, x, g, wg, wu, wd, tm, fc):
    T, D = x.shape
    FF = wg.shape[1]
    S = ys.shape[2]
    tm = min(tm, S)
    chunks = tuple((a, min(a + fc, FF)) for a in range(0, FF, fc))

    def row(width):
        return pl.BlockSpec((tm, width), lambda i: (i, 0))

    def const(shape):
        return pl.BlockSpec(shape, lambda i: (0,) * len(shape), pipeline_mode=pl.Buffered(1))

    return pl.pallas_call(
        functools.partial(_dense_ffn_body, chunks=chunks),
        grid=(T // tm,),
        in_specs=_mixer_specs(tm, S // tm) + [const((D, D)), row(D), const((1, D)),
                                              const((D, FF)), const((D, FF)), const((FF, D))],
        out_specs=row(D),
        out_shape=jax.ShapeDtypeStruct((T, D), F32),
        compiler_params=_params("arbitrary"),
        name="outproj_dense_ffn",
    )(yr, ys, yc, wo_bf, x, g.reshape(1, D), wg.astype(BF16), wu.astype(BF16), wd.astype(BF16))


MOE_TB = 512
MOE_TM = 512
RUN_ALIGN = 8
RUN_SIZES = (512, 256, 128, 64, 32, 16, 8)
BLOCK_ROWS = TOP_K * MOE_TB + N_EXPERTS * RUN_ALIGN
BLOCK_LANES = 1152
MOE_FC = 512


def _route_meta(rt, rl, n_tiles):
    T = rt.shape[0]
    nb = T // MOE_TB
    pos = rt[:, 4:4 + TOP_K].astype(jnp.int32)
    run_len = rl[:, 0, :N_EXPERTS]
    loc_off = jnp.cumsum(run_len, axis=1) - run_len
    group = jnp.sum(run_len, axis=0)
    ptiles = (group + MOE_TM - 1) // MOE_TM
    tile_end = jnp.cumsum(ptiles)
    gstart = (tile_end - ptiles) * MOE_TM
    run_start = gstart[None, :] + jnp.cumsum(run_len, axis=0) - run_len
    tile_expert = jnp.sum(jnp.arange(n_tiles, dtype=jnp.int32)[:, None] >= tile_end[None, :], axis=1)
    i32 = lambda a: a.astype(jnp.int32)
    pos3 = pos.reshape(nb, MOE_TB, TOP_K)
    return dict(
        pos_l=i32(jnp.swapaxes(pos3, 1, 2)),
        pos_c=i32(pos3.reshape(T, TOP_K)),
        loc_off=i32(loc_off.reshape(-1)), run_start=i32(run_start.reshape(-1)), run_len=i32(run_len.reshape(-1)),
        pad_start=i32(gstart + group), pad_len=i32(ptiles * MOE_TM - group),
        tile_expert=i32(jnp.minimum(tile_expert, N_EXPERTS - 1)), n_used=i32(tile_end[-1:]))


def _run_dma(src, dst, src_off, dst_off, length, sem, wait):
    off = 0
    for k in RUN_SIZES:
        part = length & k

        @pl.when(part != 0)
        def _(off=off, k=k):
            cp = pltpu.make_async_copy(src.at[pl.ds(pl.multiple_of(src_off + off, RUN_ALIGN), k)],
                                       dst.at[pl.ds(pl.multiple_of(dst_off + off, RUN_ALIGN), k)], sem)
            if wait:
                cp.wait()
            else:
                cp.start()

        off = off + part


def _dispatch_body(lo_ref, rs_ref, rl_ref, ps_ref, pn_ref, nu_ref, pos_ref, h_ref, xs_hbm, sbuf, zbuf, sem):
    b, nb = pl.program_id(0), pl.num_programs(0)
    slot = b % 2
    n_tiles = xs_hbm.shape[0] // MOE_TM

    def zero_tile(i, wait):
        cp = pltpu.make_async_copy(zbuf, xs_hbm.at[pl.ds(pl.multiple_of(i * MOE_TM, MOE_TM), MOE_TM)], sem.at[2])
        if wait:
            cp.wait()
        else:
            cp.start()

    def runs(blk, s, wait):
        for e in range(N_EXPERTS):
            j = blk * N_EXPERTS + e
            _run_dma(sbuf.at[s], xs_hbm, lo_ref[j], rs_ref[j], rl_ref[j], sem.at[s], wait)

    @pl.when(b >= 2)
    def _():
        runs(b - 2, slot, True)

    pos = pos_ref[0]
    r = lax.broadcasted_iota(jnp.int32, (BLOCK_ROWS, MOE_TB), 0)
    onehot = jnp.logical_or(r == pos[0:1, :], r == pos[1:2, :])
    sbuf[slot] = _dot(jnp.where(onehot, 1.0, 0.0).astype(BF16), h_ref[...])
    runs(b, slot, False)

    @pl.when(b == nb - 1)
    def _():
        zbuf[...] = jnp.zeros_like(zbuf)
        for e in range(N_EXPERTS):
            _run_dma(zbuf, xs_hbm, 0, ps_ref[e], pn_ref[e], sem.at[2], False)
        lax.fori_loop(nu_ref[0], n_tiles, lambda i, c: zero_tile(i, False), None)
        for e in range(N_EXPERTS):
            _run_dma(zbuf, xs_hbm, 0, ps_ref[e], pn_ref[e], sem.at[2], True)
        lax.fori_loop(nu_ref[0], n_tiles, lambda i, c: zero_tile(i, True), None)

        @pl.when(b >= 1)
        def _():
            runs(b - 1, 1 - slot, True)

        runs(b, slot, True)


def _dispatch(hn, meta, n_rows):
    T, D = hn.shape
    nb = T // MOE_TB
    grid_spec = pltpu.PrefetchScalarGridSpec(
        num_scalar_prefetch=6,
        grid=(nb,),
        in_specs=[pl.BlockSpec((1, TOP_K, MOE_TB), lambda b, *_: (b, 0, 0)),
                  pl.BlockSpec((MOE_TB, D), lambda b, *_: (b, 0))],
        out_specs=pl.BlockSpec(memory_space=pl.ANY),
        scratch_shapes=[pltpu.VMEM((2, BLOCK_ROWS, D), F32), pltpu.VMEM((MOE_TM, D), F32),
                        pltpu.SemaphoreType.DMA((3,))],
    )
    return pl.pallas_call(
        _dispatch_body,
        grid_spec=grid_spec,
        out_shape=jax.ShapeDtypeStruct((n_rows, D), F32),
        compiler_params=pltpu.CompilerParams(dimension_semantics=("arbitrary",), vmem_limit_bytes=VMEM_LIMIT,
                                             has_side_effects=True),
        name="moe_dispatch",
    )(meta["loc_off"], meta["run_start"], meta["run_len"], meta["pad_start"], meta["pad_len"], meta["n_used"],
      meta["pos_l"], hn)


def _moe_ffn_body(te_ref, nu_ref, x_ref, wg_hbm, wu_hbm, wd_hbm, o_ref, wg_s, wu_s, wd_s, st_a, st_d, sem):
    i = pl.program_id(0)
    e = te_ref[i]
    active = i < nu_ref[0]
    nf = wg_s.shape[1] // MOE_FC
    fsl = [slice(f * MOE_FC, (f + 1) * MOE_FC) for f in range(nf)]

    @pl.when(jnp.logical_and(active, jnp.logical_or(i == 0, e != te_ref[jnp.maximum(i - 1, 0)])))
    def _():
        jobs = ([(wg_hbm, wg_s, True, f) for f in range(nf)] + [(wu_hbm, wu_s, True, f) for f in range(nf)]
                + [(wd_hbm, wd_s, False, f) for f in range(nf)])

        def copy(j):
            w_hbm, _, cols, f = jobs[j]
            if cols:
                return pltpu.make_async_copy(w_hbm.at[e, :, fsl[f]], st_a.at[j % 2], sem.at[j % 2])
            return pltpu.make_async_copy(w_hbm.at[e, fsl[f], :], st_d.at[j % 2], sem.at[j % 2])

        copy(0).start()
        for j, (_, w_s, cols, f) in enumerate(jobs):
            if j + 1 < len(jobs):
                copy(j + 1).start()
            copy(j).wait()
            if cols:
                w_s[:, fsl[f]] = st_a[j % 2].astype(BF16)
            else:
                w_s[fsl[f], :] = st_d[j % 2].astype(BF16)

    @pl.when(active)
    def _():
        x = x_ref[...].astype(BF16)
        acc = jnp.zeros(o_ref.shape, F32)
        for sl in fsl:
            g = _dot(x, wg_s[:, sl])
            u = _dot(x, wu_s[:, sl])
            acc = acc + _dot((_silu(g) * u).astype(BF16), wd_s[sl, :])
        o_ref[...] = acc

    @pl.when(jnp.logical_not(active))
    def _():
        o_ref[...] = jnp.zeros_like(o_ref)


def _moe_ffn(xs, meta, wg, wu, wd):
    R, D = xs.shape
    E, _, FF = wg.shape
    grid_spec = pltpu.PrefetchScalarGridSpec(
        num_scalar_prefetch=2,
        grid=(R // MOE_TM,),
        in_specs=[pl.BlockSpec((MOE_TM, D), lambda i, te, nu: (jnp.minimum(i, nu[0] - 1), 0)),
                  pl.BlockSpec(memory_space=pl.ANY), pl.BlockSpec(memory_space=pl.ANY),
                  pl.BlockSpec(memory_space=pl.ANY)],
        out_specs=pl.BlockSpec((MOE_TM, D), lambda i, te, nu: (i, 0)),
        scratch_shapes=[pltpu.VMEM((D, FF), BF16), pltpu.VMEM((D, FF), BF16), pltpu.VMEM((FF, D), BF16),
                        pltpu.VMEM((2, D, MOE_FC), F32), pltpu.VMEM((2, MOE_FC, D), F32),
                        pltpu.SemaphoreType.DMA((2,))],
    )
    return pl.pallas_call(
        _moe_ffn_body,
        grid_spec=grid_spec,
        out_shape=jax.ShapeDtypeStruct((R, D), F32),
        compiler_params=_params("arbitrary"),
        name="moe_ffn",
    )(meta["tile_expert"], meta["n_used"], xs, wg, wu, wd)


def _combine_body(lo_ref, rs_ref, rl_ref, ys_hbm, x_ref, rt_ref, pc_ref, g_ref, o_ref, ybuf, sem):
    b, nb = pl.program_id(0), pl.num_programs(0)
    slot = b % 2

    def runs(blk, s, wait):
        for e in range(N_EXPERTS):
            j = blk * N_EXPERTS + e
            _run_dma(ys_hbm, ybuf.at[s], rs_ref[j], lo_ref[j], rl_ref[j], sem.at[s], wait)

    @pl.when(b == 0)
    def _():
        ybuf[...] = jnp.zeros_like(ybuf)
        runs(0, 0, False)

    @pl.when(b + 1 < nb)
    def _():
        runs(b + 1, 1 - slot, False)

    runs(b, slot, True)
    y = ybuf[slot].astype(BF16)
    pc, rt = pc_ref[...], rt_ref[...]
    lane = lax.broadcasted_iota(jnp.int32, (MOE_TB, BLOCK_LANES), 1)
    q0 = jnp.where(lane == pc[:, 0:1], 1.0, 0.0).astype(BF16)
    q1 = jnp.where(lane == pc[:, 1:2], 1.0, 0.0).astype(BF16)
    moe = rt[:, 2:3] * _dot(q0, y) + rt[:, 3:4] * _dot(q1, y)
    o_ref[...] = _rms(x_ref[...] + moe, g_ref[...])


def _combine(ys, meta, x, rt, g):
    T, D = x.shape
    grid_spec = pltpu.PrefetchScalarGridSpec(
        num_scalar_prefetch=3,
        grid=(T // MOE_TB,),
        in_specs=[pl.BlockSpec(memory_space=pl.ANY),
                  pl.BlockSpec((MOE_TB, D), lambda b, *_: (b, 0)),
                  pl.BlockSpec((MOE_TB, LANES), lambda b, *_: (b, 0)),
                  pl.BlockSpec((MOE_TB, TOP_K), lambda b, *_: (b, 0)),
                  pl.BlockSpec((1, D), lambda b, *_: (0, 0))],
        out_specs=pl.BlockSpec((MOE_TB, D), lambda b, *_: (b, 0)),
        scratch_shapes=[pltpu.VMEM((2, BLOCK_LANES, D), F32), pltpu.SemaphoreType.DMA((2,))],
    )
    return pl.pallas_call(
        _combine_body,
        grid_spec=grid_spec,
        out_shape=jax.ShapeDtypeStruct((T, D), F32),
        compiler_params=_params("arbitrary"),
        name="moe_combine",
    )(meta["loc_off"], meta["run_start"], meta["run_len"], ys, x, rt, meta["pos_c"], g.reshape(1, D))


def kernel(x, norm_mix_g, w_in, ret_gn_g, attn_sinks, conv_dw_w, conv_dw_b, conv_ln_g, conv_ln_b,
           conv_pw_w, w_out, norm_ffn_g, ffn_w_gate, ffn_w_up, ffn_w_down, moe_router, moe_w_gate,
           moe_w_up, moe_w_down, final_norm_g):
    B, S, D = x.shape
    T = B * S
    depth = w_in.shape[0]
    assert depth == 2 and ffn_w_gate.shape[0] == 1 and moe_router.shape[0] == 1, "dense layer then MoE layer"
    assert T % MOE_TB == 0 and moe_w_gate.shape[-1] % MOE_FC == 0

    for l in range(depth):
        rq, rk, rv, rg, sq, sk, sv, ch = _inproj(x, norm_mix_g[l], w_in[l].astype(BF16), tm=512)
        y_ret_t = _retention(rq, rk, rv, rg, ret_gn_g[l], ts=2048)
        y_swa_t = _swa(sq, sk, sv, attn_sinks[l], tq=1024)
        y_conv = _conv(ch, conv_dw_w[l], conv_dw_b[l], conv_ln_g[l], conv_ln_b[l], conv_pw_w[l], ts=512, rb=64)
        flat = lambda a: a.reshape(T, a.shape[-1])
        j = l // 2
        if l % 2 == 0:
            x = _outproj_dense_ffn(y_ret_t, y_swa_t, flat(y_conv), w_out[l].astype(BF16), flat(x),
                                   norm_ffn_g[l], ffn_w_gate[j], ffn_w_up[j], ffn_w_down[j],
                                   tm=512, fc=512).reshape(B, S, D)
        else:
            x2, hn, rt, rl = _outproj_route(y_ret_t, y_swa_t, flat(y_conv), w_out[l].astype(BF16), flat(x),
                                            norm_ffn_g[l], moe_router[j])
            n_rows = TOP_K * T + (T // MOE_TB) * N_EXPERTS * RUN_ALIGN + N_EXPERTS * MOE_TM
            meta = _route_meta(rt, rl, n_rows // MOE_TM)
            xs = _dispatch(hn, meta, n_rows)
            ys = _moe_ffn(xs, meta, moe_w_gate[j], moe_w_up[j], moe_w_down[j])
            x = _combine(ys, meta, x2, rt, final_norm_g).reshape(B, S, D)
    return x
```

```python
import functools

import jax
import jax.numpy as jnp
from jax import lax
from jax.experimental import pallas as pl
from jax.experimental.pallas import tpu as pltpu

F32 = jnp.float32
BF16 = jnp.bfloat16

HEAD_DIM = 64
RET_HEADS = 4
RET_W = RET_HEADS * HEAD_DIM
SWA_HEADS = 8
SWA_KV_HEADS = 2
SWA_GROUP = SWA_HEADS // SWA_KV_HEADS
SWA_W = SWA_HEADS * HEAD_DIM
SWA_KV_W = SWA_KV_HEADS * HEAD_DIM
CONV_CH = 256
CONV_GROUPS = 4
CONV_WIDTH = 31
WINDOW = 128
BLOCK = 128
RET_CHUNK = 128
ROPE_THETA = 500000.0
ROPE_DIM = HEAD_DIM // 4
RET_ROPE_THETA = 10000.0
N_EXPERTS = 8
TOP_K = 2
EPS = 1e-6
NEG_INF = -1e30

LANES = 128
CONV_HALO = 32
VMEM_LIMIT = 56 * 1024 * 1024

O_RQ, O_RK, O_RV, O_RG = 0, RET_W, 2 * RET_W, 3 * RET_W
O_SQ = 4 * RET_W
O_SK = O_SQ + SWA_W
O_SV = O_SK + SWA_KV_W
O_CA = O_SV + SWA_KV_W
O_CG = O_CA + CONV_CH
D_IN = O_CG + CONV_CH


def _params(*sem, flags=None):
    return pltpu.CompilerParams(dimension_semantics=sem, vmem_limit_bytes=VMEM_LIMIT, flags=flags)


def _rms(x, g):
    return x * lax.rsqrt(jnp.mean(x * x, axis=-1, keepdims=True) + EPS) * g


def _silu(x):
    return x * jax.nn.sigmoid(x)


def _dot(a, b):
    return jnp.dot(a, b, preferred_element_type=F32)


def _dot_nt(a, b):
    return lax.dot_general(a, b, (((1,), (1,)), ((), ())), preferred_element_type=F32)


def _dot_tn(a, b):
    return lax.dot_general(a, b, (((0,), (0,)), ((), ())), preferred_element_type=F32)


def _rope_tables(seq, theta, rot_dim):
    half = rot_dim // 2
    inv = 1.0 / (theta ** (jnp.arange(half, dtype=F32) / half))
    ang = jnp.arange(seq, dtype=F32)[:, None] * inv[None, :]
    cos, sin = jnp.cos(ang), jnp.sin(ang)
    rest = HEAD_DIM - rot_dim
    c = jnp.concatenate([cos, cos, jnp.ones((seq, rest), F32)], axis=1)
    s = jnp.concatenate([-sin, sin, jnp.zeros((seq, rest), F32)], axis=1)
    reps = LANES // HEAD_DIM
    return jnp.tile(c, (1, reps)), jnp.tile(s, (1, reps))


def _inproj_body(x_ref, g_ref, w_ref, wgt_ref, rc_ref, rs_ref, sc_ref, ss_ref,
                 rq_ref, rk_ref, rv_ref, rg_ref, sq_ref, sk_ref, sv_ref, ch_ref):
    h = _rms(x_ref[0], g_ref[...]).astype(BF16)
    lane = lax.broadcasted_iota(jnp.int32, (1, LANES), 1) % HEAD_DIM

    def seg(a, b):
        return _dot(h, w_ref[:, a:b])

    def rope(v, c, s, half):
        up = pltpu.roll(v, LANES - half, axis=1)
        dn = pltpu.roll(v, half, axis=1)
        return v * c + jnp.where(lane < half, up, dn) * s

    def store_heads(z, o_ref, first=0):
        for j in range(z.shape[1] // HEAD_DIM):
            o_ref[0, first + j] = z[:, j * HEAD_DIM:(j + 1) * HEAD_DIM]

    def rope_store(z, o_ref, c, s, half, scale):
        for i in range(z.shape[1] // LANES):
            r = rope(z[:, i * LANES:(i + 1) * LANES], c, s, half)
            if scale != 1.0:
                r = r * scale
            store_heads(r.astype(BF16), o_ref, first=i * (LANES // HEAD_DIM))

    rc, rs = rc_ref[...], rs_ref[...]
    sc, ss = sc_ref[...], ss_ref[...]
    scale = HEAD_DIM ** -0.5
    rope_store(seg(O_RQ, O_RK), rq_ref, rc, rs, HEAD_DIM // 2, 1.0)
    rope_store(seg(O_RK, O_RV), rk_ref, rc, rs, HEAD_DIM // 2, scale)
    store_heads(seg(O_RV, O_RG).astype(BF16), rv_ref)
    rg_ref[0] = _dot_nt(wgt_ref[...], h)
    rope_store(seg(O_SQ, O_SK), sq_ref, sc, ss, ROPE_DIM // 2, scale)
    rope_store(seg(O_SK, O_SV), sk_ref, sc, ss, ROPE_DIM // 2, 1.0)
    store_heads(seg(O_SV, O_CA).astype(BF16), sv_ref)
    ch_ref[0] = seg(O_CA, O_CG) * jax.nn.sigmoid(seg(O_CG, D_IN))


def _inproj(x, g, w_bf, tm):
    B, S, D = x.shape
    tm = min(tm, S)
    rc, rs = _rope_tables(S, RET_ROPE_THETA, HEAD_DIM)
    sc, ss = _rope_tables(S, ROPE_THETA, ROPE_DIM)
    tab = pl.BlockSpec((tm, LANES), lambda s, b: (s, 0))

    def out(width, dtype):
        return (jax.ShapeDtypeStruct((B, S, width), dtype),
                pl.BlockSpec((1, tm, width), lambda s, b: (b, s, 0)))

    def out_heads(heads):
        return (jax.ShapeDtypeStruct((B, heads, S, HEAD_DIM), BF16),
                pl.BlockSpec((1, heads, tm, HEAD_DIM), lambda s, b: (b, 0, s, 0)))

    gate_t = (jax.ShapeDtypeStruct((B, RET_W, S), F32), pl.BlockSpec((1, RET_W, tm), lambda s, b: (b, 0, s)))
    outs = [out_heads(RET_HEADS), out_heads(RET_HEADS), out_heads(RET_HEADS), gate_t,
            out_heads(SWA_HEADS), out_heads(SWA_KV_HEADS), out_heads(SWA_KV_HEADS), out(CONV_CH, F32)]
    return pl.pallas_call(
        _inproj_body,
        grid=(S // tm, B),
        in_specs=[pl.BlockSpec((1, tm, D), lambda s, b: (b, s, 0)),
                  pl.BlockSpec((1, D), lambda s, b: (0, 0)),
                  pl.BlockSpec((D, D_IN), lambda s, b: (0, 0)),
                  pl.BlockSpec((RET_W, D), lambda s, b: (0, 0)),
                  tab, tab, tab, tab],
        out_specs=[o[1] for o in outs],
        out_shape=[o[0] for o in outs],
        compiler_params=_params("arbitrary", "arbitrary"),
        name="inproj",
    )(x, g.reshape(1, D), w_bf, w_bf[:, O_RG:O_SQ].T, rc, rs, sc, ss)


def _ret_body(q_ref, k_ref, v_ref, g_ref, dm_ref, kd_ref, qd_ref, cd_ref, gn_ref, o_ref, st_ref, *, ts):
    @pl.when(pl.program_id(2) == 0)
    def _():
        st_ref[...] = jnp.zeros_like(st_ref)

    C = RET_CHUNK
    for c in range(ts // C):
        rows = slice(c * C, (c + 1) * C)
        q, k, v = q_ref[0, 0, rows, :], k_ref[0, 0, rows, :], v_ref[0, 0, rows, :]
        st = st_ref[...]
        scores = _dot_nt(k, q) * dm_ref[0]
        intra = _dot_tn(v, scores.astype(BF16))
        cross = _dot_nt(st.astype(BF16), q) * qd_ref[0]
        kdec = (k.astype(F32) * kd_ref[0]).astype(BF16)
        st_ref[...] = st * cd_ref[0] + _dot_tn(v, kdec)
        o = intra + cross
        mu = jnp.mean(o, axis=0, keepdims=True)
        d = o - mu
        var = jnp.mean(d * d, axis=0, keepdims=True)
        on = d * lax.rsqrt(var + EPS) * gn_ref[0]
        o_ref[0, :, rows] = (_silu(g_ref[0, :, rows]) * on).astype(BF16)


def _retention(rq, rk, rv, rg_t, gn_g, ts):
    B, H, S, D = rq.shape
    ts = min(ts, S)
    C, W = RET_CHUNK, RET_W
    lg = jnp.log(1.0 - 2.0 ** (-5.0 - jnp.arange(H, dtype=F32)))
    idx = jnp.arange(C)
    rel = idx[:, None] - idx[None, :]
    dmask = jnp.where(rel[None] >= 0,
                      jnp.exp(jnp.maximum(rel, 0)[None].astype(F32) * lg[:, None, None]), 0.0)
    k_decay = jnp.exp((C - 1 - idx)[:, None].astype(F32) * lg[None, :])
    q_decay = jnp.exp((idx + 1)[:, None].astype(F32) * lg[None, :])
    chunk_decay = jnp.exp(C * lg)
    dm_t = jnp.swapaxes(dmask, 1, 2)
    kd = jnp.broadcast_to(k_decay.T[:, :, None], (H, C, D))
    qd = jnp.broadcast_to(q_decay.T[:, None, :], (H, D, C))
    cd = jnp.broadcast_to(chunk_decay[:, None, None], (H, D, D))
    gn = jnp.broadcast_to(gn_g.reshape(H, D, 1), (H, D, C))
    heads = pl.BlockSpec((1, 1, ts, D), lambda b, h, n: (b, h, n, 0))
    chan = pl.BlockSpec((1, D, ts), lambda b, h, n: (b, h, n))

    def per_head(r, c):
        return pl.BlockSpec((1, r, c), lambda b, h, n: (h, 0, 0))

    return pl.pallas_call(
        functools.partial(_ret_body, ts=ts),
        grid=(B, H, S // ts),
        in_specs=[heads, heads, heads, chan, per_head(C, C), per_head(C, D), per_head(D, C),
                  per_head(D, D), per_head(D, C)],
        out_specs=chan,
        out_shape=jax.ShapeDtypeStruct((B, W, S), BF16),
        scratch_shapes=[pltpu.VMEM((D, D), F32)],
        compiler_params=_params("arbitrary", "arbitrary", "arbitrary"),
        name="retention",
    )(rq, rk, rv, rg_t, dm_t, kd, qd, cd, gn)


def _swa_body(sink_ref, q_ref, kc_ref, kp_ref, vc_ref, vp_ref, bias_ref, o_ref, kcat, vcat, sbuf, pbuf, rbuf,
              *, tq):
    hk, n = pl.program_id(1), pl.program_id(2)
    kcat[0:BLOCK], kcat[BLOCK:] = kp_ref[0, 0], kc_ref[0, 0]
    vcat[0:BLOCK], vcat[BLOCK:] = vp_ref[0, 0], vc_ref[0, 0]
    width = SWA_GROUP * BLOCK
    group = lax.broadcasted_iota(jnp.int32, (1, width), 1) // BLOCK
    sink = jnp.zeros((1, width), F32)
    for g in range(SWA_GROUP):
        sink = jnp.where(group == g, sink_ref[hk * SWA_GROUP + g], sink)
    nq = tq // BLOCK
    for j in range(nq):
        q = q_ref[0, :, j * BLOCK:(j + 1) * BLOCK, :].reshape(width, HEAD_DIM)
        kb = kcat[j * BLOCK:(j + 2) * BLOCK]
        sbuf[j] = _dot_nt(kb, q)
    for j in range(nq):
        s = sbuf[j] + (bias_ref[0] if j > 0 else bias_ref[jnp.where(n == 0, 1, 0)])
        m = jnp.maximum(jnp.max(s, axis=0, keepdims=True), sink)
        p = jnp.exp(s - m)
        rbuf[j] = 1.0 / (jnp.sum(p, axis=0, keepdims=True) + jnp.exp(sink - m))
        pbuf[j] = p.astype(BF16)
    for j in range(nq):
        vb = vcat[j * BLOCK:(j + 2) * BLOCK]
        o = _dot_tn(vb, pbuf[j]) * rbuf[j]
        for g in range(SWA_GROUP):
            o_ref[0, g * HEAD_DIM:(g + 1) * HEAD_DIM, j * BLOCK:(j + 1) * BLOCK] = (
                o[:, g * BLOCK:(g + 1) * BLOCK].astype(BF16))


def _swa(sq, sk, sv, sinks, tq):
    B, _, S, D = sq.shape
    tq = min(tq, S)
    r = tq // BLOCK
    qi = jnp.arange(BLOCK)[None, :] + BLOCK
    kj = jnp.arange(2 * BLOCK)[:, None]
    rel = qi - kj
    allowed = (rel >= 0) & (rel < WINDOW)
    allowed = jnp.stack([allowed, allowed & (kj >= BLOCK)])
    bias = jnp.tile(jnp.where(allowed, 0.0, NEG_INF).astype(F32), (1, 1, SWA_GROUP))
    nq, width = tq // BLOCK, SWA_GROUP * BLOCK
    cur = pl.BlockSpec((1, 1, tq, D), lambda b, h, n: (b, h, n, 0))
    prev = pl.BlockSpec((1, 1, BLOCK, D), lambda b, h, n: (b, h, jnp.maximum(n * r - 1, 0), 0))
    return pl.pallas_call(
        functools.partial(_swa_body, tq=tq),
        grid=(B, SWA_KV_HEADS, S // tq),
        in_specs=[pl.BlockSpec(memory_space=pltpu.SMEM),
                  pl.BlockSpec((1, SWA_GROUP, tq, D), lambda b, h, n: (b, h, n, 0)),
                  cur, prev, cur, prev,
                  pl.BlockSpec((2, 2 * BLOCK, width), lambda b, h, n: (0, 0, 0))],
        out_specs=pl.BlockSpec((1, SWA_GROUP * D, tq), lambda b, h, n: (b, h, n)),
        out_shape=jax.ShapeDtypeStruct((B, SWA_W, S), BF16),
        scratch_shapes=[pltpu.VMEM((BLOCK + tq, D), BF16), pltpu.VMEM((BLOCK + tq, D), BF16),
                        pltpu.VMEM((nq, 2 * BLOCK, width), F32), pltpu.VMEM((nq, 2 * BLOCK, width), BF16),
                        pltpu.VMEM((nq, 1, width), F32)],
        compiler_params=_params("arbitrary", "arbitrary", "arbitrary"),
        name="swa",
    )(sinks.astype(F32), sq, sk, sk, sv, sv, bias)


def _conv_body(h_ref, dw_ref, db_ref, lg_ref, lb_ref, pw_ref, o_ref, hbuf, hsh, *, ts, rb):
    s = pl.program_id(1)
    sub = 8
    span = ts + CONV_HALO - sub

    @pl.when(s == 0)
    def _():
        hbuf[0:CONV_HALO, :] = jnp.zeros((CONV_HALO, CONV_CH), F32)

    @pl.when(s > 0)
    def _():
        hbuf[0:CONV_HALO, :] = hbuf[ts:ts + CONV_HALO, :]

    hbuf[CONV_HALO:CONV_HALO + ts, :] = h_ref[0]
    for p in range(1, sub):
        hsh[p - 1] = hbuf[p:p + span, :]
    off = CONV_HALO - (CONV_WIDTH - 1)
    for r in range(ts // rb):
        base = r * rb
        acc = jnp.broadcast_to(db_ref[...], (rb, CONV_CH))
        for w in range(CONV_WIDTH):
            q, p = divmod(off + w, sub)
            lo = base + q * sub
            tap = hbuf[lo:lo + rb, :] if p == 0 else hsh[p - 1, lo:lo + rb, :]
            acc = acc + tap * dw_ref[w:w + 1, :]
        mu = jnp.mean(acc, axis=-1, keepdims=True)
        d = acc - mu
        var = jnp.mean(d * d, axis=-1, keepdims=True)
        hn = d * lax.rsqrt(var + EPS) * lg_ref[...] + lb_ref[...]
        o_ref[0, base:base + rb, :] = _dot(_silu(hn).astype(BF16), pw_ref[...]).astype(BF16)


def _conv(ch, dw_w, dw_b, ln_g, ln_b, pw_w, ts, rb):
    B, S, W = ch.shape
    ts = min(ts, S)
    rb = min(rb, ts)
    pw = jnp.zeros((W, W), F32)
    gd = W // CONV_GROUPS
    for g in range(CONV_GROUPS):
        pw = pw.at[g * gd:(g + 1) * gd, g * gd:(g + 1) * gd].set(pw_w[g])
    act = pl.BlockSpec((1, ts, W), lambda b, s: (b, s, 0))

    def const(shape):
        return pl.BlockSpec(shape, lambda b, s: (0,) * len(shape))

    return pl.pallas_call(
        functools.partial(_conv_body, ts=ts, rb=rb),
        grid=(B, S // ts),
        in_specs=[act, const((CONV_WIDTH, W)), const((1, W)), const((1, W)), const((1, W)), const((W, W))],
        out_specs=act,
        out_shape=jax.ShapeDtypeStruct((B, S, W), BF16),
        scratch_shapes=[pltpu.VMEM((CONV_HALO + ts, W), F32), pltpu.VMEM((7, CONV_HALO + ts - 8, W), F32)],
        compiler_params=_params("arbitrary", "arbitrary"),
        name="conv",
    )(ch, dw_w, dw_b.reshape(1, W), ln_g.reshape(1, W), ln_b.reshape(1, W), pw.astype(BF16))


def _mixer_residual(yr_ref, ys_ref, yc_ref, w_ref, x_ref):
    a, b = RET_W, RET_W + SWA_W
    return x_ref[...] + (_dot_tn(yr_ref[0], w_ref[0:a, :]) + _dot_tn(ys_ref[0], w_ref[a:b, :])
                         + _dot(yc_ref[...], w_ref[b:, :]))


def _mixer_specs(tm, tiles_per_seq):
    def chan(width):
        return pl.BlockSpec((1, width, tm), lambda i: (i // tiles_per_seq, 0, i % tiles_per_seq))

    return [chan(RET_W), chan(SWA_W), pl.BlockSpec((tm, CONV_CH), lambda i: (i, 0))]


def _outproj_body(yr_ref, ys_ref, yc_ref, w_ref, x_ref, g_ref, wr_ref, xo_ref, hn_ref, rt_ref, rl_ref):
    x = _mixer_residual(yr_ref, ys_ref, yc_ref, w_ref, x_ref)
    xo_ref[...] = x
    h = _rms(x, g_ref[...]).astype(BF16)
    hn_ref[...] = h
    logits = _dot(h, wr_ref[...])
    lane = lax.broadcasted_iota(jnp.int32, logits.shape, 1)
    lg = jnp.where(lane < N_EXPERTS, logits, -jnp.inf)
    m1 = jnp.max(lg, axis=-1, keepdims=True)
    i1 = jnp.min(jnp.where(lg == m1, lane, LANES), axis=-1, keepdims=True)
    lg2 = jnp.where(lane == i1, -jnp.inf, lg)
    m2 = jnp.max(lg2, axis=-1, keepdims=True)
    i2 = jnp.min(jnp.where(lg2 == m2, lane, LANES), axis=-1, keepdims=True)
    e = jnp.exp(m2 - m1)
    w1 = 1.0 / (1.0 + e)
    w2 = e / (1.0 + e)
    tm = logits.shape[0]
    hit1, hit2 = lane == i1, lane == i2
    onehot = jnp.where(jnp.logical_or(hit1, hit2), 1.0, 0.0)
    before = (lax.broadcasted_iota(jnp.int32, (tm, tm), 0) > lax.broadcasted_iota(jnp.int32, (tm, tm), 1))
    rank = _dot(jnp.where(before, 1.0, 0.0).astype(BF16), onehot.astype(BF16))
    count = jnp.sum(onehot, axis=0, keepdims=True).astype(jnp.int32)
    run_len = (count + (RUN_ALIGN - 1)) & ~(RUN_ALIGN - 1)
    lower = (lax.broadcasted_iota(jnp.int32, (LANES, LANES), 0) < lax.broadcasted_iota(jnp.int32, (LANES, LANES), 1))
    run_rows = jnp.broadcast_to(run_len.astype(F32), (8, LANES)).astype(BF16)
    loc_off = _dot(run_rows, jnp.where(lower, 1.0, 0.0).astype(BF16))[0:1, :]
    pos = rank + loc_off
    p1 = jnp.sum(jnp.where(hit1, pos, 0.0), axis=-1, keepdims=True)
    p2 = jnp.sum(jnp.where(hit2, pos, 0.0), axis=-1, keepdims=True)
    cols = (i1.astype(F32), i2.astype(F32), w1, w2, p1, p2)
    rt = jnp.zeros_like(logits)
    for c, v in enumerate(cols):
        rt = jnp.where(lane == c, v, rt)
    rt_ref[...] = rt
    rl_ref[0] = jnp.broadcast_to(run_len, (8, LANES))


def _outproj_route(yr, ys, yc, w_bf, x, g, w_router):
    T, D = x.shape
    tm = MOE_TB
    S = ys.shape[2]

    def row(width):
        return pl.BlockSpec((tm, width), lambda i: (i, 0))

    def const(shape):
        return pl.BlockSpec(shape, lambda i: (0,) * len(shape))

    wr = jnp.zeros((D, LANES), F32).at[:, :N_EXPERTS].set(w_router).astype(BF16)
    return pl.pallas_call(
        _outproj_body,
        grid=(T // tm,),
        in_specs=_mixer_specs(tm, S // tm) + [const((D, D)), row(D), const((1, D)), const((D, LANES))],
        out_specs=[row(D), row(D), row(LANES), pl.BlockSpec((1, 8, LANES), lambda i: (i, 0, 0))],
        out_shape=[jax.ShapeDtypeStruct((T, D), F32), jax.ShapeDtypeStruct((T, D), BF16),
                   jax.ShapeDtypeStruct((T, LANES), F32), jax.ShapeDtypeStruct((T // tm, 8, LANES), jnp.int32)],
        compiler_params=_params("arbitrary"),
        name="outproj_route",
    )(yr, ys, yc, w_bf, x, g.reshape(1, D), wr)


def _dense_ffn_body(yr_ref, ys_ref, yc_ref, wo_ref, x_ref, g_ref, wg_ref, wu_ref, wd_ref, o_ref, *, chunks):
    x = _mixer_residual(yr_ref, ys_ref, yc_ref, wo_ref, x_ref)
    h = _rms(x, g_ref[...]).astype(BF16)
    acc = x
    for a, b in chunks:
        g = _dot(h, wg_ref[:, a:b])
        u = _dot(h, wu_ref[:, a:b])
        acc = acc + _dot((_silu(g) * u).astype(BF16), wd_ref[a:b, :])
    o_ref[...] = acc


def _outproj_dense_ffn(yr, ys, yc, wo_bf, x, g, wg, wu, wd, tm, fc):
    T, D = x.shape
    FF = wg.shape[1]
    S = ys.shape[2]
    tm = min(tm, S)
    chunks = tuple((a, min(a + fc, FF)) for a in range(0, FF, fc))

    def row(width):
        return pl.BlockSpec((tm, width), lambda i: (i, 0))

    def const(shape):
        return pl.BlockSpec(shape, lambda i: (0,) * len(shape), pipeline_mode=pl.Buffered(1))

    return pl.pallas_call(
        functools.partial(_dense_ffn_body, chunks=chunks),
        grid=(T // tm,),
        in_specs=_mixer_specs(tm, S // tm) + [const((D, D)), row(D), const((1, D)),
                                              const((D, FF)), const((D, FF)), const((FF, D))],
        out_specs=row(D),
        out_shape=jax.ShapeDtypeStruct((T, D), F32),
        compiler_params=_params("arbitrary"),
        name="outproj_dense_ffn",
    )(yr, ys, yc, wo_bf, x, g.reshape(1, D), wg.astype(BF16), wu.astype(BF16), wd.astype(BF16))


MOE_TB = 512
MOE_TM = 512
RUN_ALIGN = 8
RUN_SIZES = (512, 256, 128, 64, 32, 16, 8)
BLOCK_ROWS = TOP_K * MOE_TB + N_EXPERTS * RUN_ALIGN
BLOCK_LANES = 1152
MOE_FC = 512


def _route_meta(rt, rl, n_tiles):
    T = rt.shape[0]
    nb = T // MOE_TB
    pos = rt[:, 4:4 + TOP_K].astype(jnp.int32)
    run_len = rl[:, 0, :N_EXPERTS]
    loc_off = jnp.cumsum(run_len, axis=1) - run_len
    group = jnp.sum(run_len, axis=0)
    ptiles = (group + MOE_TM - 1) // MOE_TM
    tile_end = jnp.cumsum(ptiles)
    gstart = (tile_end - ptiles) * MOE_TM
    run_start = gstart[None, :] + jnp.cumsum(run_len, axis=0) - run_len
    tile_expert = jnp.sum(jnp.arange(n_tiles, dtype=jnp.int32)[:, None] >= tile_end[None, :], axis=1)
    i32 = lambda a: a.astype(jnp.int32)
    pos3 = pos.reshape(nb, MOE_TB, TOP_K)
    return dict(
        pos_l=i32(jnp.swapaxes(pos3, 1, 2)),
        pos_c=i32(pos3.reshape(T, TOP_K)),
        loc_off=i32(loc_off.reshape(-1)), run_start=i32(run_start.reshape(-1)), run_len=i32(run_len.reshape(-1)),
        pad_start=i32(gstart + group), pad_len=i32(ptiles * MOE_TM - group),
        tile_expert=i32(jnp.minimum(tile_expert, N_EXPERTS - 1)), n_used=i32(tile_end[-1:]))


def _run_dma(src, dst, src_off, dst_off, length, sem, wait):
    off = 0
    for k in RUN_SIZES:
        part = length & k

        @pl.when(part != 0)
        def _(off=off, k=k):
            cp = pltpu.make_async_copy(src.at[pl.ds(pl.multiple_of(src_off + off, RUN_ALIGN), k)],
                                       dst.at[pl.ds(pl.multiple_of(dst_off + off, RUN_ALIGN), k)], sem)
            if wait:
                cp.wait()
            else:
                cp.start()

        off = off + part


def _dispatch_body(lo_ref, rs_ref, rl_ref, ps_ref, pn_ref, nu_ref, pos_ref, h_ref, xs_hbm, sbuf, zbuf, sem):
    b, nb = pl.program_id(0), pl.num_programs(0)
    slot = b % 2
    n_tiles = xs_hbm.shape[0] // MOE_TM

    def zero_tile(i, wait):
        cp = pltpu.make_async_copy(zbuf, xs_hbm.at[pl.ds(pl.multiple_of(i * MOE_TM, MOE_TM), MOE_TM)], sem.at[2])
        if wait:
            cp.wait()
        else:
            cp.start()

    def runs(blk, s, wait):
        for e in range(N_EXPERTS):
            j = blk * N_EXPERTS + e
            _run_dma(sbuf.at[s], xs_hbm, lo_ref[j], rs_ref[j], rl_ref[j], sem.at[s], wait)

    @pl.when(b >= 2)
    def _():
        runs(b - 2, slot, True)

    pos = pos_ref[0]
    r = lax.broadcasted_iota(jnp.int32, (BLOCK_ROWS, MOE_TB), 0)
    onehot = jnp.logical_or(r == pos[0:1, :], r == pos[1:2, :])
    sbuf[slot] = _dot(jnp.where(onehot, 1.0, 0.0).astype(BF16), h_ref[...])
    runs(b, slot, False)

    @pl.when(b == nb - 1)
    def _():
        zbuf[...] = jnp.zeros_like(zbuf)
        for e in range(N_EXPERTS):
            _run_dma(zbuf, xs_hbm, 0, ps_ref[e], pn_ref[e], sem.at[2], False)
        lax.fori_loop(nu_ref[0], n_tiles, lambda i, c: zero_tile(i, False), None)
        for e in range(N_EXPERTS):
            _run_dma(zbuf, xs_hbm, 0, ps_ref[e], pn_ref[e], sem.at[2], True)
        lax.fori_loop(nu_ref[0], n_tiles, lambda i, c: zero_tile(i, True), None)

        @pl.when(b >= 1)
        def _():
            runs(b - 1, 1 - slot, True)

        runs(b, slot, True)


def _dispatch(hn, meta, n_rows):
    T, D = hn.shape
    nb = T // MOE_TB
    grid_spec = pltpu.PrefetchScalarGridSpec(
        num_scalar_prefetch=6,
        grid=(nb,),
        in_specs=[pl.BlockSpec((1, TOP_K, MOE_TB), lambda b, *_: (b, 0, 0)),
                  pl.BlockSpec((MOE_TB, D), lambda b, *_: (b, 0))],
        out_specs=pl.BlockSpec(memory_space=pl.ANY),
        scratch_shapes=[pltpu.VMEM((2, BLOCK_ROWS, D), F32), pltpu.VMEM((MOE_TM, D), F32),
                        pltpu.SemaphoreType.DMA((3,))],
    )
    return pl.pallas_call(
        _dispatch_body,
        grid_spec=grid_spec,
        out_shape=jax.ShapeDtypeStruct((n_rows, D), F32),
        compiler_params=pltpu.CompilerParams(dimension_semantics=("arbitrary",), vmem_limit_bytes=VMEM_LIMIT,
                                             has_side_effects=True),
        name="moe_dispatch",
    )(meta["loc_off"], meta["run_start"], meta["run_len"], meta["pad_start"], meta["pad_len"], meta["n_used"],
      meta["pos_l"], hn)


def _moe_ffn_body(te_ref, nu_ref, x_ref, wg_hbm, wu_hbm, wd_hbm, o_ref, wg_s, wu_s, wd_s, st_a, st_d, sem):
    i = pl.program_id(0)
    e = te_ref[i]
    active = i < nu_ref[0]
    nf = wg_s.shape[1] // MOE_FC
    fsl = [slice(f * MOE_FC, (f + 1) * MOE_FC) for f in range(nf)]

    @pl.when(jnp.logical_and(active, jnp.logical_or(i == 0, e != te_ref[jnp.maximum(i - 1, 0)])))
    def _():
        jobs = ([(wg_hbm, wg_s, True, f) for f in range(nf)] + [(wu_hbm, wu_s, True, f) for f in range(nf)]
                + [(wd_hbm, wd_s, False, f) for f in range(nf)])

        def copy(j):
            w_hbm, _, cols, f = jobs[j]
            if cols:
                return pltpu.make_async_copy(w_hbm.at[e, :, fsl[f]], st_a.at[j % 2], sem.at[j % 2])
            return pltpu.make_async_copy(w_hbm.at[e, fsl[f], :], st_d.at[j % 2], sem.at[j % 2])

        copy(0).start()
        for j, (_, w_s, cols, f) in enumerate(jobs):
            if j + 1 < len(jobs):
                copy(j + 1).start()
            copy(j).wait()
            if cols:
                w_s[:, fsl[f]] = st_a[j % 2].astype(BF16)
            else:
                w_s[fsl[f], :] = st_d[j % 2].astype(BF16)

    @pl.when(active)
    def _():
        x = x_ref[...].astype(BF16)
        acc = jnp.zeros(o_ref.shape, F32)
        for sl in fsl:
            g = _dot(x, wg_s[:, sl])
            u = _dot(x, wu_s[:, sl])
            acc = acc + _dot((_silu(g) * u).astype(BF16), wd_s[sl, :])
        o_ref[...] = acc

    @pl.when(jnp.logical_not(active))
    def _():
        o_ref[...] = jnp.zeros_like(o_ref)


def _moe_ffn(xs, meta, wg, wu, wd):
    R, D = xs.shape
    E, _, FF = wg.shape
    grid_spec = pltpu.PrefetchScalarGridSpec(
        num_scalar_prefetch=2,
        grid=(R // MOE_TM,),
        in_specs=[pl.BlockSpec((MOE_TM, D), lambda i, te, nu: (jnp.minimum(i, nu[0] - 1), 0)),
                  pl.BlockSpec(memory_space=pl.ANY), pl.BlockSpec(memory_space=pl.ANY),
                  pl.BlockSpec(memory_space=pl.ANY)],
        out_specs=pl.BlockSpec((MOE_TM, D), lambda i, te, nu: (i, 0)),
        scratch_shapes=[pltpu.VMEM((D, FF), BF16), pltpu.VMEM((D, FF), BF16), pltpu.VMEM((FF, D), BF16),
                        pltpu.VMEM((2, D, MOE_FC), F32), pltpu.VMEM((2, MOE_FC, D), F32),
                        pltpu.SemaphoreType.DMA((2,))],
    )
    return pl.pallas_call(
        _moe_ffn_body,
        grid_spec=grid_spec,
        out_shape=jax.ShapeDtypeStruct((R, D), F32),
        compiler_params=_params("arbitrary"),
        name="moe_ffn",
    )(meta["tile_expert"], meta["n_used"], xs, wg, wu, wd)


def _combine_body(lo_ref, rs_ref, rl_ref, ys_hbm, x_ref, rt_ref, pc_ref, g_ref, o_ref, ybuf, sem):
    b, nb = pl.program_id(0), pl.num_programs(0)
    slot = b % 2

    def runs(blk, s, wait):
        for e in range(N_EXPERTS):
            j = blk * N_EXPERTS + e
            _run_dma(ys_hbm, ybuf.at[s], rs_ref[j], lo_ref[j], rl_ref[j], sem.at[s], wait)

    @pl.when(b == 0)
    def _():
        ybuf[...] = jnp.zeros_like(ybuf)
        runs(0, 0, False)

    @pl.when(b + 1 < nb)
    def _():
        runs(b + 1, 1 - slot, False)

    runs(b, slot, True)
    y = ybuf[slot].astype(BF16)
    pc, rt = pc_ref[...], rt_ref[...]
    lane = lax.broadcasted_iota(jnp.int32, (MOE_TB, BLOCK_LANES), 1)
    q0 = jnp.where(lane == pc[:, 0:1], 1.0, 0.0).astype(BF16)
    q1 = jnp.where(lane == pc[:, 1:2], 1.0, 0.0).astype(BF16)
    moe = rt[:, 2:3] * _dot(q0, y) + rt[:, 3:4] * _dot(q1, y)
    o_ref[...] = _rms(x_ref[...] + moe, g_ref[...])


def _combine(ys, meta, x, rt, g):
    T, D = x.shape
    grid_spec = pltpu.PrefetchScalarGridSpec(
        num_scalar_prefetch=3,
        grid=(T // MOE_TB,),
        in_specs=[pl.BlockSpec(memory_space=pl.ANY),
                  pl.BlockSpec((MOE_TB, D), lambda b, *_: (b, 0)),
                  pl.BlockSpec((MOE_TB, LANES), lambda b, *_: (b, 0)),
                  pl.BlockSpec((MOE_TB, TOP_K), lambda b, *_: (b, 0)),
                  pl.BlockSpec((1, D), lambda b, *_: (0, 0))],
        out_specs=pl.BlockSpec((MOE_TB, D), lambda b, *_: (b, 0)),
        scratch_shapes=[pltpu.VMEM((2, BLOCK_LANES, D), F32), pltpu.SemaphoreType.DMA((2,))],
    )
    return pl.pallas_call(
        _combine_body,
        grid_spec=grid_spec,
        out_shape=jax.ShapeDtypeStruct((T, D), F32),
        compiler_params=_params("arbitrary"),
        name="moe_combine",
    )(meta["loc_off"], meta["run_start"], meta["run_len"], ys, x, rt, meta["pos_c"], g.reshape(1, D))


def kernel(x, norm_mix_g, w_in, ret_gn_g, attn_sinks, conv_dw_w, conv_dw_b, conv_ln_g, conv_ln_b,
           conv_pw_w, w_out, norm_ffn_g, ffn_w_gate, ffn_w_up, ffn_w_down, moe_router, moe_w_gate,
           moe_w_up, moe_w_down, final_norm_g):
    B, S, D = x.shape
    T = B * S
    depth = w_in.shape[0]
    assert depth == 2 and ffn_w_gate.shape[0] == 1 and moe_router.shape[0] == 1, "dense layer then MoE layer"
    assert T % MOE_TB == 0 and moe_w_gate.shape[-1] % MOE_FC == 0

    for l in range(depth):
        rq, rk, rv, rg, sq, sk, sv, ch = _inproj(x, norm_mix_g[l], w_in[l].astype(BF16), tm=1024)
        y_ret_t = _retention(rq, rk, rv, rg, ret_gn_g[l], ts=2048)
        y_swa_t = _swa(sq, sk, sv, attn_sinks[l], tq=2048)
        y_conv = _conv(ch, conv_dw_w[l], conv_dw_b[l], conv_ln_g[l], conv_ln_b[l], conv_pw_w[l], ts=512, rb=64)
        flat = lambda a: a.reshape(T, a.shape[-1])
        j = l // 2
        if l % 2 == 0:
            x = _outproj_dense_ffn(y_ret_t, y_swa_t, flat(y_conv), w_out[l].astype(BF16), flat(x),
                                   norm_ffn_g[l], ffn_w_gate[j], ffn_w_up[j], ffn_w_down[j],
                                   tm=512, fc=512).reshape(B, S, D)
        else:
            x2, hn, rt, rl = _outproj_route(y_ret_t, y_swa_t, flat(y_conv), w_out[l].astype(BF16), flat(x),
                                            norm_ffn_g[l], moe_router[j])
            n_rows = TOP_K * T + (T // MOE_TB) * N_EXPERTS * RUN_ALIGN + N_EXPERTS * MOE_TM
            meta = _route_meta(rt, rl, n_rows // MOE_TM)
            xs = _dispatch(hn, meta, n_rows)
            ys = _moe_ffn(xs, meta, moe_w_gate[j], moe_w_up[j], moe_w_down[j])
            x = _combine(ys, meta, x2, rt, final_norm_g).reshape(B, S, D)
    return x
```

```python
import functools

import jax
import jax.numpy as jnp
from jax import lax
from jax.experimental import pallas as pl
from jax.experimental.pallas import tpu as pltpu

F32 = jnp.float32
BF16 = jnp.bfloat16

HEAD_DIM = 64
RET_HEADS = 4
RET_W = RET_HEADS * HEAD_DIM
SWA_HEADS = 8
SWA_KV_HEADS = 2
SWA_GROUP = SWA_HEADS // SWA_KV_HEADS
SWA_W = SWA_HEADS * HEAD_DIM
SWA_KV_W = SWA_KV_HEADS * HEAD_DIM
CONV_CH = 256
CONV_GROUPS = 4
CONV_WIDTH = 31
WINDOW = 128
BLOCK = 128
RET_CHUNK = 128
ROPE_THETA = 500000.0
ROPE_DIM = HEAD_DIM // 4
RET_ROPE_THETA = 10000.0
N_EXPERTS = 8
TOP_K = 2
EPS = 1e-6
NEG_INF = -1e30

LANES = 128
CONV_HALO = 32
VMEM_LIMIT = 56 * 1024 * 1024

O_RQ, O_RK, O_RV, O_RG = 0, RET_W, 2 * RET_W, 3 * RET_W
O_SQ = 4 * RET_W
O_SK = O_SQ + SWA_W
O_SV = O_SK + SWA_KV_W
O_CA = O_SV + SWA_KV_W
O_CG = O_CA + CONV_CH
D_IN = O_CG + CONV_CH


def _params(*sem, flags=None):
    return pltpu.CompilerParams(dimension_semantics=sem, vmem_limit_bytes=VMEM_LIMIT, flags=flags)


def _rms(x, g):
    return x * lax.rsqrt(jnp.mean(x * x, axis=-1, keepdims=True) + EPS) * g


def _silu(x):
    return x * jax.nn.sigmoid(x)


def _dot(a, b):
    return jnp.dot(a, b, preferred_element_type=F32)


def _dot_nt(a, b):
    return lax.dot_general(a, b, (((1,), (1,)), ((), ())), preferred_element_type=F32)


def _dot_tn(a, b):
    return lax.dot_general(a, b, (((0,), (0,)), ((), ())), preferred_element_type=F32)


def _rope_tables(seq, theta, rot_dim):
    half = rot_dim // 2
    inv = 1.0 / (theta ** (jnp.arange(half, dtype=F32) / half))
    ang = jnp.arange(seq, dtype=F32)[:, None] * inv[None, :]
    cos, sin = jnp.cos(ang), jnp.sin(ang)
    rest = HEAD_DIM - rot_dim
    c = jnp.concatenate([cos, cos, jnp.ones((seq, rest), F32)], axis=1)
    s = jnp.concatenate([-sin, sin, jnp.zeros((seq, rest), F32)], axis=1)
    reps = LANES // HEAD_DIM
    return jnp.tile(c, (1, reps)), jnp.tile(s, (1, reps))


def _inproj_body(x_ref, g_ref, w_ref, wgt_ref, rc_ref, rs_ref, sc_ref, ss_ref,
                 rq_ref, rk_ref, rv_ref, rg_ref, sq_ref, sk_ref, sv_ref, ch_ref, zbuf):
    h = _rms(x_ref[0], g_ref[...]).astype(BF16)
    lane = lax.broadcasted_iota(jnp.int32, (1, LANES), 1) % HEAD_DIM
    for a, b in ((O_RQ, O_RK), (O_RK, O_RV), (O_RV, O_RG), (O_SQ, O_SK), (O_SK, O_CA), (O_CA, O_CG), (O_CG, D_IN)):
        zbuf[:, a:b] = _dot(h, w_ref[:, a:b])

    def seg(a, b):
        return zbuf[:, a:b]

    def rope(v, c, s, half):
        up = pltpu.roll(v, LANES - half, axis=1)
        dn = pltpu.roll(v, half, axis=1)
        return v * c + jnp.where(lane < half, up, dn) * s

    def store_heads(z, o_ref, first=0):
        for j in range(z.shape[1] // HEAD_DIM):
            o_ref[0, first + j] = z[:, j * HEAD_DIM:(j + 1) * HEAD_DIM]

    def rope_store(z, o_ref, c, s, half, scale):
        for i in range(z.shape[1] // LANES):
            r = rope(z[:, i * LANES:(i + 1) * LANES], c, s, half)
            if scale != 1.0:
                r = r * scale
            store_heads(r.astype(BF16), o_ref, first=i * (LANES // HEAD_DIM))

    rc, rs = rc_ref[...], rs_ref[...]
    sc, ss = sc_ref[...], ss_ref[...]
    scale = HEAD_DIM ** -0.5
    rope_store(seg(O_RQ, O_RK), rq_ref, rc, rs, HEAD_DIM // 2, 1.0)
    rope_store(seg(O_RK, O_RV), rk_ref, rc, rs, HEAD_DIM // 2, scale)
    store_heads(seg(O_RV, O_RG).astype(BF16), rv_ref)
    rg_ref[0] = _dot_nt(wgt_ref[...], h)
    rope_store(seg(O_SQ, O_SK), sq_ref, sc, ss, ROPE_DIM // 2, scale)
    rope_store(seg(O_SK, O_SV), sk_ref, sc, ss, ROPE_DIM // 2, 1.0)
    store_heads(seg(O_SV, O_CA).astype(BF16), sv_ref)
    ch_ref[0] = seg(O_CA, O_CG) * jax.nn.sigmoid(seg(O_CG, D_IN))


def _inproj(x, g, w_bf, tm):
    B, S, D = x.shape
    tm = min(tm, S)
    rc, rs = _rope_tables(S, RET_ROPE_THETA, HEAD_DIM)
    sc, ss = _rope_tables(S, ROPE_THETA, ROPE_DIM)
    tab = pl.BlockSpec((tm, LANES), lambda s, b: (s, 0))

    def out(width, dtype):
        return (jax.ShapeDtypeStruct((B, S, width), dtype),
                pl.BlockSpec((1, tm, width), lambda s, b: (b, s, 0)))

    def out_heads(heads):
        return (jax.ShapeDtypeStruct((B, heads, S, HEAD_DIM), BF16),
                pl.BlockSpec((1, heads, tm, HEAD_DIM), lambda s, b: (b, 0, s, 0)))

    gate_t = (jax.ShapeDtypeStruct((B, RET_W, S), F32), pl.BlockSpec((1, RET_W, tm), lambda s, b: (b, 0, s)))
    outs = [out_heads(RET_HEADS), out_heads(RET_HEADS), out_heads(RET_HEADS), gate_t,
            out_heads(SWA_HEADS), out_heads(SWA_KV_HEADS), out_heads(SWA_KV_HEADS), out(CONV_CH, F32)]
    return pl.pallas_call(
        _inproj_body,
        grid=(S // tm, B),
        in_specs=[pl.BlockSpec((1, tm, D), lambda s, b: (b, s, 0)),
                  pl.BlockSpec((1, D), lambda s, b: (0, 0)),
                  pl.BlockSpec((D, D_IN), lambda s, b: (0, 0)),
                  pl.BlockSpec((RET_W, D), lambda s, b: (0, 0)),
                  tab, tab, tab, tab],
        out_specs=[o[1] for o in outs],
        out_shape=[o[0] for o in outs],
        scratch_shapes=[pltpu.VMEM((tm, D_IN), F32)],
        compiler_params=_params("arbitrary", "arbitrary"),
        name="inproj",
    )(x, g.reshape(1, D), w_bf, w_bf[:, O_RG:O_SQ].T, rc, rs, sc, ss)


def _ret_body(q_ref, k_ref, v_ref, g_ref, dm_ref, kd_ref, qd_ref, cd_ref, gn_ref, o_ref, st_ref, *, ts):
    @pl.when(pl.program_id(2) == 0)
    def _():
        st_ref[...] = jnp.zeros_like(st_ref)

    C = RET_CHUNK
    for c in range(ts // C):
        rows = slice(c * C, (c + 1) * C)
        q, k, v = q_ref[0, 0, rows, :], k_ref[0, 0, rows, :], v_ref[0, 0, rows, :]
        st = st_ref[...]
        scores = _dot_nt(k, q) * dm_ref[0]
        intra = _dot_tn(v, scores.astype(BF16))
        cross = _dot_nt(st.astype(BF16), q) * qd_ref[0]
        kdec = (k.astype(F32) * kd_ref[0]).astype(BF16)
        st_ref[...] = st * cd_ref[0] + _dot_tn(v, kdec)
        o = intra + cross
        mu = jnp.mean(o, axis=0, keepdims=True)
        d = o - mu
        var = jnp.mean(d * d, axis=0, keepdims=True)
        on = d * lax.rsqrt(var + EPS) * gn_ref[0]
        o_ref[0, :, rows] = (_silu(g_ref[0, :, rows]) * on).astype(BF16)


def _retention(rq, rk, rv, rg_t, gn_g, ts):
    B, H, S, D = rq.shape
    ts = min(ts, S)
    C, W = RET_CHUNK, RET_W
    lg = jnp.log(1.0 - 2.0 ** (-5.0 - jnp.arange(H, dtype=F32)))
    idx = jnp.arange(C)
    rel = idx[:, None] - idx[None, :]
    dmask = jnp.where(rel[None] >= 0,
                      jnp.exp(jnp.maximum(rel, 0)[None].astype(F32) * lg[:, None, None]), 0.0)
    k_decay = jnp.exp((C - 1 - idx)[:, None].astype(F32) * lg[None, :])
    q_decay = jnp.exp((idx + 1)[:, None].astype(F32) * lg[None, :])
    chunk_decay = jnp.exp(C * lg)
    dm_t = jnp.swapaxes(dmask, 1, 2)
    kd = jnp.broadcast_to(k_decay.T[:, :, None], (H, C, D))
    qd = jnp.broadcast_to(q_decay.T[:, None, :], (H, D, C))
    cd = jnp.broadcast_to(chunk_decay[:, None, None], (H, D, D))
    gn = jnp.broadcast_to(gn_g.reshape(H, D, 1), (H, D, C))
    heads = pl.BlockSpec((1, 1, ts, D), lambda b, h, n: (b, h, n, 0))
    chan = pl.BlockSpec((1, D, ts), lambda b, h, n: (b, h, n))

    def per_head(r, c):
        return pl.BlockSpec((1, r, c), lambda b, h, n: (h, 0, 0))

    return pl.pallas_call(
        functools.partial(_ret_body, ts=ts),
        grid=(B, H, S // ts),
        in_specs=[heads, heads, heads, chan, per_head(C, C), per_head(C, D), per_head(D, C),
                  per_head(D, D), per_head(D, C)],
        out_specs=chan,
        out_shape=jax.ShapeDtypeStruct((B, W, S), BF16),
        scratch_shapes=[pltpu.VMEM((D, D), F32)],
        compiler_params=_params("arbitrary", "arbitrary", "arbitrary"),
        name="retention",
    )(rq, rk, rv, rg_t, dm_t, kd, qd, cd, gn)


def _swa_body(sink_ref, q_ref, kc_ref, kp_ref, vc_ref, vp_ref, bias_ref, o_ref, kcat, vcat, sbuf, pbuf, rbuf,
              *, tq):
    hk, n = pl.program_id(1), pl.program_id(2)
    kcat[0:BLOCK], kcat[BLOCK:] = kp_ref[0, 0], kc_ref[0, 0]
    vcat[0:BLOCK], vcat[BLOCK:] = vp_ref[0, 0], vc_ref[0, 0]
    width = SWA_GROUP * BLOCK
    group = lax.broadcasted_iota(jnp.int32, (1, width), 1) // BLOCK
    sink = jnp.zeros((1, width), F32)
    for g in range(SWA_GROUP):
        sink = jnp.where(group == g, sink_ref[hk * SWA_GROUP + g], sink)
    nq = tq // BLOCK
    for j in range(nq):
        q = q_ref[0, :, j * BLOCK:(j + 1) * BLOCK, :].reshape(width, HEAD_DIM)
        kb = kcat[j * BLOCK:(j + 2) * BLOCK]
        sbuf[j] = _dot_nt(kb, q)
    for j in range(nq):
        s = sbuf[j] + (bias_ref[0] if j > 0 else bias_ref[jnp.where(n == 0, 1, 0)])
        m = jnp.maximum(jnp.max(s, axis=0, keepdims=True), sink)
        p = jnp.exp(s - m)
        rbuf[j] = 1.0 / (jnp.sum(p, axis=0, keepdims=True) + jnp.exp(sink - m))
        pbuf[j] = p.astype(BF16)
    for j in range(nq):
        vb = vcat[j * BLOCK:(j + 2) * BLOCK]
        o = _dot_tn(vb, pbuf[j]) * rbuf[j]
        for g in range(SWA_GROUP):
            o_ref[0, g * HEAD_DIM:(g + 1) * HEAD_DIM, j * BLOCK:(j + 1) * BLOCK] = (
                o[:, g * BLOCK:(g + 1) * BLOCK].astype(BF16))


def _swa(sq, sk, sv, sinks, tq):
    B, _, S, D = sq.shape
    tq = min(tq, S)
    r = tq // BLOCK
    qi = jnp.arange(BLOCK)[None, :] + BLOCK
    kj = jnp.arange(2 * BLOCK)[:, None]
    rel = qi - kj
    allowed = (rel >= 0) & (rel < WINDOW)
    allowed = jnp.stack([allowed, allowed & (kj >= BLOCK)])
    bias = jnp.tile(jnp.where(allowed, 0.0, NEG_INF).astype(F32), (1, 1, SWA_GROUP))
    nq, width = tq // BLOCK, SWA_GROUP * BLOCK
    cur = pl.BlockSpec((1, 1, tq, D), lambda b, h, n: (b, h, n, 0))
    prev = pl.BlockSpec((1, 1, BLOCK, D), lambda b, h, n: (b, h, jnp.maximum(n * r - 1, 0), 0))
    return pl.pallas_call(
        functools.partial(_swa_body, tq=tq),
        grid=(B, SWA_KV_HEADS, S // tq),
        in_specs=[pl.BlockSpec(memory_space=pltpu.SMEM),
                  pl.BlockSpec((1, SWA_GROUP, tq, D), lambda b, h, n: (b, h, n, 0)),
                  cur, prev, cur, prev,
                  pl.BlockSpec((2, 2 * BLOCK, width), lambda b, h, n: (0, 0, 0))],
        out_specs=pl.BlockSpec((1, SWA_GROUP * D, tq), lambda b, h, n: (b, h, n)),
        out_shape=jax.ShapeDtypeStruct((B, SWA_W, S), BF16),
        scratch_shapes=[pltpu.VMEM((BLOCK + tq, D), BF16), pltpu.VMEM((BLOCK + tq, D), BF16),
                        pltpu.VMEM((nq, 2 * BLOCK, width), F32), pltpu.VMEM((nq, 2 * BLOCK, width), BF16),
                        pltpu.VMEM((nq, 1, width), F32)],
        compiler_params=_params("arbitrary", "arbitrary", "arbitrary"),
        name="swa",
    )(sinks.astype(F32), sq, sk, sk, sv, sv, bias)


def _conv_body(h_ref, dw_ref, db_ref, lg_ref, lb_ref, pw_ref, o_ref, hbuf, hsh, *, ts, rb):
    s = pl.program_id(1)
    sub = 8
    span = ts + CONV_HALO - sub

    @pl.when(s == 0)
    def _():
        hbuf[0:CONV_HALO, :] = jnp.zeros((CONV_HALO, CONV_CH), F32)

    @pl.when(s > 0)
    def _():
        hbuf[0:CONV_HALO, :] = hbuf[ts:ts + CONV_HALO, :]

    hbuf[CONV_HALO:CONV_HALO + ts, :] = h_ref[0]
    for p in range(1, sub):
        hsh[p - 1] = hbuf[p:p + span, :]
    off = CONV_HALO - (CONV_WIDTH - 1)
    for r in range(ts // rb):
        base = r * rb
        acc = jnp.broadcast_to(db_ref[...], (rb, CONV_CH))
        for w in range(CONV_WIDTH):
            q, p = divmod(off + w, sub)
            lo = base + q * sub
            tap = hbuf[lo:lo + rb, :] if p == 0 else hsh[p - 1, lo:lo + rb, :]
            acc = acc + tap * dw_ref[w:w + 1, :]
        mu = jnp.mean(acc, axis=-1, keepdims=True)
        d = acc - mu
        var = jnp.mean(d * d, axis=-1, keepdims=True)
        hn = d * lax.rsqrt(var + EPS) * lg_ref[...] + lb_ref[...]
        o_ref[0, base:base + rb, :] = _dot(_silu(hn).astype(BF16), pw_ref[...]).astype(BF16)


def _conv(ch, dw_w, dw_b, ln_g, ln_b, pw_w, ts, rb):
    B, S, W = ch.shape
    ts = min(ts, S)
    rb = min(rb, ts)
    pw = jnp.zeros((W, W), F32)
    gd = W // CONV_GROUPS
    for g in range(CONV_GROUPS):
        pw = pw.at[g * gd:(g + 1) * gd, g * gd:(g + 1) * gd].set(pw_w[g])
    act = pl.BlockSpec((1, ts, W), lambda b, s: (b, s, 0))

    def const(shape):
        return pl.BlockSpec(shape, lambda b, s: (0,) * len(shape))

    return pl.pallas_call(
        functools.partial(_conv_body, ts=ts, rb=rb),
        grid=(B, S // ts),
        in_specs=[act, const((CONV_WIDTH, W)), const((1, W)), const((1, W)), const((1, W)), const((W, W))],
        out_specs=act,
        out_shape=jax.ShapeDtypeStruct((B, S, W), BF16),
        scratch_shapes=[pltpu.VMEM((CONV_HALO + ts, W), F32), pltpu.VMEM((7, CONV_HALO + ts - 8, W), F32)],
        compiler_params=_params("arbitrary", "arbitrary"),
        name="conv",
    )(ch, dw_w, dw_b.reshape(1, W), ln_g.reshape(1, W), ln_b.reshape(1, W), pw.astype(BF16))


def _mixer_residual(yr_ref, ys_ref, yc_ref, w_ref, x_ref):
    a, b = RET_W, RET_W + SWA_W
    return x_ref[...] + (_dot_tn(yr_ref[0], w_ref[0:a, :]) + _dot_tn(ys_ref[0], w_ref[a:b, :])
                         + _dot(yc_ref[...], w_ref[b:, :]))


def _mixer_specs(tm, tiles_per_seq):
    def chan(width):
        return pl.BlockSpec((1, width, tm), lambda i: (i // tiles_per_seq, 0, i % tiles_per_seq))

    return [chan(RET_W), chan(SWA_W), pl.BlockSpec((tm, CONV_CH), lambda i: (i, 0))]


def _outproj_body(yr_ref, ys_ref, yc_ref, w_ref, x_ref, g_ref, wr_ref, before_ref, xo_ref, hn_ref, rt_ref, rl_ref):
    x = _mixer_residual(yr_ref, ys_ref, yc_ref, w_ref, x_ref)
    xo_ref[...] = x
    h = _rms(x, g_ref[...]).astype(BF16)
    hn_ref[...] = h
    logits = _dot(h, wr_ref[...])
    lane = lax.broadcasted_iota(jnp.int32, logits.shape, 1)
    lg = jnp.where(lane < N_EXPERTS, logits, -jnp.inf)
    m1 = jnp.max(lg, axis=-1, keepdims=True)
    i1 = jnp.min(jnp.where(lg == m1, lane, LANES), axis=-1, keepdims=True)
    lg2 = jnp.where(lane == i1, -jnp.inf, lg)
    m2 = jnp.max(lg2, axis=-1, keepdims=True)
    i2 = jnp.min(jnp.where(lg2 == m2, lane, LANES), axis=-1, keepdims=True)
    e = jnp.exp(m2 - m1)
    w1 = 1.0 / (1.0 + e)
    w2 = e / (1.0 + e)
    hit1, hit2 = lane == i1, lane == i2
    onehot = jnp.where(jnp.logical_or(hit1, hit2), 1.0, 0.0)
    rank = _dot(before_ref[...], onehot.astype(BF16))
    count = jnp.sum(onehot, axis=0, keepdims=True).astype(jnp.int32)
    run_len = (count + (RUN_ALIGN - 1)) & ~(RUN_ALIGN - 1)
    lower = (lax.broadcasted_iota(jnp.int32, (LANES, LANES), 0) < lax.broadcasted_iota(jnp.int32, (LANES, LANES), 1))
    run_rows = jnp.broadcast_to(run_len.astype(F32), (8, LANES)).astype(BF16)
    loc_off = _dot(run_rows, jnp.where(lower, 1.0, 0.0).astype(BF16))[0:1, :]
    pos = rank + loc_off
    p1 = jnp.sum(jnp.where(hit1, pos, 0.0), axis=-1, keepdims=True)
    p2 = jnp.sum(jnp.where(hit2, pos, 0.0), axis=-1, keepdims=True)
    cols = (i1.astype(F32), i2.astype(F32), w1, w2, p1, p2)
    rt = jnp.zeros_like(logits)
    for c, v in enumerate(cols):
        rt = jnp.where(lane == c, v, rt)
    rt_ref[...] = rt
    rl_ref[0] = jnp.broadcast_to(run_len, (8, LANES))


def _outproj_route(yr, ys, yc, w_bf, x, g, w_router):
    T, D = x.shape
    tm = MOE_TB
    S = ys.shape[2]

    def row(width):
        return pl.BlockSpec((tm, width), lambda i: (i, 0))

    def const(shape):
        return pl.BlockSpec(shape, lambda i: (0,) * len(shape))

    wr = jnp.zeros((D, LANES), F32).at[:, :N_EXPERTS].set(w_router).astype(BF16)
    before = (jnp.arange(tm)[:, None] > jnp.arange(tm)[None, :]).astype(BF16)
    return pl.pallas_call(
        _outproj_body,
        grid=(T // tm,),
        in_specs=_mixer_specs(tm, S // tm) + [const((D, D)), row(D), const((1, D)), const((D, LANES)),
                                              const((tm, tm))],
        out_specs=[row(D), row(D), row(LANES), pl.BlockSpec((1, 8, LANES), lambda i: (i, 0, 0))],
        out_shape=[jax.ShapeDtypeStruct((T, D), F32), jax.ShapeDtypeStruct((T, D), BF16),
                   jax.ShapeDtypeStruct((T, LANES), F32), jax.ShapeDtypeStruct((T // tm, 8, LANES), jnp.int32)],
        compiler_params=_params("arbitrary"),
        name="outproj_route",
    )(yr, ys, yc, w_bf, x, g.reshape(1, D), wr, before)


def _dense_ffn_body(yr_ref, ys_ref, yc_ref, wo_ref, x_ref, g_ref, wg_ref, wu_ref, wd_ref, o_ref, *, chunks):
    x = _mixer_residual(yr_ref, ys_ref, yc_ref, wo_ref, x_ref)
    h = _rms(x, g_ref[...]).astype(BF16)
    acc = x
    for a, b in chunks:
        g = _dot(h, wg_ref[:, a:b])
        u = _dot(h, wu_ref[:, a:b])
        acc = acc + _dot((_silu(g) * u).astype(BF16), wd_ref[a:b, :])
    o_ref[...] = acc


def _outproj_dense_ffn(yr, ys, yc, wo_bf, x, g, wg, wu, wd, tm, fc):
    T, D = x.shape
    FF = wg.shape[1]
    S = ys.shape[2]
    tm = min(tm, S)
    chunks = tuple((a, min(a + fc, FF)) for a in range(0, FF, fc))

    def row(width):
        return pl.BlockSpec((tm, width), lambda i: (i, 0))

    def const(shape):
        return pl.BlockSpec(shape, lambda i: (0,) * len(shape), pipeline_mode=pl.Buffered(1))

    return pl.pallas_call(
        functools.partial(_dense_ffn_body, chunks=chunks),
        grid=(T // tm,),
        in_specs=_mixer_specs(tm, S // tm) + [const((D, D)), row(D), const((1, D)),
                                              const((D, FF)), const((D, FF)), const((FF, D))],
        out_specs=row(D),
        out_shape=jax.ShapeDtypeStruct((T, D), F32),
        compiler_params=_params("arbitrary"),
        name="outproj_dense_ffn",
    )(yr, ys, yc, wo_bf, x, g.reshape(1, D), wg.astype(BF16), wu.astype(BF16), wd.astype(BF16))


MOE_TB = 512
MOE_TM = 512
RUN_ALIGN = 8
RUN_SIZES = (512, 256, 128, 64, 32, 16, 8)
BLOCK_ROWS = TOP_K * MOE_TB + N_EXPERTS * RUN_ALIGN
BLOCK_LANES = 1152
MOE_FC = 512


def _route_meta(rt, rl, n_tiles):
    T = rt.shape[0]
    nb = T // MOE_TB
    pos = rt[:, 4:4 + TOP_K].astype(jnp.int32)
    run_len = rl[:, 0, :N_EXPERTS]
    loc_off = jnp.cumsum(run_len, axis=1) - run_len
    group = jnp.sum(run_len, axis=0)
    ptiles = (group + MOE_TM - 1) // MOE_TM
    tile_end = jnp.cumsum(ptiles)
    gstart = (tile_end - ptiles) * MOE_TM
    run_start = gstart[None, :] + jnp.cumsum(run_len, axis=0) - run_len
    tile_expert = jnp.sum(jnp.arange(n_tiles, dtype=jnp.int32)[:, None] >= tile_end[None, :], axis=1)
    i32 = lambda a: a.astype(jnp.int32)
    pos3 = pos.reshape(nb, MOE_TB, TOP_K)
    return dict(
        pos_l=i32(jnp.swapaxes(pos3, 1, 2)),
        pos_c=i32(pos3.reshape(T, TOP_K)),
        loc_off=i32(loc_off.reshape(-1)), run_start=i32(run_start.reshape(-1)), run_len=i32(run_len.reshape(-1)),
        pad_start=i32(gstart + group), pad_len=i32(ptiles * MOE_TM - group),
        tile_expert=i32(jnp.minimum(tile_expert, N_EXPERTS - 1)), n_used=i32(tile_end[-1:]))


def _run_dma(src, dst, src_off, dst_off, length, sem, wait):
    off = 0
    for k in RUN_SIZES:
        part = length & k

        @pl.when(part != 0)
        def _(off=off, k=k):
            cp = pltpu.make_async_copy(src.at[pl.ds(pl.multiple_of(src_off + off, RUN_ALIGN), k)],
                                       dst.at[pl.ds(pl.multiple_of(dst_off + off, RUN_ALIGN), k)], sem)
            if wait:
                cp.wait()
            else:
                cp.start()

        off = off + part


def _dispatch_body(lo_ref, rs_ref, rl_ref, ps_ref, pn_ref, nu_ref, pos_ref, h_ref, xs_hbm, sbuf, zbuf, sem):
    b, nb = pl.program_id(0), pl.num_programs(0)
    slot = b % 2
    n_tiles = xs_hbm.shape[0] // MOE_TM

    def zero_tile(i, wait):
        cp = pltpu.make_async_copy(zbuf, xs_hbm.at[pl.ds(pl.multiple_of(i * MOE_TM, MOE_TM), MOE_TM)], sem.at[2])
        if wait:
            cp.wait()
        else:
            cp.start()

    def runs(blk, s, wait):
        for e in range(N_EXPERTS):
            j = blk * N_EXPERTS + e
            _run_dma(sbuf.at[s], xs_hbm, lo_ref[j], rs_ref[j], rl_ref[j], sem.at[s], wait)

    @pl.when(b >= 2)
    def _():
        runs(b - 2, slot, True)

    pos = pos_ref[0]
    r = lax.broadcasted_iota(jnp.int32, (BLOCK_ROWS, MOE_TB), 0)
    onehot = jnp.logical_or(r == pos[0:1, :], r == pos[1:2, :])
    sbuf[slot] = _dot(jnp.where(onehot, 1.0, 0.0).astype(BF16), h_ref[...])
    runs(b, slot, False)

    @pl.when(b == nb - 1)
    def _():
        zbuf[...] = jnp.zeros_like(zbuf)
        for e in range(N_EXPERTS):
            _run_dma(zbuf, xs_hbm, 0, ps_ref[e], pn_ref[e], sem.at[2], False)
        lax.fori_loop(nu_ref[0], n_tiles, lambda i, c: zero_tile(i, False), None)
        for e in range(N_EXPERTS):
            _run_dma(zbuf, xs_hbm, 0, ps_ref[e], pn_ref[e], sem.at[2], True)
        lax.fori_loop(nu_ref[0], n_tiles, lambda i, c: zero_tile(i, True), None)

        @pl.when(b >= 1)
        def _():
            runs(b - 1, 1 - slot, True)

        runs(b, slot, True)


def _dispatch(hn, meta, n_rows):
    T, D = hn.shape
    nb = T // MOE_TB
    grid_spec = pltpu.PrefetchScalarGridSpec(
        num_scalar_prefetch=6,
        grid=(nb,),
        in_specs=[pl.BlockSpec((1, TOP_K, MOE_TB), lambda b, *_: (b, 0, 0)),
                  pl.BlockSpec((MOE_TB, D), lambda b, *_: (b, 0))],
        out_specs=pl.BlockSpec(memory_space=pl.ANY),
        scratch_shapes=[pltpu.VMEM((2, BLOCK_ROWS, D), F32), pltpu.VMEM((MOE_TM, D), F32),
                        pltpu.SemaphoreType.DMA((3,))],
    )
    return pl.pallas_call(
        _dispatch_body,
        grid_spec=grid_spec,
        out_shape=jax.ShapeDtypeStruct((n_rows, D), F32),
        compiler_params=pltpu.CompilerParams(dimension_semantics=("arbitrary",), vmem_limit_bytes=VMEM_LIMIT,
                                             has_side_effects=True),
        name="moe_dispatch",
    )(meta["loc_off"], meta["run_start"], meta["run_len"], meta["pad_start"], meta["pad_len"], meta["n_used"],
      meta["pos_l"], hn)


def _moe_ffn_body(te_ref, nu_ref, x_ref, wg_hbm, wu_hbm, wd_hbm, o_ref, wg_s, wu_s, wd_s, st_a, st_d, sem):
    i = pl.program_id(0)
    e = te_ref[i]
    active = i < nu_ref[0]
    nf = wg_s.shape[1] // MOE_FC
    fsl = [slice(f * MOE_FC, (f + 1) * MOE_FC) for f in range(nf)]

    @pl.when(jnp.logical_and(active, jnp.logical_or(i == 0, e != te_ref[jnp.maximum(i - 1, 0)])))
    def _():
        jobs = ([(wg_hbm, wg_s, True, f) for f in range(nf)] + [(wu_hbm, wu_s, True, f) for f in range(nf)]
                + [(wd_hbm, wd_s, False, f) for f in range(nf)])

        def copy(j):
            w_hbm, _, cols, f = jobs[j]
            if cols:
                return pltpu.make_async_copy(w_hbm.at[e, :, fsl[f]], st_a.at[j % 2], sem.at[j % 2])
            return pltpu.make_async_copy(w_hbm.at[e, fsl[f], :], st_d.at[j % 2], sem.at[j % 2])

        copy(0).start()
        for j, (_, w_s, cols, f) in enumerate(jobs):
            if j + 1 < len(jobs):
                copy(j + 1).start()
            copy(j).wait()
            if cols:
                w_s[:, fsl[f]] = st_a[j % 2].astype(BF16)
            else:
                w_s[fsl[f], :] = st_d[j % 2].astype(BF16)

    @pl.when(active)
    def _():
        x = x_ref[...].astype(BF16)
        acc = jnp.zeros(o_ref.shape, F32)
        for sl in fsl:
            g = _dot(x, wg_s[:, sl])
            u = _dot(x, wu_s[:, sl])
            acc = acc + _dot((_silu(g) * u).astype(BF16), wd_s[sl, :])
        o_ref[...] = acc

    @pl.when(jnp.logical_not(active))
    def _():
        o_ref[...] = jnp.zeros_like(o_ref)


def _moe_ffn(xs, meta, wg, wu, wd):
    R, D = xs.shape
    E, _, FF = wg.shape
    grid_spec = pltpu.PrefetchScalarGridSpec(
        num_scalar_prefetch=2,
        grid=(R // MOE_TM,),
        in_specs=[pl.BlockSpec((MOE_TM, D), lambda i, te, nu: (jnp.minimum(i, nu[0] - 1), 0)),
                  pl.BlockSpec(memory_space=pl.ANY), pl.BlockSpec(memory_space=pl.ANY),
                  pl.BlockSpec(memory_space=pl.ANY)],
        out_specs=pl.BlockSpec((MOE_TM, D), lambda i, te, nu: (i, 0)),
        scratch_shapes=[pltpu.VMEM((D, FF), BF16), pltpu.VMEM((D, FF), BF16), pltpu.VMEM((FF, D), BF16),
                        pltpu.VMEM((2, D, MOE_FC), F32), pltpu.VMEM((2, MOE_FC, D), F32),
                        pltpu.SemaphoreType.DMA((2,))],
    )
    return pl.pallas_call(
        _moe_ffn_body,
        grid_spec=grid_spec,
        out_shape=jax.ShapeDtypeStruct((R, D), F32),
        compiler_params=_params("arbitrary"),
        name="moe_ffn",
    )(meta["tile_expert"], meta["n_used"], xs, wg, wu, wd)


def _combine_body(lo_ref, rs_ref, rl_ref, ys_hbm, x_ref, rt_ref, pc_ref, g_ref, o_ref, ybuf, sem):
    b, nb = pl.program_id(0), pl.num_programs(0)
    slot = b % 2

    def runs(blk, s, wait):
        for e in range(N_EXPERTS):
            j = blk * N_EXPERTS + e
            _run_dma(ys_hbm, ybuf.at[s], rs_ref[j], lo_ref[j], rl_ref[j], sem.at[s], wait)

    @pl.when(b == 0)
    def _():
        ybuf[...] = jnp.zeros_like(ybuf)
        runs(0, 0, False)

    @pl.when(b + 1 < nb)
    def _():
        runs(b + 1, 1 - slot, False)

    runs(b, slot, True)
    y = ybuf[slot].astype(BF16)
    pc, rt = pc_ref[...], rt_ref[...]
    lane = lax.broadcasted_iota(jnp.int32, (MOE_TB, BLOCK_LANES), 1)
    q0 = jnp.where(lane == pc[:, 0:1], 1.0, 0.0).astype(BF16)
    q1 = jnp.where(lane == pc[:, 1:2], 1.0, 0.0).astype(BF16)
    moe = rt[:, 2:3] * _dot(q0, y) + rt[:, 3:4] * _dot(q1, y)
    o_ref[...] = _rms(x_ref[...] + moe, g_ref[...])


def _combine(ys, meta, x, rt, g):
    T, D = x.shape
    grid_spec = pltpu.PrefetchScalarGridSpec(
        num_scalar_prefetch=3,
        grid=(T // MOE_TB,),
        in_specs=[pl.BlockSpec(memory_space=pl.ANY),
                  pl.BlockSpec((MOE_TB, D), lambda b, *_: (b, 0)),
                  pl.BlockSpec((MOE_TB, LANES), lambda b, *_: (b, 0)),
                  pl.BlockSpec((MOE_TB, TOP_K), lambda b, *_: (b, 0)),
                  pl.BlockSpec((1, D), lambda b, *_: (0, 0))],
        out_specs=pl.BlockSpec((MOE_TB, D), lambda b, *_: (b, 0)),
        scratch_shapes=[pltpu.VMEM((2, BLOCK_LANES, D), F32), pltpu.SemaphoreType.DMA((2,))],
    )
    return pl.pallas_call(
        _combine_body,
        grid_spec=grid_spec,
        out_shape=jax.ShapeDtypeStruct((T, D), F32),
        compiler_params=_params("arbitrary"),
        name="moe_combine",
    )(meta["loc_off"], meta["run_start"], meta["run_len"], ys, x, rt, meta["pos_c"], g.reshape(1, D))


def kernel(x, norm_mix_g, w_in, ret_gn_g, attn_sinks, conv_dw_w, conv_dw_b, conv_ln_g, conv_ln_b,
           conv_pw_w, w_out, norm_ffn_g, ffn_w_gate, ffn_w_up, ffn_w_down, moe_router, moe_w_gate,
           moe_w_up, moe_w_down, final_norm_g):
    B, S, D = x.shape
    T = B * S
    depth = w_in.shape[0]
    assert depth == 2 and ffn_w_gate.shape[0] == 1 and moe_router.shape[0] == 1, "dense layer then MoE layer"
    assert T % MOE_TB == 0 and moe_w_gate.shape[-1] % MOE_FC == 0

    for l in range(depth):
        rq, rk, rv, rg, sq, sk, sv, ch = _inproj(x, norm_mix_g[l], w_in[l].astype(BF16), tm=1024)
        y_ret_t = _retention(rq, rk, rv, rg, ret_gn_g[l], ts=4096)
        y_swa_t = _swa(sq, sk, sv, attn_sinks[l], tq=2048)
        y_conv = _conv(ch, conv_dw_w[l], conv_dw_b[l], conv_ln_g[l], conv_ln_b[l], conv_pw_w[l], ts=1024, rb=64)
        flat = lambda a: a.reshape(T, a.shape[-1])
        j = l // 2
        if l % 2 == 0:
            x = _outproj_dense_ffn(y_ret_t, y_swa_t, flat(y_conv), w_out[l].astype(BF16), flat(x),
                                   norm_ffn_g[l], ffn_w_gate[j], ffn_w_up[j], ffn_w_down[j],
                                   tm=512, fc=512).reshape(B, S, D)
        else:
            x2, hn, rt, rl = _outproj_route(y_ret_t, y_swa_t, flat(y_conv), w_out[l].astype(BF16), flat(x),
                                            norm_ffn_g[l], moe_router[j])
            n_rows = TOP_K * T + (T // MOE_TB) * N_EXPERTS * RUN_ALIGN + N_EXPERTS * MOE_TM
            meta = _route_meta(rt, rl, n_rows // MOE_TM)
            xs = _dispatch(hn, meta, n_rows)
            ys = _moe_ffn(xs, meta, moe_w_gate[j], moe_w_up[j], moe_w_down[j])
            x = _combine(ys, meta, x2, rt, final_norm_g).reshape(B, S, D)
    return x
```

```python
import functools

import jax
import jax.numpy as jnp
from jax import lax
from jax.experimental import pallas as pl
from jax.experimental.pallas import tpu as pltpu

F32 = jnp.float32
BF16 = jnp.bfloat16

HEAD_DIM = 64
RET_HEADS = 4
RET_W = RET_HEADS * HEAD_DIM
SWA_HEADS = 8
SWA_KV_HEADS = 2
SWA_GROUP = SWA_HEADS // SWA_KV_HEADS
SWA_W = SWA_HEADS * HEAD_DIM
SWA_KV_W = SWA_KV_HEADS * HEAD_DIM
CONV_CH = 256
CONV_GROUPS = 4
CONV_WIDTH = 31
WINDOW = 128
BLOCK = 128
RET_CHUNK = 128
ROPE_THETA = 500000.0
ROPE_DIM = HEAD_DIM // 4
RET_ROPE_THETA = 10000.0
N_EXPERTS = 8
TOP_K = 2
EPS = 1e-6
NEG_INF = -1e30

LANES = 128
CONV_HALO = 32
VMEM_LIMIT = 56 * 1024 * 1024

O_RQ, O_RK, O_RV, O_RG = 0, RET_W, 2 * RET_W, 3 * RET_W
O_SQ = 4 * RET_W
O_SK = O_SQ + SWA_W
O_SV = O_SK + SWA_KV_W
O_CA = O_SV + SWA_KV_W
O_CG = O_CA + CONV_CH
D_IN = O_CG + CONV_CH


def _params(*sem, flags=None):
    return pltpu.CompilerParams(dimension_semantics=sem, vmem_limit_bytes=VMEM_LIMIT, flags=flags)


def _rms(x, g):
    return x * lax.rsqrt(jnp.mean(x * x, axis=-1, keepdims=True) + EPS) * g


def _silu(x):
    return x * jax.nn.sigmoid(x)


def _dot(a, b):
    return jnp.dot(a, b, preferred_element_type=F32)


def _dot_nt(a, b):
    return lax.dot_general(a, b, (((1,), (1,)), ((), ())), preferred_element_type=F32)


def _dot_tn(a, b):
    return lax.dot_general(a, b, (((0,), (0,)), ((), ())), preferred_element_type=F32)


def _rope_tables(seq, theta, rot_dim):
    half = rot_dim // 2
    inv = 1.0 / (theta ** (jnp.arange(half, dtype=F32) / half))
    ang = jnp.arange(seq, dtype=F32)[:, None] * inv[None, :]
    cos, sin = jnp.cos(ang), jnp.sin(ang)
    rest = HEAD_DIM - rot_dim
    c = jnp.concatenate([cos, cos, jnp.ones((seq, rest), F32)], axis=1)
    s = jnp.concatenate([-sin, sin, jnp.zeros((seq, rest), F32)], axis=1)
    reps = LANES // HEAD_DIM
    return jnp.tile(c, (1, reps)), jnp.tile(s, (1, reps))


def _inproj_body(x_ref, g_ref, w_ref, wgt_ref, rc_ref, rs_ref, sc_ref, ss_ref,
                 rq_ref, rk_ref, rv_ref, rg_ref, sq_ref, sk_ref, sv_ref, ch_ref, zbuf):
    h = _rms(x_ref[0], g_ref[...]).astype(BF16)
    lane = lax.broadcasted_iota(jnp.int32, (1, LANES), 1) % HEAD_DIM
    for a, b in ((O_RQ, O_RK), (O_RK, O_RV), (O_RV, O_RG), (O_SQ, O_SK), (O_SK, O_CA), (O_CA, O_CG), (O_CG, D_IN)):
        zbuf[:, a:b] = _dot(h, w_ref[:, a:b])

    def seg(a, b):
        return zbuf[:, a:b]

    def rope(v, c, s, half):
        up = pltpu.roll(v, LANES - half, axis=1)
        dn = pltpu.roll(v, half, axis=1)
        return v * c + jnp.where(lane < half, up, dn) * s

    def store_heads(z, o_ref, first=0):
        for j in range(z.shape[1] // HEAD_DIM):
            o_ref[0, first + j] = z[:, j * HEAD_DIM:(j + 1) * HEAD_DIM]

    def rope_store(z, o_ref, c, s, half, scale):
        for i in range(z.shape[1] // LANES):
            r = rope(z[:, i * LANES:(i + 1) * LANES], c, s, half)
            if scale != 1.0:
                r = r * scale
            store_heads(r.astype(BF16), o_ref, first=i * (LANES // HEAD_DIM))

    rc, rs = rc_ref[...], rs_ref[...]
    sc, ss = sc_ref[...], ss_ref[...]
    scale = HEAD_DIM ** -0.5
    rope_store(seg(O_RQ, O_RK), rq_ref, rc, rs, HEAD_DIM // 2, 1.0)
    rope_store(seg(O_RK, O_RV), rk_ref, rc, rs, HEAD_DIM // 2, scale)
    store_heads(seg(O_RV, O_RG).astype(BF16), rv_ref)
    rg_ref[0] = _dot_nt(wgt_ref[...], h)
    rope_store(seg(O_SQ, O_SK), sq_ref, sc, ss, ROPE_DIM // 2, scale)
    rope_store(seg(O_SK, O_SV), sk_ref, sc, ss, ROPE_DIM // 2, 1.0)
    store_heads(seg(O_SV, O_CA).astype(BF16), sv_ref)
    ch_ref[0] = seg(O_CA, O_CG) * jax.nn.sigmoid(seg(O_CG, D_IN))


def _inproj(x, g, w_bf, tm):
    B, S, D = x.shape
    tm = min(tm, S)
    rc, rs = _rope_tables(S, RET_ROPE_THETA, HEAD_DIM)
    sc, ss = _rope_tables(S, ROPE_THETA, ROPE_DIM)
    tab = pl.BlockSpec((tm, LANES), lambda s, b: (s, 0))

    def out(width, dtype):
        return (jax.ShapeDtypeStruct((B, S, width), dtype),
                pl.BlockSpec((1, tm, width), lambda s, b: (b, s, 0)))

    def out_heads(heads):
        return (jax.ShapeDtypeStruct((B, heads, S, HEAD_DIM), BF16),
                pl.BlockSpec((1, heads, tm, HEAD_DIM), lambda s, b: (b, 0, s, 0)))

    gate_t = (jax.ShapeDtypeStruct((B, RET_W, S), F32), pl.BlockSpec((1, RET_W, tm), lambda s, b: (b, 0, s)))
    outs = [out_heads(RET_HEADS), out_heads(RET_HEADS), out_heads(RET_HEADS), gate_t,
            out_heads(SWA_HEADS), out_heads(SWA_KV_HEADS), out_heads(SWA_KV_HEADS), out(CONV_CH, F32)]
    return pl.pallas_call(
        _inproj_body,
        grid=(S // tm, B),
        in_specs=[pl.BlockSpec((1, tm, D), lambda s, b: (b, s, 0)),
                  pl.BlockSpec((1, D), lambda s, b: (0, 0)),
                  pl.BlockSpec((D, D_IN), lambda s, b: (0, 0)),
                  pl.BlockSpec((RET_W, D), lambda s, b: (0, 0)),
                  tab, tab, tab, tab],
        out_specs=[o[1] for o in outs],
        out_shape=[o[0] for o in outs],
        scratch_shapes=[pltpu.VMEM((tm, D_IN), F32)],
        compiler_params=_params("arbitrary", "arbitrary"),
        name="inproj",
    )(x, g.reshape(1, D), w_bf, w_bf[:, O_RG:O_SQ].T, rc, rs, sc, ss)


def _ret_body(q_ref, k_ref, v_ref, g_ref, dm_ref, kd_ref, qd_ref, cd_ref, gn_ref, o_ref, st_ref, *, ts):
    @pl.when(pl.program_id(2) == 0)
    def _():
        st_ref[...] = jnp.zeros_like(st_ref)

    C = RET_CHUNK
    for c in range(ts // C):
        rows = slice(c * C, (c + 1) * C)
        q, k, v = q_ref[0, 0, rows, :], k_ref[0, 0, rows, :], v_ref[0, 0, rows, :]
        st = st_ref[...]
        scores = _dot_nt(k, q) * dm_ref[0]
        intra = _dot_tn(v, scores.astype(BF16))
        cross = _dot_nt(st.astype(BF16), q) * qd_ref[0]
        kdec = (k.astype(F32) * kd_ref[0]).astype(BF16)
        st_ref[...] = st * cd_ref[0] + _dot_tn(v, kdec)
        o = intra + cross
        mu = jnp.mean(o, axis=0, keepdims=True)
        d = o - mu
        var = jnp.mean(d * d, axis=0, keepdims=True)
        on = d * lax.rsqrt(var + EPS) * gn_ref[0]
        o_ref[0, :, rows] = (_silu(g_ref[0, :, rows]) * on).astype(BF16)


def _retention(rq, rk, rv, rg_t, gn_g, ts):
    B, H, S, D = rq.shape
    ts = min(ts, S)
    C, W = RET_CHUNK, RET_W
    lg = jnp.log(1.0 - 2.0 ** (-5.0 - jnp.arange(H, dtype=F32)))
    idx = jnp.arange(C)
    rel = idx[:, None] - idx[None, :]
    dmask = jnp.where(rel[None] >= 0,
                      jnp.exp(jnp.maximum(rel, 0)[None].astype(F32) * lg[:, None, None]), 0.0)
    k_decay = jnp.exp((C - 1 - idx)[:, None].astype(F32) * lg[None, :])
    q_decay = jnp.exp((idx + 1)[:, None].astype(F32) * lg[None, :])
    chunk_decay = jnp.exp(C * lg)
    dm_t = jnp.swapaxes(dmask, 1, 2)
    kd = jnp.broadcast_to(k_decay.T[:, :, None], (H, C, D))
    qd = jnp.broadcast_to(q_decay.T[:, None, :], (H, D, C))
    cd = jnp.broadcast_to(chunk_decay[:, None, None], (H, D, D))
    gn = jnp.broadcast_to(gn_g.reshape(H, D, 1), (H, D, C))
    heads = pl.BlockSpec((1, 1, ts, D), lambda b, h, n: (b, h, n, 0))
    chan = pl.BlockSpec((1, D, ts), lambda b, h, n: (b, h, n))

    def per_head(r, c):
        return pl.BlockSpec((1, r, c), lambda b, h, n: (h, 0, 0))

    return pl.pallas_call(
        functools.partial(_ret_body, ts=ts),
        grid=(B, H, S // ts),
        in_specs=[heads, heads, heads, chan, per_head(C, C), per_head(C, D), per_head(D, C),
                  per_head(D, D), per_head(D, C)],
        out_specs=chan,
        out_shape=jax.ShapeDtypeStruct((B, W, S), BF16),
        scratch_shapes=[pltpu.VMEM((D, D), F32)],
        compiler_params=_params("arbitrary", "arbitrary", "arbitrary"),
        name="retention",
    )(rq, rk, rv, rg_t, dm_t, kd, qd, cd, gn)


def _swa_body(sink_ref, q_ref, kc_ref, kp_ref, vc_ref, vp_ref, bias_ref, o_ref, kcat, vcat, sbuf, pbuf, rbuf,
              *, tq):
    hk, n = pl.program_id(1), pl.program_id(2)
    kcat[0:BLOCK], kcat[BLOCK:] = kp_ref[0, 0], kc_ref[0, 0]
    vcat[0:BLOCK], vcat[BLOCK:] = vp_ref[0, 0], vc_ref[0, 0]
    width = SWA_GROUP * BLOCK
    group = lax.broadcasted_iota(jnp.int32, (1, width), 1) // BLOCK
    sink = jnp.zeros((1, width), F32)
    for g in range(SWA_GROUP):
        sink = jnp.where(group == g, sink_ref[hk * SWA_GROUP + g], sink)
    nq = tq // BLOCK
    for j in range(nq):
        q = q_ref[0, :, j * BLOCK:(j + 1) * BLOCK, :].reshape(width, HEAD_DIM)
        kb = kcat[j * BLOCK:(j + 2) * BLOCK]
        sbuf[j] = _dot_nt(kb, q)
    for j in range(nq):
        s = sbuf[j] + (bias_ref[0] if j > 0 else bias_ref[jnp.where(n == 0, 1, 0)])
        m = jnp.maximum(jnp.max(s, axis=0, keepdims=True), sink)
        p = jnp.exp(s - m)
        rbuf[j] = 1.0 / (jnp.sum(p, axis=0, keepdims=True) + jnp.exp(sink - m))
        pbuf[j] = p.astype(BF16)
    for j in range(nq):
        vb = vcat[j * BLOCK:(j + 2) * BLOCK]
        o = _dot_tn(vb, pbuf[j]) * rbuf[j]
        for g in range(SWA_GROUP):
            o_ref[0, g * HEAD_DIM:(g + 1) * HEAD_DIM, j * BLOCK:(j + 1) * BLOCK] = (
                o[:, g * BLOCK:(g + 1) * BLOCK].astype(BF16))


def _swa(sq, sk, sv, sinks, tq):
    B, _, S, D = sq.shape
    tq = min(tq, S)
    r = tq // BLOCK
    qi = jnp.arange(BLOCK)[None, :] + BLOCK
    kj = jnp.arange(2 * BLOCK)[:, None]
    rel = qi - kj
    allowed = (rel >= 0) & (rel < WINDOW)
    allowed = jnp.stack([allowed, allowed & (kj >= BLOCK)])
    bias = jnp.tile(jnp.where(allowed, 0.0, NEG_INF).astype(F32), (1, 1, SWA_GROUP))
    nq, width = tq // BLOCK, SWA_GROUP * BLOCK
    cur = pl.BlockSpec((1, 1, tq, D), lambda b, h, n: (b, h, n, 0))
    prev = pl.BlockSpec((1, 1, BLOCK, D), lambda b, h, n: (b, h, jnp.maximum(n * r - 1, 0), 0))
    return pl.pallas_call(
        functools.partial(_swa_body, tq=tq),
        grid=(B, SWA_KV_HEADS, S // tq),
        in_specs=[pl.BlockSpec(memory_space=pltpu.SMEM),
                  pl.BlockSpec((1, SWA_GROUP, tq, D), lambda b, h, n: (b, h, n, 0)),
                  cur, prev, cur, prev,
                  pl.BlockSpec((2, 2 * BLOCK, width), lambda b, h, n: (0, 0, 0))],
        out_specs=pl.BlockSpec((1, SWA_GROUP * D, tq), lambda b, h, n: (b, h, n)),
        out_shape=jax.ShapeDtypeStruct((B, SWA_W, S), BF16),
        scratch_shapes=[pltpu.VMEM((BLOCK + tq, D), BF16), pltpu.VMEM((BLOCK + tq, D), BF16),
                        pltpu.VMEM((nq, 2 * BLOCK, width), F32), pltpu.VMEM((nq, 2 * BLOCK, width), BF16),
                        pltpu.VMEM((nq, 1, width), F32)],
        compiler_params=_params("arbitrary", "arbitrary", "arbitrary"),
        name="swa",
    )(sinks.astype(F32), sq, sk, sk, sv, sv, bias)


def _conv_body(h_ref, dw_ref, db_ref, lg_ref, lb_ref, pw_ref, o_ref, hbuf, hsh, *, ts, rb):
    s = pl.program_id(1)
    sub = 8
    span = ts + CONV_HALO - sub

    @pl.when(s == 0)
    def _():
        hbuf[0:CONV_HALO, :] = jnp.zeros((CONV_HALO, CONV_CH), F32)

    @pl.when(s > 0)
    def _():
        hbuf[0:CONV_HALO, :] = hbuf[ts:ts + CONV_HALO, :]

    hbuf[CONV_HALO:CONV_HALO + ts, :] = h_ref[0]
    for p in range(1, sub):
        hsh[p - 1] = hbuf[p:p + span, :]
    off = CONV_HALO - (CONV_WIDTH - 1)
    for r in range(ts // rb):
        base = r * rb
        acc = jnp.broadcast_to(db_ref[...], (rb, CONV_CH))
        for w in range(CONV_WIDTH):
            q, p = divmod(off + w, sub)
            lo = base + q * sub
            tap = hbuf[lo:lo + rb, :] if p == 0 else hsh[p - 1, lo:lo + rb, :]
            acc = acc + tap * dw_ref[w:w + 1, :]
        mu = jnp.mean(acc, axis=-1, keepdims=True)
        d = acc - mu
        var = jnp.mean(d * d, axis=-1, keepdims=True)
        hn = d * lax.rsqrt(var + EPS) * lg_ref[...] + lb_ref[...]
        o_ref[0, base:base + rb, :] = _dot(_silu(hn).astype(BF16), pw_ref[...]).astype(BF16)


def _conv(ch, dw_w, dw_b, ln_g, ln_b, pw_w, ts, rb):
    B, S, W = ch.shape
    ts = min(ts, S)
    rb = min(rb, ts)
    pw = jnp.zeros((W, W), F32)
    gd = W // CONV_GROUPS
    for g in range(CONV_GROUPS):
        pw = pw.at[g * gd:(g + 1) * gd, g * gd:(g + 1) * gd].set(pw_w[g])
    act = pl.BlockSpec((1, ts, W), lambda b, s: (b, s, 0))

    def const(shape):
        return pl.BlockSpec(shape, lambda b, s: (0,) * len(shape))

    return pl.pallas_call(
        functools.partial(_conv_body, ts=ts, rb=rb),
        grid=(B, S // ts),
        in_specs=[act, const((CONV_WIDTH, W)), const((1, W)), const((1, W)), const((1, W)), const((W, W))],
        out_specs=act,
        out_shape=jax.ShapeDtypeStruct((B, S, W), BF16),
        scratch_shapes=[pltpu.VMEM((CONV_HALO + ts, W), F32), pltpu.VMEM((7, CONV_HALO + ts - 8, W), F32)],
        compiler_params=_params("arbitrary", "arbitrary"),
        name="conv",
    )(ch, dw_w, dw_b.reshape(1, W), ln_g.reshape(1, W), ln_b.reshape(1, W), pw.astype(BF16))


def _mixer_residual(yr_ref, ys_ref, yc_ref, w_ref, x_ref):
    a, b = RET_W, RET_W + SWA_W
    return x_ref[...] + (_dot_tn(yr_ref[0], w_ref[0:a, :]) + _dot_tn(ys_ref[0], w_ref[a:b, :])
                         + _dot(yc_ref[...], w_ref[b:, :]))


def _mixer_specs(tm, tiles_per_seq):
    def chan(width):
        return pl.BlockSpec((1, width, tm), lambda i: (i // tiles_per_seq, 0, i % tiles_per_seq))

    return [chan(RET_W), chan(SWA_W), pl.BlockSpec((tm, CONV_CH), lambda i: (i, 0))]


def _outproj_body(yr_ref, ys_ref, yc_ref, w_ref, x_ref, g_ref, wr_ref, before_ref, xo_ref, hn_ref, rt_ref, rl_ref):
    x = _mixer_residual(yr_ref, ys_ref, yc_ref, w_ref, x_ref)
    xo_ref[...] = x
    h = _rms(x, g_ref[...]).astype(BF16)
    hn_ref[...] = h
    logits = _dot(h, wr_ref[...])
    lane = lax.broadcasted_iota(jnp.int32, logits.shape, 1)
    lg = jnp.where(lane < N_EXPERTS, logits, -jnp.inf)
    m1 = jnp.max(lg, axis=-1, keepdims=True)
    i1 = jnp.min(jnp.where(lg == m1, lane, LANES), axis=-1, keepdims=True)
    lg2 = jnp.where(lane == i1, -jnp.inf, lg)
    m2 = jnp.max(lg2, axis=-1, keepdims=True)
    i2 = jnp.min(jnp.where(lg2 == m2, lane, LANES), axis=-1, keepdims=True)
    e = jnp.exp(m2 - m1)
    w1 = 1.0 / (1.0 + e)
    w2 = e / (1.0 + e)
    hit1, hit2 = lane == i1, lane == i2
    onehot = jnp.where(jnp.logical_or(hit1, hit2), 1.0, 0.0)
    rank = _dot(before_ref[...], onehot.astype(BF16))
    count = jnp.sum(onehot, axis=0, keepdims=True).astype(jnp.int32)
    run_len = (count + (RUN_ALIGN - 1)) & ~(RUN_ALIGN - 1)
    lower = (lax.broadcasted_iota(jnp.int32, (LANES, LANES), 0) < lax.broadcasted_iota(jnp.int32, (LANES, LANES), 1))
    run_rows = jnp.broadcast_to(run_len.astype(F32), (8, LANES)).astype(BF16)
    loc_off = _dot(run_rows, jnp.where(lower, 1.0, 0.0).astype(BF16))[0:1, :]
    pos = rank + loc_off
    p1 = jnp.sum(jnp.where(hit1, pos, 0.0), axis=-1, keepdims=True)
    p2 = jnp.sum(jnp.where(hit2, pos, 0.0), axis=-1, keepdims=True)
    cols = (i1.astype(F32), i2.astype(F32), w1, w2, p1, p2)
    rt = jnp.zeros_like(logits)
    for c, v in enumerate(cols):
        rt = jnp.where(lane == c, v, rt)
    rt_ref[...] = rt
    rl_ref[0] = jnp.broadcast_to(run_len, (8, LANES))


def _outproj_route(yr, ys, yc, w_bf, x, g, w_router):
    T, D = x.shape
    tm = MOE_TB
    S = ys.shape[2]

    def row(width):
        return pl.BlockSpec((tm, width), lambda i: (i, 0))

    def const(shape):
        return pl.BlockSpec(shape, lambda i: (0,) * len(shape))

    wr = jnp.zeros((D, LANES), F32).at[:, :N_EXPERTS].set(w_router).astype(BF16)
    before = (jnp.arange(tm)[:, None] > jnp.arange(tm)[None, :]).astype(BF16)
    return pl.pallas_call(
        _outproj_body,
        grid=(T // tm,),
        in_specs=_mixer_specs(tm, S // tm) + [const((D, D)), row(D), const((1, D)), const((D, LANES)),
                                              const((tm, tm))],
        out_specs=[row(D), row(D), row(LANES), pl.BlockSpec((1, 8, LANES), lambda i: (i, 0, 0))],
        out_shape=[jax.ShapeDtypeStruct((T, D), F32), jax.ShapeDtypeStruct((T, D), BF16),
                   jax.ShapeDtypeStruct((T, LANES), F32), jax.ShapeDtypeStruct((T // tm, 8, LANES), jnp.int32)],
        compiler_params=_params("arbitrary"),
        name="outproj_route",
    )(yr, ys, yc, w_bf, x, g.reshape(1, D), wr, before)


def _dense_ffn_body(yr_ref, ys_ref, yc_ref, wo_ref, x_ref, g_ref, wg_ref, wu_ref, wd_ref, o_ref, *, chunks):
    x = _mixer_residual(yr_ref, ys_ref, yc_ref, wo_ref, x_ref)
    h = _rms(x, g_ref[...]).astype(BF16)
    acc = x
    for a, b in chunks:
        g = _dot(h, wg_ref[:, a:b])
        u = _dot(h, wu_ref[:, a:b])
        acc = acc + _dot((_silu(g) * u).astype(BF16), wd_ref[a:b, :])
    o_ref[...] = acc


def _outproj_dense_ffn(yr, ys, yc, wo_bf, x, g, wg, wu, wd, tm, fc):
    T, D = x.shape
    FF = wg.shape[1]
    S = ys.shape[2]
    tm = min(tm, S)
    chunks = tuple((a, min(a + fc, FF)) for a in range(0, FF, fc))

    def row(width):
        return pl.BlockSpec((tm, width), lambda i: (i, 0))

    def const(shape):
        return pl.BlockSpec(shape, lambda i: (0,) * len(shape), pipeline_mode=pl.Buffered(1))

    return pl.pallas_call(
        functools.partial(_dense_ffn_body, chunks=chunks),
        grid=(T // tm,),
        in_specs=_mixer_specs(tm, S // tm) + [const((D, D)), row(D), const((1, D)),
                                              const((D, FF)), const((D, FF)), const((FF, D))],
        out_specs=row(D),
        out_shape=jax.ShapeDtypeStruct((T, D), F32),
        compiler_params=_params("arbitrary"),
        name="outproj_dense_ffn",
    )(yr, ys, yc, wo_bf, x, g.reshape(1, D), wg.astype(BF16), wu.astype(BF16), wd.astype(BF16))


MOE_TB = 512
MOE_TM = 512
RUN_ALIGN = 8
RUN_SIZES = (512, 256, 128, 64, 32, 16, 8)
BLOCK_ROWS = TOP_K * MOE_TB + N_EXPERTS * RUN_ALIGN
BLOCK_LANES = 1152
MOE_FC = 512
MOE_LOAD_FC = 256
MOE_LOAD_SLOTS = 4


def _route_meta(rt, rl, n_tiles):
    T = rt.shape[0]
    nb = T // MOE_TB
    pos = rt[:, 4:4 + TOP_K].astype(jnp.int32)
    run_len = rl[:, 0, :N_EXPERTS]
    loc_off = jnp.cumsum(run_len, axis=1) - run_len
    group = jnp.sum(run_len, axis=0)
    ptiles = (group + MOE_TM - 1) // MOE_TM
    tile_end = jnp.cumsum(ptiles)
    gstart = (tile_end - ptiles) * MOE_TM
    run_start = gstart[None, :] + jnp.cumsum(run_len, axis=0) - run_len
    tile_expert = jnp.sum(jnp.arange(n_tiles, dtype=jnp.int32)[:, None] >= tile_end[None, :], axis=1)
    i32 = lambda a: a.astype(jnp.int32)
    pos3 = pos.reshape(nb, MOE_TB, TOP_K)
    return dict(
        pos_l=i32(jnp.swapaxes(pos3, 1, 2)),
        pos_c=i32(pos3.reshape(T, TOP_K)),
        loc_off=i32(loc_off.reshape(-1)), run_start=i32(run_start.reshape(-1)), run_len=i32(run_len.reshape(-1)),
        pad_start=i32(gstart + group), pad_len=i32(ptiles * MOE_TM - group),
        tile_expert=i32(jnp.minimum(tile_expert, N_EXPERTS - 1)), n_used=i32(tile_end[-1:]))


def _run_dma(src, dst, src_off, dst_off, length, sem, wait):
    off = 0
    for k in RUN_SIZES:
        part = length & k

        @pl.when(part != 0)
        def _(off=off, k=k):
            cp = pltpu.make_async_copy(src.at[pl.ds(pl.multiple_of(src_off + off, RUN_ALIGN), k)],
                                       dst.at[pl.ds(pl.multiple_of(dst_off + off, RUN_ALIGN), k)], sem)
            if wait:
                cp.wait()
            else:
                cp.start()

        off = off + part


def _dispatch_body(lo_ref, rs_ref, rl_ref, ps_ref, pn_ref, nu_ref, pos_ref, h_ref, xs_hbm, sbuf, zbuf, sem):
    b, nb = pl.program_id(0), pl.num_programs(0)
    slot = b % 2
    n_tiles = xs_hbm.shape[0] // MOE_TM

    def zero_tile(i, wait):
        cp = pltpu.make_async_copy(zbuf, xs_hbm.at[pl.ds(pl.multiple_of(i * MOE_TM, MOE_TM), MOE_TM)], sem.at[2])
        if wait:
            cp.wait()
        else:
            cp.start()

    def runs(blk, s, wait):
        for e in range(N_EXPERTS):
            j = blk * N_EXPERTS + e
            _run_dma(sbuf.at[s], xs_hbm, lo_ref[j], rs_ref[j], rl_ref[j], sem.at[s], wait)

    @pl.when(b >= 2)
    def _():
        runs(b - 2, slot, True)

    pos = pos_ref[0]
    r = lax.broadcasted_iota(jnp.int32, (BLOCK_ROWS, MOE_TB), 0)
    onehot = jnp.logical_or(r == pos[0:1, :], r == pos[1:2, :])
    sbuf[slot] = _dot(jnp.where(onehot, 1.0, 0.0).astype(BF16), h_ref[...])
    runs(b, slot, False)

    @pl.when(b == nb - 1)
    def _():
        zbuf[...] = jnp.zeros_like(zbuf)
        for e in range(N_EXPERTS):
            _run_dma(zbuf, xs_hbm, 0, ps_ref[e], pn_ref[e], sem.at[2], False)
        lax.fori_loop(nu_ref[0], n_tiles, lambda i, c: zero_tile(i, False), None)
        for e in range(N_EXPERTS):
            _run_dma(zbuf, xs_hbm, 0, ps_ref[e], pn_ref[e], sem.at[2], True)
        lax.fori_loop(nu_ref[0], n_tiles, lambda i, c: zero_tile(i, True), None)

        @pl.when(b >= 1)
        def _():
            runs(b - 1, 1 - slot, True)

        runs(b, slot, True)


def _dispatch(hn, meta, n_rows):
    T, D = hn.shape
    nb = T // MOE_TB
    grid_spec = pltpu.PrefetchScalarGridSpec(
        num_scalar_prefetch=6,
        grid=(nb,),
        in_specs=[pl.BlockSpec((1, TOP_K, MOE_TB), lambda b, *_: (b, 0, 0)),
                  pl.BlockSpec((MOE_TB, D), lambda b, *_: (b, 0))],
        out_specs=pl.BlockSpec(memory_space=pl.ANY),
        scratch_shapes=[pltpu.VMEM((2, BLOCK_ROWS, D), F32), pltpu.VMEM((MOE_TM, D), F32),
                        pltpu.SemaphoreType.DMA((3,))],
    )
    return pl.pallas_call(
        _dispatch_body,
        grid_spec=grid_spec,
        out_shape=jax.ShapeDtypeStruct((n_rows, D), F32),
        compiler_params=pltpu.CompilerParams(dimension_semantics=("arbitrary",), vmem_limit_bytes=VMEM_LIMIT,
                                             has_side_effects=True),
        name="moe_dispatch",
    )(meta["loc_off"], meta["run_start"], meta["run_len"], meta["pad_start"], meta["pad_len"], meta["n_used"],
      meta["pos_l"], hn)


def _moe_ffn_body(te_ref, nu_ref, x_ref, wg_hbm, wu_hbm, wd_hbm, o_ref, wg_s, wu_s, wd_s, st_a, st_d, sem):
    i = pl.program_id(0)
    e = te_ref[i]
    active = i < nu_ref[0]
    nf = wg_s.shape[1] // MOE_FC
    fsl = [slice(f * MOE_FC, (f + 1) * MOE_FC) for f in range(nf)]

    @pl.when(jnp.logical_and(active, jnp.logical_or(i == 0, e != te_ref[jnp.maximum(i - 1, 0)])))
    def _():
        slots, lfc = st_a.shape[0], st_a.shape[2]
        lsl = [slice(f * lfc, (f + 1) * lfc) for f in range(wg_s.shape[1] // lfc)]
        jobs = ([(wg_hbm, wg_s, True, s) for s in lsl] + [(wu_hbm, wu_s, True, s) for s in lsl]
                + [(wd_hbm, wd_s, False, s) for s in lsl])

        def copy(j):
            w_hbm, _, cols, s = jobs[j]
            if cols:
                return pltpu.make_async_copy(w_hbm.at[e, :, s], st_a.at[j % slots], sem.at[j % slots])
            return pltpu.make_async_copy(w_hbm.at[e, s, :], st_d.at[j % slots], sem.at[j % slots])

        for j in range(slots - 1):
            copy(j).start()
        for j, (_, w_s, cols, s) in enumerate(jobs):
            if j + slots - 1 < len(jobs):
                copy(j + slots - 1).start()
            copy(j).wait()
            if cols:
                w_s[:, s] = st_a[j % slots].astype(BF16)
            else:
                w_s[s, :] = st_d[j % slots].astype(BF16)

    @pl.when(active)
    def _():
        x = x_ref[...].astype(BF16)
        acc = jnp.zeros(o_ref.shape, F32)
        for sl in fsl:
            g = _dot(x, wg_s[:, sl])
            u = _dot(x, wu_s[:, sl])
            acc = acc + _dot((_silu(g) * u).astype(BF16), wd_s[sl, :])
        o_ref[...] = acc

    @pl.when(jnp.logical_not(active))
    def _():
        o_ref[...] = jnp.zeros_like(o_ref)


def _moe_ffn(xs, meta, wg, wu, wd):
    R, D = xs.shape
    E, _, FF = wg.shape
    grid_spec = pltpu.PrefetchScalarGridSpec(
        num_scalar_prefetch=2,
        grid=(R // MOE_TM,),
        in_specs=[pl.BlockSpec((MOE_TM, D), lambda i, te, nu: (jnp.minimum(i, nu[0] - 1), 0)),
                  pl.BlockSpec(memory_space=pl.ANY), pl.BlockSpec(memory_space=pl.ANY),
                  pl.BlockSpec(memory_space=pl.ANY)],
        out_specs=pl.BlockSpec((MOE_TM, D), lambda i, te, nu: (i, 0)),
        scratch_shapes=[pltpu.VMEM((D, FF), BF16), pltpu.VMEM((D, FF), BF16), pltpu.VMEM((FF, D), BF16),
                        pltpu.VMEM((MOE_LOAD_SLOTS, D, MOE_LOAD_FC), F32),
                        pltpu.VMEM((MOE_LOAD_SLOTS, MOE_LOAD_FC, D), F32),
                        pltpu.SemaphoreType.DMA((MOE_LOAD_SLOTS,))],
    )
    return pl.pallas_call(
        _moe_ffn_body,
        grid_spec=grid_spec,
        out_shape=jax.ShapeDtypeStruct((R, D), F32),
        compiler_params=_params("arbitrary"),
        name="moe_ffn",
    )(meta["tile_expert"], meta["n_used"], xs, wg, wu, wd)


def _combine_body(lo_ref, rs_ref, rl_ref, ys_hbm, x_ref, rt_ref, pc_ref, g_ref, o_ref, ybuf, sem):
    b, nb = pl.program_id(0), pl.num_programs(0)
    slot = b % 2

    def runs(blk, s, wait):
        for e in range(N_EXPERTS):
            j = blk * N_EXPERTS + e
            _run_dma(ys_hbm, ybuf.at[s], rs_ref[j], lo_ref[j], rl_ref[j], sem.at[s], wait)

    @pl.when(b == 0)
    def _():
        ybuf[...] = jnp.zeros_like(ybuf)
        runs(0, 0, False)

    @pl.when(b + 1 < nb)
    def _():
        runs(b + 1, 1 - slot, False)

    runs(b, slot, True)
    y = ybuf[slot].astype(BF16)
    pc, rt = pc_ref[...], rt_ref[...]
    lane = lax.broadcasted_iota(jnp.int32, (MOE_TB, BLOCK_LANES), 1)
    q0 = jnp.where(lane == pc[:, 0:1], 1.0, 0.0).astype(BF16)
    q1 = jnp.where(lane == pc[:, 1:2], 1.0, 0.0).astype(BF16)
    moe = rt[:, 2:3] * _dot(q0, y) + rt[:, 3:4] * _dot(q1, y)
    o_ref[...] = _rms(x_ref[...] + moe, g_ref[...])


def _combine(ys, meta, x, rt, g):
    T, D = x.shape
    grid_spec = pltpu.PrefetchScalarGridSpec(
        num_scalar_prefetch=3,
        grid=(T // MOE_TB,),
        in_specs=[pl.BlockSpec(memory_space=pl.ANY),
                  pl.BlockSpec((MOE_TB, D), lambda b, *_: (b, 0)),
                  pl.BlockSpec((MOE_TB, LANES), lambda b, *_: (b, 0)),
                  pl.BlockSpec((MOE_TB, TOP_K), lambda b, *_: (b, 0)),
                  pl.BlockSpec((1, D), lambda b, *_: (0, 0))],
        out_specs=pl.BlockSpec((MOE_TB, D), lambda b, *_: (b, 0)),
        scratch_shapes=[pltpu.VMEM((2, BLOCK_LANES, D), F32), pltpu.SemaphoreType.DMA((2,))],
    )
    return pl.pallas_call(
        _combine_body,
        grid_spec=grid_spec,
        out_shape=jax.ShapeDtypeStruct((T, D), F32),
        compiler_params=_params("arbitrary"),
        name="moe_combine",
    )(meta["loc_off"], meta["run_start"], meta["run_len"], ys, x, rt, meta["pos_c"], g.reshape(1, D))


def kernel(x, norm_mix_g, w_in, ret_gn_g, attn_sinks, conv_dw_w, conv_dw_b, conv_ln_g, conv_ln_b,
           conv_pw_w, w_out, norm_ffn_g, ffn_w_gate, ffn_w_up, ffn_w_down, moe_router, moe_w_gate,
           moe_w_up, moe_w_down, final_norm_g):
    B, S, D = x.shape
    T = B * S
    depth = w_in.shape[0]
    assert depth == 2 and ffn_w_gate.shape[0] == 1 and moe_router.shape[0] == 1, "dense layer then MoE layer"
    assert T % MOE_TB == 0 and S % MOE_TB == 0
    assert moe_w_gate.shape[-1] % MOE_FC == 0 and moe_w_gate.shape[-1] % MOE_LOAD_FC == 0

    for l in range(depth):
        rq, rk, rv, rg, sq, sk, sv, ch = _inproj(x, norm_mix_g[l], w_in[l].astype(BF16), tm=1024)
        y_ret_t = _retention(rq, rk, rv, rg, ret_gn_g[l], ts=4096)
        y_swa_t = _swa(sq, sk, sv, attn_sinks[l], tq=2048)
        y_conv = _conv(ch, conv_dw_w[l], conv_dw_b[l], conv_ln_g[l], conv_ln_b[l], conv_pw_w[l], ts=1024, rb=64)
        flat = lambda a: a.reshape(T, a.shape[-1])
        j = l // 2
        if l % 2 == 0:
            x = _outproj_dense_ffn(y_ret_t, y_swa_t, flat(y_conv), w_out[l].astype(BF16), flat(x),
                                   norm_ffn_g[l], ffn_w_gate[j], ffn_w_up[j], ffn_w_down[j],
                                   tm=512, fc=512).reshape(B, S, D)
        else:
            x2, hn, rt, rl = _outproj_route(y_ret_t, y_swa_t, flat(y_conv), w_out[l].astype(BF16), flat(x),
                                            norm_ffn_g[l], moe_router[j])
            n_rows = TOP_K * T + (T // MOE_TB) * N_EXPERTS * RUN_ALIGN + N_EXPERTS * MOE_TM
            meta = _route_meta(rt, rl, n_rows // MOE_TM)
            xs = _dispatch(hn, meta, n_rows)
            ys = _moe_ffn(xs, meta, moe_w_gate[j], moe_w_up[j], moe_w_down[j])
            x = _combine(ys, meta, x2, rt, final_norm_g).reshape(B, S, D)
    return x
```

```python
import functools

import jax
import jax.numpy as jnp
from jax import lax
from jax.experimental import pallas as pl
from jax.experimental.pallas import tpu as pltpu

F32 = jnp.float32
BF16 = jnp.bfloat16

HEAD_DIM = 64
RET_HEADS = 4
RET_W = RET_HEADS * HEAD_DIM
SWA_HEADS = 8
SWA_KV_HEADS = 2
SWA_GROUP = SWA_HEADS // SWA_KV_HEADS
SWA_W = SWA_HEADS * HEAD_DIM
SWA_KV_W = SWA_KV_HEADS * HEAD_DIM
CONV_CH = 256
CONV_GROUPS = 4
CONV_WIDTH = 31
WINDOW = 128
BLOCK = 128
RET_CHUNK = 128
ROPE_THETA = 500000.0
ROPE_DIM = HEAD_DIM // 4
RET_ROPE_THETA = 10000.0
N_EXPERTS = 8
TOP_K = 2
EPS = 1e-6
NEG_INF = -1e30

LANES = 128
CONV_HALO = 32
VMEM_LIMIT = 56 * 1024 * 1024

O_RQ, O_RK, O_RV, O_RG = 0, RET_W, 2 * RET_W, 3 * RET_W
O_SQ = 4 * RET_W
O_SK = O_SQ + SWA_W
O_SV = O_SK + SWA_KV_W
O_CA = O_SV + SWA_KV_W
O_CG = O_CA + CONV_CH
D_IN = O_CG + CONV_CH


def _params(*sem, flags=None):
    return pltpu.CompilerParams(dimension_semantics=sem, vmem_limit_bytes=VMEM_LIMIT, flags=flags)


def _rms(x, g):
    return x * lax.rsqrt(jnp.mean(x * x, axis=-1, keepdims=True) + EPS) * g


def _silu(x):
    return x * jax.nn.sigmoid(x)


def _dot(a, b):
    return jnp.dot(a, b, preferred_element_type=F32)


def _dot_nt(a, b):
    return lax.dot_general(a, b, (((1,), (1,)), ((), ())), preferred_element_type=F32)


def _dot_tn(a, b):
    return lax.dot_general(a, b, (((0,), (0,)), ((), ())), preferred_element_type=F32)


def _rope_tables(seq, theta, rot_dim):
    half = rot_dim // 2
    inv = 1.0 / (theta ** (jnp.arange(half, dtype=F32) / half))
    ang = jnp.arange(seq, dtype=F32)[:, None] * inv[None, :]
    cos, sin = jnp.cos(ang), jnp.sin(ang)
    rest = HEAD_DIM - rot_dim
    c = jnp.concatenate([cos, cos, jnp.ones((seq, rest), F32)], axis=1)
    s = jnp.concatenate([-sin, sin, jnp.zeros((seq, rest), F32)], axis=1)
    reps = LANES // HEAD_DIM
    return jnp.tile(c, (1, reps)), jnp.tile(s, (1, reps))


def _inproj_body(x_ref, g_ref, w_ref, wgt_ref, rc_ref, rs_ref, sc_ref, ss_ref,
                 rq_ref, rk_ref, rv_ref, rg_ref, sq_ref, sk_ref, sv_ref, ch_ref, zbuf):
    h = _rms(x_ref[0], g_ref[...]).astype(BF16)
    lane = lax.broadcasted_iota(jnp.int32, (1, LANES), 1) % HEAD_DIM
    for a, b in ((O_RQ, O_RK), (O_RK, O_RV), (O_RV, O_RG), (O_SQ, O_SK), (O_SK, O_CA), (O_CA, O_CG), (O_CG, D_IN)):
        zbuf[:, a:b] = _dot(h, w_ref[:, a:b])

    def seg(a, b):
        return zbuf[:, a:b]

    def rope(v, c, s, half):
        up = pltpu.roll(v, LANES - half, axis=1)
        dn = pltpu.roll(v, half, axis=1)
        return v * c + jnp.where(lane < half, up, dn) * s

    def store_heads(z, o_ref, first=0):
        for j in range(z.shape[1] // HEAD_DIM):
            o_ref[0, first + j] = z[:, j * HEAD_DIM:(j + 1) * HEAD_DIM]

    def rope_store(z, o_ref, c, s, half, scale):
        for i in range(z.shape[1] // LANES):
            r = rope(z[:, i * LANES:(i + 1) * LANES], c, s, half)
            if scale != 1.0:
                r = r * scale
            store_heads(r.astype(BF16), o_ref, first=i * (LANES // HEAD_DIM))

    rc, rs = rc_ref[...], rs_ref[...]
    sc, ss = sc_ref[...], ss_ref[...]
    scale = HEAD_DIM ** -0.5
    rope_store(seg(O_RQ, O_RK), rq_ref, rc, rs, HEAD_DIM // 2, 1.0)
    rope_store(seg(O_RK, O_RV), rk_ref, rc, rs, HEAD_DIM // 2, scale)
    store_heads(seg(O_RV, O_RG).astype(BF16), rv_ref)
    rg_ref[0] = _dot_nt(wgt_ref[...], h)
    rope_store(seg(O_SQ, O_SK), sq_ref, sc, ss, ROPE_DIM // 2, scale)
    rope_store(seg(O_SK, O_SV), sk_ref, sc, ss, ROPE_DIM // 2, 1.0)
    store_heads(seg(O_SV, O_CA).astype(BF16), sv_ref)
    ch_ref[0] = seg(O_CA, O_CG) * jax.nn.sigmoid(seg(O_CG, D_IN))


def _inproj(x, g, w_bf, tm):
    B, S, D = x.shape
    tm = min(tm, S)
    rc, rs = _rope_tables(S, RET_ROPE_THETA, HEAD_DIM)
    sc, ss = _rope_tables(S, ROPE_THETA, ROPE_DIM)
    tab = pl.BlockSpec((tm, LANES), lambda s, b: (s, 0))

    def out(width, dtype):
        return (jax.ShapeDtypeStruct((B, S, width), dtype),
                pl.BlockSpec((1, tm, width), lambda s, b: (b, s, 0)))

    def out_heads(heads):
        return (jax.ShapeDtypeStruct((B, heads, S, HEAD_DIM), BF16),
                pl.BlockSpec((1, heads, tm, HEAD_DIM), lambda s, b: (b, 0, s, 0)))

    gate_t = (jax.ShapeDtypeStruct((B, RET_W, S), F32), pl.BlockSpec((1, RET_W, tm), lambda s, b: (b, 0, s)))
    outs = [out_heads(RET_HEADS), out_heads(RET_HEADS), out_heads(RET_HEADS), gate_t,
            out_heads(SWA_HEADS), out_heads(SWA_KV_HEADS), out_heads(SWA_KV_HEADS), out(CONV_CH, F32)]
    return pl.pallas_call(
        _inproj_body,
        grid=(S // tm, B),
        in_specs=[pl.BlockSpec((1, tm, D), lambda s, b: (b, s, 0)),
                  pl.BlockSpec((1, D), lambda s, b: (0, 0)),
                  pl.BlockSpec((D, D_IN), lambda s, b: (0, 0)),
                  pl.BlockSpec((RET_W, D), lambda s, b: (0, 0)),
                  tab, tab, tab, tab],
        out_specs=[o[1] for o in outs],
        out_shape=[o[0] for o in outs],
        scratch_shapes=[pltpu.VMEM((tm, D_IN), F32)],
        compiler_params=_params("arbitrary", "arbitrary"),
        name="inproj",
    )(x, g.reshape(1, D), w_bf, w_bf[:, O_RG:O_SQ].T, rc, rs, sc, ss)


def _ret_body(q_ref, k_ref, v_ref, g_ref, dm_ref, kd_ref, qd_ref, cd_ref, gn_ref, o_ref, st_ref, *, ts):
    @pl.when(pl.program_id(2) == 0)
    def _():
        st_ref[...] = jnp.zeros_like(st_ref)

    C = RET_CHUNK
    for c in range(ts // C):
        rows = slice(c * C, (c + 1) * C)
        q, k, v = q_ref[0, 0, rows, :], k_ref[0, 0, rows, :], v_ref[0, 0, rows, :]
        st = st_ref[...]
        scores = _dot_nt(k, q) * dm_ref[0]
        intra = _dot_tn(v, scores.astype(BF16))
        cross = _dot_nt(st.astype(BF16), q) * qd_ref[0]
        kdec = (k.astype(F32) * kd_ref[0]).astype(BF16)
        st_ref[...] = st * cd_ref[0] + _dot_tn(v, kdec)
        o = intra + cross
        mu = jnp.mean(o, axis=0, keepdims=True)
        d = o - mu
        var = jnp.mean(d * d, axis=0, keepdims=True)
        on = d * lax.rsqrt(var + EPS) * gn_ref[0]
        o_ref[0, :, rows] = (_silu(g_ref[0, :, rows]) * on).astype(BF16)


def _retention(rq, rk, rv, rg_t, gn_g, ts):
    B, H, S, D = rq.shape
    ts = min(ts, S)
    C, W = RET_CHUNK, RET_W
    lg = jnp.log(1.0 - 2.0 ** (-5.0 - jnp.arange(H, dtype=F32)))
    idx = jnp.arange(C)
    rel = idx[:, None] - idx[None, :]
    dmask = jnp.where(rel[None] >= 0,
                      jnp.exp(jnp.maximum(rel, 0)[None].astype(F32) * lg[:, None, None]), 0.0)
    k_decay = jnp.exp((C - 1 - idx)[:, None].astype(F32) * lg[None, :])
    q_decay = jnp.exp((idx + 1)[:, None].astype(F32) * lg[None, :])
    chunk_decay = jnp.exp(C * lg)
    dm_t = jnp.swapaxes(dmask, 1, 2)
    kd = jnp.broadcast_to(k_decay.T[:, :, None], (H, C, D))
    qd = jnp.broadcast_to(q_decay.T[:, None, :], (H, D, C))
    cd = jnp.broadcast_to(chunk_decay[:, None, None], (H, D, D))
    gn = jnp.broadcast_to(gn_g.reshape(H, D, 1), (H, D, C))
    heads = pl.BlockSpec((1, 1, ts, D), lambda b, h, n: (b, h, n, 0))
    chan = pl.BlockSpec((1, D, ts), lambda b, h, n: (b, h, n))

    def per_head(r, c):
        return pl.BlockSpec((1, r, c), lambda b, h, n: (h, 0, 0))

    return pl.pallas_call(
        functools.partial(_ret_body, ts=ts),
        grid=(B, H, S // ts),
        in_specs=[heads, heads, heads, chan, per_head(C, C), per_head(C, D), per_head(D, C),
                  per_head(D, D), per_head(D, C)],
        out_specs=chan,
        out_shape=jax.ShapeDtypeStruct((B, W, S), BF16),
        scratch_shapes=[pltpu.VMEM((D, D), F32)],
        compiler_params=_params("arbitrary", "arbitrary", "arbitrary"),
        name="retention",
    )(rq, rk, rv, rg_t, dm_t, kd, qd, cd, gn)


def _swa_body(sink_ref, q_ref, kc_ref, kp_ref, vc_ref, vp_ref, bias_ref, o_ref, kcat, vcat, sbuf, pbuf, rbuf,
              *, tq):
    hk, n = pl.program_id(1), pl.program_id(2)
    kcat[0:BLOCK], kcat[BLOCK:] = kp_ref[0, 0], kc_ref[0, 0]
    vcat[0:BLOCK], vcat[BLOCK:] = vp_ref[0, 0], vc_ref[0, 0]
    width = SWA_GROUP * BLOCK
    group = lax.broadcasted_iota(jnp.int32, (1, width), 1) // BLOCK
    sink = jnp.zeros((1, width), F32)
    for g in range(SWA_GROUP):
        sink = jnp.where(group == g, sink_ref[hk * SWA_GROUP + g], sink)
    nq = tq // BLOCK
    for j in range(nq):
        q = q_ref[0, :, j * BLOCK:(j + 1) * BLOCK, :].reshape(width, HEAD_DIM)
        kb = kcat[j * BLOCK:(j + 2) * BLOCK]
        sbuf[j] = _dot_nt(kb, q)
    for j in range(nq):
        s = sbuf[j] + (bias_ref[0] if j > 0 else bias_ref[jnp.where(n == 0, 1, 0)])
        m = jnp.maximum(jnp.max(s, axis=0, keepdims=True), sink)
        p = jnp.exp(s - m)
        rbuf[j] = 1.0 / (jnp.sum(p, axis=0, keepdims=True) + jnp.exp(sink - m))
        pbuf[j] = p.astype(BF16)
    for j in range(nq):
        vb = vcat[j * BLOCK:(j + 2) * BLOCK]
        o = _dot_tn(vb, pbuf[j]) * rbuf[j]
        for g in range(SWA_GROUP):
            o_ref[0, g * HEAD_DIM:(g + 1) * HEAD_DIM, j * BLOCK:(j + 1) * BLOCK] = (
                o[:, g * BLOCK:(g + 1) * BLOCK].astype(BF16))


def _swa(sq, sk, sv, sinks, tq):
    B, _, S, D = sq.shape
    tq = min(tq, S)
    r = tq // BLOCK
    qi = jnp.arange(BLOCK)[None, :] + BLOCK
    kj = jnp.arange(2 * BLOCK)[:, None]
    rel = qi - kj
    allowed = (rel >= 0) & (rel < WINDOW)
    allowed = jnp.stack([allowed, allowed & (kj >= BLOCK)])
    bias = jnp.tile(jnp.where(allowed, 0.0, NEG_INF).astype(F32), (1, 1, SWA_GROUP))
    nq, width = tq // BLOCK, SWA_GROUP * BLOCK
    cur = pl.BlockSpec((1, 1, tq, D), lambda b, h, n: (b, h, n, 0))
    prev = pl.BlockSpec((1, 1, BLOCK, D), lambda b, h, n: (b, h, jnp.maximum(n * r - 1, 0), 0))
    return pl.pallas_call(
        functools.partial(_swa_body, tq=tq),
        grid=(B, SWA_KV_HEADS, S // tq),
        in_specs=[pl.BlockSpec(memory_space=pltpu.SMEM),
                  pl.BlockSpec((1, SWA_GROUP, tq, D), lambda b, h, n: (b, h, n, 0)),
                  cur, prev, cur, prev,
                  pl.BlockSpec((2, 2 * BLOCK, width), lambda b, h, n: (0, 0, 0))],
        out_specs=pl.BlockSpec((1, SWA_GROUP * D, tq), lambda b, h, n: (b, h, n)),
        out_shape=jax.ShapeDtypeStruct((B, SWA_W, S), BF16),
        scratch_shapes=[pltpu.VMEM((BLOCK + tq, D), BF16), pltpu.VMEM((BLOCK + tq, D), BF16),
                        pltpu.VMEM((nq, 2 * BLOCK, width), F32), pltpu.VMEM((nq, 2 * BLOCK, width), BF16),
                        pltpu.VMEM((nq, 1, width), F32)],
        compiler_params=_params("arbitrary", "arbitrary", "arbitrary"),
        name="swa",
    )(sinks.astype(F32), sq, sk, sk, sv, sv, bias)


def _conv_body(h_ref, dw_ref, db_ref, lg_ref, lb_ref, pw_ref, o_ref, hbuf, hsh, *, ts, rb):
    s = pl.program_id(1)
    sub = 8
    span = ts + CONV_HALO - sub

    @pl.when(s == 0)
    def _():
        hbuf[0:CONV_HALO, :] = jnp.zeros((CONV_HALO, CONV_CH), F32)

    @pl.when(s > 0)
    def _():
        hbuf[0:CONV_HALO, :] = hbuf[ts:ts + CONV_HALO, :]

    hbuf[CONV_HALO:CONV_HALO + ts, :] = h_ref[0]
    for p in range(1, sub):
        hsh[p - 1] = hbuf[p:p + span, :]
    off = CONV_HALO - (CONV_WIDTH - 1)
    for r in range(ts // rb):
        base = r * rb
        acc = jnp.broadcast_to(db_ref[...], (rb, CONV_CH))
        for w in range(CONV_WIDTH):
            q, p = divmod(off + w, sub)
            lo = base + q * sub
            tap = hbuf[lo:lo + rb, :] if p == 0 else hsh[p - 1, lo:lo + rb, :]
            acc = acc + tap * dw_ref[w:w + 1, :]
        mu = jnp.mean(acc, axis=-1, keepdims=True)
        d = acc - mu
        var = jnp.mean(d * d, axis=-1, keepdims=True)
        hn = d * lax.rsqrt(var + EPS) * lg_ref[...] + lb_ref[...]
        o_ref[0, base:base + rb, :] = _dot(_silu(hn).astype(BF16), pw_ref[...]).astype(BF16)


def _conv(ch, dw_w, dw_b, ln_g, ln_b, pw_w, ts, rb):
    B, S, W = ch.shape
    ts = min(ts, S)
    rb = min(rb, ts)
    pw = jnp.zeros((W, W), F32)
    gd = W // CONV_GROUPS
    for g in range(CONV_GROUPS):
        pw = pw.at[g * gd:(g + 1) * gd, g * gd:(g + 1) * gd].set(pw_w[g])
    act = pl.BlockSpec((1, ts, W), lambda b, s: (b, s, 0))

    def const(shape):
        return pl.BlockSpec(shape, lambda b, s: (0,) * len(shape))

    return pl.pallas_call(
        functools.partial(_conv_body, ts=ts, rb=rb),
        grid=(B, S // ts),
        in_specs=[act, const((CONV_WIDTH, W)), const((1, W)), const((1, W)), const((1, W)), const((W, W))],
        out_specs=act,
        out_shape=jax.ShapeDtypeStruct((B, S, W), BF16),
        scratch_shapes=[pltpu.VMEM((CONV_HALO + ts, W), F32), pltpu.VMEM((7, CONV_HALO + ts - 8, W), F32)],
        compiler_params=_params("arbitrary", "arbitrary"),
        name="conv",
    )(ch, dw_w, dw_b.reshape(1, W), ln_g.reshape(1, W), ln_b.reshape(1, W), pw.astype(BF16))


def _mixer_residual(yr_ref, ys_ref, yc_ref, w_ref, x_ref):
    a, b = RET_W, RET_W + SWA_W
    return x_ref[...] + (_dot_tn(yr_ref[0], w_ref[0:a, :]) + _dot_tn(ys_ref[0], w_ref[a:b, :])
                         + _dot(yc_ref[...], w_ref[b:, :]))


def _mixer_specs(tm, tiles_per_seq):
    def chan(width):
        return pl.BlockSpec((1, width, tm), lambda i: (i // tiles_per_seq, 0, i % tiles_per_seq))

    return [chan(RET_W), chan(SWA_W), pl.BlockSpec((tm, CONV_CH), lambda i: (i, 0))]


def _outproj_body(yr_ref, ys_ref, yc_ref, w_ref, x_ref, g_ref, wr_ref, before_ref, xo_ref, hn_ref, rt_ref, rl_ref):
    x = _mixer_residual(yr_ref, ys_ref, yc_ref, w_ref, x_ref)
    xo_ref[...] = x
    h = _rms(x, g_ref[...]).astype(BF16)
    hn_ref[...] = h
    logits = _dot(h, wr_ref[...])
    lane = lax.broadcasted_iota(jnp.int32, logits.shape, 1)
    lg = jnp.where(lane < N_EXPERTS, logits, -jnp.inf)
    m1 = jnp.max(lg, axis=-1, keepdims=True)
    i1 = jnp.min(jnp.where(lg == m1, lane, LANES), axis=-1, keepdims=True)
    lg2 = jnp.where(lane == i1, -jnp.inf, lg)
    m2 = jnp.max(lg2, axis=-1, keepdims=True)
    i2 = jnp.min(jnp.where(lg2 == m2, lane, LANES), axis=-1, keepdims=True)
    e = jnp.exp(m2 - m1)
    w1 = 1.0 / (1.0 + e)
    w2 = e / (1.0 + e)
    hit1, hit2 = lane == i1, lane == i2
    onehot = jnp.where(jnp.logical_or(hit1, hit2), 1.0, 0.0)
    rank = _dot(before_ref[...], onehot.astype(BF16))
    count = jnp.sum(onehot, axis=0, keepdims=True).astype(jnp.int32)
    run_len = (count + (RUN_ALIGN - 1)) & ~(RUN_ALIGN - 1)
    lower = (lax.broadcasted_iota(jnp.int32, (LANES, LANES), 0) < lax.broadcasted_iota(jnp.int32, (LANES, LANES), 1))
    run_rows = jnp.broadcast_to(run_len.astype(F32), (8, LANES)).astype(BF16)
    loc_off = _dot(run_rows, jnp.where(lower, 1.0, 0.0).astype(BF16))[0:1, :]
    pos = rank + loc_off
    p1 = jnp.sum(jnp.where(hit1, pos, 0.0), axis=-1, keepdims=True)
    p2 = jnp.sum(jnp.where(hit2, pos, 0.0), axis=-1, keepdims=True)
    cols = (i1.astype(F32), i2.astype(F32), w1, w2, p1, p2)
    rt = jnp.zeros_like(logits)
    for c, v in enumerate(cols):
        rt = jnp.where(lane == c, v, rt)
    rt_ref[...] = rt
    rl_ref[0] = jnp.broadcast_to(run_len, (8, LANES))


def _outproj_route(yr, ys, yc, w_bf, x, g, w_router):
    T, D = x.shape
    tm = MOE_TB
    S = ys.shape[2]

    def row(width):
        return pl.BlockSpec((tm, width), lambda i: (i, 0))

    def const(shape):
        return pl.BlockSpec(shape, lambda i: (0,) * len(shape))

    wr = jnp.zeros((D, LANES), F32).at[:, :N_EXPERTS].set(w_router).astype(BF16)
    before = (jnp.arange(tm)[:, None] > jnp.arange(tm)[None, :]).astype(BF16)
    return pl.pallas_call(
        _outproj_body,
        grid=(T // tm,),
        in_specs=_mixer_specs(tm, S // tm) + [const((D, D)), row(D), const((1, D)), const((D, LANES)),
                                              const((tm, tm))],
        out_specs=[row(D), row(D), row(LANES), pl.BlockSpec((1, 8, LANES), lambda i: (i, 0, 0))],
        out_shape=[jax.ShapeDtypeStruct((T, D), F32), jax.ShapeDtypeStruct((T, D), BF16),
                   jax.ShapeDtypeStruct((T, LANES), F32), jax.ShapeDtypeStruct((T // tm, 8, LANES), jnp.int32)],
        compiler_params=_params("arbitrary"),
        name="outproj_route",
    )(yr, ys, yc, w_bf, x, g.reshape(1, D), wr, before)


def _dense_ffn_body(yr_ref, ys_ref, yc_ref, wo_ref, x_ref, g_ref, wg_ref, wu_ref, wd_ref, o_ref, *, chunks):
    x = _mixer_residual(yr_ref, ys_ref, yc_ref, wo_ref, x_ref)
    h = _rms(x, g_ref[...]).astype(BF16)
    acc = x
    for a, b in chunks:
        g = _dot(h, wg_ref[:, a:b])
        u = _dot(h, wu_ref[:, a:b])
        acc = acc + _dot((_silu(g) * u).astype(BF16), wd_ref[a:b, :])
    o_ref[...] = acc


def _outproj_dense_ffn(yr, ys, yc, wo_bf, x, g, wg, wu, wd, tm, fc):
    T, D = x.shape
    FF = wg.shape[1]
    S = ys.shape[2]
    tm = min(tm, S)
    chunks = tuple((a, min(a + fc, FF)) for a in range(0, FF, fc))

    def row(width):
        return pl.BlockSpec((tm, width), lambda i: (i, 0))

    def const(shape):
        return pl.BlockSpec(shape, lambda i: (0,) * len(shape), pipeline_mode=pl.Buffered(1))

    return pl.pallas_call(
        functools.partial(_dense_ffn_body, chunks=chunks),
        grid=(T // tm,),
        in_specs=_mixer_specs(tm, S // tm) + [const((D, D)), row(D), const((1, D)),
                                              const((D, FF)), const((D, FF)), const((FF, D))],
        out_specs=row(D),
        out_shape=jax.ShapeDtypeStruct((T, D), F32),
        compiler_params=_params("arbitrary"),
        name="outproj_dense_ffn",
    )(yr, ys, yc, wo_bf, x, g.reshape(1, D), wg.astype(BF16), wu.astype(BF16), wd.astype(BF16))


MOE_TB = 512
MOE_TM = 512
RUN_ALIGN = 8
RUN_SIZES = (512, 256, 128, 64, 32, 16, 8)
BLOCK_ROWS = TOP_K * MOE_TB + N_EXPERTS * RUN_ALIGN
BLOCK_LANES = 1152
MOE_FC = 512
MOE_LOAD_ROWS = 224
MOE_LOAD_SLOTS = 4


def _route_meta(rt, rl, n_tiles):
    T = rt.shape[0]
    nb = T // MOE_TB
    pos = rt[:, 4:4 + TOP_K].astype(jnp.int32)
    run_len = rl[:, 0, :N_EXPERTS]
    loc_off = jnp.cumsum(run_len, axis=1) - run_len
    group = jnp.sum(run_len, axis=0)
    ptiles = (group + MOE_TM - 1) // MOE_TM
    tile_end = jnp.cumsum(ptiles)
    gstart = (tile_end - ptiles) * MOE_TM
    run_start = gstart[None, :] + jnp.cumsum(run_len, axis=0) - run_len
    tile_expert = jnp.sum(jnp.arange(n_tiles, dtype=jnp.int32)[:, None] >= tile_end[None, :], axis=1)
    i32 = lambda a: a.astype(jnp.int32)
    pos3 = pos.reshape(nb, MOE_TB, TOP_K)
    return dict(
        pos_l=i32(jnp.swapaxes(pos3, 1, 2)),
        pos_c=i32(pos3.reshape(T, TOP_K)),
        loc_off=i32(loc_off.reshape(-1)), run_start=i32(run_start.reshape(-1)), run_len=i32(run_len.reshape(-1)),
        pad_start=i32(gstart + group), pad_len=i32(ptiles * MOE_TM - group),
        tile_expert=i32(jnp.minimum(tile_expert, N_EXPERTS - 1)), n_used=i32(tile_end[-1:]))


def _run_dma(src, dst, src_off, dst_off, length, sem, wait):
    off = 0
    for k in RUN_SIZES:
        part = length & k

        @pl.when(part != 0)
        def _(off=off, k=k):
            cp = pltpu.make_async_copy(src.at[pl.ds(pl.multiple_of(src_off + off, RUN_ALIGN), k)],
                                       dst.at[pl.ds(pl.multiple_of(dst_off + off, RUN_ALIGN), k)], sem)
            if wait:
                cp.wait()
            else:
                cp.start()

        off = off + part


def _dispatch_body(lo_ref, rs_ref, rl_ref, ps_ref, pn_ref, nu_ref, pos_ref, h_ref, xs_hbm, sbuf, zbuf, sem):
    b, nb = pl.program_id(0), pl.num_programs(0)
    slot = b % 2
    n_tiles = xs_hbm.shape[0] // MOE_TM

    def zero_tile(i, wait):
        cp = pltpu.make_async_copy(zbuf, xs_hbm.at[pl.ds(pl.multiple_of(i * MOE_TM, MOE_TM), MOE_TM)], sem.at[2])
        if wait:
            cp.wait()
        else:
            cp.start()

    def runs(blk, s, wait):
        for e in range(N_EXPERTS):
            j = blk * N_EXPERTS + e
            _run_dma(sbuf.at[s], xs_hbm, lo_ref[j], rs_ref[j], rl_ref[j], sem.at[s], wait)

    @pl.when(b >= 2)
    def _():
        runs(b - 2, slot, True)

    pos = pos_ref[0]
    r = lax.broadcasted_iota(jnp.int32, (BLOCK_ROWS, MOE_TB), 0)
    onehot = jnp.logical_or(r == pos[0:1, :], r == pos[1:2, :])
    sbuf[slot] = _dot(jnp.where(onehot, 1.0, 0.0).astype(BF16), h_ref[...])
    runs(b, slot, False)

    @pl.when(b == nb - 1)
    def _():
        zbuf[...] = jnp.zeros_like(zbuf)
        for e in range(N_EXPERTS):
            _run_dma(zbuf, xs_hbm, 0, ps_ref[e], pn_ref[e], sem.at[2], False)
        lax.fori_loop(nu_ref[0], n_tiles, lambda i, c: zero_tile(i, False), None)
        for e in range(N_EXPERTS):
            _run_dma(zbuf, xs_hbm, 0, ps_ref[e], pn_ref[e], sem.at[2], True)
        lax.fori_loop(nu_ref[0], n_tiles, lambda i, c: zero_tile(i, True), None)

        @pl.when(b >= 1)
        def _():
            runs(b - 1, 1 - slot, True)

        runs(b, slot, True)


def _dispatch(hn, meta, n_rows):
    T, D = hn.shape
    nb = T // MOE_TB
    grid_spec = pltpu.PrefetchScalarGridSpec(
        num_scalar_prefetch=6,
        grid=(nb,),
        in_specs=[pl.BlockSpec((1, TOP_K, MOE_TB), lambda b, *_: (b, 0, 0)),
                  pl.BlockSpec((MOE_TB, D), lambda b, *_: (b, 0))],
        out_specs=pl.BlockSpec(memory_space=pl.ANY),
        scratch_shapes=[pltpu.VMEM((2, BLOCK_ROWS, D), F32), pltpu.VMEM((MOE_TM, D), F32),
                        pltpu.SemaphoreType.DMA((3,))],
    )
    return pl.pallas_call(
        _dispatch_body,
        grid_spec=grid_spec,
        out_shape=jax.ShapeDtypeStruct((n_rows, D), F32),
        compiler_params=pltpu.CompilerParams(dimension_semantics=("arbitrary",), vmem_limit_bytes=VMEM_LIMIT,
                                             has_side_effects=True),
        name="moe_dispatch",
    )(meta["loc_off"], meta["run_start"], meta["run_len"], meta["pad_start"], meta["pad_len"], meta["n_used"],
      meta["pos_l"], hn)


def _moe_ffn_body(te_ref, nu_ref, x_ref, wg_hbm, wu_hbm, wd_hbm, o_ref, wg_s, wu_s, wd_s, st_a, st_d, sem):
    i = pl.program_id(0)
    e = te_ref[i]
    active = i < nu_ref[0]
    nf = wg_s.shape[1] // MOE_FC
    fsl = [slice(f * MOE_FC, (f + 1) * MOE_FC) for f in range(nf)]

    @pl.when(jnp.logical_and(active, jnp.logical_or(i == 0, e != te_ref[jnp.maximum(i - 1, 0)])))
    def _():
        slots = st_a.shape[0]

        def row_chunks(stage, w_s):
            rows = stage.shape[1]
            return [slice(r, r + rows) for r in range(0, w_s.shape[0], rows)]

        jobs = ([(wg_hbm, wg_s, st_a, s) for s in row_chunks(st_a, wg_s)]
                + [(wu_hbm, wu_s, st_a, s) for s in row_chunks(st_a, wu_s)]
                + [(wd_hbm, wd_s, st_d, s) for s in row_chunks(st_d, wd_s)])

        def copy(j):
            w_hbm, _, stage, s = jobs[j]
            return pltpu.make_async_copy(w_hbm.at[e, s, :], stage.at[j % slots], sem.at[j % slots])

        for j in range(slots - 1):
            copy(j).start()
        for j, (_, w_s, stage, s) in enumerate(jobs):
            if j + slots - 1 < len(jobs):
                copy(j + slots - 1).start()
            copy(j).wait()
            w_s[s, :] = stage[j % slots].astype(BF16)

    @pl.when(active)
    def _():
        x = x_ref[...].astype(BF16)
        acc = jnp.zeros(o_ref.shape, F32)
        for sl in fsl:
            g = _dot(x, wg_s[:, sl])
            u = _dot(x, wu_s[:, sl])
            acc = acc + _dot((_silu(g) * u).astype(BF16), wd_s[sl, :])
        o_ref[...] = acc

    @pl.when(jnp.logical_not(active))
    def _():
        o_ref[...] = jnp.zeros_like(o_ref)


def _moe_ffn(xs, meta, wg, wu, wd):
    R, D = xs.shape
    E, _, FF = wg.shape
    grid_spec = pltpu.PrefetchScalarGridSpec(
        num_scalar_prefetch=2,
        grid=(R // MOE_TM,),
        in_specs=[pl.BlockSpec((MOE_TM, D), lambda i, te, nu: (jnp.minimum(i, nu[0] - 1), 0)),
                  pl.BlockSpec(memory_space=pl.ANY), pl.BlockSpec(memory_space=pl.ANY),
                  pl.BlockSpec(memory_space=pl.ANY)],
        out_specs=pl.BlockSpec((MOE_TM, D), lambda i, te, nu: (i, 0)),
        scratch_shapes=[pltpu.VMEM((D, FF), BF16), pltpu.VMEM((D, FF), BF16), pltpu.VMEM((FF, D), BF16),
                        pltpu.VMEM((MOE_LOAD_SLOTS, MOE_LOAD_ROWS * D // FF, FF), F32),
                        pltpu.VMEM((MOE_LOAD_SLOTS, MOE_LOAD_ROWS, D), F32),
                        pltpu.SemaphoreType.DMA((MOE_LOAD_SLOTS,))],
    )
    return pl.pallas_call(
        _moe_ffn_body,
        grid_spec=grid_spec,
        out_shape=jax.ShapeDtypeStruct((R, D), F32),
        compiler_params=_params("arbitrary"),
        name="moe_ffn",
    )(meta["tile_expert"], meta["n_used"], xs, wg, wu, wd)


def _combine_body(lo_ref, rs_ref, rl_ref, ys_hbm, x_ref, rt_ref, pc_ref, g_ref, o_ref, ybuf, sem):
    b, nb = pl.program_id(0), pl.num_programs(0)
    slot = b % 2

    def runs(blk, s, wait):
        for e in range(N_EXPERTS):
            j = blk * N_EXPERTS + e
            _run_dma(ys_hbm, ybuf.at[s], rs_ref[j], lo_ref[j], rl_ref[j], sem.at[s], wait)

    @pl.when(b == 0)
    def _():
        ybuf[...] = jnp.zeros_like(ybuf)
        runs(0, 0, False)

    @pl.when(b + 1 < nb)
    def _():
        runs(b + 1, 1 - slot, False)

    runs(b, slot, True)
    y = ybuf[slot].astype(BF16)
    pc, rt = pc_ref[...], rt_ref[...]
    lane = lax.broadcasted_iota(jnp.int32, (MOE_TB, BLOCK_LANES), 1)
    q0 = jnp.where(lane == pc[:, 0:1], 1.0, 0.0).astype(BF16)
    q1 = jnp.where(lane == pc[:, 1:2], 1.0, 0.0).astype(BF16)
    moe = rt[:, 2:3] * _dot(q0, y) + rt[:, 3:4] * _dot(q1, y)
    o_ref[...] = _rms(x_ref[...] + moe, g_ref[...])


def _combine(ys, meta, x, rt, g):
    T, D = x.shape
    grid_spec = pltpu.PrefetchScalarGridSpec(
        num_scalar_prefetch=3,
        grid=(T // MOE_TB,),
        in_specs=[pl.BlockSpec(memory_space=pl.ANY),
                  pl.BlockSpec((MOE_TB, D), lambda b, *_: (b, 0)),
                  pl.BlockSpec((MOE_TB, LANES), lambda b, *_: (b, 0)),
                  pl.BlockSpec((MOE_TB, TOP_K), lambda b, *_: (b, 0)),
                  pl.BlockSpec((1, D), lambda b, *_: (0, 0))],
        out_specs=pl.BlockSpec((MOE_TB, D), lambda b, *_: (b, 0)),
        scratch_shapes=[pltpu.VMEM((2, BLOCK_LANES, D), F32), pltpu.SemaphoreType.DMA((2,))],
    )
    return pl.pallas_call(
        _combine_body,
        grid_spec=grid_spec,
        out_shape=jax.ShapeDtypeStruct((T, D), F32),
        compiler_params=_params("arbitrary"),
        name="moe_combine",
    )(meta["loc_off"], meta["run_start"], meta["run_len"], ys, x, rt, meta["pos_c"], g.reshape(1, D))


def kernel(x, norm_mix_g, w_in, ret_gn_g, attn_sinks, conv_dw_w, conv_dw_b, conv_ln_g, conv_ln_b,
           conv_pw_w, w_out, norm_ffn_g, ffn_w_gate, ffn_w_up, ffn_w_down, moe_router, moe_w_gate,
           moe_w_up, moe_w_down, final_norm_g):
    B, S, D = x.shape
    T = B * S
    depth = w_in.shape[0]
    assert depth == 2 and ffn_w_gate.shape[0] == 1 and moe_router.shape[0] == 1, "dense layer then MoE layer"
    assert T % MOE_TB == 0 and S % MOE_TB == 0
    assert moe_w_gate.shape[-1] % MOE_FC == 0 and moe_w_gate.shape[-1] % MOE_LOAD_ROWS == 0

    for l in range(depth):
        rq, rk, rv, rg, sq, sk, sv, ch = _inproj(x, norm_mix_g[l], w_in[l].astype(BF16), tm=1024)
        y_ret_t = _retention(rq, rk, rv, rg, ret_gn_g[l], ts=4096)
        y_swa_t = _swa(sq, sk, sv, attn_sinks[l], tq=4096)
        y_conv = _conv(ch, conv_dw_w[l], conv_dw_b[l], conv_ln_g[l], conv_ln_b[l], conv_pw_w[l], ts=1024, rb=64)
        flat = lambda a: a.reshape(T, a.shape[-1])
        j = l // 2
        if l % 2 == 0:
            x = _outproj_dense_ffn(y_ret_t, y_swa_t, flat(y_conv), w_out[l].astype(BF16), flat(x),
                                   norm_ffn_g[l], ffn_w_gate[j], ffn_w_up[j], ffn_w_down[j],
                                   tm=1024, fc=512).reshape(B, S, D)
        else:
            x2, hn, rt, rl = _outproj_route(y_ret_t, y_swa_t, flat(y_conv), w_out[l].astype(BF16), flat(x),
                                            norm_ffn_g[l], moe_router[j])
            n_rows = TOP_K * T + (T // MOE_TB) * N_EXPERTS * RUN_ALIGN + N_EXPERTS * MOE_TM
            meta = _route_meta(rt, rl, n_rows // MOE_TM)
            xs = _dispatch(hn, meta, n_rows)
            ys = _moe_ffn(xs, meta, moe_w_gate[j], moe_w_up[j], moe_w_down[j])
            x = _combine(ys, meta, x2, rt, final_norm_g).reshape(B, S, D)
    return x
```

```python
import functools

import jax
import jax.numpy as jnp
from jax import lax
from jax.experimental import pallas as pl
from jax.experimental.pallas import tpu as pltpu

F32 = jnp.float32
BF16 = jnp.bfloat16

HEAD_DIM = 64
RET_HEADS = 4
RET_W = RET_HEADS * HEAD_DIM
SWA_HEADS = 8
SWA_KV_HEADS = 2
SWA_GROUP = SWA_HEADS // SWA_KV_HEADS
SWA_W = SWA_HEADS * HEAD_DIM
SWA_KV_W = SWA_KV_HEADS * HEAD_DIM
CONV_CH = 256
CONV_GROUPS = 4
CONV_WIDTH = 31
WINDOW = 128
BLOCK = 128
RET_CHUNK = 128
ROPE_THETA = 500000.0
ROPE_DIM = HEAD_DIM // 4
RET_ROPE_THETA = 10000.0
N_EXPERTS = 8
TOP_K = 2
EPS = 1e-6
NEG_INF = -1e30

LANES = 128
CONV_HALO = 32
VMEM_LIMIT = 56 * 1024 * 1024

O_RQ, O_RK, O_RV, O_RG = 0, RET_W, 2 * RET_W, 3 * RET_W
O_SQ = 4 * RET_W
O_SK = O_SQ + SWA_W
O_SV = O_SK + SWA_KV_W
O_CA = O_SV + SWA_KV_W
O_CG = O_CA + CONV_CH
D_IN = O_CG + CONV_CH


def _params(*sem, flags=None):
    return pltpu.CompilerParams(dimension_semantics=sem, vmem_limit_bytes=VMEM_LIMIT, flags=flags)


def _rms(x, g):
    return x * lax.rsqrt(jnp.mean(x * x, axis=-1, keepdims=True) + EPS) * g


def _silu(x):
    return x * jax.nn.sigmoid(x)


def _dot(a, b):
    return jnp.dot(a, b, preferred_element_type=F32)


def _dot_nt(a, b):
    return lax.dot_general(a, b, (((1,), (1,)), ((), ())), preferred_element_type=F32)


def _dot_tn(a, b):
    return lax.dot_general(a, b, (((0,), (0,)), ((), ())), preferred_element_type=F32)


def _rope_tables(seq, theta, rot_dim):
    half = rot_dim // 2
    inv = 1.0 / (theta ** (jnp.arange(half, dtype=F32) / half))
    ang = jnp.arange(seq, dtype=F32)[:, None] * inv[None, :]
    cos, sin = jnp.cos(ang), jnp.sin(ang)
    rest = HEAD_DIM - rot_dim
    c = jnp.concatenate([cos, cos, jnp.ones((seq, rest), F32)], axis=1)
    s = jnp.concatenate([-sin, sin, jnp.zeros((seq, rest), F32)], axis=1)
    reps = LANES // HEAD_DIM
    return jnp.tile(c, (1, reps)), jnp.tile(s, (1, reps))


def _inproj_body(x_ref, g_ref, w_ref, wgt_ref, rc_ref, rs_ref, sc_ref, ss_ref,
                 rq_ref, rk_ref, rv_ref, rg_ref, sq_ref, sk_ref, sv_ref, ch_ref, zbuf):
    h = _rms(x_ref[0], g_ref[...]).astype(BF16)
    lane = lax.broadcasted_iota(jnp.int32, (1, LANES), 1) % HEAD_DIM
    for a, b in ((O_RQ, O_RK), (O_RK, O_RV), (O_RV, O_RG), (O_SQ, O_SK), (O_SK, O_CA), (O_CA, O_CG), (O_CG, D_IN)):
        zbuf[:, a:b] = _dot(h, w_ref[:, a:b])

    def seg(a, b):
        return zbuf[:, a:b]

    def rope(v, c, s, half):
        up = pltpu.roll(v, LANES - half, axis=1)
        dn = pltpu.roll(v, half, axis=1)
        return v * c + jnp.where(lane < half, up, dn) * s

    def store_heads(z, o_ref, first=0):
        for j in range(z.shape[1] // HEAD_DIM):
            o_ref[0, first + j] = z[:, j * HEAD_DIM:(j + 1) * HEAD_DIM]

    def rope_store(z, o_ref, c, s, half, scale):
        for i in range(z.shape[1] // LANES):
            r = rope(z[:, i * LANES:(i + 1) * LANES], c, s, half)
            if scale != 1.0:
                r = r * scale
            store_heads(r.astype(BF16), o_ref, first=i * (LANES // HEAD_DIM))

    rc, rs = rc_ref[...], rs_ref[...]
    sc, ss = sc_ref[...], ss_ref[...]
    scale = HEAD_DIM ** -0.5
    rope_store(seg(O_RQ, O_RK), rq_ref, rc, rs, HEAD_DIM // 2, 1.0)
    rope_store(seg(O_RK, O_RV), rk_ref, rc, rs, HEAD_DIM // 2, scale)
    store_heads(seg(O_RV, O_RG).astype(BF16), rv_ref)
    rg_ref[0] = _dot_nt(wgt_ref[...], h)
    rope_store(seg(O_SQ, O_SK), sq_ref, sc, ss, ROPE_DIM // 2, scale)
    rope_store(seg(O_SK, O_SV), sk_ref, sc, ss, ROPE_DIM // 2, 1.0)
    store_heads(seg(O_SV, O_CA).astype(BF16), sv_ref)
    ch_ref[0] = seg(O_CA, O_CG) * jax.nn.sigmoid(seg(O_CG, D_IN))


def _inproj(x, g, w_bf, tm):
    B, S, D = x.shape
    tm = min(tm, S)
    rc, rs = _rope_tables(S, RET_ROPE_THETA, HEAD_DIM)
    sc, ss = _rope_tables(S, ROPE_THETA, ROPE_DIM)
    tab = pl.BlockSpec((tm, LANES), lambda s, b: (s, 0))

    def out(width, dtype):
        return (jax.ShapeDtypeStruct((B, S, width), dtype),
                pl.BlockSpec((1, tm, width), lambda s, b: (b, s, 0)))

    def out_heads(heads):
        return (jax.ShapeDtypeStruct((B, heads, S, HEAD_DIM), BF16),
                pl.BlockSpec((1, heads, tm, HEAD_DIM), lambda s, b: (b, 0, s, 0)))

    gate_t = (jax.ShapeDtypeStruct((B, RET_W, S), F32), pl.BlockSpec((1, RET_W, tm), lambda s, b: (b, 0, s)))
    outs = [out_heads(RET_HEADS), out_heads(RET_HEADS), out_heads(RET_HEADS), gate_t,
            out_heads(SWA_HEADS), out_heads(SWA_KV_HEADS), out_heads(SWA_KV_HEADS), out(CONV_CH, F32)]
    return pl.pallas_call(
        _inproj_body,
        grid=(S // tm, B),
        in_specs=[pl.BlockSpec((1, tm, D), lambda s, b: (b, s, 0)),
                  pl.BlockSpec((1, D), lambda s, b: (0, 0)),
                  pl.BlockSpec((D, D_IN), lambda s, b: (0, 0)),
                  pl.BlockSpec((RET_W, D), lambda s, b: (0, 0)),
                  tab, tab, tab, tab],
        out_specs=[o[1] for o in outs],
        out_shape=[o[0] for o in outs],
        scratch_shapes=[pltpu.VMEM((tm, D_IN), F32)],
        compiler_params=_params("arbitrary", "arbitrary"),
        name="inproj",
    )(x, g.reshape(1, D), w_bf, w_bf[:, O_RG:O_SQ].T, rc, rs, sc, ss)


def _ret_body(q_ref, k_ref, v_ref, g_ref, dm_ref, kd_ref, qd_ref, cd_ref, gn_ref, o_ref, st_ref, *, ts):
    @pl.when(pl.program_id(2) == 0)
    def _():
        st_ref[...] = jnp.zeros_like(st_ref)

    C = RET_CHUNK
    for c in range(ts // C):
        rows = slice(c * C, (c + 1) * C)
        q, k, v = q_ref[0, 0, rows, :], k_ref[0, 0, rows, :], v_ref[0, 0, rows, :]
        st = st_ref[...]
        scores = _dot_nt(k, q) * dm_ref[0]
        intra = _dot_tn(v, scores.astype(BF16))
        cross = _dot_nt(st.astype(BF16), q) * qd_ref[0]
        kdec = (k.astype(F32) * kd_ref[0]).astype(BF16)
        st_ref[...] = st * cd_ref[0] + _dot_tn(v, kdec)
        o = intra + cross
        mu = jnp.mean(o, axis=0, keepdims=True)
        d = o - mu
        var = jnp.mean(d * d, axis=0, keepdims=True)
        on = d * lax.rsqrt(var + EPS) * gn_ref[0]
        o_ref[0, :, rows] = (_silu(g_ref[0, :, rows]) * on).astype(BF16)


def _retention(rq, rk, rv, rg_t, gn_g, ts):
    B, H, S, D = rq.shape
    ts = min(ts, S)
    C, W = RET_CHUNK, RET_W
    lg = jnp.log(1.0 - 2.0 ** (-5.0 - jnp.arange(H, dtype=F32)))
    idx = jnp.arange(C)
    rel = idx[:, None] - idx[None, :]
    dmask = jnp.where(rel[None] >= 0,
                      jnp.exp(jnp.maximum(rel, 0)[None].astype(F32) * lg[:, None, None]), 0.0)
    k_decay = jnp.exp((C - 1 - idx)[:, None].astype(F32) * lg[None, :])
    q_decay = jnp.exp((idx + 1)[:, None].astype(F32) * lg[None, :])
    chunk_decay = jnp.exp(C * lg)
    dm_t = jnp.swapaxes(dmask, 1, 2)
    kd = jnp.broadcast_to(k_decay.T[:, :, None], (H, C, D))
    qd = jnp.broadcast_to(q_decay.T[:, None, :], (H, D, C))
    cd = jnp.broadcast_to(chunk_decay[:, None, None], (H, D, D))
    gn = jnp.broadcast_to(gn_g.reshape(H, D, 1), (H, D, C))
    heads = pl.BlockSpec((1, 1, ts, D), lambda b, h, n: (b, h, n, 0))
    chan = pl.BlockSpec((1, D, ts), lambda b, h, n: (b, h, n))

    def per_head(r, c):
        return pl.BlockSpec((1, r, c), lambda b, h, n: (h, 0, 0))

    return pl.pallas_call(
        functools.partial(_ret_body, ts=ts),
        grid=(B, H, S // ts),
        in_specs=[heads, heads, heads, chan, per_head(C, C), per_head(C, D), per_head(D, C),
                  per_head(D, D), per_head(D, C)],
        out_specs=chan,
        out_shape=jax.ShapeDtypeStruct((B, W, S), BF16),
        scratch_shapes=[pltpu.VMEM((D, D), F32)],
        compiler_params=_params("arbitrary", "arbitrary", "arbitrary"),
        name="retention",
    )(rq, rk, rv, rg_t, dm_t, kd, qd, cd, gn)


def _swa_body(sink_ref, q_ref, kc_ref, kp_ref, vc_ref, vp_ref, bias_ref, o_ref, kcat, vcat, sbuf, pbuf, rbuf,
              *, tq):
    hk, n = pl.program_id(1), pl.program_id(2)
    kcat[0:BLOCK], kcat[BLOCK:] = kp_ref[0, 0], kc_ref[0, 0]
    vcat[0:BLOCK], vcat[BLOCK:] = vp_ref[0, 0], vc_ref[0, 0]
    width = SWA_GROUP * BLOCK
    group = lax.broadcasted_iota(jnp.int32, (1, width), 1) // BLOCK
    sink = jnp.zeros((1, width), F32)
    for g in range(SWA_GROUP):
        sink = jnp.where(group == g, sink_ref[hk * SWA_GROUP + g], sink)
    nq = tq // BLOCK
    for j in range(nq):
        q = q_ref[0, :, j * BLOCK:(j + 1) * BLOCK, :].reshape(width, HEAD_DIM)
        kb = kcat[j * BLOCK:(j + 2) * BLOCK]
        sbuf[j] = _dot_nt(kb, q)
    for j in range(nq):
        s = sbuf[j] + (bias_ref[0] if j > 0 else bias_ref[jnp.where(n == 0, 1, 0)])
        m = jnp.maximum(jnp.max(s, axis=0, keepdims=True), sink)
        p = jnp.exp(s - m)
        rbuf[j] = 1.0 / (jnp.sum(p, axis=0, keepdims=True) + jnp.exp(sink - m))
        pbuf[j] = p.astype(BF16)
    for j in range(nq):
        vb = vcat[j * BLOCK:(j + 2) * BLOCK]
        o = _dot_tn(vb, pbuf[j]) * rbuf[j]
        for g in range(SWA_GROUP):
            o_ref[0, g * HEAD_DIM:(g + 1) * HEAD_DIM, j * BLOCK:(j + 1) * BLOCK] = (
                o[:, g * BLOCK:(g + 1) * BLOCK].astype(BF16))


def _swa(sq, sk, sv, sinks, tq):
    B, _, S, D = sq.shape
    tq = min(tq, S)
    r = tq // BLOCK
    qi = jnp.arange(BLOCK)[None, :] + BLOCK
    kj = jnp.arange(2 * BLOCK)[:, None]
    rel = qi - kj
    allowed = (rel >= 0) & (rel < WINDOW)
    allowed = jnp.stack([allowed, allowed & (kj >= BLOCK)])
    bias = jnp.tile(jnp.where(allowed, 0.0, NEG_INF).astype(F32), (1, 1, SWA_GROUP))
    nq, width = tq // BLOCK, SWA_GROUP * BLOCK
    cur = pl.BlockSpec((1, 1, tq, D), lambda b, h, n: (b, h, n, 0))
    prev = pl.BlockSpec((1, 1, BLOCK, D), lambda b, h, n: (b, h, jnp.maximum(n * r - 1, 0), 0))
    return pl.pallas_call(
        functools.partial(_swa_body, tq=tq),
        grid=(B, SWA_KV_HEADS, S // tq),
        in_specs=[pl.BlockSpec(memory_space=pltpu.SMEM),
                  pl.BlockSpec((1, SWA_GROUP, tq, D), lambda b, h, n: (b, h, n, 0)),
                  cur, prev, cur, prev,
                  pl.BlockSpec((2, 2 * BLOCK, width), lambda b, h, n: (0, 0, 0))],
        out_specs=pl.BlockSpec((1, SWA_GROUP * D, tq), lambda b, h, n: (b, h, n)),
        out_shape=jax.ShapeDtypeStruct((B, SWA_W, S), BF16),
        scratch_shapes=[pltpu.VMEM((BLOCK + tq, D), BF16), pltpu.VMEM((BLOCK + tq, D), BF16),
                        pltpu.VMEM((nq, 2 * BLOCK, width), F32), pltpu.VMEM((nq, 2 * BLOCK, width), BF16),
                        pltpu.VMEM((nq, 1, width), F32)],
        compiler_params=_params("arbitrary", "arbitrary", "arbitrary"),
        name="swa",
    )(sinks.astype(F32), sq, sk, sk, sv, sv, bias)


def _conv_body(h_ref, dw_ref, db_ref, lg_ref, lb_ref, pw_ref, o_ref, hbuf, hsh, *, ts, rb):
    s = pl.program_id(1)
    sub = 8
    span = ts + CONV_HALO - sub

    @pl.when(s == 0)
    def _():
        hbuf[0:CONV_HALO, :] = jnp.zeros((CONV_HALO, CONV_CH), F32)

    @pl.when(s > 0)
    def _():
        hbuf[0:CONV_HALO, :] = hbuf[ts:ts + CONV_HALO, :]

    hbuf[CONV_HALO:CONV_HALO + ts, :] = h_ref[0]
    for p in range(1, sub):
        hsh[p - 1] = hbuf[p:p + span, :]
    off = CONV_HALO - (CONV_WIDTH - 1)
    for r in range(ts // rb):
        base = r * rb
        acc = jnp.broadcast_to(db_ref[...], (rb, CONV_CH))
        for w in range(CONV_WIDTH):
            q, p = divmod(off + w, sub)
            lo = base + q * sub
            tap = hbuf[lo:lo + rb, :] if p == 0 else hsh[p - 1, lo:lo + rb, :]
            acc = acc + tap * dw_ref[w:w + 1, :]
        mu = jnp.mean(acc, axis=-1, keepdims=True)
        d = acc - mu
        var = jnp.mean(d * d, axis=-1, keepdims=True)
        hn = d * lax.rsqrt(var + EPS) * lg_ref[...] + lb_ref[...]
        o_ref[0, base:base + rb, :] = _dot(_silu(hn).astype(BF16), pw_ref[...]).astype(BF16)


def _conv(ch, dw_w, dw_b, ln_g, ln_b, pw_w, ts, rb):
    B, S, W = ch.shape
    ts = min(ts, S)
    rb = min(rb, ts)
    pw = jnp.zeros((W, W), F32)
    gd = W // CONV_GROUPS
    for g in range(CONV_GROUPS):
        pw = pw.at[g * gd:(g + 1) * gd, g * gd:(g + 1) * gd].set(pw_w[g])
    act = pl.BlockSpec((1, ts, W), lambda b, s: (b, s, 0))

    def const(shape):
        return pl.BlockSpec(shape, lambda b, s: (0,) * len(shape))

    return pl.pallas_call(
        functools.partial(_conv_body, ts=ts, rb=rb),
        grid=(B, S // ts),
        in_specs=[act, const((CONV_WIDTH, W)), const((1, W)), const((1, W)), const((1, W)), const((W, W))],
        out_specs=act,
        out_shape=jax.ShapeDtypeStruct((B, S, W), BF16),
        scratch_shapes=[pltpu.VMEM((CONV_HALO + ts, W), F32), pltpu.VMEM((7, CONV_HALO + ts - 8, W), F32)],
        compiler_params=_params("arbitrary", "arbitrary"),
        name="conv",
    )(ch, dw_w, dw_b.reshape(1, W), ln_g.reshape(1, W), ln_b.reshape(1, W), pw.astype(BF16))


def _mixer_residual(yr_ref, ys_ref, yc_ref, w_ref, x_ref):
    a, b = RET_W, RET_W + SWA_W
    return x_ref[...] + (_dot_tn(yr_ref[0], w_ref[0:a, :]) + _dot_tn(ys_ref[0], w_ref[a:b, :])
                         + _dot(yc_ref[...], w_ref[b:, :]))


def _mixer_specs(tm, tiles_per_seq):
    def chan(width):
        return pl.BlockSpec((1, width, tm), lambda i: (i // tiles_per_seq, 0, i % tiles_per_seq))

    return [chan(RET_W), chan(SWA_W), pl.BlockSpec((tm, CONV_CH), lambda i: (i, 0))]


def _outproj_body(yr_ref, ys_ref, yc_ref, w_ref, x_ref, g_ref, wr_ref, before_ref, xo_ref, hn_ref, rt_ref, rl_ref):
    x = _mixer_residual(yr_ref, ys_ref, yc_ref, w_ref, x_ref)
    xo_ref[...] = x
    h = _rms(x, g_ref[...]).astype(BF16)
    hn_ref[...] = h
    logits = _dot(h, wr_ref[...])
    lane = lax.broadcasted_iota(jnp.int32, logits.shape, 1).astype(F32)
    lg = jnp.where(lane < N_EXPERTS, logits, -jnp.inf)
    m1 = jnp.max(lg, axis=-1, keepdims=True)
    i1 = jnp.min(jnp.where(lg == m1, lane, float(LANES)), axis=-1, keepdims=True)
    lg2 = jnp.where(lane == i1, -jnp.inf, lg)
    m2 = jnp.max(lg2, axis=-1, keepdims=True)
    i2 = jnp.min(jnp.where(lg2 == m2, lane, float(LANES)), axis=-1, keepdims=True)
    e = jnp.exp(m2 - m1)
    w1 = 1.0 / (1.0 + e)
    w2 = e / (1.0 + e)
    hit1, hit2 = lane == i1, lane == i2
    onehot = jnp.where(jnp.logical_or(hit1, hit2), 1.0, 0.0)
    rank = _dot(before_ref[...], onehot.astype(BF16))
    count = jnp.sum(onehot, axis=0, keepdims=True).astype(jnp.int32)
    run_len = (count + (RUN_ALIGN - 1)) & ~(RUN_ALIGN - 1)
    lower = (lax.broadcasted_iota(jnp.int32, (LANES, LANES), 0) < lax.broadcasted_iota(jnp.int32, (LANES, LANES), 1))
    run_rows = jnp.broadcast_to(run_len.astype(F32), (8, LANES)).astype(BF16)
    loc_off = _dot(run_rows, jnp.where(lower, 1.0, 0.0).astype(BF16))[0:1, :]
    pos = rank + loc_off
    p1 = jnp.sum(jnp.where(hit1, pos, 0.0), axis=-1, keepdims=True)
    p2 = jnp.sum(jnp.where(hit2, pos, 0.0), axis=-1, keepdims=True)
    cols = (i1, i2, w1, w2, p1, p2)
    rt = jnp.zeros_like(logits)
    for c, v in enumerate(cols):
        rt = jnp.where(lane == c, v, rt)
    rt_ref[...] = rt
    rl_ref[0] = jnp.broadcast_to(run_len, (8, LANES))


def _outproj_route(yr, ys, yc, w_bf, x, g, w_router):
    T, D = x.shape
    tm = MOE_TB
    S = ys.shape[2]

    def row(width):
        return pl.BlockSpec((tm, width), lambda i: (i, 0))

    def const(shape):
        return pl.BlockSpec(shape, lambda i: (0,) * len(shape))

    wr = jnp.zeros((D, LANES), F32).at[:, :N_EXPERTS].set(w_router).astype(BF16)
    before = (jnp.arange(tm)[:, None] > jnp.arange(tm)[None, :]).astype(BF16)
    return pl.pallas_call(
        _outproj_body,
        grid=(T // tm,),
        in_specs=_mixer_specs(tm, S // tm) + [const((D, D)), row(D), const((1, D)), const((D, LANES)),
                                              const((tm, tm))],
        out_specs=[row(D), row(D), row(LANES), pl.BlockSpec((1, 8, LANES), lambda i: (i, 0, 0))],
        out_shape=[jax.ShapeDtypeStruct((T, D), F32), jax.ShapeDtypeStruct((T, D), BF16),
                   jax.ShapeDtypeStruct((T, LANES), F32), jax.ShapeDtypeStruct((T // tm, 8, LANES), jnp.int32)],
        compiler_params=_params("arbitrary"),
        name="outproj_route",
    )(yr, ys, yc, w_bf, x, g.reshape(1, D), wr, before)


def _dense_ffn_body(yr_ref, ys_ref, yc_ref, wo_ref, x_ref, g_ref, wg_ref, wu_ref, wd_ref, o_ref, *, chunks):
    x = _mixer_residual(yr_ref, ys_ref, yc_ref, wo_ref, x_ref)
    h = _rms(x, g_ref[...]).astype(BF16)
    acc = x
    for a, b in chunks:
        g = _dot(h, wg_ref[:, a:b])
        u = _dot(h, wu_ref[:, a:b])
        acc = acc + _dot((_silu(g) * u).astype(BF16), wd_ref[a:b, :])
    o_ref[...] = acc


def _outproj_dense_ffn(yr, ys, yc, wo_bf, x, g, wg, wu, wd, tm, fc):
    T, D = x.shape
    FF = wg.shape[1]
    S = ys.shape[2]
    tm = min(tm, S)
    chunks = tuple((a, min(a + fc, FF)) for a in range(0, FF, fc))

    def row(width):
        return pl.BlockSpec((tm, width), lambda i: (i, 0))

    def const(shape):
        return pl.BlockSpec(shape, lambda i: (0,) * len(shape), pipeline_mode=pl.Buffered(1))

    return pl.pallas_call(
        functools.partial(_dense_ffn_body, chunks=chunks),
        grid=(T // tm,),
        in_specs=_mixer_specs(tm, S // tm) + [const((D, D)), row(D), const((1, D)),
                                              const((D, FF)), const((D, FF)), const((FF, D))],
        out_specs=row(D),
        out_shape=jax.ShapeDtypeStruct((T, D), F32),
        compiler_params=_params("arbitrary"),
        name="outproj_dense_ffn",
    )(yr, ys, yc, wo_bf, x, g.reshape(1, D), wg.astype(BF16), wu.astype(BF16), wd.astype(BF16))


MOE_TB = 512
MOE_TM = 512
RUN_ALIGN = 8
RUN_SIZES = (512, 256, 128, 64, 32, 16, 8)
BLOCK_ROWS = TOP_K * MOE_TB + N_EXPERTS * RUN_ALIGN
BLOCK_LANES = 1152
MOE_FC = 512
MOE_LOAD_ROWS = 224
MOE_LOAD_SLOTS = 4


def _route_meta(rt, rl, n_tiles):
    T = rt.shape[0]
    nb = T // MOE_TB
    pos = rt[:, 4:4 + TOP_K].astype(jnp.int32)
    run_len = rl[:, 0, :N_EXPERTS]
    loc_off = jnp.cumsum(run_len, axis=1) - run_len
    group = jnp.sum(run_len, axis=0)
    ptiles = (group + MOE_TM - 1) // MOE_TM
    tile_end = jnp.cumsum(ptiles)
    gstart = (tile_end - ptiles) * MOE_TM
    run_start = gstart[None, :] + jnp.cumsum(run_len, axis=0) - run_len
    tile_expert = jnp.sum(jnp.arange(n_tiles, dtype=jnp.int32)[:, None] >= tile_end[None, :], axis=1)
    i32 = lambda a: a.astype(jnp.int32)
    pos3 = pos.reshape(nb, MOE_TB, TOP_K)
    return dict(
        pos_l=i32(jnp.swapaxes(pos3, 1, 2)),
        pos_c=i32(pos3.reshape(T, TOP_K)),
        loc_off=i32(loc_off.reshape(-1)), run_start=i32(run_start.reshape(-1)), run_len=i32(run_len.reshape(-1)),
        pad_start=i32(gstart + group), pad_len=i32(ptiles * MOE_TM - group),
        tile_expert=i32(jnp.minimum(tile_expert, N_EXPERTS - 1)), n_used=i32(tile_end[-1:]))


def _run_dma(src, dst, src_off, dst_off, length, sem, wait):
    off = 0
    for k in RUN_SIZES:
        part = length & k

        @pl.when(part != 0)
        def _(off=off, k=k):
            cp = pltpu.make_async_copy(src.at[pl.ds(pl.multiple_of(src_off + off, RUN_ALIGN), k)],
                                       dst.at[pl.ds(pl.multiple_of(dst_off + off, RUN_ALIGN), k)], sem)
            if wait:
                cp.wait()
            else:
                cp.start()

        off = off + part


def _dispatch_body(lo_ref, rs_ref, rl_ref, ps_ref, pn_ref, nu_ref, pos_ref, h_ref, xs_hbm, sbuf, zbuf, sem):
    b, nb = pl.program_id(0), pl.num_programs(0)
    slot = b % 2
    n_tiles = xs_hbm.shape[0] // MOE_TM

    def zero_tile(i, wait):
        cp = pltpu.make_async_copy(zbuf, xs_hbm.at[pl.ds(pl.multiple_of(i * MOE_TM, MOE_TM), MOE_TM)], sem.at[2])
        if wait:
            cp.wait()
        else:
            cp.start()

    def runs(blk, s, wait):
        for e in range(N_EXPERTS):
            j = blk * N_EXPERTS + e
            _run_dma(sbuf.at[s], xs_hbm, lo_ref[j], rs_ref[j], rl_ref[j], sem.at[s], wait)

    @pl.when(b >= 2)
    def _():
        runs(b - 2, slot, True)

    pos = pos_ref[0]
    r = lax.broadcasted_iota(jnp.int32, (BLOCK_ROWS, MOE_TB), 0)
    onehot = jnp.logical_or(r == pos[0:1, :], r == pos[1:2, :])
    sbuf[slot] = _dot(jnp.where(onehot, 1.0, 0.0).astype(BF16), h_ref[...])
    runs(b, slot, False)

    @pl.when(b == nb - 1)
    def _():
        zbuf[...] = jnp.zeros_like(zbuf)
        for e in range(N_EXPERTS):
            _run_dma(zbuf, xs_hbm, 0, ps_ref[e], pn_ref[e], sem.at[2], False)
        lax.fori_loop(nu_ref[0], n_tiles, lambda i, c: zero_tile(i, False), None)
        for e in range(N_EXPERTS):
            _run_dma(zbuf, xs_hbm, 0, ps_ref[e], pn_ref[e], sem.at[2], True)
        lax.fori_loop(nu_ref[0], n_tiles, lambda i, c: zero_tile(i, True), None)

        @pl.when(b >= 1)
        def _():
            runs(b - 1, 1 - slot, True)

        runs(b, slot, True)


def _dispatch(hn, meta, n_rows):
    T, D = hn.shape
    nb = T // MOE_TB
    grid_spec = pltpu.PrefetchScalarGridSpec(
        num_scalar_prefetch=6,
        grid=(nb,),
        in_specs=[pl.BlockSpec((1, TOP_K, MOE_TB), lambda b, *_: (b, 0, 0)),
                  pl.BlockSpec((MOE_TB, D), lambda b, *_: (b, 0))],
        out_specs=pl.BlockSpec(memory_space=pl.ANY),
        scratch_shapes=[pltpu.VMEM((2, BLOCK_ROWS, D), F32), pltpu.VMEM((MOE_TM, D), F32),
                        pltpu.SemaphoreType.DMA((3,))],
    )
    return pl.pallas_call(
        _dispatch_body,
        grid_spec=grid_spec,
        out_shape=jax.ShapeDtypeStruct((n_rows, D), F32),
        compiler_params=pltpu.CompilerParams(dimension_semantics=("arbitrary",), vmem_limit_bytes=VMEM_LIMIT,
                                             has_side_effects=True),
        name="moe_dispatch",
    )(meta["loc_off"], meta["run_start"], meta["run_len"], meta["pad_start"], meta["pad_len"], meta["n_used"],
      meta["pos_l"], hn)


def _moe_ffn_body(te_ref, nu_ref, x_ref, wg_hbm, wu_hbm, wd_hbm, o_ref, wg_s, wu_s, wd_s, st_a, st_d, sem):
    i = pl.program_id(0)
    e = te_ref[i]
    active = i < nu_ref[0]
    nf = wg_s.shape[1] // MOE_FC
    fsl = [slice(f * MOE_FC, (f + 1) * MOE_FC) for f in range(nf)]

    @pl.when(jnp.logical_and(active, jnp.logical_or(i == 0, e != te_ref[jnp.maximum(i - 1, 0)])))
    def _():
        slots = st_a.shape[0]

        def row_chunks(stage, w_s):
            rows = stage.shape[1]
            return [slice(r, r + rows) for r in range(0, w_s.shape[0], rows)]

        jobs = ([(wg_hbm, wg_s, st_a, s) for s in row_chunks(st_a, wg_s)]
                + [(wu_hbm, wu_s, st_a, s) for s in row_chunks(st_a, wu_s)]
                + [(wd_hbm, wd_s, st_d, s) for s in row_chunks(st_d, wd_s)])

        def copy(j):
            w_hbm, _, stage, s = jobs[j]
            return pltpu.make_async_copy(w_hbm.at[e, s, :], stage.at[j % slots], sem.at[j % slots])

        for j in range(slots - 1):
            copy(j).start()
        for j, (_, w_s, stage, s) in enumerate(jobs):
            if j + slots - 1 < len(jobs):
                copy(j + slots - 1).start()
            copy(j).wait()
            w_s[s, :] = stage[j % slots].astype(BF16)

    @pl.when(active)
    def _():
        x = x_ref[...].astype(BF16)
        acc = jnp.zeros(o_ref.shape, F32)
        for sl in fsl:
            g = _dot(x, wg_s[:, sl])
            u = _dot(x, wu_s[:, sl])
            acc = acc + _dot((_silu(g) * u).astype(BF16), wd_s[sl, :])
        o_ref[...] = acc

    @pl.when(jnp.logical_not(active))
    def _():
        o_ref[...] = jnp.zeros_like(o_ref)


def _moe_ffn(xs, meta, wg, wu, wd):
    R, D = xs.shape
    E, _, FF = wg.shape
    grid_spec = pltpu.PrefetchScalarGridSpec(
        num_scalar_prefetch=2,
        grid=(R // MOE_TM,),
        in_specs=[pl.BlockSpec((MOE_TM, D), lambda i, te, nu: (jnp.minimum(i, nu[0] - 1), 0)),
                  pl.BlockSpec(memory_space=pl.ANY), pl.BlockSpec(memory_space=pl.ANY),
                  pl.BlockSpec(memory_space=pl.ANY)],
        out_specs=pl.BlockSpec((MOE_TM, D), lambda i, te, nu: (i, 0)),
        scratch_shapes=[pltpu.VMEM((D, FF), BF16), pltpu.VMEM((D, FF), BF16), pltpu.VMEM((FF, D), BF16),
                        pltpu.VMEM((MOE_LOAD_SLOTS, MOE_LOAD_ROWS * D // FF, FF), F32),
                        pltpu.VMEM((MOE_LOAD_SLOTS, MOE_LOAD_ROWS, D), F32),
                        pltpu.SemaphoreType.DMA((MOE_LOAD_SLOTS,))],
    )
    return pl.pallas_call(
        _moe_ffn_body,
        grid_spec=grid_spec,
        out_shape=jax.ShapeDtypeStruct((R, D), F32),
        compiler_params=_params("arbitrary"),
        name="moe_ffn",
    )(meta["tile_expert"], meta["n_used"], xs, wg, wu, wd)


def _combine_body(lo_ref, rs_ref, rl_ref, ys_hbm, x_ref, rt_ref, pc_ref, g_ref, o_ref, ybuf, sem):
    b, nb = pl.program_id(0), pl.num_programs(0)
    slot = b % 2

    def runs(blk, s, wait):
        for e in range(N_EXPERTS):
            j = blk * N_EXPERTS + e
            _run_dma(ys_hbm, ybuf.at[s], rs_ref[j], lo_ref[j], rl_ref[j], sem.at[s], wait)

    @pl.when(b == 0)
    def _():
        ybuf[...] = jnp.zeros_like(ybuf)
        runs(0, 0, False)

    @pl.when(b + 1 < nb)
    def _():
        runs(b + 1, 1 - slot, False)

    runs(b, slot, True)
    y = ybuf[slot].astype(BF16)
    pc, rt = pc_ref[...], rt_ref[...]
    lane = lax.broadcasted_iota(jnp.int32, (MOE_TB, BLOCK_LANES), 1)
    q = jnp.where(lane == pc[:, 0:1], rt[:, 2:3], jnp.where(lane == pc[:, 1:2], rt[:, 3:4], 0.0)).astype(BF16)
    o_ref[...] = _rms(x_ref[...] + _dot(q, y), g_ref[...])


def _combine(ys, meta, x, rt, g):
    T, D = x.shape
    grid_spec = pltpu.PrefetchScalarGridSpec(
        num_scalar_prefetch=3,
        grid=(T // MOE_TB,),
        in_specs=[pl.BlockSpec(memory_space=pl.ANY),
                  pl.BlockSpec((MOE_TB, D), lambda b, *_: (b, 0)),
                  pl.BlockSpec((MOE_TB, LANES), lambda b, *_: (b, 0)),
                  pl.BlockSpec((MOE_TB, TOP_K), lambda b, *_: (b, 0)),
                  pl.BlockSpec((1, D), lambda b, *_: (0, 0))],
        out_specs=pl.BlockSpec((MOE_TB, D), lambda b, *_: (b, 0)),
        scratch_shapes=[pltpu.VMEM((2, BLOCK_LANES, D), F32), pltpu.SemaphoreType.DMA((2,))],
    )
    return pl.pallas_call(
        _combine_body,
        grid_spec=grid_spec,
        out_shape=jax.ShapeDtypeStruct((T, D), F32),
        compiler_params=_params("arbitrary"),
        name="moe_combine",
    )(meta["loc_off"], meta["run_start"], meta["run_len"], ys, x, rt, meta["pos_c"], g.reshape(1, D))


def kernel(x, norm_mix_g, w_in, ret_gn_g, attn_sinks, conv_dw_w, conv_dw_b, conv_ln_g, conv_ln_b,
           conv_pw_w, w_out, norm_ffn_g, ffn_w_gate, ffn_w_up, ffn_w_down, moe_router, moe_w_gate,
           moe_w_up, moe_w_down, final_norm_g):
    B, S, D = x.shape
    T = B * S
    depth = w_in.shape[0]
    assert depth == 2 and ffn_w_gate.shape[0] == 1 and moe_router.shape[0] == 1, "dense layer then MoE layer"
    assert T % MOE_TB == 0 and S % MOE_TB == 0
    assert moe_w_gate.shape[-1] % MOE_FC == 0 and moe_w_gate.shape[-1] % MOE_LOAD_ROWS == 0

    for l in range(depth):
        rq, rk, rv, rg, sq, sk, sv, ch = _inproj(x, norm_mix_g[l], w_in[l].astype(BF16), tm=1024)
        y_ret_t = _retention(rq, rk, rv, rg, ret_gn_g[l], ts=4096)
        y_swa_t = _swa(sq, sk, sv, attn_sinks[l], tq=4096)
        y_conv = _conv(ch, conv_dw_w[l], conv_dw_b[l], conv_ln_g[l], conv_ln_b[l], conv_pw_w[l], ts=1024, rb=64)
        flat = lambda a: a.reshape(T, a.shape[-1])
        j = l // 2
        if l % 2 == 0:
            x = _outproj_dense_ffn(y_ret_t, y_swa_t, flat(y_conv), w_out[l].astype(BF16), flat(x),
                                   norm_ffn_g[l], ffn_w_gate[j], ffn_w_up[j], ffn_w_down[j],
                                   tm=1024, fc=512).reshape(B, S, D)
        else:
            x2, hn, rt, rl = _outproj_route(y_ret_t, y_swa_t, flat(y_conv), w_out[l].astype(BF16), flat(x),
                                            norm_ffn_g[l], moe_router[j])
            n_rows = TOP_K * T + (T // MOE_TB) * N_EXPERTS * RUN_ALIGN + N_EXPERTS * MOE_TM
            meta = _route_meta(rt, rl, n_rows // MOE_TM)
            xs = _dispatch(hn, meta, n_rows)
            ys = _moe_ffn(xs, meta, moe_w_gate[j], moe_w_up[j], moe_w_down[j])
            x = _combine(ys, meta, x2, rt, final_norm_g).reshape(B, S, D)
    return x
```

```python
import functools

import jax
import jax.numpy as jnp
from jax import lax
from jax.experimental import pallas as pl
from jax.experimental.pallas import tpu as pltpu

F32 = jnp.float32
BF16 = jnp.bfloat16

HEAD_DIM = 64
RET_HEADS = 4
RET_W = RET_HEADS * HEAD_DIM
SWA_HEADS = 8
SWA_KV_HEADS = 2
SWA_GROUP = SWA_HEADS // SWA_KV_HEADS
SWA_W = SWA_HEADS * HEAD_DIM
SWA_KV_W = SWA_KV_HEADS * HEAD_DIM
CONV_CH = 256
CONV_GROUPS = 4
CONV_WIDTH = 31
WINDOW = 128
BLOCK = 128
RET_CHUNK = 128
ROPE_THETA = 500000.0
ROPE_DIM = HEAD_DIM // 4
RET_ROPE_THETA = 10000.0
N_EXPERTS = 8
TOP_K = 2
EPS = 1e-6
NEG_INF = -1e30

LANES = 128
CONV_HALO = 32
VMEM_LIMIT = 56 * 1024 * 1024

O_RQ, O_RK, O_RV, O_RG = 0, RET_W, 2 * RET_W, 3 * RET_W
O_SQ = 4 * RET_W
O_SK = O_SQ + SWA_W
O_SV = O_SK + SWA_KV_W
O_CA = O_SV + SWA_KV_W
O_CG = O_CA + CONV_CH
D_IN = O_CG + CONV_CH


def _params(*sem, flags=None):
    return pltpu.CompilerParams(dimension_semantics=sem, vmem_limit_bytes=VMEM_LIMIT, flags=flags)


def _rms(x, g):
    return x * lax.rsqrt(jnp.mean(x * x, axis=-1, keepdims=True) + EPS) * g


def _silu(x):
    return x * jax.nn.sigmoid(x)


def _dot(a, b):
    return jnp.dot(a, b, preferred_element_type=F32)


def _dot_nt(a, b):
    return lax.dot_general(a, b, (((1,), (1,)), ((), ())), preferred_element_type=F32)


def _dot_tn(a, b):
    return lax.dot_general(a, b, (((0,), (0,)), ((), ())), preferred_element_type=F32)


def _rope_tables(seq, theta, rot_dim):
    half = rot_dim // 2
    inv = 1.0 / (theta ** (jnp.arange(half, dtype=F32) / half))
    ang = jnp.arange(seq, dtype=F32)[:, None] * inv[None, :]
    cos, sin = jnp.cos(ang), jnp.sin(ang)
    rest = HEAD_DIM - rot_dim
    c = jnp.concatenate([cos, cos, jnp.ones((seq, rest), F32)], axis=1)
    s = jnp.concatenate([-sin, sin, jnp.zeros((seq, rest), F32)], axis=1)
    reps = LANES // HEAD_DIM
    return jnp.tile(c, (1, reps)), jnp.tile(s, (1, reps))


def _inproj_body(x_ref, g_ref, w_ref, wgt_ref, rc_ref, rs_ref, sc_ref, ss_ref,
                 rq_ref, rk_ref, rv_ref, rg_ref, sq_ref, sk_ref, sv_ref, ch_ref, zbuf):
    h = _rms(x_ref[0], g_ref[...]).astype(BF16)
    lane = lax.broadcasted_iota(jnp.int32, (1, LANES), 1) % HEAD_DIM
    for a, b in ((O_RQ, O_RK), (O_RK, O_RV), (O_RV, O_RG), (O_SQ, O_SK), (O_SK, O_CA), (O_CA, O_CG), (O_CG, D_IN)):
        zbuf[:, a:b] = _dot(h, w_ref[:, a:b])

    def seg(a, b):
        return zbuf[:, a:b]

    def rope(v, c, s, half):
        up = pltpu.roll(v, LANES - half, axis=1)
        dn = pltpu.roll(v, half, axis=1)
        return v * c + jnp.where(lane < half, up, dn) * s

    def store_heads(z, o_ref, first=0):
        for j in range(z.shape[1] // HEAD_DIM):
            o_ref[0, first + j] = z[:, j * HEAD_DIM:(j + 1) * HEAD_DIM]

    def rope_store(z, o_ref, c, s, half, scale):
        for i in range(z.shape[1] // LANES):
            r = rope(z[:, i * LANES:(i + 1) * LANES], c, s, half)
            if scale != 1.0:
                r = r * scale
            store_heads(r.astype(BF16), o_ref, first=i * (LANES // HEAD_DIM))

    rc, rs = rc_ref[...], rs_ref[...]
    sc, ss = sc_ref[...], ss_ref[...]
    scale = HEAD_DIM ** -0.5
    rope_store(seg(O_RQ, O_RK), rq_ref, rc, rs, HEAD_DIM // 2, 1.0)
    rope_store(seg(O_RK, O_RV), rk_ref, rc, rs, HEAD_DIM // 2, scale)
    store_heads(seg(O_RV, O_RG).astype(BF16), rv_ref)
    rg_ref[0] = _dot_nt(wgt_ref[...], h)
    rope_store(seg(O_SQ, O_SK), sq_ref, sc, ss, ROPE_DIM // 2, scale)
    rope_store(seg(O_SK, O_SV), sk_ref, sc, ss, ROPE_DIM // 2, 1.0)
    store_heads(seg(O_SV, O_CA).astype(BF16), sv_ref)
    ch_ref[0] = seg(O_CA, O_CG) * jax.nn.sigmoid(seg(O_CG, D_IN))


def _inproj(x, g, w_bf, tm):
    B, S, D = x.shape
    tm = min(tm, S)
    rc, rs = _rope_tables(S, RET_ROPE_THETA, HEAD_DIM)
    sc, ss = _rope_tables(S, ROPE_THETA, ROPE_DIM)
    tab = pl.BlockSpec((tm, LANES), lambda s, b: (s, 0))

    def out(width, dtype):
        return (jax.ShapeDtypeStruct((B, S, width), dtype),
                pl.BlockSpec((1, tm, width), lambda s, b: (b, s, 0)))

    def out_heads(heads):
        return (jax.ShapeDtypeStruct((B, heads, S, HEAD_DIM), BF16),
                pl.BlockSpec((1, heads, tm, HEAD_DIM), lambda s, b: (b, 0, s, 0)))

    gate_t = (jax.ShapeDtypeStruct((B, RET_W, S), F32), pl.BlockSpec((1, RET_W, tm), lambda s, b: (b, 0, s)))
    outs = [out_heads(RET_HEADS), out_heads(RET_HEADS), out_heads(RET_HEADS), gate_t,
            out_heads(SWA_HEADS), out_heads(SWA_KV_HEADS), out_heads(SWA_KV_HEADS), out(CONV_CH, F32)]
    return pl.pallas_call(
        _inproj_body,
        grid=(S // tm, B),
        in_specs=[pl.BlockSpec((1, tm, D), lambda s, b: (b, s, 0)),
                  pl.BlockSpec((1, D), lambda s, b: (0, 0)),
                  pl.BlockSpec((D, D_IN), lambda s, b: (0, 0)),
                  pl.BlockSpec((RET_W, D), lambda s, b: (0, 0)),
                  tab, tab, tab, tab],
        out_specs=[o[1] for o in outs],
        out_shape=[o[0] for o in outs],
        scratch_shapes=[pltpu.VMEM((tm, D_IN), F32)],
        compiler_params=_params("arbitrary", "arbitrary"),
        name="inproj",
    )(x, g.reshape(1, D), w_bf, w_bf[:, O_RG:O_SQ].T, rc, rs, sc, ss)


def _ret_body(q_ref, k_ref, v_ref, g_ref, dm_ref, kd_ref, qd_ref, cd_ref, gn_ref, o_ref, st_ref, *, ts):
    @pl.when(pl.program_id(2) == 0)
    def _():
        st_ref[...] = jnp.zeros_like(st_ref)

    C = RET_CHUNK
    for c in range(ts // C):
        rows = slice(c * C, (c + 1) * C)
        q, k, v = q_ref[0, 0, rows, :], k_ref[0, 0, rows, :], v_ref[0, 0, rows, :]
        st = st_ref[...]
        scores = _dot_nt(k, q) * dm_ref[0]
        intra = _dot_tn(v, scores.astype(BF16))
        cross = _dot_nt(st.astype(BF16), q) * qd_ref[0]
        kdec = (k.astype(F32) * kd_ref[0]).astype(BF16)
        st_ref[...] = st * cd_ref[0] + _dot_tn(v, kdec)
        o = intra + cross
        mu = jnp.mean(o, axis=0, keepdims=True)
        d = o - mu
        var = jnp.mean(d * d, axis=0, keepdims=True)
        on = d * lax.rsqrt(var + EPS) * gn_ref[0]
        o_ref[0, :, rows] = (_silu(g_ref[0, :, rows]) * on).astype(BF16)


def _retention(rq, rk, rv, rg_t, gn_g, ts):
    B, H, S, D = rq.shape
    ts = min(ts, S)
    C, W = RET_CHUNK, RET_W
    lg = jnp.log(1.0 - 2.0 ** (-5.0 - jnp.arange(H, dtype=F32)))
    idx = jnp.arange(C)
    rel = idx[:, None] - idx[None, :]
    dmask = jnp.where(rel[None] >= 0,
                      jnp.exp(jnp.maximum(rel, 0)[None].astype(F32) * lg[:, None, None]), 0.0)
    k_decay = jnp.exp((C - 1 - idx)[:, None].astype(F32) * lg[None, :])
    q_decay = jnp.exp((idx + 1)[:, None].astype(F32) * lg[None, :])
    chunk_decay = jnp.exp(C * lg)
    dm_t = jnp.swapaxes(dmask, 1, 2)
    kd = jnp.broadcast_to(k_decay.T[:, :, None], (H, C, D))
    qd = jnp.broadcast_to(q_decay.T[:, None, :], (H, D, C))
    cd = jnp.broadcast_to(chunk_decay[:, None, None], (H, D, D))
    gn = jnp.broadcast_to(gn_g.reshape(H, D, 1), (H, D, C))
    heads = pl.BlockSpec((1, 1, ts, D), lambda b, h, n: (b, h, n, 0))
    chan = pl.BlockSpec((1, D, ts), lambda b, h, n: (b, h, n))

    def per_head(r, c):
        return pl.BlockSpec((1, r, c), lambda b, h, n: (h, 0, 0))

    return pl.pallas_call(
        functools.partial(_ret_body, ts=ts),
        grid=(B, H, S // ts),
        in_specs=[heads, heads, heads, chan, per_head(C, C), per_head(C, D), per_head(D, C),
                  per_head(D, D), per_head(D, C)],
        out_specs=chan,
        out_shape=jax.ShapeDtypeStruct((B, W, S), BF16),
        scratch_shapes=[pltpu.VMEM((D, D), F32)],
        compiler_params=_params("arbitrary", "arbitrary", "arbitrary"),
        name="retention",
    )(rq, rk, rv, rg_t, dm_t, kd, qd, cd, gn)


def _swa_body(sink_ref, q_ref, kc_ref, kp_ref, vc_ref, vp_ref, bias_ref, o_ref, kcat, vcat, sbuf, pbuf, rbuf,
              *, tq):
    hk, n = pl.program_id(1), pl.program_id(2)
    kcat[0:BLOCK], kcat[BLOCK:] = kp_ref[0, 0], kc_ref[0, 0]
    vcat[0:BLOCK], vcat[BLOCK:] = vp_ref[0, 0], vc_ref[0, 0]
    width = SWA_GROUP * BLOCK
    group = lax.broadcasted_iota(jnp.int32, (1, width), 1) // BLOCK
    sink = jnp.zeros((1, width), F32)
    for g in range(SWA_GROUP):
        sink = jnp.where(group == g, sink_ref[hk * SWA_GROUP + g], sink)
    nq = tq // BLOCK
    for j in range(nq):
        q = q_ref[0, :, j * BLOCK:(j + 1) * BLOCK, :].reshape(width, HEAD_DIM)
        kb = kcat[j * BLOCK:(j + 2) * BLOCK]
        sbuf[j] = _dot_nt(kb, q)
    for j in range(nq):
        s = sbuf[j] + (bias_ref[0] if j > 0 else bias_ref[jnp.where(n == 0, 1, 0)])
        m = jnp.maximum(jnp.max(s, axis=0, keepdims=True), sink)
        p = jnp.exp(s - m)
        rbuf[j] = 1.0 / (jnp.sum(p, axis=0, keepdims=True) + jnp.exp(sink - m))
        pbuf[j] = p.astype(BF16)
    for j in range(nq):
        vb = vcat[j * BLOCK:(j + 2) * BLOCK]
        o = _dot_tn(vb, pbuf[j]) * rbuf[j]
        for g in range(SWA_GROUP):
            o_ref[0, g * HEAD_DIM:(g + 1) * HEAD_DIM, j * BLOCK:(j + 1) * BLOCK] = (
                o[:, g * BLOCK:(g + 1) * BLOCK].astype(BF16))


def _swa(sq, sk, sv, sinks, tq):
    B, _, S, D = sq.shape
    tq = min(tq, S)
    r = tq // BLOCK
    qi = jnp.arange(BLOCK)[None, :] + BLOCK
    kj = jnp.arange(2 * BLOCK)[:, None]
    rel = qi - kj
    allowed = (rel >= 0) & (rel < WINDOW)
    allowed = jnp.stack([allowed, allowed & (kj >= BLOCK)])
    bias = jnp.tile(jnp.where(allowed, 0.0, NEG_INF).astype(F32), (1, 1, SWA_GROUP))
    nq, width = tq // BLOCK, SWA_GROUP * BLOCK
    cur = pl.BlockSpec((1, 1, tq, D), lambda b, h, n: (b, h, n, 0))
    prev = pl.BlockSpec((1, 1, BLOCK, D), lambda b, h, n: (b, h, jnp.maximum(n * r - 1, 0), 0))
    return pl.pallas_call(
        functools.partial(_swa_body, tq=tq),
        grid=(B, SWA_KV_HEADS, S // tq),
        in_specs=[pl.BlockSpec(memory_space=pltpu.SMEM),
                  pl.BlockSpec((1, SWA_GROUP, tq, D), lambda b, h, n: (b, h, n, 0)),
                  cur, prev, cur, prev,
                  pl.BlockSpec((2, 2 * BLOCK, width), lambda b, h, n: (0, 0, 0))],
        out_specs=pl.BlockSpec((1, SWA_GROUP * D, tq), lambda b, h, n: (b, h, n)),
        out_shape=jax.ShapeDtypeStruct((B, SWA_W, S), BF16),
        scratch_shapes=[pltpu.VMEM((BLOCK + tq, D), BF16), pltpu.VMEM((BLOCK + tq, D), BF16),
                        pltpu.VMEM((nq, 2 * BLOCK, width), F32), pltpu.VMEM((nq, 2 * BLOCK, width), BF16),
                        pltpu.VMEM((nq, 1, width), F32)],
        compiler_params=_params("arbitrary", "arbitrary", "arbitrary"),
        name="swa",
    )(sinks.astype(F32), sq, sk, sk, sv, sv, bias)


def _conv_body(h_ref, dw_ref, db_ref, lg_ref, lb_ref, pw_ref, o_ref, hbuf, hsh, *, ts, rb):
    s = pl.program_id(1)
    sub = 8
    span = ts + CONV_HALO - sub

    @pl.when(s == 0)
    def _():
        hbuf[0:CONV_HALO, :] = jnp.zeros((CONV_HALO, CONV_CH), F32)

    @pl.when(s > 0)
    def _():
        hbuf[0:CONV_HALO, :] = hbuf[ts:ts + CONV_HALO, :]

    hbuf[CONV_HALO:CONV_HALO + ts, :] = h_ref[0]
    for p in range(1, sub):
        hsh[p - 1] = hbuf[p:p + span, :]
    off = CONV_HALO - (CONV_WIDTH - 1)
    for r in range(ts // rb):
        base = r * rb
        acc = jnp.broadcast_to(db_ref[...], (rb, CONV_CH))
        for w in range(CONV_WIDTH):
            q, p = divmod(off + w, sub)
            lo = base + q * sub
            tap = hbuf[lo:lo + rb, :] if p == 0 else hsh[p - 1, lo:lo + rb, :]
            acc = acc + tap * dw_ref[w:w + 1, :]
        mu = jnp.mean(acc, axis=-1, keepdims=True)
        d = acc - mu
        var = jnp.mean(d * d, axis=-1, keepdims=True)
        hn = d * lax.rsqrt(var + EPS) * lg_ref[...] + lb_ref[...]
        o_ref[0, base:base + rb, :] = _dot(_silu(hn).astype(BF16), pw_ref[...]).astype(BF16)


def _conv(ch, dw_w, dw_b, ln_g, ln_b, pw_w, ts, rb):
    B, S, W = ch.shape
    ts = min(ts, S)
    rb = min(rb, ts)
    pw = jnp.zeros((W, W), F32)
    gd = W // CONV_GROUPS
    for g in range(CONV_GROUPS):
        pw = pw.at[g * gd:(g + 1) * gd, g * gd:(g + 1) * gd].set(pw_w[g])
    act = pl.BlockSpec((1, ts, W), lambda b, s: (b, s, 0))

    def const(shape):
        return pl.BlockSpec(shape, lambda b, s: (0,) * len(shape))

    return pl.pallas_call(
        functools.partial(_conv_body, ts=ts, rb=rb),
        grid=(B, S // ts),
        in_specs=[act, const((CONV_WIDTH, W)), const((1, W)), const((1, W)), const((1, W)), const((W, W))],
        out_specs=act,
        out_shape=jax.ShapeDtypeStruct((B, S, W), BF16),
        scratch_shapes=[pltpu.VMEM((CONV_HALO + ts, W), F32), pltpu.VMEM((7, CONV_HALO + ts - 8, W), F32)],
        compiler_params=_params("arbitrary", "arbitrary"),
        name="conv",
    )(ch, dw_w, dw_b.reshape(1, W), ln_g.reshape(1, W), ln_b.reshape(1, W), pw.astype(BF16))


def _mixer_residual(yr_ref, ys_ref, yc_ref, w_ref, x_ref):
    a, b = RET_W, RET_W + SWA_W
    return x_ref[...] + (_dot_tn(yr_ref[0], w_ref[0:a, :]) + _dot_tn(ys_ref[0], w_ref[a:b, :])
                         + _dot(yc_ref[...], w_ref[b:, :]))


def _mixer_specs(tm, tiles_per_seq):
    def chan(width):
        return pl.BlockSpec((1, width, tm), lambda i: (i // tiles_per_seq, 0, i % tiles_per_seq))

    return [chan(RET_W), chan(SWA_W), pl.BlockSpec((tm, CONV_CH), lambda i: (i, 0))]


def _outproj_body(yr_ref, ys_ref, yc_ref, w_ref, x_ref, g_ref, wr_ref, before_ref, xo_ref, hn_ref, rt_ref, rl_ref):
    x = _mixer_residual(yr_ref, ys_ref, yc_ref, w_ref, x_ref)
    xo_ref[...] = x
    h = _rms(x, g_ref[...]).astype(BF16)
    hn_ref[...] = h
    logits = _dot(h, wr_ref[...])
    lane = lax.broadcasted_iota(jnp.int32, logits.shape, 1).astype(F32)
    lg = jnp.where(lane < N_EXPERTS, logits, -jnp.inf)
    m1 = jnp.max(lg, axis=-1, keepdims=True)
    i1 = jnp.min(jnp.where(lg == m1, lane, float(LANES)), axis=-1, keepdims=True)
    lg2 = jnp.where(lane == i1, -jnp.inf, lg)
    m2 = jnp.max(lg2, axis=-1, keepdims=True)
    i2 = jnp.min(jnp.where(lg2 == m2, lane, float(LANES)), axis=-1, keepdims=True)
    e = jnp.exp(m2 - m1)
    w1 = 1.0 / (1.0 + e)
    w2 = e / (1.0 + e)
    hit1, hit2 = lane == i1, lane == i2
    onehot = jnp.where(jnp.logical_or(hit1, hit2), 1.0, 0.0)
    rank = _dot(before_ref[...], onehot.astype(BF16))
    count = jnp.sum(onehot, axis=0, keepdims=True).astype(jnp.int32)
    run_len = (count + (RUN_ALIGN - 1)) & ~(RUN_ALIGN - 1)
    lower = (lax.broadcasted_iota(jnp.int32, (LANES, LANES), 0) < lax.broadcasted_iota(jnp.int32, (LANES, LANES), 1))
    run_rows = jnp.broadcast_to(run_len.astype(F32), (8, LANES)).astype(BF16)
    loc_off = _dot(run_rows, jnp.where(lower, 1.0, 0.0).astype(BF16))[0:1, :]
    pos = rank + loc_off
    p1 = jnp.sum(jnp.where(hit1, pos, 0.0), axis=-1, keepdims=True)
    p2 = jnp.sum(jnp.where(hit2, pos, 0.0), axis=-1, keepdims=True)
    cols = (i1, i2, w1, w2, p1, p2)
    rt = jnp.zeros_like(logits)
    for c, v in enumerate(cols):
        rt = jnp.where(lane == c, v, rt)
    rt_ref[...] = rt
    rl_ref[0] = jnp.broadcast_to(run_len, (8, LANES))


def _outproj_route(yr, ys, yc, w_bf, x, g, w_router):
    T, D = x.shape
    tm = MOE_TB
    S = ys.shape[2]

    def row(width):
        return pl.BlockSpec((tm, width), lambda i: (i, 0))

    def const(shape):
        return pl.BlockSpec(shape, lambda i: (0,) * len(shape))

    wr = jnp.zeros((D, LANES), F32).at[:, :N_EXPERTS].set(w_router).astype(BF16)
    before = (jnp.arange(tm)[:, None] > jnp.arange(tm)[None, :]).astype(BF16)
    return pl.pallas_call(
        _outproj_body,
        grid=(T // tm,),
        in_specs=_mixer_specs(tm, S // tm) + [const((D, D)), row(D), const((1, D)), const((D, LANES)),
                                              const((tm, tm))],
        out_specs=[row(D), row(D), row(LANES), pl.BlockSpec((1, 8, LANES), lambda i: (i, 0, 0))],
        out_shape=[jax.ShapeDtypeStruct((T, D), F32), jax.ShapeDtypeStruct((T, D), BF16),
                   jax.ShapeDtypeStruct((T, LANES), F32), jax.ShapeDtypeStruct((T // tm, 8, LANES), jnp.int32)],
        compiler_params=_params("arbitrary"),
        name="outproj_route",
    )(yr, ys, yc, w_bf, x, g.reshape(1, D), wr, before)


def _dense_ffn_body(yr_ref, ys_ref, yc_ref, wo_ref, x_ref, g_ref, wg_ref, wu_ref, wd_ref, o_ref, *, chunks):
    x = _mixer_residual(yr_ref, ys_ref, yc_ref, wo_ref, x_ref)
    h = _rms(x, g_ref[...]).astype(BF16)
    acc = x
    for a, b in chunks:
        g = _dot(h, wg_ref[:, a:b])
        u = _dot(h, wu_ref[:, a:b])
        acc = acc + _dot((_silu(g) * u).astype(BF16), wd_ref[a:b, :])
    o_ref[...] = acc


def _outproj_dense_ffn(yr, ys, yc, wo_bf, x, g, wg, wu, wd, tm, fc):
    T, D = x.shape
    FF = wg.shape[1]
    S = ys.shape[2]
    tm = min(tm, S)
    chunks = tuple((a, min(a + fc, FF)) for a in range(0, FF, fc))

    def row(width):
        return pl.BlockSpec((tm, width), lambda i: (i, 0))

    def const(shape):
        return pl.BlockSpec(shape, lambda i: (0,) * len(shape), pipeline_mode=pl.Buffered(1))

    return pl.pallas_call(
        functools.partial(_dense_ffn_body, chunks=chunks),
        grid=(T // tm,),
        in_specs=_mixer_specs(tm, S // tm) + [const((D, D)), row(D), const((1, D)),
                                              const((D, FF)), const((D, FF)), const((FF, D))],
        out_specs=row(D),
        out_shape=jax.ShapeDtypeStruct((T, D), F32),
        compiler_params=_params("arbitrary"),
        name="outproj_dense_ffn",
    )(yr, ys, yc, wo_bf, x, g.reshape(1, D), wg.astype(BF16), wu.astype(BF16), wd.astype(BF16))


MOE_TB = 512
MOE_TM = 512
RUN_ALIGN = 8
RUN_SIZES = (512, 256, 128, 64, 32, 16, 8)
BLOCK_ROWS = TOP_K * MOE_TB + N_EXPERTS * RUN_ALIGN
BLOCK_LANES = 1152
MOE_FC = 512
MOE_LOAD_SLOTS = 6


def _route_meta(rt, rl, n_tiles):
    T = rt.shape[0]
    nb = T // MOE_TB
    pos = rt[:, 4:4 + TOP_K].astype(jnp.int32)
    run_len = rl[:, 0, :N_EXPERTS]
    loc_off = jnp.cumsum(run_len, axis=1) - run_len
    group = jnp.sum(run_len, axis=0)
    ptiles = (group + MOE_TM - 1) // MOE_TM
    tile_end = jnp.cumsum(ptiles)
    gstart = (tile_end - ptiles) * MOE_TM
    run_start = gstart[None, :] + jnp.cumsum(run_len, axis=0) - run_len
    tile_expert = jnp.sum(jnp.arange(n_tiles, dtype=jnp.int32)[:, None] >= tile_end[None, :], axis=1)
    i32 = lambda a: a.astype(jnp.int32)
    pos3 = pos.reshape(nb, MOE_TB, TOP_K)
    return dict(
        pos_l=i32(jnp.swapaxes(pos3, 1, 2)),
        pos_c=i32(pos3.reshape(T, TOP_K)),
        loc_off=i32(loc_off.reshape(-1)), run_start=i32(run_start.reshape(-1)), run_len=i32(run_len.reshape(-1)),
        pad_start=i32(gstart + group), pad_len=i32(ptiles * MOE_TM - group),
        tile_expert=i32(jnp.minimum(tile_expert, N_EXPERTS - 1)), n_used=i32(tile_end[-1:]))


def _run_dma(src, dst, src_off, dst_off, length, sem, wait):
    off = 0
    for k in RUN_SIZES:
        part = length & k

        @pl.when(part != 0)
        def _(off=off, k=k):
            cp = pltpu.make_async_copy(src.at[pl.ds(pl.multiple_of(src_off + off, RUN_ALIGN), k)],
                                       dst.at[pl.ds(pl.multiple_of(dst_off + off, RUN_ALIGN), k)], sem)
            if wait:
                cp.wait()
            else:
                cp.start()

        off = off + part


def _dispatch_body(lo_ref, rs_ref, rl_ref, ps_ref, pn_ref, nu_ref, pos_ref, h_ref, xs_hbm, sbuf, zbuf, sem):
    b, nb = pl.program_id(0), pl.num_programs(0)
    slot = b % 2
    n_tiles = xs_hbm.shape[0] // MOE_TM

    def zero_tile(i, wait):
        cp = pltpu.make_async_copy(zbuf, xs_hbm.at[pl.ds(pl.multiple_of(i * MOE_TM, MOE_TM), MOE_TM)], sem.at[2])
        if wait:
            cp.wait()
        else:
            cp.start()

    def runs(blk, s, wait):
        for e in range(N_EXPERTS):
            j = blk * N_EXPERTS + e
            _run_dma(sbuf.at[s], xs_hbm, lo_ref[j], rs_ref[j], rl_ref[j], sem.at[s], wait)

    @pl.when(b >= 2)
    def _():
        runs(b - 2, slot, True)

    pos = pos_ref[0]
    r = lax.broadcasted_iota(jnp.int32, (BLOCK_ROWS, MOE_TB), 0)
    onehot = jnp.logical_or(r == pos[0:1, :], r == pos[1:2, :])
    sbuf[slot] = _dot(jnp.where(onehot, 1.0, 0.0).astype(BF16), h_ref[...])
    runs(b, slot, False)

    @pl.when(b == nb - 1)
    def _():
        zbuf[...] = jnp.zeros_like(zbuf)
        for e in range(N_EXPERTS):
            _run_dma(zbuf, xs_hbm, 0, ps_ref[e], pn_ref[e], sem.at[2], False)
        lax.fori_loop(nu_ref[0], n_tiles, lambda i, c: zero_tile(i, False), None)
        for e in range(N_EXPERTS):
            _run_dma(zbuf, xs_hbm, 0, ps_ref[e], pn_ref[e], sem.at[2], True)
        lax.fori_loop(nu_ref[0], n_tiles, lambda i, c: zero_tile(i, True), None)

        @pl.when(b >= 1)
        def _():
            runs(b - 1, 1 - slot, True)

        runs(b, slot, True)


def _dispatch(hn, meta, n_rows):
    T, D = hn.shape
    nb = T // MOE_TB
    grid_spec = pltpu.PrefetchScalarGridSpec(
        num_scalar_prefetch=6,
        grid=(nb,),
        in_specs=[pl.BlockSpec((1, TOP_K, MOE_TB), lambda b, *_: (b, 0, 0)),
                  pl.BlockSpec((MOE_TB, D), lambda b, *_: (b, 0))],
        out_specs=pl.BlockSpec(memory_space=pl.ANY),
        scratch_shapes=[pltpu.VMEM((2, BLOCK_ROWS, D), F32), pltpu.VMEM((MOE_TM, D), F32),
                        pltpu.SemaphoreType.DMA((3,))],
    )
    return pl.pallas_call(
        _dispatch_body,
        grid_spec=grid_spec,
        out_shape=jax.ShapeDtypeStruct((n_rows, D), F32),
        compiler_params=pltpu.CompilerParams(dimension_semantics=("arbitrary",), vmem_limit_bytes=VMEM_LIMIT,
                                             has_side_effects=True),
        name="moe_dispatch",
    )(meta["loc_off"], meta["run_start"], meta["run_len"], meta["pad_start"], meta["pad_len"], meta["n_used"],
      meta["pos_l"], hn)


def _moe_ffn_body(te_ref, nu_ref, x_ref, wg_hbm, wu_hbm, wd_hbm, o_ref, wg_s, wu_s, wd_s, stage, sem):
    i = pl.program_id(0)
    e = te_ref[i]
    active = i < nu_ref[0]
    first = jnp.logical_or(i == 0, e != te_ref[jnp.maximum(i - 1, 0)])
    D, FF = wg_s.shape
    fsl = [slice(f * MOE_FC, (f + 1) * MOE_FC) for f in range(FF // MOE_FC)]
    slots, piece = stage.shape[0], stage.shape[1]
    psl = [slice(r, r + piece) for r in range(0, D, piece)]
    jobs = []
    for f in fsl:
        jobs += [(wg_hbm, wg_s, p, f) for p in psl] + [(wu_hbm, wu_s, p, f) for p in psl]
        jobs += [(wd_hbm, wd_s, f, p) for p in psl]
    per_chunk = len(jobs) // len(fsl)

    def copy(j):
        w_hbm, _, rows, cols = jobs[j]
        return pltpu.make_async_copy(w_hbm.at[e, rows, cols], stage.at[j % slots], sem.at[j % slots])

    def ffn(load):
        x = x_ref[...].astype(BF16)
        acc = jnp.zeros(o_ref.shape, F32)
        if load:
            for j in range(slots - 1):
                copy(j).start()
        for c, sl in enumerate(fsl):
            if load:
                for j in range(c * per_chunk, (c + 1) * per_chunk):
                    if j + slots - 1 < len(jobs):
                        copy(j + slots - 1).start()
                    copy(j).wait()
                    _, w_s, rows, cols = jobs[j]
                    w_s[rows, cols] = stage[j % slots].astype(BF16)
            g = _dot(x, wg_s[:, sl])
            u = _dot(x, wu_s[:, sl])
            acc = acc + _dot((_silu(g) * u).astype(BF16), wd_s[sl, :])
        o_ref[...] = acc

    @pl.when(jnp.logical_and(active, first))
    def _():
        ffn(load=True)

    @pl.when(jnp.logical_and(active, jnp.logical_not(first)))
    def _():
        ffn(load=False)

    @pl.when(jnp.logical_not(active))
    def _():
        o_ref[...] = jnp.zeros_like(o_ref)


def _moe_ffn(xs, meta, wg, wu, wd):
    R, D = xs.shape
    E, _, FF = wg.shape
    grid_spec = pltpu.PrefetchScalarGridSpec(
        num_scalar_prefetch=2,
        grid=(R // MOE_TM,),
        in_specs=[pl.BlockSpec((MOE_TM, D), lambda i, te, nu: (jnp.minimum(i, nu[0] - 1), 0)),
                  pl.BlockSpec(memory_space=pl.ANY), pl.BlockSpec(memory_space=pl.ANY),
                  pl.BlockSpec(memory_space=pl.ANY)],
        out_specs=pl.BlockSpec((MOE_TM, D), lambda i, te, nu: (i, 0)),
        scratch_shapes=[pltpu.VMEM((D, FF), BF16), pltpu.VMEM((D, FF), BF16), pltpu.VMEM((FF, D), BF16),
                        pltpu.VMEM((MOE_LOAD_SLOTS, MOE_FC, MOE_FC), F32),
                        pltpu.SemaphoreType.DMA((MOE_LOAD_SLOTS,))],
    )
    return pl.pallas_call(
        _moe_ffn_body,
        grid_spec=grid_spec,
        out_shape=jax.ShapeDtypeStruct((R, D), F32),
        compiler_params=_params("arbitrary"),
        name="moe_ffn",
    )(meta["tile_expert"], meta["n_used"], xs, wg, wu, wd)


def _combine_body(lo_ref, rs_ref, rl_ref, ys_hbm, x_ref, rt_ref, pc_ref, g_ref, o_ref, ybuf, sem):
    b, nb = pl.program_id(0), pl.num_programs(0)
    slot = b % 2

    def runs(blk, s, wait):
        for e in range(N_EXPERTS):
            j = blk * N_EXPERTS + e
            _run_dma(ys_hbm, ybuf.at[s], rs_ref[j], lo_ref[j], rl_ref[j], sem.at[s], wait)

    @pl.when(b == 0)
    def _():
        ybuf[...] = jnp.zeros_like(ybuf)
        runs(0, 0, False)

    @pl.when(b + 1 < nb)
    def _():
        runs(b + 1, 1 - slot, False)

    runs(b, slot, True)
    y = ybuf[slot].astype(BF16)
    pc, rt = pc_ref[...], rt_ref[...]
    lane = lax.broadcasted_iota(jnp.int32, (MOE_TB, BLOCK_LANES), 1)
    q = jnp.where(lane == pc[:, 0:1], rt[:, 2:3], jnp.where(lane == pc[:, 1:2], rt[:, 3:4], 0.0)).astype(BF16)
    o_ref[...] = _rms(x_ref[...] + _dot(q, y), g_ref[...])


def _combine(ys, meta, x, rt, g):
    T, D = x.shape
    grid_spec = pltpu.PrefetchScalarGridSpec(
        num_scalar_prefetch=3,
        grid=(T // MOE_TB,),
        in_specs=[pl.BlockSpec(memory_space=pl.ANY),
                  pl.BlockSpec((MOE_TB, D), lambda b, *_: (b, 0)),
                  pl.BlockSpec((MOE_TB, LANES), lambda b, *_: (b, 0)),
                  pl.BlockSpec((MOE_TB, TOP_K), lambda b, *_: (b, 0)),
                  pl.BlockSpec((1, D), lambda b, *_: (0, 0))],
        out_specs=pl.BlockSpec((MOE_TB, D), lambda b, *_: (b, 0)),
        scratch_shapes=[pltpu.VMEM((2, BLOCK_LANES, D), F32), pltpu.SemaphoreType.DMA((2,))],
    )
    return pl.pallas_call(
        _combine_body,
        grid_spec=grid_spec,
        out_shape=jax.ShapeDtypeStruct((T, D), F32),
        compiler_params=_params("arbitrary"),
        name="moe_combine",
    )(meta["loc_off"], meta["run_start"], meta["run_len"], ys, x, rt, meta["pos_c"], g.reshape(1, D))


def kernel(x, norm_mix_g, w_in, ret_gn_g, attn_sinks, conv_dw_w, conv_dw_b, conv_ln_g, conv_ln_b,
           conv_pw_w, w_out, norm_ffn_g, ffn_w_gate, ffn_w_up, ffn_w_down, moe_router, moe_w_gate,
           moe_w_up, moe_w_down, final_norm_g):
    B, S, D = x.shape
    T = B * S
    depth = w_in.shape[0]
    assert depth == 2 and ffn_w_gate.shape[0] == 1 and moe_router.shape[0] == 1, "dense layer then MoE layer"
    assert T % MOE_TB == 0 and S % MOE_TB == 0
    assert moe_w_gate.shape[-1] % MOE_FC == 0 and D % MOE_FC == 0

    for l in range(depth):
        rq, rk, rv, rg, sq, sk, sv, ch = _inproj(x, norm_mix_g[l], w_in[l].astype(BF16), tm=1024)
        y_ret_t = _retention(rq, rk, rv, rg, ret_gn_g[l], ts=4096)
        y_swa_t = _swa(sq, sk, sv, attn_sinks[l], tq=4096)
        y_conv = _conv(ch, conv_dw_w[l], conv_dw_b[l], conv_ln_g[l], conv_ln_b[l], conv_pw_w[l], ts=1024, rb=64)
        flat = lambda a: a.reshape(T, a.shape[-1])
        j = l // 2
        if l % 2 == 0:
            x = _outproj_dense_ffn(y_ret_t, y_swa_t, flat(y_conv), w_out[l].astype(BF16), flat(x),
                                   norm_ffn_g[l], ffn_w_gate[j], ffn_w_up[j], ffn_w_down[j],
                                   tm=1024, fc=512).reshape(B, S, D)
        else:
            x2, hn, rt, rl = _outproj_route(y_ret_t, y_swa_t, flat(y_conv), w_out[l].astype(BF16), flat(x),
                                            norm_ffn_g[l], moe_router[j])
            n_rows = TOP_K * T + (T // MOE_TB) * N_EXPERTS * RUN_ALIGN + N_EXPERTS * MOE_TM
            meta = _route_meta(rt, rl, n_rows // MOE_TM)
            xs = _dispatch(hn, meta, n_rows)
            ys = _moe_ffn(xs, meta, moe_w_gate[j], moe_w_up[j], moe_w_down[j])
            x = _combine(ys, meta, x2, rt, final_norm_g).reshape(B, S, D)
    return x
```

```python
import functools

import jax
import jax.numpy as jnp
from jax import lax
from jax.experimental import pallas as pl
from jax.experimental.pallas import tpu as pltpu

F32 = jnp.float32
BF16 = jnp.bfloat16

HEAD_DIM = 64
RET_HEADS = 4
RET_W = RET_HEADS * HEAD_DIM
SWA_HEADS = 8
SWA_KV_HEADS = 2
SWA_GROUP = SWA_HEADS // SWA_KV_HEADS
SWA_W = SWA_HEADS * HEAD_DIM
SWA_KV_W = SWA_KV_HEADS * HEAD_DIM
CONV_CH = 256
CONV_GROUPS = 4
CONV_WIDTH = 31
WINDOW = 128
BLOCK = 128
RET_CHUNK = 128
ROPE_THETA = 500000.0
ROPE_DIM = HEAD_DIM // 4
RET_ROPE_THETA = 10000.0
N_EXPERTS = 8
TOP_K = 2
EPS = 1e-6
NEG_INF = -1e30

LANES = 128
CONV_HALO = 32
VMEM_LIMIT = 56 * 1024 * 1024

O_RQ, O_RK, O_RV, O_RG = 0, RET_W, 2 * RET_W, 3 * RET_W
O_SQ = 4 * RET_W
O_SK = O_SQ + SWA_W
O_SV = O_SK + SWA_KV_W
O_CA = O_SV + SWA_KV_W
O_CG = O_CA + CONV_CH
D_IN = O_CG + CONV_CH


INPROJ_TM = 1024
RET_TS = 4096
SWA_TQ = 4096
CONV_TS = 1024
CONV_RB = 256
FFN_TM = 1024
FFN_FC = 512


def _params(*sem):
    return pltpu.CompilerParams(dimension_semantics=sem, vmem_limit_bytes=VMEM_LIMIT)


def _rms(x, g):
    return x * lax.rsqrt(jnp.mean(x * x, axis=-1, keepdims=True) + EPS) * g


def _silu(x):
    return x * jax.nn.sigmoid(x)


def _dot(a, b):
    return jnp.dot(a, b, preferred_element_type=F32)


def _dot_nt(a, b):
    return lax.dot_general(a, b, (((1,), (1,)), ((), ())), preferred_element_type=F32)


def _dot_tn(a, b):
    return lax.dot_general(a, b, (((0,), (0,)), ((), ())), preferred_element_type=F32)


def _rope_tables(seq, theta, rot_dim):
    half = rot_dim // 2
    inv = 1.0 / (theta ** (jnp.arange(half, dtype=F32) / half))
    ang = jnp.arange(seq, dtype=F32)[:, None] * inv[None, :]
    cos, sin = jnp.cos(ang), jnp.sin(ang)
    rest = HEAD_DIM - rot_dim
    c = jnp.concatenate([cos, cos, jnp.ones((seq, rest), F32)], axis=1)
    s = jnp.concatenate([-sin, sin, jnp.zeros((seq, rest), F32)], axis=1)
    reps = LANES // HEAD_DIM
    return jnp.tile(c, (1, reps)), jnp.tile(s, (1, reps))


def _inproj_body(x_ref, g_ref, w_ref, wgt_ref, rc_ref, rs_ref, sc_ref, ss_ref,
                 rq_ref, rk_ref, rv_ref, rg_ref, sq_ref, sk_ref, sv_ref, ch_ref, zbuf):
    h = _rms(x_ref[0], g_ref[...]).astype(BF16)
    lane = lax.broadcasted_iota(jnp.int32, (1, LANES), 1) % HEAD_DIM
    for a, b in ((O_RQ, O_RK), (O_RK, O_RV), (O_RV, O_RG), (O_SQ, O_SK), (O_SK, O_CA), (O_CA, O_CG), (O_CG, D_IN)):
        zbuf[:, a:b] = _dot(h, w_ref[:, a:b])

    def seg(a, b):
        return zbuf[:, a:b]

    def rope(v, c, s, half):
        up = pltpu.roll(v, LANES - half, axis=1)
        dn = pltpu.roll(v, half, axis=1)
        return v * c + jnp.where(lane < half, up, dn) * s

    def store_heads(z, o_ref, first=0):
        for j in range(z.shape[1] // HEAD_DIM):
            o_ref[0, first + j] = z[:, j * HEAD_DIM:(j + 1) * HEAD_DIM]

    def rope_store(z, o_ref, c, s, half, scale):
        for i in range(z.shape[1] // LANES):
            r = rope(z[:, i * LANES:(i + 1) * LANES], c, s, half)
            if scale != 1.0:
                r = r * scale
            store_heads(r.astype(BF16), o_ref, first=i * (LANES // HEAD_DIM))

    rc, rs = rc_ref[...], rs_ref[...]
    sc, ss = sc_ref[...], ss_ref[...]
    scale = HEAD_DIM ** -0.5
    rope_store(seg(O_RQ, O_RK), rq_ref, rc, rs, HEAD_DIM // 2, 1.0)
    rope_store(seg(O_RK, O_RV), rk_ref, rc, rs, HEAD_DIM // 2, scale)
    store_heads(seg(O_RV, O_RG).astype(BF16), rv_ref)
    rg_ref[0] = _dot_nt(wgt_ref[...], h)
    rope_store(seg(O_SQ, O_SK), sq_ref, sc, ss, ROPE_DIM // 2, scale)
    rope_store(seg(O_SK, O_SV), sk_ref, sc, ss, ROPE_DIM // 2, 1.0)
    store_heads(seg(O_SV, O_CA).astype(BF16), sv_ref)
    ch_ref[0] = seg(O_CA, O_CG) * jax.nn.sigmoid(seg(O_CG, D_IN))


def _inproj(x, g, w_bf, tm):
    B, S, D = x.shape
    tm = min(tm, S)
    rc, rs = _rope_tables(S, RET_ROPE_THETA, HEAD_DIM)
    sc, ss = _rope_tables(S, ROPE_THETA, ROPE_DIM)
    tab = pl.BlockSpec((tm, LANES), lambda s, b: (s, 0))

    def out(width, dtype):
        return (jax.ShapeDtypeStruct((B, S, width), dtype),
                pl.BlockSpec((1, tm, width), lambda s, b: (b, s, 0)))

    def out_heads(heads):
        return (jax.ShapeDtypeStruct((B, heads, S, HEAD_DIM), BF16),
                pl.BlockSpec((1, heads, tm, HEAD_DIM), lambda s, b: (b, 0, s, 0)))

    gate_t = (jax.ShapeDtypeStruct((B, RET_W, S), F32), pl.BlockSpec((1, RET_W, tm), lambda s, b: (b, 0, s)))
    outs = [out_heads(RET_HEADS), out_heads(RET_HEADS), out_heads(RET_HEADS), gate_t,
            out_heads(SWA_HEADS), out_heads(SWA_KV_HEADS), out_heads(SWA_KV_HEADS), out(CONV_CH, F32)]
    return pl.pallas_call(
        _inproj_body,
        grid=(S // tm, B),
        in_specs=[pl.BlockSpec((1, tm, D), lambda s, b: (b, s, 0)),
                  pl.BlockSpec((1, D), lambda s, b: (0, 0)),
                  pl.BlockSpec((D, D_IN), lambda s, b: (0, 0)),
                  pl.BlockSpec((RET_W, D), lambda s, b: (0, 0)),
                  tab, tab, tab, tab],
        out_specs=[o[1] for o in outs],
        out_shape=[o[0] for o in outs],
        scratch_shapes=[pltpu.VMEM((tm, D_IN), F32)],
        compiler_params=_params("arbitrary", "arbitrary"),
        name="inproj",
    )(x, g.reshape(1, D), w_bf, w_bf[:, O_RG:O_SQ].T, rc, rs, sc, ss)


def _ret_body(q_ref, k_ref, v_ref, g_ref, dm_ref, kd_ref, qd_ref, cd_ref, gn_ref, o_ref, st_ref, *, ts):
    @pl.when(pl.program_id(2) == 0)
    def _():
        st_ref[...] = jnp.zeros_like(st_ref)

    C = RET_CHUNK
    for c in range(ts // C):
        rows = slice(c * C, (c + 1) * C)
        q, k, v = q_ref[0, 0, rows, :], k_ref[0, 0, rows, :], v_ref[0, 0, rows, :]
        st = st_ref[...]
        scores = _dot_nt(k, q) * dm_ref[0]
        intra = _dot_tn(v, scores.astype(BF16))
        cross = _dot_nt(st.astype(BF16), q) * qd_ref[0]
        kdec = (k.astype(F32) * kd_ref[0]).astype(BF16)
        st_ref[...] = st * cd_ref[0] + _dot_tn(v, kdec)
        o = intra + cross
        mu = jnp.mean(o, axis=0, keepdims=True)
        d = o - mu
        var = jnp.mean(d * d, axis=0, keepdims=True)
        on = d * lax.rsqrt(var + EPS) * gn_ref[0]
        o_ref[0, :, rows] = (_silu(g_ref[0, :, rows]) * on).astype(BF16)


def _retention(rq, rk, rv, rg_t, gn_g, ts):
    B, H, S, D = rq.shape
    ts = min(ts, S)
    C, W = RET_CHUNK, RET_W
    lg = jnp.log(1.0 - 2.0 ** (-5.0 - jnp.arange(H, dtype=F32)))
    idx = jnp.arange(C)
    rel = idx[:, None] - idx[None, :]
    dmask = jnp.where(rel[None] >= 0,
                      jnp.exp(jnp.maximum(rel, 0)[None].astype(F32) * lg[:, None, None]), 0.0)
    k_decay = jnp.exp((C - 1 - idx)[:, None].astype(F32) * lg[None, :])
    q_decay = jnp.exp((idx + 1)[:, None].astype(F32) * lg[None, :])
    chunk_decay = jnp.exp(C * lg)
    dm_t = jnp.swapaxes(dmask, 1, 2)
    kd = jnp.broadcast_to(k_decay.T[:, :, None], (H, C, D))
    qd = jnp.broadcast_to(q_decay.T[:, None, :], (H, D, C))
    cd = jnp.broadcast_to(chunk_decay[:, None, None], (H, D, D))
    gn = jnp.broadcast_to(gn_g.reshape(H, D, 1), (H, D, C))
    heads = pl.BlockSpec((1, 1, ts, D), lambda b, h, n: (b, h, n, 0))
    chan = pl.BlockSpec((1, D, ts), lambda b, h, n: (b, h, n))

    def per_head(r, c):
        return pl.BlockSpec((1, r, c), lambda b, h, n: (h, 0, 0))

    return pl.pallas_call(
        functools.partial(_ret_body, ts=ts),
        grid=(B, H, S // ts),
        in_specs=[heads, heads, heads, chan, per_head(C, C), per_head(C, D), per_head(D, C),
                  per_head(D, D), per_head(D, C)],
        out_specs=chan,
        out_shape=jax.ShapeDtypeStruct((B, W, S), BF16),
        scratch_shapes=[pltpu.VMEM((D, D), F32)],
        compiler_params=_params("arbitrary", "arbitrary", "arbitrary"),
        name="retention",
    )(rq, rk, rv, rg_t, dm_t, kd, qd, cd, gn)


def _swa_body(sink_ref, q_ref, kc_ref, kp_ref, vc_ref, vp_ref, bias_ref, o_ref, kcat, vcat, sbuf, pbuf, rbuf,
              *, tq):
    hk, n = pl.program_id(1), pl.program_id(2)
    kcat[0:BLOCK], kcat[BLOCK:] = kp_ref[0, 0], kc_ref[0, 0]
    vcat[0:BLOCK], vcat[BLOCK:] = vp_ref[0, 0], vc_ref[0, 0]
    width = SWA_GROUP * BLOCK
    group = lax.broadcasted_iota(jnp.int32, (1, width), 1) // BLOCK
    sink = jnp.zeros((1, width), F32)
    for g in range(SWA_GROUP):
        sink = jnp.where(group == g, sink_ref[hk * SWA_GROUP + g], sink)
    nq = tq // BLOCK
    for j in range(nq):
        q = q_ref[0, :, j * BLOCK:(j + 1) * BLOCK, :].reshape(width, HEAD_DIM)
        kb = kcat[j * BLOCK:(j + 2) * BLOCK]
        sbuf[j] = _dot_nt(kb, q)
    for j in range(nq):
        s = sbuf[j] + (bias_ref[0] if j > 0 else bias_ref[jnp.where(n == 0, 1, 0)])
        m = jnp.maximum(jnp.max(s, axis=0, keepdims=True), sink)
        p = jnp.exp(s - m)
        rbuf[j] = 1.0 / (jnp.sum(p, axis=0, keepdims=True) + jnp.exp(sink - m))
        pbuf[j] = p.astype(BF16)
    for j in range(nq):
        vb = vcat[j * BLOCK:(j + 2) * BLOCK]
        o = _dot_tn(vb, pbuf[j]) * rbuf[j]
        for g in range(SWA_GROUP):
            o_ref[0, g * HEAD_DIM:(g + 1) * HEAD_DIM, j * BLOCK:(j + 1) * BLOCK] = (
                o[:, g * BLOCK:(g + 1) * BLOCK].astype(BF16))


def _swa(sq, sk, sv, sinks, tq):
    B, _, S, D = sq.shape
    tq = min(tq, S)
    r = tq // BLOCK
    qi = jnp.arange(BLOCK)[None, :] + BLOCK
    kj = jnp.arange(2 * BLOCK)[:, None]
    rel = qi - kj
    allowed = (rel >= 0) & (rel < WINDOW)
    allowed = jnp.stack([allowed, allowed & (kj >= BLOCK)])
    bias = jnp.tile(jnp.where(allowed, 0.0, NEG_INF).astype(F32), (1, 1, SWA_GROUP))
    nq, width = tq // BLOCK, SWA_GROUP * BLOCK
    cur = pl.BlockSpec((1, 1, tq, D), lambda b, h, n: (b, h, n, 0))
    prev = pl.BlockSpec((1, 1, BLOCK, D), lambda b, h, n: (b, h, jnp.maximum(n * r - 1, 0), 0))
    return pl.pallas_call(
        functools.partial(_swa_body, tq=tq),
        grid=(B, SWA_KV_HEADS, S // tq),
        in_specs=[pl.BlockSpec(memory_space=pltpu.SMEM),
                  pl.BlockSpec((1, SWA_GROUP, tq, D), lambda b, h, n: (b, h, n, 0)),
                  cur, prev, cur, prev,
                  pl.BlockSpec((2, 2 * BLOCK, width), lambda b, h, n: (0, 0, 0))],
        out_specs=pl.BlockSpec((1, SWA_GROUP * D, tq), lambda b, h, n: (b, h, n)),
        out_shape=jax.ShapeDtypeStruct((B, SWA_W, S), BF16),
        scratch_shapes=[pltpu.VMEM((BLOCK + tq, D), BF16), pltpu.VMEM((BLOCK + tq, D), BF16),
                        pltpu.VMEM((nq, 2 * BLOCK, width), F32), pltpu.VMEM((nq, 2 * BLOCK, width), BF16),
                        pltpu.VMEM((nq, 1, width), F32)],
        compiler_params=_params("arbitrary", "arbitrary", "arbitrary"),
        name="swa",
    )(sinks.astype(F32), sq, sk, sk, sv, sv, bias)


def _conv_body(h_ref, dw_ref, db_ref, lg_ref, lb_ref, pw_ref, o_ref, hbuf, hsh, *, ts, rb):
    s = pl.program_id(1)
    sub = 8
    span = ts + CONV_HALO - sub

    @pl.when(s == 0)
    def _():
        hbuf[0:CONV_HALO, :] = jnp.zeros((CONV_HALO, CONV_CH), F32)

    @pl.when(s > 0)
    def _():
        hbuf[0:CONV_HALO, :] = hbuf[ts:ts + CONV_HALO, :]

    hbuf[CONV_HALO:CONV_HALO + ts, :] = h_ref[0]
    for p in range(1, sub):
        hsh[p - 1] = hbuf[p:p + span, :]
    off = CONV_HALO - (CONV_WIDTH - 1)
    for r in range(ts // rb):
        base = r * rb
        acc = jnp.broadcast_to(db_ref[...], (rb, CONV_CH))
        for w in range(CONV_WIDTH):
            q, p = divmod(off + w, sub)
            lo = base + q * sub
            tap = hbuf[lo:lo + rb, :] if p == 0 else hsh[p - 1, lo:lo + rb, :]
            acc = acc + tap * dw_ref[w:w + 1, :]
        mu = jnp.mean(acc, axis=-1, keepdims=True)
        d = acc - mu
        var = jnp.mean(d * d, axis=-1, keepdims=True)
        hn = d * lax.rsqrt(var + EPS) * lg_ref[...] + lb_ref[...]
        o_ref[0, base:base + rb, :] = _dot(_silu(hn).astype(BF16), pw_ref[...]).astype(BF16)


def _conv(ch, dw_w, dw_b, ln_g, ln_b, pw_w, ts, rb):
    B, S, W = ch.shape
    ts = min(ts, S)
    rb = min(rb, ts)
    pw = jnp.zeros((W, W), F32)
    gd = W // CONV_GROUPS
    for g in range(CONV_GROUPS):
        pw = pw.at[g * gd:(g + 1) * gd, g * gd:(g + 1) * gd].set(pw_w[g])
    act = pl.BlockSpec((1, ts, W), lambda b, s: (b, s, 0))

    def const(shape):
        return pl.BlockSpec(shape, lambda b, s: (0,) * len(shape))

    return pl.pallas_call(
        functools.partial(_conv_body, ts=ts, rb=rb),
        grid=(B, S // ts),
        in_specs=[act, const((CONV_WIDTH, W)), const((1, W)), const((1, W)), const((1, W)), const((W, W))],
        out_specs=act,
        out_shape=jax.ShapeDtypeStruct((B, S, W), BF16),
        scratch_shapes=[pltpu.VMEM((CONV_HALO + ts, W), F32), pltpu.VMEM((7, CONV_HALO + ts - 8, W), F32)],
        compiler_params=_params("arbitrary", "arbitrary"),
        name="conv",
    )(ch, dw_w, dw_b.reshape(1, W), ln_g.reshape(1, W), ln_b.reshape(1, W), pw.astype(BF16))


def _mixer_residual(yr_ref, ys_ref, yc_ref, w_ref, x_ref):
    a, b = RET_W, RET_W + SWA_W
    return x_ref[...] + (_dot_tn(yr_ref[0], w_ref[0:a, :]) + _dot_tn(ys_ref[0], w_ref[a:b, :])
                         + _dot(yc_ref[...], w_ref[b:, :]))


def _mixer_specs(tm, tiles_per_seq):
    def chan(width):
        return pl.BlockSpec((1, width, tm), lambda i: (i // tiles_per_seq, 0, i % tiles_per_seq))

    return [chan(RET_W), chan(SWA_W), pl.BlockSpec((tm, CONV_CH), lambda i: (i, 0))]


def _outproj_body(yr_ref, ys_ref, yc_ref, w_ref, x_ref, g_ref, wr_ref, before_ref, xo_ref, hn_ref, rt_ref, rl_ref):
    x = _mixer_residual(yr_ref, ys_ref, yc_ref, w_ref, x_ref)
    xo_ref[...] = x
    h = _rms(x, g_ref[...]).astype(BF16)
    hn_ref[...] = h
    logits = _dot(h, wr_ref[...])
    lane = lax.broadcasted_iota(jnp.int32, logits.shape, 1).astype(F32)
    lg = jnp.where(lane < N_EXPERTS, logits, -jnp.inf)
    m1 = jnp.max(lg, axis=-1, keepdims=True)
    i1 = jnp.min(jnp.where(lg == m1, lane, float(LANES)), axis=-1, keepdims=True)
    lg2 = jnp.where(lane == i1, -jnp.inf, lg)
    m2 = jnp.max(lg2, axis=-1, keepdims=True)
    i2 = jnp.min(jnp.where(lg2 == m2, lane, float(LANES)), axis=-1, keepdims=True)
    e = jnp.exp(m2 - m1)
    w1 = 1.0 / (1.0 + e)
    w2 = e / (1.0 + e)
    hit1, hit2 = lane == i1, lane == i2
    onehot = jnp.where(jnp.logical_or(hit1, hit2), 1.0, 0.0)
    rank = _dot(before_ref[...], onehot.astype(BF16))
    count = jnp.sum(onehot, axis=0, keepdims=True).astype(jnp.int32)
    run_len = (count + (RUN_ALIGN - 1)) & ~(RUN_ALIGN - 1)
    lower = (lax.broadcasted_iota(jnp.int32, (LANES, LANES), 0) < lax.broadcasted_iota(jnp.int32, (LANES, LANES), 1))
    run_rows = jnp.broadcast_to(run_len.astype(F32), (8, LANES)).astype(BF16)
    loc_off = _dot(run_rows, jnp.where(lower, 1.0, 0.0).astype(BF16))[0:1, :]
    pos = rank + loc_off
    p1 = jnp.sum(jnp.where(hit1, pos, 0.0), axis=-1, keepdims=True)
    p2 = jnp.sum(jnp.where(hit2, pos, 0.0), axis=-1, keepdims=True)
    cols = (i1, i2, w1, w2, p1, p2)
    rt = jnp.zeros_like(logits)
    for c, v in enumerate(cols):
        rt = jnp.where(lane == c, v, rt)
    rt_ref[...] = rt
    rl_ref[0] = jnp.broadcast_to(run_len, (8, LANES))


def _outproj_route(yr, ys, yc, w_bf, x, g, w_router):
    T, D = x.shape
    tm = MOE_TB
    S = ys.shape[2]

    def row(width):
        return pl.BlockSpec((tm, width), lambda i: (i, 0))

    def const(shape):
        return pl.BlockSpec(shape, lambda i: (0,) * len(shape))

    wr = jnp.zeros((D, LANES), F32).at[:, :N_EXPERTS].set(w_router).astype(BF16)
    before = (jnp.arange(tm)[:, None] > jnp.arange(tm)[None, :]).astype(BF16)
    return pl.pallas_call(
        _outproj_body,
        grid=(T // tm,),
        in_specs=_mixer_specs(tm, S // tm) + [const((D, D)), row(D), const((1, D)), const((D, LANES)),
                                              const((tm, tm))],
        out_specs=[row(D), row(D), row(LANES), pl.BlockSpec((1, 8, LANES), lambda i: (i, 0, 0))],
        out_shape=[jax.ShapeDtypeStruct((T, D), F32), jax.ShapeDtypeStruct((T, D), BF16),
                   jax.ShapeDtypeStruct((T, LANES), F32), jax.ShapeDtypeStruct((T // tm, 8, LANES), jnp.int32)],
        compiler_params=_params("arbitrary"),
        name="outproj_route",
    )(yr, ys, yc, w_bf, x, g.reshape(1, D), wr, before)


def _dense_ffn_body(yr_ref, ys_ref, yc_ref, wo_ref, x_ref, g_ref, wg_ref, wu_ref, wd_ref, o_ref, *, chunks):
    x = _mixer_residual(yr_ref, ys_ref, yc_ref, wo_ref, x_ref)
    h = _rms(x, g_ref[...]).astype(BF16)
    acc = x
    for a, b in chunks:
        g = _dot(h, wg_ref[:, a:b])
        u = _dot(h, wu_ref[:, a:b])
        acc = acc + _dot((_silu(g) * u).astype(BF16), wd_ref[a:b, :])
    o_ref[...] = acc


def _outproj_dense_ffn(yr, ys, yc, wo_bf, x, g, wg, wu, wd, tm, fc):
    T, D = x.shape
    FF = wg.shape[1]
    S = ys.shape[2]
    tm = min(tm, S)
    chunks = tuple((a, min(a + fc, FF)) for a in range(0, FF, fc))

    def row(width):
        return pl.BlockSpec((tm, width), lambda i: (i, 0))

    def const(shape):
        return pl.BlockSpec(shape, lambda i: (0,) * len(shape), pipeline_mode=pl.Buffered(1))

    return pl.pallas_call(
        functools.partial(_dense_ffn_body, chunks=chunks),
        grid=(T // tm,),
        in_specs=_mixer_specs(tm, S // tm) + [const((D, D)), row(D), const((1, D)),
                                              const((D, FF)), const((D, FF)), const((FF, D))],
        out_specs=row(D),
        out_shape=jax.ShapeDtypeStruct((T, D), F32),
        compiler_params=_params("arbitrary"),
        name="outproj_dense_ffn",
    )(yr, ys, yc, wo_bf, x, g.reshape(1, D), wg.astype(BF16), wu.astype(BF16), wd.astype(BF16))


MOE_TB = 512
MOE_TM = 512
RUN_ALIGN = 8
RUN_SIZES = (512, 256, 128, 64, 32, 16, 8)
BLOCK_ROWS = TOP_K * MOE_TB + N_EXPERTS * RUN_ALIGN
BLOCK_LANES = 1152
MOE_FC = 512
MOE_LOAD_SLOTS = 6


def _route_meta(rt, rl, n_tiles):
    T = rt.shape[0]
    nb = T // MOE_TB
    pos = rt[:, 4:4 + TOP_K].astype(jnp.int32)
    run_len = rl[:, 0, :N_EXPERTS]
    loc_off = jnp.cumsum(run_len, axis=1) - run_len
    group = jnp.sum(run_len, axis=0)
    ptiles = (group + MOE_TM - 1) // MOE_TM
    tile_end = jnp.cumsum(ptiles)
    gstart = (tile_end - ptiles) * MOE_TM
    run_start = gstart[None, :] + jnp.cumsum(run_len, axis=0) - run_len
    tile_expert = jnp.sum(jnp.arange(n_tiles, dtype=jnp.int32)[:, None] >= tile_end[None, :], axis=1)
    i32 = lambda a: a.astype(jnp.int32)
    pos3 = pos.reshape(nb, MOE_TB, TOP_K)
    return dict(
        pos_l=i32(jnp.swapaxes(pos3, 1, 2)),
        pos_c=i32(pos3.reshape(T, TOP_K)),
        loc_off=i32(loc_off.reshape(-1)), run_start=i32(run_start.reshape(-1)), run_len=i32(run_len.reshape(-1)),
        pad_start=i32(gstart + group), pad_len=i32(ptiles * MOE_TM - group),
        tile_expert=i32(jnp.minimum(tile_expert, N_EXPERTS - 1)), n_used=i32(tile_end[-1:]))


def _run_dma(src, dst, src_off, dst_off, length, sem, wait):
    off = 0
    for k in RUN_SIZES:
        part = length & k

        @pl.when(part != 0)
        def _(off=off, k=k):
            cp = pltpu.make_async_copy(src.at[pl.ds(pl.multiple_of(src_off + off, RUN_ALIGN), k)],
                                       dst.at[pl.ds(pl.multiple_of(dst_off + off, RUN_ALIGN), k)], sem)
            if wait:
                cp.wait()
            else:
                cp.start()

        off = off + part


def _dispatch_body(lo_ref, rs_ref, rl_ref, ps_ref, pn_ref, nu_ref, pos_ref, h_ref, xs_hbm, sbuf, zbuf, sem):
    b, nb = pl.program_id(0), pl.num_programs(0)
    slot = b % 2
    n_tiles = xs_hbm.shape[0] // MOE_TM

    def zero_tile(i, wait):
        cp = pltpu.make_async_copy(zbuf, xs_hbm.at[pl.ds(pl.multiple_of(i * MOE_TM, MOE_TM), MOE_TM)], sem.at[2])
        if wait:
            cp.wait()
        else:
            cp.start()

    def runs(blk, s, wait):
        for e in range(N_EXPERTS):
            j = blk * N_EXPERTS + e
            _run_dma(sbuf.at[s], xs_hbm, lo_ref[j], rs_ref[j], rl_ref[j], sem.at[s], wait)

    @pl.when(b >= 2)
    def _():
        runs(b - 2, slot, True)

    pos = pos_ref[0]
    r = lax.broadcasted_iota(jnp.int32, (BLOCK_ROWS, MOE_TB), 0)
    onehot = jnp.logical_or(r == pos[0:1, :], r == pos[1:2, :])
    sbuf[slot] = _dot(jnp.where(onehot, 1.0, 0.0).astype(BF16), h_ref[...])
    runs(b, slot, False)

    @pl.when(b == nb - 1)
    def _():
        zbuf[...] = jnp.zeros_like(zbuf)
        for e in range(N_EXPERTS):
            _run_dma(zbuf, xs_hbm, 0, ps_ref[e], pn_ref[e], sem.at[2], False)
        lax.fori_loop(nu_ref[0], n_tiles, lambda i, c: zero_tile(i, False), None)
        for e in range(N_EXPERTS):
            _run_dma(zbuf, xs_hbm, 0, ps_ref[e], pn_ref[e], sem.at[2], True)
        lax.fori_loop(nu_ref[0], n_tiles, lambda i, c: zero_tile(i, True), None)

        @pl.when(b >= 1)
        def _():
            runs(b - 1, 1 - slot, True)

        runs(b, slot, True)


def _dispatch(hn, meta, n_rows):
    T, D = hn.shape
    nb = T // MOE_TB
    grid_spec = pltpu.PrefetchScalarGridSpec(
        num_scalar_prefetch=6,
        grid=(nb,),
        in_specs=[pl.BlockSpec((1, TOP_K, MOE_TB), lambda b, *_: (b, 0, 0)),
                  pl.BlockSpec((MOE_TB, D), lambda b, *_: (b, 0))],
        out_specs=pl.BlockSpec(memory_space=pl.ANY),
        scratch_shapes=[pltpu.VMEM((2, BLOCK_ROWS, D), F32), pltpu.VMEM((MOE_TM, D), F32),
                        pltpu.SemaphoreType.DMA((3,))],
    )
    return pl.pallas_call(
        _dispatch_body,
        grid_spec=grid_spec,
        out_shape=jax.ShapeDtypeStruct((n_rows, D), F32),
        compiler_params=pltpu.CompilerParams(dimension_semantics=("arbitrary",), vmem_limit_bytes=VMEM_LIMIT,
                                             has_side_effects=True),
        name="moe_dispatch",
    )(meta["loc_off"], meta["run_start"], meta["run_len"], meta["pad_start"], meta["pad_len"], meta["n_used"],
      meta["pos_l"], hn)


def _moe_ffn_body(te_ref, nu_ref, x_ref, wg_hbm, wu_hbm, wd_hbm, o_ref, wg_s, wu_s, wd_s, stage, sem):
    i = pl.program_id(0)
    e = te_ref[i]
    active = i < nu_ref[0]
    first = jnp.logical_or(i == 0, e != te_ref[jnp.maximum(i - 1, 0)])
    D, FF = wg_s.shape
    fsl = [slice(f * MOE_FC, (f + 1) * MOE_FC) for f in range(FF // MOE_FC)]
    slots, piece = stage.shape[0], stage.shape[1]
    psl = [slice(r, r + piece) for r in range(0, D, piece)]
    jobs = []
    for f in fsl:
        jobs += [(wg_hbm, wg_s, p, f) for p in psl] + [(wu_hbm, wu_s, p, f) for p in psl]
        jobs += [(wd_hbm, wd_s, f, p) for p in psl]
    per_chunk = len(jobs) // len(fsl)

    def copy(j):
        w_hbm, _, rows, cols = jobs[j]
        return pltpu.make_async_copy(w_hbm.at[e, rows, cols], stage.at[j % slots], sem.at[j % slots])

    def ffn(load):
        x = x_ref[...].astype(BF16)
        acc = jnp.zeros(o_ref.shape, F32)
        if load:
            for j in range(slots - 1):
                copy(j).start()
        for c, sl in enumerate(fsl):
            if load:
                for j in range(c * per_chunk, (c + 1) * per_chunk):
                    if j + slots - 1 < len(jobs):
                        copy(j + slots - 1).start()
                    copy(j).wait()
                    _, w_s, rows, cols = jobs[j]
                    w_s[rows, cols] = stage[j % slots].astype(BF16)
            g = _dot(x, wg_s[:, sl])
            u = _dot(x, wu_s[:, sl])
            acc = acc + _dot((_silu(g) * u).astype(BF16), wd_s[sl, :])
        o_ref[...] = acc

    @pl.when(jnp.logical_and(active, first))
    def _():
        ffn(load=True)

    @pl.when(jnp.logical_and(active, jnp.logical_not(first)))
    def _():
        ffn(load=False)

    @pl.when(jnp.logical_not(active))
    def _():
        o_ref[...] = jnp.zeros_like(o_ref)


def _moe_ffn(xs, meta, wg, wu, wd):
    R, D = xs.shape
    E, _, FF = wg.shape
    grid_spec = pltpu.PrefetchScalarGridSpec(
        num_scalar_prefetch=2,
        grid=(R // MOE_TM,),
        in_specs=[pl.BlockSpec((MOE_TM, D), lambda i, te, nu: (jnp.minimum(i, nu[0] - 1), 0)),
                  pl.BlockSpec(memory_space=pl.ANY), pl.BlockSpec(memory_space=pl.ANY),
                  pl.BlockSpec(memory_space=pl.ANY)],
        out_specs=pl.BlockSpec((MOE_TM, D), lambda i, te, nu: (i, 0)),
        scratch_shapes=[pltpu.VMEM((D, FF), BF16), pltpu.VMEM((D, FF), BF16), pltpu.VMEM((FF, D), BF16),
                        pltpu.VMEM((MOE_LOAD_SLOTS, MOE_FC, MOE_FC), F32),
                        pltpu.SemaphoreType.DMA((MOE_LOAD_SLOTS,))],
    )
    return pl.pallas_call(
        _moe_ffn_body,
        grid_spec=grid_spec,
        out_shape=jax.ShapeDtypeStruct((R, D), F32),
        compiler_params=_params("arbitrary"),
        name="moe_ffn",
    )(meta["tile_expert"], meta["n_used"], xs, wg, wu, wd)


def _combine_body(lo_ref, rs_ref, rl_ref, ys_hbm, x_ref, rt_ref, pc_ref, g_ref, o_ref, ybuf, sem):
    b, nb = pl.program_id(0), pl.num_programs(0)
    slot = b % 2

    def runs(blk, s, wait):
        for e in range(N_EXPERTS):
            j = blk * N_EXPERTS + e
            _run_dma(ys_hbm, ybuf.at[s], rs_ref[j], lo_ref[j], rl_ref[j], sem.at[s], wait)

    @pl.when(b == 0)
    def _():
        ybuf[...] = jnp.zeros_like(ybuf)
        runs(0, 0, False)

    @pl.when(b + 1 < nb)
    def _():
        runs(b + 1, 1 - slot, False)

    runs(b, slot, True)
    y = ybuf[slot].astype(BF16)
    pc, rt = pc_ref[...], rt_ref[...]
    lane = lax.broadcasted_iota(jnp.int32, (MOE_TB, BLOCK_LANES), 1)
    q = jnp.where(lane == pc[:, 0:1], rt[:, 2:3], jnp.where(lane == pc[:, 1:2], rt[:, 3:4], 0.0)).astype(BF16)
    o_ref[...] = _rms(x_ref[...] + _dot(q, y), g_ref[...])


def _combine(ys, meta, x, rt, g):
    T, D = x.shape
    grid_spec = pltpu.PrefetchScalarGridSpec(
        num_scalar_prefetch=3,
        grid=(T // MOE_TB,),
        in_specs=[pl.BlockSpec(memory_space=pl.ANY),
                  pl.BlockSpec((MOE_TB, D), lambda b, *_: (b, 0)),
                  pl.BlockSpec((MOE_TB, LANES), lambda b, *_: (b, 0)),
                  pl.BlockSpec((MOE_TB, TOP_K), lambda b, *_: (b, 0)),
                  pl.BlockSpec((1, D), lambda b, *_: (0, 0))],
        out_specs=pl.BlockSpec((MOE_TB, D), lambda b, *_: (b, 0)),
        scratch_shapes=[pltpu.VMEM((2, BLOCK_LANES, D), F32), pltpu.SemaphoreType.DMA((2,))],
    )
    return pl.pallas_call(
        _combine_body,
        grid_spec=grid_spec,
        out_shape=jax.ShapeDtypeStruct((T, D), F32),
        compiler_params=_params("arbitrary"),
        name="moe_combine",
    )(meta["loc_off"], meta["run_start"], meta["run_len"], ys, x, rt, meta["pos_c"], g.reshape(1, D))


def kernel(x, norm_mix_g, w_in, ret_gn_g, attn_sinks, conv_dw_w, conv_dw_b, conv_ln_g, conv_ln_b,
           conv_pw_w, w_out, norm_ffn_g, ffn_w_gate, ffn_w_up, ffn_w_down, moe_router, moe_w_gate,
           moe_w_up, moe_w_down, final_norm_g):
    B, S, D = x.shape
    T = B * S
    depth = w_in.shape[0]
    assert depth == 2 and ffn_w_gate.shape[0] == 1 and moe_router.shape[0] == 1, "dense layer then MoE layer"
    assert T % MOE_TB == 0 and S % MOE_TB == 0
    assert moe_w_gate.shape[-1] % MOE_FC == 0 and D % MOE_FC == 0

    for l in range(depth):
        rq, rk, rv, rg, sq, sk, sv, ch = _inproj(x, norm_mix_g[l], w_in[l].astype(BF16), tm=INPROJ_TM)
        y_ret_t = _retention(rq, rk, rv, rg, ret_gn_g[l], ts=RET_TS)
        y_swa_t = _swa(sq, sk, sv, attn_sinks[l], tq=SWA_TQ)
        y_conv = _conv(ch, conv_dw_w[l], conv_dw_b[l], conv_ln_g[l], conv_ln_b[l], conv_pw_w[l],
                       ts=CONV_TS, rb=CONV_RB)
        flat = lambda a: a.reshape(T, a.shape[-1])
        j = l // 2
        if l % 2 == 0:
            x = _outproj_dense_ffn(y_ret_t, y_swa_t, flat(y_conv), w_out[l].astype(BF16), flat(x),
                                   norm_ffn_g[l], ffn_w_gate[j], ffn_w_up[j], ffn_w_down[j],
                                   tm=FFN_TM, fc=FFN_FC).reshape(B, S, D)
        else:
            x2, hn, rt, rl = _outproj_route(y_ret_t, y_swa_t, flat(y_conv), w_out[l].astype(BF16), flat(x),
                                            norm_ffn_g[l], moe_router[j])
            n_rows = TOP_K * T + (T // MOE_TB) * N_EXPERTS * RUN_ALIGN + N_EXPERTS * MOE_TM
            meta = _route_meta(rt, rl, n_rows // MOE_TM)
            xs = _dispatch(hn, meta, n_rows)
            ys = _moe_ffn(xs, meta, moe_w_gate[j], moe_w_up[j], moe_w_down[j])
            x = _combine(ys, meta, x2, rt, final_norm_g).reshape(B, S, D)
    return x
```

```python
import functools

import jax
import jax.numpy as jnp
from jax import lax
from jax.experimental import pallas as pl
from jax.experimental.pallas import tpu as pltpu

F32 = jnp.float32
BF16 = jnp.bfloat16

HEAD_DIM = 64
RET_HEADS = 4
RET_W = RET_HEADS * HEAD_DIM
SWA_HEADS = 8
SWA_KV_HEADS = 2
SWA_GROUP = SWA_HEADS // SWA_KV_HEADS
SWA_W = SWA_HEADS * HEAD_DIM
SWA_KV_W = SWA_KV_HEADS * HEAD_DIM
CONV_CH = 256
CONV_GROUPS = 4
CONV_WIDTH = 31
WINDOW = 128
BLOCK = 128
RET_CHUNK = 128
ROPE_THETA = 500000.0
ROPE_DIM = HEAD_DIM // 4
RET_ROPE_THETA = 10000.0
N_EXPERTS = 8
TOP_K = 2
EPS = 1e-6
NEG_INF = -1e30

LANES = 128
CONV_HALO = 32
VMEM_LIMIT = 56 * 1024 * 1024

O_RQ, O_RK, O_RV, O_RG = 0, RET_W, 2 * RET_W, 3 * RET_W
O_SQ = 4 * RET_W
O_SK = O_SQ + SWA_W
O_SV = O_SK + SWA_KV_W
O_CA = O_SV + SWA_KV_W
O_CG = O_CA + CONV_CH
D_IN = O_CG + CONV_CH


INPROJ_TM = 1024
RET_TS = 4096
SWA_TQ = 4096
CONV_TS = 1024
CONV_RB = 256
FFN_TM = 1024
FFN_FC = 512


def _params(*sem):
    return pltpu.CompilerParams(dimension_semantics=sem, vmem_limit_bytes=VMEM_LIMIT)


def _rms(x, g):
    return x * lax.rsqrt(jnp.mean(x * x, axis=-1, keepdims=True) + EPS) * g


def _silu(x):
    return x * jax.nn.sigmoid(x)


def _dot(a, b):
    return jnp.dot(a, b, preferred_element_type=F32)


def _dot_nt(a, b):
    return lax.dot_general(a, b, (((1,), (1,)), ((), ())), preferred_element_type=F32)


def _dot_tn(a, b):
    return lax.dot_general(a, b, (((0,), (0,)), ((), ())), preferred_element_type=F32)


def _rope_tables(seq, theta, rot_dim):
    half = rot_dim // 2
    inv = 1.0 / (theta ** (jnp.arange(half, dtype=F32) / half))
    ang = jnp.arange(seq, dtype=F32)[:, None] * inv[None, :]
    cos, sin = jnp.cos(ang), jnp.sin(ang)
    rest = HEAD_DIM - rot_dim
    c = jnp.concatenate([cos, cos, jnp.ones((seq, rest), F32)], axis=1)
    s = jnp.concatenate([-sin, sin, jnp.zeros((seq, rest), F32)], axis=1)
    reps = LANES // HEAD_DIM
    return jnp.tile(c, (1, reps)), jnp.tile(s, (1, reps))


def _inproj_body(x_ref, g_ref, w_ref, wgt_ref, rc_ref, rs_ref, sc_ref, ss_ref,
                 rq_ref, rk_ref, rv_ref, rg_ref, sq_ref, sk_ref, sv_ref, ch_ref, zbuf):
    h = _rms(x_ref[0], g_ref[...]).astype(BF16)
    lane = lax.broadcasted_iota(jnp.int32, (1, LANES), 1) % HEAD_DIM
    for a, b in ((O_RQ, O_RK), (O_RK, O_RV), (O_RV, O_RG), (O_SQ, O_SK), (O_SK, O_CA), (O_CA, O_CG), (O_CG, D_IN)):
        zbuf[:, a:b] = _dot(h, w_ref[:, a:b])

    def seg(a, b):
        return zbuf[:, a:b]

    def rope(v, c, s, half):
        up = pltpu.roll(v, LANES - half, axis=1)
        dn = pltpu.roll(v, half, axis=1)
        return v * c + jnp.where(lane < half, up, dn) * s

    def store_heads(z, o_ref, first=0):
        for j in range(z.shape[1] // HEAD_DIM):
            o_ref[0, first + j] = z[:, j * HEAD_DIM:(j + 1) * HEAD_DIM]

    def rope_store(z, o_ref, c, s, half, scale):
        for i in range(z.shape[1] // LANES):
            r = rope(z[:, i * LANES:(i + 1) * LANES], c, s, half)
            if scale != 1.0:
                r = r * scale
            store_heads(r.astype(BF16), o_ref, first=i * (LANES // HEAD_DIM))

    rc, rs = rc_ref[...], rs_ref[...]
    sc, ss = sc_ref[...], ss_ref[...]
    scale = HEAD_DIM ** -0.5
    rope_store(seg(O_RQ, O_RK), rq_ref, rc, rs, HEAD_DIM // 2, 1.0)
    rope_store(seg(O_RK, O_RV), rk_ref, rc, rs, HEAD_DIM // 2, scale)
    store_heads(seg(O_RV, O_RG).astype(BF16), rv_ref)
    rg_ref[0] = _dot_nt(wgt_ref[...], h)
    rope_store(seg(O_SQ, O_SK), sq_ref, sc, ss, ROPE_DIM // 2, scale)
    rope_store(seg(O_SK, O_SV), sk_ref, sc, ss, ROPE_DIM // 2, 1.0)
    store_heads(seg(O_SV, O_CA).astype(BF16), sv_ref)
    ch_ref[0] = seg(O_CA, O_CG) * jax.nn.sigmoid(seg(O_CG, D_IN))


def _inproj(x, g, w_bf, tm):
    B, S, D = x.shape
    tm = min(tm, S)
    rc, rs = _rope_tables(S, RET_ROPE_THETA, HEAD_DIM)
    sc, ss = _rope_tables(S, ROPE_THETA, ROPE_DIM)
    tab = pl.BlockSpec((tm, LANES), lambda s, b: (s, 0))

    def out(width, dtype):
        return (jax.ShapeDtypeStruct((B, S, width), dtype),
                pl.BlockSpec((1, tm, width), lambda s, b: (b, s, 0)))

    def out_heads(heads):
        return (jax.ShapeDtypeStruct((B, heads, S, HEAD_DIM), BF16),
                pl.BlockSpec((1, heads, tm, HEAD_DIM), lambda s, b: (b, 0, s, 0)))

    gate_t = (jax.ShapeDtypeStruct((B, RET_W, S), F32), pl.BlockSpec((1, RET_W, tm), lambda s, b: (b, 0, s)))
    outs = [out_heads(RET_HEADS), out_heads(RET_HEADS), out_heads(RET_HEADS), gate_t,
            out_heads(SWA_HEADS), out_heads(SWA_KV_HEADS), out_heads(SWA_KV_HEADS), out(CONV_CH, F32)]
    return pl.pallas_call(
        _inproj_body,
        grid=(S // tm, B),
        in_specs=[pl.BlockSpec((1, tm, D), lambda s, b: (b, s, 0)),
                  pl.BlockSpec((1, D), lambda s, b: (0, 0)),
                  pl.BlockSpec((D, D_IN), lambda s, b: (0, 0)),
                  pl.BlockSpec((RET_W, D), lambda s, b: (0, 0)),
                  tab, tab, tab, tab],
        out_specs=[o[1] for o in outs],
        out_shape=[o[0] for o in outs],
        scratch_shapes=[pltpu.VMEM((tm, D_IN), F32)],
        compiler_params=_params("arbitrary", "arbitrary"),
        name="inproj",
    )(x, g.reshape(1, D), w_bf, w_bf[:, O_RG:O_SQ].T, rc, rs, sc, ss)


def _ret_body(q_ref, k_ref, v_ref, g_ref, dm_ref, kd_ref, qd_ref, cd_ref, gn_ref, o_ref, st_ref, *, ts):
    @pl.when(pl.program_id(2) == 0)
    def _():
        st_ref[...] = jnp.zeros_like(st_ref)

    C = RET_CHUNK
    for c in range(ts // C):
        rows = slice(c * C, (c + 1) * C)
        q, k, v = q_ref[0, 0, rows, :], k_ref[0, 0, rows, :], v_ref[0, 0, rows, :]
        st = st_ref[...]
        scores = _dot_nt(k, q) * dm_ref[0]
        intra = _dot_tn(v, scores.astype(BF16))
        cross = _dot_nt(st.astype(BF16), q) * qd_ref[0]
        kdec = (k.astype(F32) * kd_ref[0]).astype(BF16)
        st_ref[...] = st * cd_ref[0] + _dot_tn(v, kdec)
        o = intra + cross
        mu = jnp.mean(o, axis=0, keepdims=True)
        d = o - mu
        var = jnp.mean(d * d, axis=0, keepdims=True)
        on = d * lax.rsqrt(var + EPS) * gn_ref[0]
        o_ref[0, :, rows] = (_silu(g_ref[0, :, rows]) * on).astype(BF16)


def _retention(rq, rk, rv, rg_t, gn_g, ts):
    B, H, S, D = rq.shape
    ts = min(ts, S)
    C, W = RET_CHUNK, RET_W
    lg = jnp.log(1.0 - 2.0 ** (-5.0 - jnp.arange(H, dtype=F32)))
    idx = jnp.arange(C)
    rel = idx[:, None] - idx[None, :]
    dmask = jnp.where(rel[None] >= 0,
                      jnp.exp(jnp.maximum(rel, 0)[None].astype(F32) * lg[:, None, None]), 0.0)
    k_decay = jnp.exp((C - 1 - idx)[:, None].astype(F32) * lg[None, :])
    q_decay = jnp.exp((idx + 1)[:, None].astype(F32) * lg[None, :])
    chunk_decay = jnp.exp(C * lg)
    dm_t = jnp.swapaxes(dmask, 1, 2)
    kd = jnp.broadcast_to(k_decay.T[:, :, None], (H, C, D))
    qd = jnp.broadcast_to(q_decay.T[:, None, :], (H, D, C))
    cd = jnp.broadcast_to(chunk_decay[:, None, None], (H, D, D))
    gn = jnp.broadcast_to(gn_g.reshape(H, D, 1), (H, D, C))
    heads = pl.BlockSpec((1, 1, ts, D), lambda b, h, n: (b, h, n, 0))
    chan = pl.BlockSpec((1, D, ts), lambda b, h, n: (b, h, n))

    def per_head(r, c):
        return pl.BlockSpec((1, r, c), lambda b, h, n: (h, 0, 0))

    return pl.pallas_call(
        functools.partial(_ret_body, ts=ts),
        grid=(B, H, S // ts),
        in_specs=[heads, heads, heads, chan, per_head(C, C), per_head(C, D), per_head(D, C),
                  per_head(D, D), per_head(D, C)],
        out_specs=chan,
        out_shape=jax.ShapeDtypeStruct((B, W, S), BF16),
        scratch_shapes=[pltpu.VMEM((D, D), F32)],
        compiler_params=_params("arbitrary", "arbitrary", "arbitrary"),
        name="retention",
    )(rq, rk, rv, rg_t, dm_t, kd, qd, cd, gn)


def _swa_body(sink_ref, q_ref, kc_ref, kp_ref, vc_ref, vp_ref, bias_ref, o_ref, kcat, vcat, sbuf, pbuf, rbuf,
              *, tq):
    hk, n = pl.program_id(1), pl.program_id(2)
    kcat[0:BLOCK], kcat[BLOCK:] = kp_ref[0, 0], kc_ref[0, 0]
    vcat[0:BLOCK], vcat[BLOCK:] = vp_ref[0, 0], vc_ref[0, 0]
    width = SWA_GROUP * BLOCK
    group = lax.broadcasted_iota(jnp.int32, (1, width), 1) // BLOCK
    sink = jnp.zeros((1, width), F32)
    for g in range(SWA_GROUP):
        sink = jnp.where(group == g, sink_ref[hk * SWA_GROUP + g], sink)
    nq = tq // BLOCK
    for j in range(nq):
        q = q_ref[0, :, j * BLOCK:(j + 1) * BLOCK, :].reshape(width, HEAD_DIM)
        kb = kcat[j * BLOCK:(j + 2) * BLOCK]
        sbuf[j] = _dot_nt(kb, q)
    for j in range(nq):
        s = sbuf[j] + (bias_ref[0] if j > 0 else bias_ref[jnp.where(n == 0, 1, 0)])
        m = jnp.maximum(jnp.max(s, axis=0, keepdims=True), sink)
        p = jnp.exp(s - m)
        rbuf[j] = 1.0 / (jnp.sum(p, axis=0, keepdims=True) + jnp.exp(sink - m))
        pbuf[j] = p.astype(BF16)
    for j in range(nq):
        vb = vcat[j * BLOCK:(j + 2) * BLOCK]
        o = _dot_tn(vb, pbuf[j]) * rbuf[j]
        for g in range(SWA_GROUP):
            o_ref[0, g * HEAD_DIM:(g + 1) * HEAD_DIM, j * BLOCK:(j + 1) * BLOCK] = (
                o[:, g * BLOCK:(g + 1) * BLOCK].astype(BF16))


def _swa(sq, sk, sv, sinks, tq):
    B, _, S, D = sq.shape
    tq = min(tq, S)
    r = tq // BLOCK
    qi = jnp.arange(BLOCK)[None, :] + BLOCK
    kj = jnp.arange(2 * BLOCK)[:, None]
    rel = qi - kj
    allowed = (rel >= 0) & (rel < WINDOW)
    allowed = jnp.stack([allowed, allowed & (kj >= BLOCK)])
    bias = jnp.tile(jnp.where(allowed, 0.0, NEG_INF).astype(F32), (1, 1, SWA_GROUP))
    nq, width = tq // BLOCK, SWA_GROUP * BLOCK
    cur = pl.BlockSpec((1, 1, tq, D), lambda b, h, n: (b, h, n, 0))
    prev = pl.BlockSpec((1, 1, BLOCK, D), lambda b, h, n: (b, h, jnp.maximum(n * r - 1, 0), 0))
    return pl.pallas_call(
        functools.partial(_swa_body, tq=tq),
        grid=(B, SWA_KV_HEADS, S // tq),
        in_specs=[pl.BlockSpec(memory_space=pltpu.SMEM),
                  pl.BlockSpec((1, SWA_GROUP, tq, D), lambda b, h, n: (b, h, n, 0)),
                  cur, prev, cur, prev,
                  pl.BlockSpec((2, 2 * BLOCK, width), lambda b, h, n: (0, 0, 0))],
        out_specs=pl.BlockSpec((1, SWA_GROUP * D, tq), lambda b, h, n: (b, h, n)),
        out_shape=jax.ShapeDtypeStruct((B, SWA_W, S), BF16),
        scratch_shapes=[pltpu.VMEM((BLOCK + tq, D), BF16), pltpu.VMEM((BLOCK + tq, D), BF16),
                        pltpu.VMEM((nq, 2 * BLOCK, width), F32), pltpu.VMEM((nq, 2 * BLOCK, width), BF16),
                        pltpu.VMEM((nq, 1, width), F32)],
        compiler_params=_params("arbitrary", "arbitrary", "arbitrary"),
        name="swa",
    )(sinks.astype(F32), sq, sk, sk, sv, sv, bias)


def _conv_body(h_ref, dw_ref, db_ref, lg_ref, lb_ref, pw_ref, o_ref, hbuf, hsh, *, ts, rb):
    s = pl.program_id(1)
    sub = 8
    span = ts + CONV_HALO - sub

    @pl.when(s == 0)
    def _():
        hbuf[0:CONV_HALO, :] = jnp.zeros((CONV_HALO, CONV_CH), F32)

    @pl.when(s > 0)
    def _():
        hbuf[0:CONV_HALO, :] = hbuf[ts:ts + CONV_HALO, :]

    hbuf[CONV_HALO:CONV_HALO + ts, :] = h_ref[0]
    for p in range(1, sub):
        hsh[p - 1] = hbuf[p:p + span, :]
    off = CONV_HALO - (CONV_WIDTH - 1)
    for r in range(ts // rb):
        base = r * rb
        acc = jnp.broadcast_to(db_ref[...], (rb, CONV_CH))
        for w in range(CONV_WIDTH):
            q, p = divmod(off + w, sub)
            lo = base + q * sub
            tap = hbuf[lo:lo + rb, :] if p == 0 else hsh[p - 1, lo:lo + rb, :]
            acc = acc + tap * dw_ref[w:w + 1, :]
        mu = jnp.mean(acc, axis=-1, keepdims=True)
        d = acc - mu
        var = jnp.mean(d * d, axis=-1, keepdims=True)
        hn = d * lax.rsqrt(var + EPS) * lg_ref[...] + lb_ref[...]
        o_ref[0, base:base + rb, :] = _dot(_silu(hn).astype(BF16), pw_ref[...]).astype(BF16)


def _conv(ch, dw_w, dw_b, ln_g, ln_b, pw_w, ts, rb):
    B, S, W = ch.shape
    ts = min(ts, S)
    rb = min(rb, ts)
    pw = jnp.zeros((W, W), F32)
    gd = W // CONV_GROUPS
    for g in range(CONV_GROUPS):
        pw = pw.at[g * gd:(g + 1) * gd, g * gd:(g + 1) * gd].set(pw_w[g])
    act = pl.BlockSpec((1, ts, W), lambda b, s: (b, s, 0))

    def const(shape):
        return pl.BlockSpec(shape, lambda b, s: (0,) * len(shape))

    return pl.pallas_call(
        functools.partial(_conv_body, ts=ts, rb=rb),
        grid=(B, S // ts),
        in_specs=[act, const((CONV_WIDTH, W)), const((1, W)), const((1, W)), const((1, W)), const((W, W))],
        out_specs=act,
        out_shape=jax.ShapeDtypeStruct((B, S, W), BF16),
        scratch_shapes=[pltpu.VMEM((CONV_HALO + ts, W), F32), pltpu.VMEM((7, CONV_HALO + ts - 8, W), F32)],
        compiler_params=_params("arbitrary", "arbitrary"),
        name="conv",
    )(ch, dw_w, dw_b.reshape(1, W), ln_g.reshape(1, W), ln_b.reshape(1, W), pw.astype(BF16))


def _mixer_residual(yr_ref, ys_ref, yc_ref, w_ref, x_ref):
    a, b = RET_W, RET_W + SWA_W
    return x_ref[...] + (_dot_tn(yr_ref[0], w_ref[0:a, :]) + _dot_tn(ys_ref[0], w_ref[a:b, :])
                         + _dot(yc_ref[...], w_ref[b:, :]))


def _mixer_specs(tm, tiles_per_seq):
    def chan(width):
        return pl.BlockSpec((1, width, tm), lambda i: (i // tiles_per_seq, 0, i % tiles_per_seq))

    return [chan(RET_W), chan(SWA_W), pl.BlockSpec((tm, CONV_CH), lambda i: (i, 0))]


def _outproj_body(yr_ref, ys_ref, yc_ref, w_ref, x_ref, g_ref, wr_ref, before_ref, xo_ref, hn_ref, rt_ref, rl_ref):
    x = _mixer_residual(yr_ref, ys_ref, yc_ref, w_ref, x_ref)
    xo_ref[...] = x
    h = _rms(x, g_ref[...]).astype(BF16)
    hn_ref[...] = h
    logits = _dot(h, wr_ref[...])
    lane = lax.broadcasted_iota(jnp.int32, logits.shape, 1).astype(F32)
    lg = jnp.where(lane < N_EXPERTS, logits, -jnp.inf)
    m1 = jnp.max(lg, axis=-1, keepdims=True)
    i1 = jnp.min(jnp.where(lg == m1, lane, float(LANES)), axis=-1, keepdims=True)
    lg2 = jnp.where(lane == i1, -jnp.inf, lg)
    m2 = jnp.max(lg2, axis=-1, keepdims=True)
    i2 = jnp.min(jnp.where(lg2 == m2, lane, float(LANES)), axis=-1, keepdims=True)
    e = jnp.exp(m2 - m1)
    w1 = 1.0 / (1.0 + e)
    w2 = e / (1.0 + e)
    hit1, hit2 = lane == i1, lane == i2
    onehot = jnp.where(jnp.logical_or(hit1, hit2), 1.0, 0.0)
    rank = _dot(before_ref[...], onehot.astype(BF16))
    count = jnp.sum(onehot, axis=0, keepdims=True).astype(jnp.int32)
    run_len = (count + (RUN_ALIGN - 1)) & ~(RUN_ALIGN - 1)
    lower = (lax.broadcasted_iota(jnp.int32, (LANES, LANES), 0) < lax.broadcasted_iota(jnp.int32, (LANES, LANES), 1))
    run_rows = jnp.broadcast_to(run_len.astype(F32), (8, LANES)).astype(BF16)
    loc_off = _dot(run_rows, jnp.where(lower, 1.0, 0.0).astype(BF16))[0:1, :]
    pos = rank + loc_off
    p1 = jnp.sum(jnp.where(hit1, pos, 0.0), axis=-1, keepdims=True)
    p2 = jnp.sum(jnp.where(hit2, pos, 0.0), axis=-1, keepdims=True)
    cols = (i1, i2, w1, w2, p1, p2)
    rt = jnp.zeros_like(logits)
    for c, v in enumerate(cols):
        rt = jnp.where(lane == c, v, rt)
    rt_ref[...] = rt
    rl_ref[0] = jnp.broadcast_to(run_len, (8, LANES))


def _outproj_route(yr, ys, yc, w_bf, x, g, w_router):
    T, D = x.shape
    tm = MOE_TB
    S = ys.shape[2]

    def row(width):
        return pl.BlockSpec((tm, width), lambda i: (i, 0))

    def const(shape):
        return pl.BlockSpec(shape, lambda i: (0,) * len(shape))

    wr = jnp.zeros((D, LANES), F32).at[:, :N_EXPERTS].set(w_router).astype(BF16)
    before = (jnp.arange(tm)[:, None] > jnp.arange(tm)[None, :]).astype(BF16)
    return pl.pallas_call(
        _outproj_body,
        grid=(T // tm,),
        in_specs=_mixer_specs(tm, S // tm) + [const((D, D)), row(D), const((1, D)), const((D, LANES)),
                                              const((tm, tm))],
        out_specs=[row(D), row(D), row(LANES), pl.BlockSpec((1, 8, LANES), lambda i: (i, 0, 0))],
        out_shape=[jax.ShapeDtypeStruct((T, D), F32), jax.ShapeDtypeStruct((T, D), BF16),
                   jax.ShapeDtypeStruct((T, LANES), F32), jax.ShapeDtypeStruct((T // tm, 8, LANES), jnp.int32)],
        compiler_params=_params("arbitrary"),
        name="outproj_route",
    )(yr, ys, yc, w_bf, x, g.reshape(1, D), wr, before)


def _dense_ffn_body(yr_ref, ys_ref, yc_ref, wo_ref, x_ref, g_ref, wg_ref, wu_ref, wd_ref, o_ref, *, chunks):
    x = _mixer_residual(yr_ref, ys_ref, yc_ref, wo_ref, x_ref)
    h = _rms(x, g_ref[...]).astype(BF16)
    acc = x
    for a, b in chunks:
        g = _dot(h, wg_ref[:, a:b])
        u = _dot(h, wu_ref[:, a:b])
        acc = acc + _dot((_silu(g) * u).astype(BF16), wd_ref[a:b, :])
    o_ref[...] = acc


def _outproj_dense_ffn(yr, ys, yc, wo_bf, x, g, wg, wu, wd, tm, fc):
    T, D = x.shape
    FF = wg.shape[1]
    S = ys.shape[2]
    tm = min(tm, S)
    chunks = tuple((a, min(a + fc, FF)) for a in range(0, FF, fc))

    def row(width):
        return pl.BlockSpec((tm, width), lambda i: (i, 0))

    def const(shape):
        return pl.BlockSpec(shape, lambda i: (0,) * len(shape), pipeline_mode=pl.Buffered(1))

    return pl.pallas_call(
        functools.partial(_dense_ffn_body, chunks=chunks),
        grid=(T // tm,),
        in_specs=_mixer_specs(tm, S // tm) + [const((D, D)), row(D), const((1, D)),
                                              const((D, FF)), const((D, FF)), const((FF, D))],
        out_specs=row(D),
        out_shape=jax.ShapeDtypeStruct((T, D), F32),
        compiler_params=_params("arbitrary"),
        name="outproj_dense_ffn",
    )(yr, ys, yc, wo_bf, x, g.reshape(1, D), wg.astype(BF16), wu.astype(BF16), wd.astype(BF16))


MOE_TB = 512
MOE_TM = 512
RUN_ALIGN = 8
RUN_SIZES = (512, 256, 128, 64, 32, 16, 8)
BLOCK_ROWS = TOP_K * MOE_TB + N_EXPERTS * RUN_ALIGN
BLOCK_LANES = 1152
MOE_FC = 512
MOE_LOAD_SLOTS = 12


def _route_meta(rt, rl, n_tiles):
    T = rt.shape[0]
    nb = T // MOE_TB
    pos = rt[:, 4:4 + TOP_K].astype(jnp.int32)
    run_len = rl[:, 0, :N_EXPERTS]
    loc_off = jnp.cumsum(run_len, axis=1) - run_len
    group = jnp.sum(run_len, axis=0)
    ptiles = (group + MOE_TM - 1) // MOE_TM
    tile_end = jnp.cumsum(ptiles)
    gstart = (tile_end - ptiles) * MOE_TM
    run_start = gstart[None, :] + jnp.cumsum(run_len, axis=0) - run_len
    tile_expert = jnp.sum(jnp.arange(n_tiles, dtype=jnp.int32)[:, None] >= tile_end[None, :], axis=1)
    i32 = lambda a: a.astype(jnp.int32)
    pos3 = pos.reshape(nb, MOE_TB, TOP_K)
    return dict(
        pos_l=i32(jnp.swapaxes(pos3, 1, 2)),
        pos_c=i32(pos3.reshape(T, TOP_K)),
        loc_off=i32(loc_off.reshape(-1)), run_start=i32(run_start.reshape(-1)), run_len=i32(run_len.reshape(-1)),
        pad_start=i32(gstart + group), pad_len=i32(ptiles * MOE_TM - group),
        tile_expert=i32(jnp.minimum(tile_expert, N_EXPERTS - 1)), n_used=i32(tile_end[-1:]))


def _run_dma(src, dst, src_off, dst_off, length, sem, wait):
    off = 0
    for k in RUN_SIZES:
        part = length & k

        @pl.when(part != 0)
        def _(off=off, k=k):
            cp = pltpu.make_async_copy(src.at[pl.ds(pl.multiple_of(src_off + off, RUN_ALIGN), k)],
                                       dst.at[pl.ds(pl.multiple_of(dst_off + off, RUN_ALIGN), k)], sem)
            if wait:
                cp.wait()
            else:
                cp.start()

        off = off + part


def _dispatch_body(lo_ref, rs_ref, rl_ref, ps_ref, pn_ref, nu_ref, pos_ref, h_ref, xs_hbm, sbuf, zbuf, sem):
    b, nb = pl.program_id(0), pl.num_programs(0)
    slot = b % 2
    n_tiles = xs_hbm.shape[0] // MOE_TM

    def zero_tile(i, wait):
        cp = pltpu.make_async_copy(zbuf, xs_hbm.at[pl.ds(pl.multiple_of(i * MOE_TM, MOE_TM), MOE_TM)], sem.at[2])
        if wait:
            cp.wait()
        else:
            cp.start()

    def runs(blk, s, wait):
        for e in range(N_EXPERTS):
            j = blk * N_EXPERTS + e
            _run_dma(sbuf.at[s], xs_hbm, lo_ref[j], rs_ref[j], rl_ref[j], sem.at[s], wait)

    @pl.when(b >= 2)
    def _():
        runs(b - 2, slot, True)

    pos = pos_ref[0]
    r = lax.broadcasted_iota(jnp.int32, (BLOCK_ROWS, MOE_TB), 0)
    onehot = jnp.logical_or(r == pos[0:1, :], r == pos[1:2, :])
    sbuf[slot] = _dot(jnp.where(onehot, 1.0, 0.0).astype(BF16), h_ref[...])
    runs(b, slot, False)

    @pl.when(b == nb - 1)
    def _():
        zbuf[...] = jnp.zeros_like(zbuf)
        for e in range(N_EXPERTS):
            _run_dma(zbuf, xs_hbm, 0, ps_ref[e], pn_ref[e], sem.at[2], False)
        lax.fori_loop(nu_ref[0], n_tiles, lambda i, c: zero_tile(i, False), None)
        for e in range(N_EXPERTS):
            _run_dma(zbuf, xs_hbm, 0, ps_ref[e], pn_ref[e], sem.at[2], True)
        lax.fori_loop(nu_ref[0], n_tiles, lambda i, c: zero_tile(i, True), None)

        @pl.when(b >= 1)
        def _():
            runs(b - 1, 1 - slot, True)

        runs(b, slot, True)


def _dispatch(hn, meta, n_rows):
    T, D = hn.shape
    nb = T // MOE_TB
    grid_spec = pltpu.PrefetchScalarGridSpec(
        num_scalar_prefetch=6,
        grid=(nb,),
        in_specs=[pl.BlockSpec((1, TOP_K, MOE_TB), lambda b, *_: (b, 0, 0)),
                  pl.BlockSpec((MOE_TB, D), lambda b, *_: (b, 0))],
        out_specs=pl.BlockSpec(memory_space=pl.ANY),
        scratch_shapes=[pltpu.VMEM((2, BLOCK_ROWS, D), F32), pltpu.VMEM((MOE_TM, D), F32),
                        pltpu.SemaphoreType.DMA((3,))],
    )
    return pl.pallas_call(
        _dispatch_body,
        grid_spec=grid_spec,
        out_shape=jax.ShapeDtypeStruct((n_rows, D), F32),
        compiler_params=pltpu.CompilerParams(dimension_semantics=("arbitrary",), vmem_limit_bytes=VMEM_LIMIT,
                                             has_side_effects=True),
        name="moe_dispatch",
    )(meta["loc_off"], meta["run_start"], meta["run_len"], meta["pad_start"], meta["pad_len"], meta["n_used"],
      meta["pos_l"], hn)


def _moe_ffn_body(te_ref, nu_ref, x_ref, wg_hbm, wu_hbm, wd_hbm, o_ref, wg_s, wu_s, wd_s, stage, sem):
    i = pl.program_id(0)
    e = te_ref[i]
    active = i < nu_ref[0]
    first = jnp.logical_or(i == 0, e != te_ref[jnp.maximum(i - 1, 0)])
    D, FF = wg_s.shape
    fsl = [slice(f * MOE_FC, (f + 1) * MOE_FC) for f in range(FF // MOE_FC)]
    slots, piece = stage.shape[0], stage.shape[1]
    psl = [slice(r, r + piece) for r in range(0, D, piece)]
    jobs = []
    for f in fsl:
        jobs += [(wg_hbm, wg_s, p, f) for p in psl] + [(wu_hbm, wu_s, p, f) for p in psl]
        jobs += [(wd_hbm, wd_s, f, p) for p in psl]
    per_chunk = len(jobs) // len(fsl)

    def copy(j):
        w_hbm, _, rows, cols = jobs[j]
        return pltpu.make_async_copy(w_hbm.at[e, rows, cols], stage.at[j % slots], sem.at[j % slots])

    def ffn(load):
        x = x_ref[...].astype(BF16)
        acc = jnp.zeros(o_ref.shape, F32)
        if load:
            for j in range(slots - 1):
                copy(j).start()
        for c, sl in enumerate(fsl):
            if load:
                for j in range(c * per_chunk, (c + 1) * per_chunk):
                    if j + slots - 1 < len(jobs):
                        copy(j + slots - 1).start()
                    copy(j).wait()
                    _, w_s, rows, cols = jobs[j]
                    w_s[rows, cols] = stage[j % slots].astype(BF16)
            g = _dot(x, wg_s[:, sl])
            u = _dot(x, wu_s[:, sl])
            acc = acc + _dot((_silu(g) * u).astype(BF16), wd_s[sl, :])
        o_ref[...] = acc

    @pl.when(jnp.logical_and(active, first))
    def _():
        ffn(load=True)

    @pl.when(jnp.logical_and(active, jnp.logical_not(first)))
    def _():
        ffn(load=False)

    @pl.when(jnp.logical_not(active))
    def _():
        o_ref[...] = jnp.zeros_like(o_ref)


def _moe_ffn(xs, meta, wg, wu, wd):
    R, D = xs.shape
    E, _, FF = wg.shape
    grid_spec = pltpu.PrefetchScalarGridSpec(
        num_scalar_prefetch=2,
        grid=(R // MOE_TM,),
        in_specs=[pl.BlockSpec((MOE_TM, D), lambda i, te, nu: (jnp.minimum(i, nu[0] - 1), 0)),
                  pl.BlockSpec(memory_space=pl.ANY), pl.BlockSpec(memory_space=pl.ANY),
                  pl.BlockSpec(memory_space=pl.ANY)],
        out_specs=pl.BlockSpec((MOE_TM, D), lambda i, te, nu: (i, 0)),
        scratch_shapes=[pltpu.VMEM((D, FF), BF16), pltpu.VMEM((D, FF), BF16), pltpu.VMEM((FF, D), BF16),
                        pltpu.VMEM((MOE_LOAD_SLOTS, MOE_FC, MOE_FC), F32),
                        pltpu.SemaphoreType.DMA((MOE_LOAD_SLOTS,))],
    )
    return pl.pallas_call(
        _moe_ffn_body,
        grid_spec=grid_spec,
        out_shape=jax.ShapeDtypeStruct((R, D), F32),
        compiler_params=_params("arbitrary"),
        name="moe_ffn",
    )(meta["tile_expert"], meta["n_used"], xs, wg, wu, wd)


def _combine_body(lo_ref, rs_ref, rl_ref, ys_hbm, x_ref, rt_ref, pc_ref, g_ref, o_ref, ybuf, sem):
    b, nb = pl.program_id(0), pl.num_programs(0)
    slot = b % 2

    def runs(blk, s, wait):
        for e in range(N_EXPERTS):
            j = blk * N_EXPERTS + e
            _run_dma(ys_hbm, ybuf.at[s], rs_ref[j], lo_ref[j], rl_ref[j], sem.at[s], wait)

    @pl.when(b == 0)
    def _():
        ybuf[...] = jnp.zeros_like(ybuf)
        runs(0, 0, False)

    @pl.when(b + 1 < nb)
    def _():
        runs(b + 1, 1 - slot, False)

    runs(b, slot, True)
    y = ybuf[slot].astype(BF16)
    pc, rt = pc_ref[...], rt_ref[...]
    lane = lax.broadcasted_iota(jnp.int32, (MOE_TB, BLOCK_LANES), 1)
    q = jnp.where(lane == pc[:, 0:1], rt[:, 2:3], jnp.where(lane == pc[:, 1:2], rt[:, 3:4], 0.0)).astype(BF16)
    o_ref[...] = _rms(x_ref[...] + _dot(q, y), g_ref[...])


def _combine(ys, meta, x, rt, g):
    T, D = x.shape
    grid_spec = pltpu.PrefetchScalarGridSpec(
        num_scalar_prefetch=3,
        grid=(T // MOE_TB,),
        in_specs=[pl.BlockSpec(memory_space=pl.ANY),
                  pl.BlockSpec((MOE_TB, D), lambda b, *_: (b, 0)),
                  pl.BlockSpec((MOE_TB, LANES), lambda b, *_: (b, 0)),
                  pl.BlockSpec((MOE_TB, TOP_K), lambda b, *_: (b, 0)),
                  pl.BlockSpec((1, D), lambda b, *_: (0, 0))],
        out_specs=pl.BlockSpec((MOE_TB, D), lambda b, *_: (b, 0)),
        scratch_shapes=[pltpu.VMEM((2, BLOCK_LANES, D), F32), pltpu.SemaphoreType.DMA((2,))],
    )
    return pl.pallas_call(
        _combine_body,
        grid_spec=grid_spec,
        out_shape=jax.ShapeDtypeStruct((T, D), F32),
        compiler_params=_params("arbitrary"),
        name="moe_combine",
    )(meta["loc_off"], meta["run_start"], meta["run_len"], ys, x, rt, meta["pos_c"], g.reshape(1, D))


def kernel(x, norm_mix_g, w_in, ret_gn_g, attn_sinks, conv_dw_w, conv_dw_b, conv_ln_g, conv_ln_b,
           conv_pw_w, w_out, norm_ffn_g, ffn_w_gate, ffn_w_up, ffn_w_down, moe_router, moe_w_gate,
           moe_w_up, moe_w_down, final_norm_g):
    B, S, D = x.shape
    T = B * S
    depth = w_in.shape[0]
    assert depth == 2 and ffn_w_gate.shape[0] == 1 and moe_router.shape[0] == 1, "dense layer then MoE layer"
    assert T % MOE_TB == 0 and S % MOE_TB == 0
    assert moe_w_gate.shape[-1] % MOE_FC == 0 and D % MOE_FC == 0

    for l in range(depth):
        rq, rk, rv, rg, sq, sk, sv, ch = _inproj(x, norm_mix_g[l], w_in[l].astype(BF16), tm=INPROJ_TM)
        y_ret_t = _retention(rq, rk, rv, rg, ret_gn_g[l], ts=RET_TS)
        y_swa_t = _swa(sq, sk, sv, attn_sinks[l], tq=SWA_TQ)
        y_conv = _conv(ch, conv_dw_w[l], conv_dw_b[l], conv_ln_g[l], conv_ln_b[l], conv_pw_w[l],
                       ts=CONV_TS, rb=CONV_RB)
        flat = lambda a: a.reshape(T, a.shape[-1])
        j = l // 2
        if l % 2 == 0:
            x = _outproj_dense_ffn(y_ret_t, y_swa_t, flat(y_conv), w_out[l].astype(BF16), flat(x),
                                   norm_ffn_g[l], ffn_w_gate[j], ffn_w_up[j], ffn_w_down[j],
                                   tm=FFN_TM, fc=FFN_FC).reshape(B, S, D)
        else:
            x2, hn, rt, rl = _outproj_route(y_ret_t, y_swa_t, flat(y_conv), w_out[l].astype(BF16), flat(x),
                                            norm_ffn_g[l], moe_router[j])
            n_rows = TOP_K * T + (T // MOE_TB) * N_EXPERTS * RUN_ALIGN + N_EXPERTS * MOE_TM
            meta = _route_meta(rt, rl, n_rows // MOE_TM)
            xs = _dispatch(hn, meta, n_rows)
            ys = _moe_ffn(xs, meta, moe_w_gate[j], moe_w_up[j], moe_w_down[j])
            x = _combine(ys, meta, x2, rt, final_norm_g).reshape(B, S, D)
    return x
```

```python
import functools

import jax
import jax.numpy as jnp
from jax import lax
from jax.experimental import pallas as pl
from jax.experimental.pallas import tpu as pltpu

F32 = jnp.float32
BF16 = jnp.bfloat16

HEAD_DIM = 64
RET_HEADS = 4
RET_W = RET_HEADS * HEAD_DIM
SWA_HEADS = 8
SWA_KV_HEADS = 2
SWA_GROUP = SWA_HEADS // SWA_KV_HEADS
SWA_W = SWA_HEADS * HEAD_DIM
SWA_KV_W = SWA_KV_HEADS * HEAD_DIM
CONV_CH = 256
CONV_GROUPS = 4
CONV_WIDTH = 31
WINDOW = 128
BLOCK = 128
RET_CHUNK = 128
ROPE_THETA = 500000.0
ROPE_DIM = HEAD_DIM // 4
RET_ROPE_THETA = 10000.0
N_EXPERTS = 8
TOP_K = 2
EPS = 1e-6
NEG_INF = -1e30

LANES = 128
CONV_HALO = 32
VMEM_LIMIT = 56 * 1024 * 1024

O_RQ, O_RK, O_RV, O_RG = 0, RET_W, 2 * RET_W, 3 * RET_W
O_SQ = 4 * RET_W
O_SK = O_SQ + SWA_W
O_SV = O_SK + SWA_KV_W
O_CA = O_SV + SWA_KV_W
O_CG = O_CA + CONV_CH
D_IN = O_CG + CONV_CH


INPROJ_TM = 1024
RET_TS = 4096
SWA_TQ = 4096
CONV_TS = 1024
CONV_RB = 256
FFN_TM = 1024
FFN_FC = 512


def _params(*sem):
    return pltpu.CompilerParams(dimension_semantics=sem, vmem_limit_bytes=VMEM_LIMIT)


def _rms(x, g):
    return x * lax.rsqrt(jnp.mean(x * x, axis=-1, keepdims=True) + EPS) * g


def _silu(x):
    return x * jax.nn.sigmoid(x)


def _dot(a, b):
    return jnp.dot(a, b, preferred_element_type=F32)


def _dot_nt(a, b):
    return lax.dot_general(a, b, (((1,), (1,)), ((), ())), preferred_element_type=F32)


def _dot_tn(a, b):
    return lax.dot_general(a, b, (((0,), (0,)), ((), ())), preferred_element_type=F32)


def _rope_tables(seq, theta, rot_dim):
    half = rot_dim // 2
    inv = 1.0 / (theta ** (jnp.arange(half, dtype=F32) / half))
    ang = jnp.arange(seq, dtype=F32)[:, None] * inv[None, :]
    cos, sin = jnp.cos(ang), jnp.sin(ang)
    rest = HEAD_DIM - rot_dim
    c = jnp.concatenate([cos, cos, jnp.ones((seq, rest), F32)], axis=1)
    s = jnp.concatenate([-sin, sin, jnp.zeros((seq, rest), F32)], axis=1)
    reps = LANES // HEAD_DIM
    return jnp.tile(c, (1, reps)), jnp.tile(s, (1, reps))


def _inproj_body(x_ref, g_ref, w_ref, wgt_ref, rc_ref, rs_ref, sc_ref, ss_ref,
                 rq_ref, rk_ref, rv_ref, rg_ref, sq_ref, sk_ref, sv_ref, ch_ref, zbuf):
    h = _rms(x_ref[0], g_ref[...]).astype(BF16)
    lane = lax.broadcasted_iota(jnp.int32, (1, LANES), 1) % HEAD_DIM
    for a, b in ((O_RQ, O_RK), (O_RK, O_RV), (O_RV, O_RG), (O_SQ, O_SK), (O_SK, O_CA), (O_CA, O_CG), (O_CG, D_IN)):
        zbuf[:, a:b] = _dot(h, w_ref[:, a:b])

    def seg(a, b):
        return zbuf[:, a:b]

    def rope(v, c, s, half):
        up = pltpu.roll(v, LANES - half, axis=1)
        dn = pltpu.roll(v, half, axis=1)
        return v * c + jnp.where(lane < half, up, dn) * s

    def store_heads(z, o_ref, first=0):
        for j in range(z.shape[1] // HEAD_DIM):
            o_ref[0, first + j] = z[:, j * HEAD_DIM:(j + 1) * HEAD_DIM]

    def rope_store(z, o_ref, c, s, half, scale):
        for i in range(z.shape[1] // LANES):
            r = rope(z[:, i * LANES:(i + 1) * LANES], c, s, half)
            if scale != 1.0:
                r = r * scale
            store_heads(r.astype(BF16), o_ref, first=i * (LANES // HEAD_DIM))

    rc, rs = rc_ref[...], rs_ref[...]
    sc, ss = sc_ref[...], ss_ref[...]
    scale = HEAD_DIM ** -0.5
    rope_store(seg(O_RQ, O_RK), rq_ref, rc, rs, HEAD_DIM // 2, 1.0)
    rope_store(seg(O_RK, O_RV), rk_ref, rc, rs, HEAD_DIM // 2, scale)
    store_heads(seg(O_RV, O_RG).astype(BF16), rv_ref)
    rg_ref[0] = _dot_nt(wgt_ref[...], h)
    rope_store(seg(O_SQ, O_SK), sq_ref, sc, ss, ROPE_DIM // 2, scale)
    rope_store(seg(O_SK, O_SV), sk_ref, sc, ss, ROPE_DIM // 2, 1.0)
    store_heads(seg(O_SV, O_CA).astype(BF16), sv_ref)
    ch_ref[0] = seg(O_CA, O_CG) * jax.nn.sigmoid(seg(O_CG, D_IN))


def _inproj(x, g, w_bf, tm):
    B, S, D = x.shape
    tm = min(tm, S)
    rc, rs = _rope_tables(S, RET_ROPE_THETA, HEAD_DIM)
    sc, ss = _rope_tables(S, ROPE_THETA, ROPE_DIM)
    tab = pl.BlockSpec((tm, LANES), lambda s, b: (s, 0))

    def out(width, dtype):
        return (jax.ShapeDtypeStruct((B, S, width), dtype),
                pl.BlockSpec((1, tm, width), lambda s, b: (b, s, 0)))

    def out_heads(heads):
        return (jax.ShapeDtypeStruct((B, heads, S, HEAD_DIM), BF16),
                pl.BlockSpec((1, heads, tm, HEAD_DIM), lambda s, b: (b, 0, s, 0)))

    gate_t = (jax.ShapeDtypeStruct((B, RET_W, S), F32), pl.BlockSpec((1, RET_W, tm), lambda s, b: (b, 0, s)))
    outs = [out_heads(RET_HEADS), out_heads(RET_HEADS), out_heads(RET_HEADS), gate_t,
            out_heads(SWA_HEADS), out_heads(SWA_KV_HEADS), out_heads(SWA_KV_HEADS), out(CONV_CH, F32)]
    return pl.pallas_call(
        _inproj_body,
        grid=(S // tm, B),
        in_specs=[pl.BlockSpec((1, tm, D), lambda s, b: (b, s, 0)),
                  pl.BlockSpec((1, D), lambda s, b: (0, 0)),
                  pl.BlockSpec((D, D_IN), lambda s, b: (0, 0)),
                  pl.BlockSpec((RET_W, D), lambda s, b: (0, 0)),
                  tab, tab, tab, tab],
        out_specs=[o[1] for o in outs],
        out_shape=[o[0] for o in outs],
        scratch_shapes=[pltpu.VMEM((tm, D_IN), F32)],
        compiler_params=_params("arbitrary", "arbitrary"),
        name="inproj",
    )(x, g.reshape(1, D), w_bf, w_bf[:, O_RG:O_SQ].T, rc, rs, sc, ss)


def _ret_body(q_ref, k_ref, v_ref, g_ref, dm_ref, kd_ref, qd_ref, cd_ref, gn_ref, o_ref, st_ref, *, ts):
    @pl.when(pl.program_id(2) == 0)
    def _():
        st_ref[...] = jnp.zeros_like(st_ref)

    C = RET_CHUNK
    for c in range(ts // C):
        rows = slice(c * C, (c + 1) * C)
        q, k, v = q_ref[0, 0, rows, :], k_ref[0, 0, rows, :], v_ref[0, 0, rows, :]
        st = st_ref[...]
        scores = _dot_nt(k, q) * dm_ref[0]
        intra = _dot_tn(v, scores.astype(BF16))
        cross = _dot_nt(st.astype(BF16), q) * qd_ref[0]
        kdec = (k.astype(F32) * kd_ref[0]).astype(BF16)
        st_ref[...] = st * cd_ref[0] + _dot_tn(v, kdec)
        o = intra + cross
        mu = jnp.mean(o, axis=0, keepdims=True)
        d = o - mu
        var = jnp.mean(d * d, axis=0, keepdims=True)
        on = d * lax.rsqrt(var + EPS) * gn_ref[0]
        o_ref[0, :, rows] = (_silu(g_ref[0, :, rows]) * on).astype(BF16)


def _retention(rq, rk, rv, rg_t, gn_g, ts):
    B, H, S, D = rq.shape
    ts = min(ts, S)
    C, W = RET_CHUNK, RET_W
    lg = jnp.log(1.0 - 2.0 ** (-5.0 - jnp.arange(H, dtype=F32)))
    idx = jnp.arange(C)
    rel = idx[:, None] - idx[None, :]
    dmask = jnp.where(rel[None] >= 0,
                      jnp.exp(jnp.maximum(rel, 0)[None].astype(F32) * lg[:, None, None]), 0.0)
    k_decay = jnp.exp((C - 1 - idx)[:, None].astype(F32) * lg[None, :])
    q_decay = jnp.exp((idx + 1)[:, None].astype(F32) * lg[None, :])
    chunk_decay = jnp.exp(C * lg)
    dm_t = jnp.swapaxes(dmask, 1, 2)
    kd = jnp.broadcast_to(k_decay.T[:, :, None], (H, C, D))
    qd = jnp.broadcast_to(q_decay.T[:, None, :], (H, D, C))
    cd = jnp.broadcast_to(chunk_decay[:, None, None], (H, D, D))
    gn = jnp.broadcast_to(gn_g.reshape(H, D, 1), (H, D, C))
    heads = pl.BlockSpec((1, 1, ts, D), lambda b, h, n: (b, h, n, 0))
    chan = pl.BlockSpec((1, D, ts), lambda b, h, n: (b, h, n))

    def per_head(r, c):
        return pl.BlockSpec((1, r, c), lambda b, h, n: (h, 0, 0))

    return pl.pallas_call(
        functools.partial(_ret_body, ts=ts),
        grid=(B, H, S // ts),
        in_specs=[heads, heads, heads, chan, per_head(C, C), per_head(C, D), per_head(D, C),
                  per_head(D, D), per_head(D, C)],
        out_specs=chan,
        out_shape=jax.ShapeDtypeStruct((B, W, S), BF16),
        scratch_shapes=[pltpu.VMEM((D, D), F32)],
        compiler_params=_params("arbitrary", "arbitrary", "arbitrary"),
        name="retention",
    )(rq, rk, rv, rg_t, dm_t, kd, qd, cd, gn)


def _swa_body(sink_ref, q_ref, kc_ref, kp_ref, vc_ref, vp_ref, bias_ref, o_ref, kcat, vcat, sbuf, pbuf, rbuf,
              *, tq):
    hk, n = pl.program_id(1), pl.program_id(2)
    kcat[0:BLOCK], kcat[BLOCK:] = kp_ref[0, 0], kc_ref[0, 0]
    vcat[0:BLOCK], vcat[BLOCK:] = vp_ref[0, 0], vc_ref[0, 0]
    width = SWA_GROUP * BLOCK
    group = lax.broadcasted_iota(jnp.int32, (1, width), 1) // BLOCK
    sink = jnp.zeros((1, width), F32)
    for g in range(SWA_GROUP):
        sink = jnp.where(group == g, sink_ref[hk * SWA_GROUP + g], sink)
    nq = tq // BLOCK
    for j in range(nq):
        q = q_ref[0, :, j * BLOCK:(j + 1) * BLOCK, :].reshape(width, HEAD_DIM)
        kb = kcat[j * BLOCK:(j + 2) * BLOCK]
        sbuf[j] = _dot_nt(kb, q)
    for j in range(nq):
        s = sbuf[j] + (bias_ref[0] if j > 0 else bias_ref[jnp.where(n == 0, 1, 0)])
        m = jnp.maximum(jnp.max(s, axis=0, keepdims=True), sink)
        p = jnp.exp(s - m)
        rbuf[j] = 1.0 / (jnp.sum(p, axis=0, keepdims=True) + jnp.exp(sink - m))
        pbuf[j] = p.astype(BF16)
    for j in range(nq):
        vb = vcat[j * BLOCK:(j + 2) * BLOCK]
        o = _dot_tn(vb, pbuf[j]) * rbuf[j]
        for g in range(SWA_GROUP):
            o_ref[0, g * HEAD_DIM:(g + 1) * HEAD_DIM, j * BLOCK:(j + 1) * BLOCK] = (
                o[:, g * BLOCK:(g + 1) * BLOCK].astype(BF16))


def _swa(sq, sk, sv, sinks, tq):
    B, _, S, D = sq.shape
    tq = min(tq, S)
    r = tq // BLOCK
    qi = jnp.arange(BLOCK)[None, :] + BLOCK
    kj = jnp.arange(2 * BLOCK)[:, None]
    rel = qi - kj
    allowed = (rel >= 0) & (rel < WINDOW)
    allowed = jnp.stack([allowed, allowed & (kj >= BLOCK)])
    bias = jnp.tile(jnp.where(allowed, 0.0, NEG_INF).astype(F32), (1, 1, SWA_GROUP))
    nq, width = tq // BLOCK, SWA_GROUP * BLOCK
    cur = pl.BlockSpec((1, 1, tq, D), lambda b, h, n: (b, h, n, 0))
    prev = pl.BlockSpec((1, 1, BLOCK, D), lambda b, h, n: (b, h, jnp.maximum(n * r - 1, 0), 0))
    return pl.pallas_call(
        functools.partial(_swa_body, tq=tq),
        grid=(B, SWA_KV_HEADS, S // tq),
        in_specs=[pl.BlockSpec(memory_space=pltpu.SMEM),
                  pl.BlockSpec((1, SWA_GROUP, tq, D), lambda b, h, n: (b, h, n, 0)),
                  cur, prev, cur, prev,
                  pl.BlockSpec((2, 2 * BLOCK, width), lambda b, h, n: (0, 0, 0))],
        out_specs=pl.BlockSpec((1, SWA_GROUP * D, tq), lambda b, h, n: (b, h, n)),
        out_shape=jax.ShapeDtypeStruct((B, SWA_W, S), BF16),
        scratch_shapes=[pltpu.VMEM((BLOCK + tq, D), BF16), pltpu.VMEM((BLOCK + tq, D), BF16),
                        pltpu.VMEM((nq, 2 * BLOCK, width), F32), pltpu.VMEM((nq, 2 * BLOCK, width), BF16),
                        pltpu.VMEM((nq, 1, width), F32)],
        compiler_params=_params("arbitrary", "arbitrary", "arbitrary"),
        name="swa",
    )(sinks.astype(F32), sq, sk, sk, sv, sv, bias)


def _conv_body(h_ref, dw_ref, db_ref, lg_ref, lb_ref, pw_ref, o_ref, hbuf, hsh, *, ts, rb):
    s = pl.program_id(1)
    sub = 8
    span = ts + CONV_HALO - sub

    @pl.when(s == 0)
    def _():
        hbuf[0:CONV_HALO, :] = jnp.zeros((CONV_HALO, CONV_CH), F32)

    @pl.when(s > 0)
    def _():
        hbuf[0:CONV_HALO, :] = hbuf[ts:ts + CONV_HALO, :]

    hbuf[CONV_HALO:CONV_HALO + ts, :] = h_ref[0]
    for p in range(1, sub):
        hsh[p - 1] = hbuf[p:p + span, :]
    off = CONV_HALO - (CONV_WIDTH - 1)
    for r in range(ts // rb):
        base = r * rb
        acc = jnp.broadcast_to(db_ref[...], (rb, CONV_CH))
        for w in range(CONV_WIDTH):
            q, p = divmod(off + w, sub)
            lo = base + q * sub
            tap = hbuf[lo:lo + rb, :] if p == 0 else hsh[p - 1, lo:lo + rb, :]
            acc = acc + tap * dw_ref[w:w + 1, :]
        mu = jnp.mean(acc, axis=-1, keepdims=True)
        d = acc - mu
        var = jnp.mean(d * d, axis=-1, keepdims=True)
        hn = d * lax.rsqrt(var + EPS) * lg_ref[...] + lb_ref[...]
        o_ref[0, base:base + rb, :] = _dot(_silu(hn).astype(BF16), pw_ref[...]).astype(BF16)


def _conv(ch, dw_w, dw_b, ln_g, ln_b, pw_w, ts, rb):
    B, S, W = ch.shape
    ts = min(ts, S)
    rb = min(rb, ts)
    pw = jnp.zeros((W, W), F32)
    gd = W // CONV_GROUPS
    for g in range(CONV_GROUPS):
        pw = pw.at[g * gd:(g + 1) * gd, g * gd:(g + 1) * gd].set(pw_w[g])
    act = pl.BlockSpec((1, ts, W), lambda b, s: (b, s, 0))

    def const(shape):
        return pl.BlockSpec(shape, lambda b, s: (0,) * len(shape))

    return pl.pallas_call(
        functools.partial(_conv_body, ts=ts, rb=rb),
        grid=(B, S // ts),
        in_specs=[act, const((CONV_WIDTH, W)), const((1, W)), const((1, W)), const((1, W)), const((W, W))],
        out_specs=act,
        out_shape=jax.ShapeDtypeStruct((B, S, W), BF16),
        scratch_shapes=[pltpu.VMEM((CONV_HALO + ts, W), F32), pltpu.VMEM((7, CONV_HALO + ts - 8, W), F32)],
        compiler_params=_params("arbitrary", "arbitrary"),
        name="conv",
    )(ch, dw_w, dw_b.reshape(1, W), ln_g.reshape(1, W), ln_b.reshape(1, W), pw.astype(BF16))


def _mixer_residual(yr_ref, ys_ref, yc_ref, w_ref, x_ref):
    a, b = RET_W, RET_W + SWA_W
    return x_ref[...] + (_dot_tn(yr_ref[0], w_ref[0:a, :]) + _dot_tn(ys_ref[0], w_ref[a:b, :])
                         + _dot(yc_ref[...], w_ref[b:, :]))


def _mixer_specs(tm, tiles_per_seq):
    def chan(width):
        return pl.BlockSpec((1, width, tm), lambda i: (i // tiles_per_seq, 0, i % tiles_per_seq))

    return [chan(RET_W), chan(SWA_W), pl.BlockSpec((tm, CONV_CH), lambda i: (i, 0))]


def _outproj_body(yr_ref, ys_ref, yc_ref, w_ref, x_ref, g_ref, wr_ref, before_ref, xo_ref, hn_ref, rt_ref, rl_ref):
    x = _mixer_residual(yr_ref, ys_ref, yc_ref, w_ref, x_ref)
    xo_ref[...] = x
    h = _rms(x, g_ref[...]).astype(BF16)
    hn_ref[...] = h
    logits = _dot(h, wr_ref[...])
    lane = lax.broadcasted_iota(jnp.int32, logits.shape, 1).astype(F32)
    lg = jnp.where(lane < N_EXPERTS, logits, -jnp.inf)
    m1 = jnp.max(lg, axis=-1, keepdims=True)
    i1 = jnp.min(jnp.where(lg == m1, lane, float(LANES)), axis=-1, keepdims=True)
    lg2 = jnp.where(lane == i1, -jnp.inf, lg)
    m2 = jnp.max(lg2, axis=-1, keepdims=True)
    i2 = jnp.min(jnp.where(lg2 == m2, lane, float(LANES)), axis=-1, keepdims=True)
    e = jnp.exp(m2 - m1)
    w1 = 1.0 / (1.0 + e)
    w2 = e / (1.0 + e)
    hit1, hit2 = lane == i1, lane == i2
    onehot = jnp.where(jnp.logical_or(hit1, hit2), 1.0, 0.0)
    rank = _dot(before_ref[...], onehot.astype(BF16))
    count = jnp.sum(onehot, axis=0, keepdims=True).astype(jnp.int32)
    run_len = (count + (RUN_ALIGN - 1)) & ~(RUN_ALIGN - 1)
    lower = (lax.broadcasted_iota(jnp.int32, (LANES, LANES), 0) < lax.broadcasted_iota(jnp.int32, (LANES, LANES), 1))
    run_rows = jnp.broadcast_to(run_len.astype(F32), (8, LANES)).astype(BF16)
    loc_off = _dot(run_rows, jnp.where(lower, 1.0, 0.0).astype(BF16))[0:1, :]
    pos = rank + loc_off
    p1 = jnp.sum(jnp.where(hit1, pos, 0.0), axis=-1, keepdims=True)
    p2 = jnp.sum(jnp.where(hit2, pos, 0.0), axis=-1, keepdims=True)
    cols = (i1, i2, w1, w2, p1, p2)
    rt = jnp.zeros_like(logits)
    for c, v in enumerate(cols):
        rt = jnp.where(lane == c, v, rt)
    rt_ref[...] = rt
    rl_ref[0] = jnp.broadcast_to(run_len, (8, LANES))


def _outproj_route(yr, ys, yc, w_bf, x, g, w_router):
    T, D = x.shape
    tm = MOE_TB
    S = ys.shape[2]

    def row(width):
        return pl.BlockSpec((tm, width), lambda i: (i, 0))

    def const(shape):
        return pl.BlockSpec(shape, lambda i: (0,) * len(shape))

    wr = jnp.zeros((D, LANES), F32).at[:, :N_EXPERTS].set(w_router).astype(BF16)
    before = (jnp.arange(tm)[:, None] > jnp.arange(tm)[None, :]).astype(BF16)
    return pl.pallas_call(
        _outproj_body,
        grid=(T // tm,),
        in_specs=_mixer_specs(tm, S // tm) + [const((D, D)), row(D), const((1, D)), const((D, LANES)),
                                              const((tm, tm))],
        out_specs=[row(D), row(D), row(LANES), pl.BlockSpec((1, 8, LANES), lambda i: (i, 0, 0))],
        out_shape=[jax.ShapeDtypeStruct((T, D), F32), jax.ShapeDtypeStruct((T, D), BF16),
                   jax.ShapeDtypeStruct((T, LANES), F32), jax.ShapeDtypeStruct((T // tm, 8, LANES), jnp.int32)],
        compiler_params=_params("arbitrary"),
        name="outproj_route",
    )(yr, ys, yc, w_bf, x, g.reshape(1, D), wr, before)


def _dense_ffn_body(yr_ref, ys_ref, yc_ref, wo_ref, x_ref, g_ref, wg_ref, wu_ref, wd_ref, o_ref, *, chunks):
    x = _mixer_residual(yr_ref, ys_ref, yc_ref, wo_ref, x_ref)
    h = _rms(x, g_ref[...]).astype(BF16)
    acc = x
    for a, b in chunks:
        g = _dot(h, wg_ref[:, a:b])
        u = _dot(h, wu_ref[:, a:b])
        acc = acc + _dot((_silu(g) * u).astype(BF16), wd_ref[a:b, :])
    o_ref[...] = acc


def _outproj_dense_ffn(yr, ys, yc, wo_bf, x, g, wg, wu, wd, tm, fc):
    T, D = x.shape
    FF = wg.shape[1]
    S = ys.shape[2]
    tm = min(tm, S)
    chunks = tuple((a, min(a + fc, FF)) for a in range(0, FF, fc))

    def row(width):
        return pl.BlockSpec((tm, width), lambda i: (i, 0))

    def const(shape):
        return pl.BlockSpec(shape, lambda i: (0,) * len(shape), pipeline_mode=pl.Buffered(1))

    return pl.pallas_call(
        functools.partial(_dense_ffn_body, chunks=chunks),
        grid=(T // tm,),
        in_specs=_mixer_specs(tm, S // tm) + [const((D, D)), row(D), const((1, D)),
                                              const((D, FF)), const((D, FF)), const((FF, D))],
        out_specs=row(D),
        out_shape=jax.ShapeDtypeStruct((T, D), F32),
        compiler_params=_params("arbitrary"),
        name="outproj_dense_ffn",
    )(yr, ys, yc, wo_bf, x, g.reshape(1, D), wg.astype(BF16), wu.astype(BF16), wd.astype(BF16))


MOE_TB = 512
MOE_TM = 512
RUN_ALIGN = 8
RUN_SIZES = (512, 256, 128, 64, 32, 16, 8)
BLOCK_ROWS = TOP_K * MOE_TB + N_EXPERTS * RUN_ALIGN
BLOCK_LANES = 1152
MOE_FC = 512
MOE_LOAD_SLOTS = 18


def _route_meta(rt, rl, n_tiles):
    T = rt.shape[0]
    nb = T // MOE_TB
    pos = rt[:, 4:4 + TOP_K].astype(jnp.int32)
    run_len = rl[:, 0, :N_EXPERTS]
    loc_off = jnp.cumsum(run_len, axis=1) - run_len
    group = jnp.sum(run_len, axis=0)
    ptiles = (group + MOE_TM - 1) // MOE_TM
    tile_end = jnp.cumsum(ptiles)
    gstart = (tile_end - ptiles) * MOE_TM
    run_start = gstart[None, :] + jnp.cumsum(run_len, axis=0) - run_len
    tile_expert = jnp.sum(jnp.arange(n_tiles, dtype=jnp.int32)[:, None] >= tile_end[None, :], axis=1)
    i32 = lambda a: a.astype(jnp.int32)
    pos3 = pos.reshape(nb, MOE_TB, TOP_K)
    return dict(
        pos_l=i32(jnp.swapaxes(pos3, 1, 2)),
        pos_c=i32(pos3.reshape(T, TOP_K)),
        loc_off=i32(loc_off.reshape(-1)), run_start=i32(run_start.reshape(-1)), run_len=i32(run_len.reshape(-1)),
        pad_start=i32(gstart + group), pad_len=i32(ptiles * MOE_TM - group),
        tile_expert=i32(jnp.minimum(tile_expert, N_EXPERTS - 1)), n_used=i32(tile_end[-1:]))


def _run_dma(src, dst, src_off, dst_off, length, sem, wait):
    off = 0
    for k in RUN_SIZES:
        part = length & k

        @pl.when(part != 0)
        def _(off=off, k=k):
            cp = pltpu.make_async_copy(src.at[pl.ds(pl.multiple_of(src_off + off, RUN_ALIGN), k)],
                                       dst.at[pl.ds(pl.multiple_of(dst_off + off, RUN_ALIGN), k)], sem)
            if wait:
                cp.wait()
            else:
                cp.start()

        off = off + part


def _dispatch_body(lo_ref, rs_ref, rl_ref, ps_ref, pn_ref, nu_ref, pos_ref, h_ref, xs_hbm, sbuf, zbuf, sem):
    b, nb = pl.program_id(0), pl.num_programs(0)
    slot = b % 2
    n_tiles = xs_hbm.shape[0] // MOE_TM

    def zero_tile(i, wait):
        cp = pltpu.make_async_copy(zbuf, xs_hbm.at[pl.ds(pl.multiple_of(i * MOE_TM, MOE_TM), MOE_TM)], sem.at[2])
        if wait:
            cp.wait()
        else:
            cp.start()

    def runs(blk, s, wait):
        for e in range(N_EXPERTS):
            j = blk * N_EXPERTS + e
            _run_dma(sbuf.at[s], xs_hbm, lo_ref[j], rs_ref[j], rl_ref[j], sem.at[s], wait)

    @pl.when(b >= 2)
    def _():
        runs(b - 2, slot, True)

    pos = pos_ref[0]
    r = lax.broadcasted_iota(jnp.int32, (BLOCK_ROWS, MOE_TB), 0)
    onehot = jnp.logical_or(r == pos[0:1, :], r == pos[1:2, :])
    sbuf[slot] = _dot(jnp.where(onehot, 1.0, 0.0).astype(BF16), h_ref[...])
    runs(b, slot, False)

    @pl.when(b == nb - 1)
    def _():
        zbuf[...] = jnp.zeros_like(zbuf)
        for e in range(N_EXPERTS):
            _run_dma(zbuf, xs_hbm, 0, ps_ref[e], pn_ref[e], sem.at[2], False)
        lax.fori_loop(nu_ref[0], n_tiles, lambda i, c: zero_tile(i, False), None)
        for e in range(N_EXPERTS):
            _run_dma(zbuf, xs_hbm, 0, ps_ref[e], pn_ref[e], sem.at[2], True)
        lax.fori_loop(nu_ref[0], n_tiles, lambda i, c: zero_tile(i, True), None)

        @pl.when(b >= 1)
        def _():
            runs(b - 1, 1 - slot, True)

        runs(b, slot, True)


def _dispatch(hn, meta, n_rows):
    T, D = hn.shape
    nb = T // MOE_TB
    grid_spec = pltpu.PrefetchScalarGridSpec(
        num_scalar_prefetch=6,
        grid=(nb,),
        in_specs=[pl.BlockSpec((1, TOP_K, MOE_TB), lambda b, *_: (b, 0, 0)),
                  pl.BlockSpec((MOE_TB, D), lambda b, *_: (b, 0))],
        out_specs=pl.BlockSpec(memory_space=pl.ANY),
        scratch_shapes=[pltpu.VMEM((2, BLOCK_ROWS, D), F32), pltpu.VMEM((MOE_TM, D), F32),
                        pltpu.SemaphoreType.DMA((3,))],
    )
    return pl.pallas_call(
        _dispatch_body,
        grid_spec=grid_spec,
        out_shape=jax.ShapeDtypeStruct((n_rows, D), F32),
        compiler_params=pltpu.CompilerParams(dimension_semantics=("arbitrary",), vmem_limit_bytes=VMEM_LIMIT,
                                             has_side_effects=True),
        name="moe_dispatch",
    )(meta["loc_off"], meta["run_start"], meta["run_len"], meta["pad_start"], meta["pad_len"], meta["n_used"],
      meta["pos_l"], hn)


def _moe_ffn_body(te_ref, nu_ref, x_ref, wg_hbm, wu_hbm, wd_hbm, o_ref, wg_s, wu_s, wd_s, stage, sem):
    i = pl.program_id(0)
    e = te_ref[i]
    active = i < nu_ref[0]
    first = jnp.logical_or(i == 0, e != te_ref[jnp.maximum(i - 1, 0)])
    D, FF = wg_s.shape
    fsl = [slice(f * MOE_FC, (f + 1) * MOE_FC) for f in range(FF // MOE_FC)]
    slots, piece = stage.shape[0], stage.shape[1]
    psl = [slice(r, r + piece) for r in range(0, D, piece)]
    jobs = []
    for f in fsl:
        jobs += [(wg_hbm, wg_s, p, f) for p in psl] + [(wu_hbm, wu_s, p, f) for p in psl]
        jobs += [(wd_hbm, wd_s, f, p) for p in psl]
    per_chunk = len(jobs) // len(fsl)

    def copy(j):
        w_hbm, _, rows, cols = jobs[j]
        return pltpu.make_async_copy(w_hbm.at[e, rows, cols], stage.at[j % slots], sem.at[j % slots])

    def ffn(load):
        x = x_ref[...].astype(BF16)
        acc = jnp.zeros(o_ref.shape, F32)
        if load:
            for j in range(slots - 1):
                copy(j).start()
        for c, sl in enumerate(fsl):
            if load:
                for j in range(c * per_chunk, (c + 1) * per_chunk):
                    if j + slots - 1 < len(jobs):
                        copy(j + slots - 1).start()
                    copy(j).wait()
                    _, w_s, rows, cols = jobs[j]
                    w_s[rows, cols] = stage[j % slots].astype(BF16)
            g = _dot(x, wg_s[:, sl])
            u = _dot(x, wu_s[:, sl])
            acc = acc + _dot((_silu(g) * u).astype(BF16), wd_s[sl, :])
        o_ref[...] = acc

    @pl.when(jnp.logical_and(active, first))
    def _():
        ffn(load=True)

    @pl.when(jnp.logical_and(active, jnp.logical_not(first)))
    def _():
        ffn(load=False)

    @pl.when(jnp.logical_not(active))
    def _():
        o_ref[...] = jnp.zeros_like(o_ref)


def _moe_ffn(xs, meta, wg, wu, wd):
    R, D = xs.shape
    E, _, FF = wg.shape
    grid_spec = pltpu.PrefetchScalarGridSpec(
        num_scalar_prefetch=2,
        grid=(R // MOE_TM,),
        in_specs=[pl.BlockSpec((MOE_TM, D), lambda i, te, nu: (jnp.minimum(i, nu[0] - 1), 0)),
                  pl.BlockSpec(memory_space=pl.ANY), pl.BlockSpec(memory_space=pl.ANY),
                  pl.BlockSpec(memory_space=pl.ANY)],
        out_specs=pl.BlockSpec((MOE_TM, D), lambda i, te, nu: (i, 0)),
        scratch_shapes=[pltpu.VMEM((D, FF), BF16), pltpu.VMEM((D, FF), BF16), pltpu.VMEM((FF, D), BF16),
                        pltpu.VMEM((MOE_LOAD_SLOTS, MOE_FC, MOE_FC), F32),
                        pltpu.SemaphoreType.DMA((MOE_LOAD_SLOTS,))],
    )
    return pl.pallas_call(
        _moe_ffn_body,
        grid_spec=grid_spec,
        out_shape=jax.ShapeDtypeStruct((R, D), F32),
        compiler_params=_params("arbitrary"),
        name="moe_ffn",
    )(meta["tile_expert"], meta["n_used"], xs, wg, wu, wd)


def _combine_body(lo_ref, rs_ref, rl_ref, ys_hbm, x_ref, rt_ref, pc_ref, g_ref, o_ref, ybuf, sem):
    b, nb = pl.program_id(0), pl.num_programs(0)
    slot = b % 2

    def runs(blk, s, wait):
        for e in range(N_EXPERTS):
            j = blk * N_EXPERTS + e
            _run_dma(ys_hbm, ybuf.at[s], rs_ref[j], lo_ref[j], rl_ref[j], sem.at[s], wait)

    @pl.when(b == 0)
    def _():
        ybuf[...] = jnp.zeros_like(ybuf)
        runs(0, 0, False)

    @pl.when(b + 1 < nb)
    def _():
        runs(b + 1, 1 - slot, False)

    runs(b, slot, True)
    y = ybuf[slot].astype(BF16)
    pc, rt = pc_ref[...], rt_ref[...]
    lane = lax.broadcasted_iota(jnp.int32, (MOE_TB, BLOCK_LANES), 1)
    q = jnp.where(lane == pc[:, 0:1], rt[:, 2:3], jnp.where(lane == pc[:, 1:2], rt[:, 3:4], 0.0)).astype(BF16)
    o_ref[...] = _rms(x_ref[...] + _dot(q, y), g_ref[...])


def _combine(ys, meta, x, rt, g):
    T, D = x.shape
    grid_spec = pltpu.PrefetchScalarGridSpec(
        num_scalar_prefetch=3,
        grid=(T // MOE_TB,),
        in_specs=[pl.BlockSpec(memory_space=pl.ANY),
                  pl.BlockSpec((MOE_TB, D), lambda b, *_: (b, 0)),
                  pl.BlockSpec((MOE_TB, LANES), lambda b, *_: (b, 0)),
                  pl.BlockSpec((MOE_TB, TOP_K), lambda b, *_: (b, 0)),
                  pl.BlockSpec((1, D), lambda b, *_: (0, 0))],
        out_specs=pl.BlockSpec((MOE_TB, D), lambda b, *_: (b, 0)),
        scratch_shapes=[pltpu.VMEM((2, BLOCK_LANES, D), F32), pltpu.SemaphoreType.DMA((2,))],
    )
    return pl.pallas_call(
        _combine_body,
        grid_spec=grid_spec,
        out_shape=jax.ShapeDtypeStruct((T, D), F32),
        compiler_params=_params("arbitrary"),
        name="moe_combine",
    )(meta["loc_off"], meta["run_start"], meta["run_len"], ys, x, rt, meta["pos_c"], g.reshape(1, D))


def kernel(x, norm_mix_g, w_in, ret_gn_g, attn_sinks, conv_dw_w, conv_dw_b, conv_ln_g, conv_ln_b,
           conv_pw_w, w_out, norm_ffn_g, ffn_w_gate, ffn_w_up, ffn_w_down, moe_router, moe_w_gate,
           moe_w_up, moe_w_down, final_norm_g):
    B, S, D = x.shape
    T = B * S
    depth = w_in.shape[0]
    assert depth == 2 and ffn_w_gate.shape[0] == 1 and moe_router.shape[0] == 1, "dense layer then MoE layer"
    assert T % MOE_TB == 0 and S % MOE_TB == 0
    assert moe_w_gate.shape[-1] % MOE_FC == 0 and D % MOE_FC == 0

    for l in range(depth):
        rq, rk, rv, rg, sq, sk, sv, ch = _inproj(x, norm_mix_g[l], w_in[l].astype(BF16), tm=INPROJ_TM)
        y_ret_t = _retention(rq, rk, rv, rg, ret_gn_g[l], ts=RET_TS)
        y_swa_t = _swa(sq, sk, sv, attn_sinks[l], tq=SWA_TQ)
        y_conv = _conv(ch, conv_dw_w[l], conv_dw_b[l], conv_ln_g[l], conv_ln_b[l], conv_pw_w[l],
                       ts=CONV_TS, rb=CONV_RB)
        flat = lambda a: a.reshape(T, a.shape[-1])
        j = l // 2
        if l % 2 == 0:
            x = _outproj_dense_ffn(y_ret_t, y_swa_t, flat(y_conv), w_out[l].astype(BF16), flat(x),
                                   norm_ffn_g[l], ffn_w_gate[j], ffn_w_up[j], ffn_w_down[j],
                                   tm=FFN_TM, fc=FFN_FC).reshape(B, S, D)
        else:
            x2, hn, rt, rl = _outproj_route(y_ret_t, y_swa_t, flat(y_conv), w_out[l].astype(BF16), flat(x),
                                            norm_ffn_g[l], moe_router[j])
            n_rows = TOP_K * T + (T // MOE_TB) * N_EXPERTS * RUN_ALIGN + N_EXPERTS * MOE_TM
            meta = _route_meta(rt, rl, n_rows // MOE_TM)
            xs = _dispatch(hn, meta, n_rows)
            ys = _moe_ffn(xs, meta, moe_w_gate[j], moe_w_up[j], moe_w_down[j])
            x = _combine(ys, meta, x2, rt, final_norm_g).reshape(B, S, D)
    return x
```

```python
import functools

import jax
import jax.numpy as jnp
from jax import lax
from jax.experimental import pallas as pl
from jax.experimental.pallas import tpu as pltpu

F32 = jnp.float32
BF16 = jnp.bfloat16

HEAD_DIM = 64
RET_HEADS = 4
RET_W = RET_HEADS * HEAD_DIM
SWA_HEADS = 8
SWA_KV_HEADS = 2
SWA_GROUP = SWA_HEADS // SWA_KV_HEADS
SWA_W = SWA_HEADS * HEAD_DIM
SWA_KV_W = SWA_KV_HEADS * HEAD_DIM
CONV_CH = 256
CONV_GROUPS = 4
CONV_WIDTH = 31
WINDOW = 128
BLOCK = 128
RET_CHUNK = 128
ROPE_THETA = 500000.0
ROPE_DIM = HEAD_DIM // 4
RET_ROPE_THETA = 10000.0
N_EXPERTS = 8
TOP_K = 2
EPS = 1e-6
NEG_INF = -1e30

LANES = 128
CONV_HALO = 32
VMEM_LIMIT = 56 * 1024 * 1024

O_RQ, O_RK, O_RV, O_RG = 0, RET_W, 2 * RET_W, 3 * RET_W
O_SQ = 4 * RET_W
O_SK = O_SQ + SWA_W
O_SV = O_SK + SWA_KV_W
O_CA = O_SV + SWA_KV_W
O_CG = O_CA + CONV_CH
D_IN = O_CG + CONV_CH


INPROJ_TM = 1024
RET_TS = 4096
SWA_TQ = 4096
CONV_TS = 1024
CONV_RB = 256
FFN_TM = 1024
FFN_FC = 512


def _params(*sem):
    return pltpu.CompilerParams(dimension_semantics=sem, vmem_limit_bytes=VMEM_LIMIT)


def _rms(x, g):
    return x * lax.rsqrt(jnp.mean(x * x, axis=-1, keepdims=True) + EPS) * g


def _silu(x):
    return x * jax.nn.sigmoid(x)


def _dot(a, b):
    return jnp.dot(a, b, preferred_element_type=F32)


def _dot_nt(a, b):
    return lax.dot_general(a, b, (((1,), (1,)), ((), ())), preferred_element_type=F32)


def _dot_tn(a, b):
    return lax.dot_general(a, b, (((0,), (0,)), ((), ())), preferred_element_type=F32)


def _rope_tables(seq, theta, rot_dim):
    half = rot_dim // 2
    inv = 1.0 / (theta ** (jnp.arange(half, dtype=F32) / half))
    ang = jnp.arange(seq, dtype=F32)[:, None] * inv[None, :]
    cos, sin = jnp.cos(ang), jnp.sin(ang)
    rest = HEAD_DIM - rot_dim
    c = jnp.concatenate([cos, cos, jnp.ones((seq, rest), F32)], axis=1)
    s = jnp.concatenate([-sin, sin, jnp.zeros((seq, rest), F32)], axis=1)
    reps = LANES // HEAD_DIM
    return jnp.tile(c, (1, reps)), jnp.tile(s, (1, reps))


def _inproj_body(x_ref, g_ref, w_ref, wgt_ref, rc_ref, rs_ref, sc_ref, ss_ref,
                 rq_ref, rk_ref, rv_ref, rg_ref, sq_ref, sk_ref, sv_ref, ch_ref, zbuf, ztbuf):
    h = _rms(x_ref[0], g_ref[...]).astype(BF16)
    lane = lax.broadcasted_iota(jnp.int32, (1, LANES), 1) % HEAD_DIM
    for a, b in ((O_SQ, O_SK), (O_SK, O_CA), (O_CA, O_CG), (O_CG, D_IN)):
        zbuf[:, a:b] = _dot(h, w_ref[:, a:b])
    for a, b in ((O_RQ, O_RK), (O_RK, O_RV), (O_RV, O_RG), (O_RG, O_SQ)):
        ztbuf[a:b, :] = _dot_nt(wgt_ref[a:b, :], h)

    def seg(a, b):
        return zbuf[:, a:b]

    def rope(v, c, s, half):
        up = pltpu.roll(v, LANES - half, axis=1)
        dn = pltpu.roll(v, half, axis=1)
        return v * c + jnp.where(lane < half, up, dn) * s

    def store_heads(z, o_ref, first=0):
        for j in range(z.shape[1] // HEAD_DIM):
            o_ref[0, first + j] = z[:, j * HEAD_DIM:(j + 1) * HEAD_DIM]

    def rope_store(z, o_ref, c, s, half, scale):
        for i in range(z.shape[1] // LANES):
            r = rope(z[:, i * LANES:(i + 1) * LANES], c, s, half)
            if scale != 1.0:
                r = r * scale
            store_heads(r.astype(BF16), o_ref, first=i * (LANES // HEAD_DIM))

    rc, rs = rc_ref[...], rs_ref[...]
    sc, ss = sc_ref[...], ss_ref[...]
    scale = HEAD_DIM ** -0.5
    half = HEAD_DIM // 2
    for o_ref, a, k_scale in ((rq_ref, O_RQ, 1.0), (rk_ref, O_RK, scale)):
        for hd in range(RET_HEADS):
            lo, mid, hi = a + hd * HEAD_DIM, a + hd * HEAD_DIM + half, a + (hd + 1) * HEAD_DIM
            z = ztbuf[lo:hi, :]
            partner = jnp.concatenate([ztbuf[mid:hi, :], ztbuf[lo:mid, :]], axis=0)
            r = z * rc + partner * rs
            if k_scale != 1.0:
                r = r * k_scale
            o_ref[0, hd * HEAD_DIM:(hd + 1) * HEAD_DIM, :] = r.astype(BF16)
    rv_ref[0] = ztbuf[O_RV:O_RG, :].astype(BF16)
    rg_ref[0] = ztbuf[O_RG:O_SQ, :]
    rope_store(seg(O_SQ, O_SK), sq_ref, sc, ss, ROPE_DIM // 2, scale)
    rope_store(seg(O_SK, O_SV), sk_ref, sc, ss, ROPE_DIM // 2, 1.0)
    store_heads(seg(O_SV, O_CA).astype(BF16), sv_ref)
    ch_ref[0] = seg(O_CA, O_CG) * jax.nn.sigmoid(seg(O_CG, D_IN))


def _inproj(x, g, w_bf, tm):
    B, S, D = x.shape
    tm = min(tm, S)
    rc, rs = _rope_tables(S, RET_ROPE_THETA, HEAD_DIM)
    rc, rs = rc[:, :HEAD_DIM].T, rs[:, :HEAD_DIM].T
    sc, ss = _rope_tables(S, ROPE_THETA, ROPE_DIM)
    tab = pl.BlockSpec((tm, LANES), lambda s, b: (s, 0))
    tab_t = pl.BlockSpec((HEAD_DIM, tm), lambda s, b: (0, s))

    def out(width, dtype):
        return (jax.ShapeDtypeStruct((B, S, width), dtype),
                pl.BlockSpec((1, tm, width), lambda s, b: (b, s, 0)))

    def out_heads(heads):
        return (jax.ShapeDtypeStruct((B, heads, S, HEAD_DIM), BF16),
                pl.BlockSpec((1, heads, tm, HEAD_DIM), lambda s, b: (b, 0, s, 0)))

    def chan(dtype):
        return (jax.ShapeDtypeStruct((B, RET_W, S), dtype), pl.BlockSpec((1, RET_W, tm), lambda s, b: (b, 0, s)))

    outs = [chan(BF16), chan(BF16), chan(BF16), chan(F32),
            out_heads(SWA_HEADS), out_heads(SWA_KV_HEADS), out_heads(SWA_KV_HEADS), out(CONV_CH, F32)]
    return pl.pallas_call(
        _inproj_body,
        grid=(S // tm, B),
        in_specs=[pl.BlockSpec((1, tm, D), lambda s, b: (b, s, 0)),
                  pl.BlockSpec((1, D), lambda s, b: (0, 0)),
                  pl.BlockSpec((D, D_IN), lambda s, b: (0, 0)),
                  pl.BlockSpec((O_SQ, D), lambda s, b: (0, 0)),
                  tab_t, tab_t, tab, tab],
        out_specs=[o[1] for o in outs],
        out_shape=[o[0] for o in outs],
        scratch_shapes=[pltpu.VMEM((tm, D_IN), F32), pltpu.VMEM((O_SQ, tm), F32)],
        compiler_params=_params("arbitrary", "arbitrary"),
        name="inproj",
    )(x, g.reshape(1, D), w_bf, w_bf[:, :O_SQ].T, rc, rs, sc, ss)


def _ret_body(q_ref, k_ref, v_ref, g_ref, dm_ref, kd_ref, qd_ref, cd_ref, gn_ref, o_ref, st_ref, *, ts):
    @pl.when(pl.program_id(2) == 0)
    def _():
        st_ref[...] = jnp.zeros_like(st_ref)

    C = RET_CHUNK
    for c in range(ts // C):
        rows = slice(c * C, (c + 1) * C)
        q, k, v = q_ref[0, :, rows], k_ref[0, :, rows], v_ref[0, :, rows]
        st = st_ref[...]
        scores = _dot_tn(k, q) * dm_ref[0]
        intra = _dot(v, scores.astype(BF16))
        cross = _dot(st.astype(BF16), q) * qd_ref[0]
        kdec = (k.astype(F32) * kd_ref[0]).astype(BF16)
        st_ref[...] = st * cd_ref[0] + _dot_nt(v, kdec)
        o = intra + cross
        mu = jnp.mean(o, axis=0, keepdims=True)
        d = o - mu
        var = jnp.mean(d * d, axis=0, keepdims=True)
        on = d * lax.rsqrt(var + EPS) * gn_ref[0]
        o_ref[0, :, rows] = (_silu(g_ref[0, :, rows]) * on).astype(BF16)


def _retention(rq, rk, rv, rg_t, gn_g, ts):
    B, W, S = rq.shape
    H, D, C = RET_HEADS, HEAD_DIM, RET_CHUNK
    ts = min(ts, S)
    lg = jnp.log(1.0 - 2.0 ** (-5.0 - jnp.arange(H, dtype=F32)))
    idx = jnp.arange(C)
    rel = idx[:, None] - idx[None, :]
    dmask = jnp.where(rel[None] >= 0,
                      jnp.exp(jnp.maximum(rel, 0)[None].astype(F32) * lg[:, None, None]), 0.0)
    k_decay = jnp.exp((C - 1 - idx)[:, None].astype(F32) * lg[None, :])
    q_decay = jnp.exp((idx + 1)[:, None].astype(F32) * lg[None, :])
    chunk_decay = jnp.exp(C * lg)
    dm_t = jnp.swapaxes(dmask, 1, 2)
    kd = jnp.broadcast_to(k_decay.T[:, None, :], (H, D, C))
    qd = jnp.broadcast_to(q_decay.T[:, None, :], (H, D, C))
    cd = jnp.broadcast_to(chunk_decay[:, None, None], (H, D, D))
    gn = jnp.broadcast_to(gn_g.reshape(H, D, 1), (H, D, C))
    chan = pl.BlockSpec((1, D, ts), lambda b, h, n: (b, h, n))

    def per_head(r, c):
        return pl.BlockSpec((1, r, c), lambda b, h, n: (h, 0, 0))

    return pl.pallas_call(
        functools.partial(_ret_body, ts=ts),
        grid=(B, H, S // ts),
        in_specs=[chan, chan, chan, chan, per_head(C, C), per_head(D, C), per_head(D, C),
                  per_head(D, D), per_head(D, C)],
        out_specs=chan,
        out_shape=jax.ShapeDtypeStruct((B, W, S), BF16),
        scratch_shapes=[pltpu.VMEM((D, D), F32)],
        compiler_params=_params("arbitrary", "arbitrary", "arbitrary"),
        name="retention",
    )(rq, rk, rv, rg_t, dm_t, kd, qd, cd, gn)


def _swa_body(sink_ref, q_ref, kc_ref, kp_ref, vc_ref, vp_ref, bias_ref, o_ref, kcat, vcat, sbuf, pbuf, rbuf,
              *, tq):
    hk, n = pl.program_id(1), pl.program_id(2)
    kcat[0:BLOCK], kcat[BLOCK:] = kp_ref[0, 0], kc_ref[0, 0]
    vcat[0:BLOCK], vcat[BLOCK:] = vp_ref[0, 0], vc_ref[0, 0]
    width = SWA_GROUP * BLOCK
    group = lax.broadcasted_iota(jnp.int32, (1, width), 1) // BLOCK
    sink = jnp.zeros((1, width), F32)
    for g in range(SWA_GROUP):
        sink = jnp.where(group == g, sink_ref[hk * SWA_GROUP + g], sink)
    nq = tq // BLOCK
    for j in range(nq):
        q = q_ref[0, :, j * BLOCK:(j + 1) * BLOCK, :].reshape(width, HEAD_DIM)
        kb = kcat[j * BLOCK:(j + 2) * BLOCK]
        sbuf[j] = _dot_nt(kb, q)
    for j in range(nq):
        s = sbuf[j] + (bias_ref[0] if j > 0 else bias_ref[jnp.where(n == 0, 1, 0)])
        m = jnp.maximum(jnp.max(s, axis=0, keepdims=True), sink)
        p = jnp.exp(s - m)
        rbuf[j] = 1.0 / (jnp.sum(p, axis=0, keepdims=True) + jnp.exp(sink - m))
        pbuf[j] = p.astype(BF16)
    for j in range(nq):
        vb = vcat[j * BLOCK:(j + 2) * BLOCK]
        o = _dot_tn(vb, pbuf[j]) * rbuf[j]
        for g in range(SWA_GROUP):
            o_ref[0, g * HEAD_DIM:(g + 1) * HEAD_DIM, j * BLOCK:(j + 1) * BLOCK] = (
                o[:, g * BLOCK:(g + 1) * BLOCK].astype(BF16))


def _swa(sq, sk, sv, sinks, tq):
    B, _, S, D = sq.shape
    tq = min(tq, S)
    r = tq // BLOCK
    qi = jnp.arange(BLOCK)[None, :] + BLOCK
    kj = jnp.arange(2 * BLOCK)[:, None]
    rel = qi - kj
    allowed = (rel >= 0) & (rel < WINDOW)
    allowed = jnp.stack([allowed, allowed & (kj >= BLOCK)])
    bias = jnp.tile(jnp.where(allowed, 0.0, NEG_INF).astype(F32), (1, 1, SWA_GROUP))
    nq, width = tq // BLOCK, SWA_GROUP * BLOCK
    cur = pl.BlockSpec((1, 1, tq, D), lambda b, h, n: (b, h, n, 0))
    prev = pl.BlockSpec((1, 1, BLOCK, D), lambda b, h, n: (b, h, jnp.maximum(n * r - 1, 0), 0))
    return pl.pallas_call(
        functools.partial(_swa_body, tq=tq),
        grid=(B, SWA_KV_HEADS, S // tq),
        in_specs=[pl.BlockSpec(memory_space=pltpu.SMEM),
                  pl.BlockSpec((1, SWA_GROUP, tq, D), lambda b, h, n: (b, h, n, 0)),
                  cur, prev, cur, prev,
                  pl.BlockSpec((2, 2 * BLOCK, width), lambda b, h, n: (0, 0, 0))],
        out_specs=pl.BlockSpec((1, SWA_GROUP * D, tq), lambda b, h, n: (b, h, n)),
        out_shape=jax.ShapeDtypeStruct((B, SWA_W, S), BF16),
        scratch_shapes=[pltpu.VMEM((BLOCK + tq, D), BF16), pltpu.VMEM((BLOCK + tq, D), BF16),
                        pltpu.VMEM((nq, 2 * BLOCK, width), F32), pltpu.VMEM((nq, 2 * BLOCK, width), BF16),
                        pltpu.VMEM((nq, 1, width), F32)],
        compiler_params=_params("arbitrary", "arbitrary", "arbitrary"),
        name="swa",
    )(sinks.astype(F32), sq, sk, sk, sv, sv, bias)


def _conv_body(h_ref, dw_ref, db_ref, lg_ref, lb_ref, pw_ref, o_ref, hbuf, hsh, *, ts, rb):
    s = pl.program_id(1)
    sub = 8
    span = ts + CONV_HALO - sub

    @pl.when(s == 0)
    def _():
        hbuf[0:CONV_HALO, :] = jnp.zeros((CONV_HALO, CONV_CH), F32)

    @pl.when(s > 0)
    def _():
        hbuf[0:CONV_HALO, :] = hbuf[ts:ts + CONV_HALO, :]

    hbuf[CONV_HALO:CONV_HALO + ts, :] = h_ref[0]
    for p in range(1, sub):
        hsh[p - 1] = hbuf[p:p + span, :]
    off = CONV_HALO - (CONV_WIDTH - 1)
    for r in range(ts // rb):
        base = r * rb
        acc = jnp.broadcast_to(db_ref[...], (rb, CONV_CH))
        for w in range(CONV_WIDTH):
            q, p = divmod(off + w, sub)
            lo = base + q * sub
            tap = hbuf[lo:lo + rb, :] if p == 0 else hsh[p - 1, lo:lo + rb, :]
            acc = acc + tap * dw_ref[w:w + 1, :]
        mu = jnp.mean(acc, axis=-1, keepdims=True)
        d = acc - mu
        var = jnp.mean(d * d, axis=-1, keepdims=True)
        hn = d * lax.rsqrt(var + EPS) * lg_ref[...] + lb_ref[...]
        o_ref[0, base:base + rb, :] = _dot(_silu(hn).astype(BF16), pw_ref[...]).astype(BF16)


def _conv(ch, dw_w, dw_b, ln_g, ln_b, pw_w, ts, rb):
    B, S, W = ch.shape
    ts = min(ts, S)
    rb = min(rb, ts)
    pw = jnp.zeros((W, W), F32)
    gd = W // CONV_GROUPS
    for g in range(CONV_GROUPS):
        pw = pw.at[g * gd:(g + 1) * gd, g * gd:(g + 1) * gd].set(pw_w[g])
    act = pl.BlockSpec((1, ts, W), lambda b, s: (b, s, 0))

    def const(shape):
        return pl.BlockSpec(shape, lambda b, s: (0,) * len(shape))

    return pl.pallas_call(
        functools.partial(_conv_body, ts=ts, rb=rb),
        grid=(B, S // ts),
        in_specs=[act, const((CONV_WIDTH, W)), const((1, W)), const((1, W)), const((1, W)), const((W, W))],
        out_specs=act,
        out_shape=jax.ShapeDtypeStruct((B, S, W), BF16),
        scratch_shapes=[pltpu.VMEM((CONV_HALO + ts, W), F32), pltpu.VMEM((7, CONV_HALO + ts - 8, W), F32)],
        compiler_params=_params("arbitrary", "arbitrary"),
        name="conv",
    )(ch, dw_w, dw_b.reshape(1, W), ln_g.reshape(1, W), ln_b.reshape(1, W), pw.astype(BF16))


def _mixer_residual(yr_ref, ys_ref, yc_ref, w_ref, x_ref):
    a, b = RET_W, RET_W + SWA_W
    return x_ref[...] + (_dot_tn(yr_ref[0], w_ref[0:a, :]) + _dot_tn(ys_ref[0], w_ref[a:b, :])
                         + _dot(yc_ref[...], w_ref[b:, :]))


def _mixer_specs(tm, tiles_per_seq):
    def chan(width):
        return pl.BlockSpec((1, width, tm), lambda i: (i // tiles_per_seq, 0, i % tiles_per_seq))

    return [chan(RET_W), chan(SWA_W), pl.BlockSpec((tm, CONV_CH), lambda i: (i, 0))]


def _outproj_body(yr_ref, ys_ref, yc_ref, w_ref, x_ref, g_ref, wr_ref, before_ref, xo_ref, hn_ref, rt_ref, rl_ref):
    x = _mixer_residual(yr_ref, ys_ref, yc_ref, w_ref, x_ref)
    xo_ref[...] = x
    h = _rms(x, g_ref[...]).astype(BF16)
    hn_ref[...] = h
    logits = _dot(h, wr_ref[...])
    lane = lax.broadcasted_iota(jnp.int32, logits.shape, 1).astype(F32)
    lg = jnp.where(lane < N_EXPERTS, logits, -jnp.inf)
    m1 = jnp.max(lg, axis=-1, keepdims=True)
    i1 = jnp.min(jnp.where(lg == m1, lane, float(LANES)), axis=-1, keepdims=True)
    lg2 = jnp.where(lane == i1, -jnp.inf, lg)
    m2 = jnp.max(lg2, axis=-1, keepdims=True)
    i2 = jnp.min(jnp.where(lg2 == m2, lane, float(LANES)), axis=-1, keepdims=True)
    e = jnp.exp(m2 - m1)
    w1 = 1.0 / (1.0 + e)
    w2 = e / (1.0 + e)
    hit1, hit2 = lane == i1, lane == i2
    onehot = jnp.where(jnp.logical_or(hit1, hit2), 1.0, 0.0)
    rank = _dot(before_ref[...], onehot.astype(BF16))
    count = jnp.sum(onehot, axis=0, keepdims=True).astype(jnp.int32)
    run_len = (count + (RUN_ALIGN - 1)) & ~(RUN_ALIGN - 1)
    lower = (lax.broadcasted_iota(jnp.int32, (LANES, LANES), 0) < lax.broadcasted_iota(jnp.int32, (LANES, LANES), 1))
    run_rows = jnp.broadcast_to(run_len.astype(F32), (8, LANES)).astype(BF16)
    loc_off = _dot(run_rows, jnp.where(lower, 1.0, 0.0).astype(BF16))[0:1, :]
    pos = rank + loc_off
    p1 = jnp.sum(jnp.where(hit1, pos, 0.0), axis=-1, keepdims=True)
    p2 = jnp.sum(jnp.where(hit2, pos, 0.0), axis=-1, keepdims=True)
    cols = (i1, i2, w1, w2, p1, p2)
    rt = jnp.zeros_like(logits)
    for c, v in enumerate(cols):
        rt = jnp.where(lane == c, v, rt)
    rt_ref[...] = rt
    rl_ref[0] = jnp.broadcast_to(run_len, (8, LANES))


def _outproj_route(yr, ys, yc, w_bf, x, g, w_router):
    T, D = x.shape
    tm = MOE_TB
    S = ys.shape[2]

    def row(width):
        return pl.BlockSpec((tm, width), lambda i: (i, 0))

    def const(shape):
        return pl.BlockSpec(shape, lambda i: (0,) * len(shape))

    wr = jnp.zeros((D, LANES), F32).at[:, :N_EXPERTS].set(w_router).astype(BF16)
    before = (jnp.arange(tm)[:, None] > jnp.arange(tm)[None, :]).astype(BF16)
    return pl.pallas_call(
        _outproj_body,
        grid=(T // tm,),
        in_specs=_mixer_specs(tm, S // tm) + [const((D, D)), row(D), const((1, D)), const((D, LANES)),
                                              const((tm, tm))],
        out_specs=[row(D), row(D), row(LANES), pl.BlockSpec((1, 8, LANES), lambda i: (i, 0, 0))],
        out_shape=[jax.ShapeDtypeStruct((T, D), F32), jax.ShapeDtypeStruct((T, D), BF16),
                   jax.ShapeDtypeStruct((T, LANES), F32), jax.ShapeDtypeStruct((T // tm, 8, LANES), jnp.int32)],
        compiler_params=_params("arbitrary"),
        name="outproj_route",
    )(yr, ys, yc, w_bf, x, g.reshape(1, D), wr, before)


def _dense_ffn_body(yr_ref, ys_ref, yc_ref, wo_ref, x_ref, g_ref, wg_ref, wu_ref, wd_ref, o_ref, *, chunks):
    x = _mixer_residual(yr_ref, ys_ref, yc_ref, wo_ref, x_ref)
    h = _rms(x, g_ref[...]).astype(BF16)
    acc = x
    for a, b in chunks:
        g = _dot(h, wg_ref[:, a:b])
        u = _dot(h, wu_ref[:, a:b])
        acc = acc + _dot((_silu(g) * u).astype(BF16), wd_ref[a:b, :])
    o_ref[...] = acc


def _outproj_dense_ffn(yr, ys, yc, wo_bf, x, g, wg, wu, wd, tm, fc):
    T, D = x.shape
    FF = wg.shape[1]
    S = ys.shape[2]
    tm = min(tm, S)
    chunks = tuple((a, min(a + fc, FF)) for a in range(0, FF, fc))

    def row(width):
        return pl.BlockSpec((tm, width), lambda i: (i, 0))

    def const(shape):
        return pl.BlockSpec(shape, lambda i: (0,) * len(shape), pipeline_mode=pl.Buffered(1))

    return pl.pallas_call(
        functools.partial(_dense_ffn_body, chunks=chunks),
        grid=(T // tm,),
        in_specs=_mixer_specs(tm, S // tm) + [const((D, D)), row(D), const((1, D)),
                                              const((D, FF)), const((D, FF)), const((FF, D))],
        out_specs=row(D),
        out_shape=jax.ShapeDtypeStruct((T, D), F32),
        compiler_params=_params("arbitrary"),
        name="outproj_dense_ffn",
    )(yr, ys, yc, wo_bf, x, g.reshape(1, D), wg.astype(BF16), wu.astype(BF16), wd.astype(BF16))


MOE_TB = 512
MOE_TM = 512
RUN_ALIGN = 8
RUN_SIZES = (512, 256, 128, 64, 32, 16, 8)
BLOCK_ROWS = TOP_K * MOE_TB + N_EXPERTS * RUN_ALIGN
BLOCK_LANES = 1152
MOE_FC = 512
MOE_LOAD_SLOTS = 12


def _route_meta(rt, rl, n_tiles):
    T = rt.shape[0]
    nb = T // MOE_TB
    pos = rt[:, 4:4 + TOP_K].astype(jnp.int32)
    run_len = rl[:, 0, :N_EXPERTS]
    loc_off = jnp.cumsum(run_len, axis=1) - run_len
    group = jnp.sum(run_len, axis=0)
    ptiles = (group + MOE_TM - 1) // MOE_TM
    tile_end = jnp.cumsum(ptiles)
    gstart = (tile_end - ptiles) * MOE_TM
    run_start = gstart[None, :] + jnp.cumsum(run_len, axis=0) - run_len
    tile_expert = jnp.sum(jnp.arange(n_tiles, dtype=jnp.int32)[:, None] >= tile_end[None, :], axis=1)
    i32 = lambda a: a.astype(jnp.int32)
    pos3 = pos.reshape(nb, MOE_TB, TOP_K)
    return dict(
        pos_l=i32(jnp.swapaxes(pos3, 1, 2)),
        pos_c=i32(pos3.reshape(T, TOP_K)),
        loc_off=i32(loc_off.reshape(-1)), run_start=i32(run_start.reshape(-1)), run_len=i32(run_len.reshape(-1)),
        pad_start=i32(gstart + group), pad_len=i32(ptiles * MOE_TM - group),
        tile_expert=i32(jnp.minimum(tile_expert, N_EXPERTS - 1)), n_used=i32(tile_end[-1:]))


def _run_dma(src, dst, src_off, dst_off, length, sem, wait):
    off = 0
    for k in RUN_SIZES:
        part = length & k

        @pl.when(part != 0)
        def _(off=off, k=k):
            cp = pltpu.make_async_copy(src.at[pl.ds(pl.multiple_of(src_off + off, RUN_ALIGN), k)],
                                       dst.at[pl.ds(pl.multiple_of(dst_off + off, RUN_ALIGN), k)], sem)
            if wait:
                cp.wait()
            else:
                cp.start()

        off = off + part


def _dispatch_body(lo_ref, rs_ref, rl_ref, ps_ref, pn_ref, nu_ref, pos_ref, h_ref, xs_hbm, sbuf, zbuf, sem):
    b, nb = pl.program_id(0), pl.num_programs(0)
    slot = b % 2
    n_tiles = xs_hbm.shape[0] // MOE_TM

    def zero_tile(i, wait):
        cp = pltpu.make_async_copy(zbuf, xs_hbm.at[pl.ds(pl.multiple_of(i * MOE_TM, MOE_TM), MOE_TM)], sem.at[2])
        if wait:
            cp.wait()
        else:
            cp.start()

    def runs(blk, s, wait):
        for e in range(N_EXPERTS):
            j = blk * N_EXPERTS + e
            _run_dma(sbuf.at[s], xs_hbm, lo_ref[j], rs_ref[j], rl_ref[j], sem.at[s], wait)

    @pl.when(b >= 2)
    def _():
        runs(b - 2, slot, True)

    pos = pos_ref[0]
    r = lax.broadcasted_iota(jnp.int32, (BLOCK_ROWS, MOE_TB), 0)
    onehot = jnp.logical_or(r == pos[0:1, :], r == pos[1:2, :])
    sbuf[slot] = _dot(jnp.where(onehot, 1.0, 0.0).astype(BF16), h_ref[...])
    runs(b, slot, False)

    @pl.when(b == nb - 1)
    def _():
        zbuf[...] = jnp.zeros_like(zbuf)
        for e in range(N_EXPERTS):
            _run_dma(zbuf, xs_hbm, 0, ps_ref[e], pn_ref[e], sem.at[2], False)
        lax.fori_loop(nu_ref[0], n_tiles, lambda i, c: zero_tile(i, False), None)
        for e in range(N_EXPERTS):
            _run_dma(zbuf, xs_hbm, 0, ps_ref[e], pn_ref[e], sem.at[2], True)
        lax.fori_loop(nu_ref[0], n_tiles, lambda i, c: zero_tile(i, True), None)

        @pl.when(b >= 1)
        def _():
            runs(b - 1, 1 - slot, True)

        runs(b, slot, True)


def _dispatch(hn, meta, n_rows):
    T, D = hn.shape
    nb = T // MOE_TB
    grid_spec = pltpu.PrefetchScalarGridSpec(
        num_scalar_prefetch=6,
        grid=(nb,),
        in_specs=[pl.BlockSpec((1, TOP_K, MOE_TB), lambda b, *_: (b, 0, 0)),
                  pl.BlockSpec((MOE_TB, D), lambda b, *_: (b, 0))],
        out_specs=pl.BlockSpec(memory_space=pl.ANY),
        scratch_shapes=[pltpu.VMEM((2, BLOCK_ROWS, D), F32), pltpu.VMEM((MOE_TM, D), F32),
                        pltpu.SemaphoreType.DMA((3,))],
    )
    return pl.pallas_call(
        _dispatch_body,
        grid_spec=grid_spec,
        out_shape=jax.ShapeDtypeStruct((n_rows, D), F32),
        compiler_params=pltpu.CompilerParams(dimension_semantics=("arbitrary",), vmem_limit_bytes=VMEM_LIMIT,
                                             has_side_effects=True),
        name="moe_dispatch",
    )(meta["loc_off"], meta["run_start"], meta["run_len"], meta["pad_start"], meta["pad_len"], meta["n_used"],
      meta["pos_l"], hn)


def _moe_ffn_body(te_ref, nu_ref, x_ref, wg_hbm, wu_hbm, wd_hbm, o_ref, wg_s, wu_s, wd_s, stage, sem):
    i = pl.program_id(0)
    e = te_ref[i]
    active = i < nu_ref[0]
    first = jnp.logical_or(i == 0, e != te_ref[jnp.maximum(i - 1, 0)])
    D, FF = wg_s.shape
    fsl = [slice(f * MOE_FC, (f + 1) * MOE_FC) for f in range(FF // MOE_FC)]
    slots, piece = stage.shape[0], stage.shape[1]
    psl = [slice(r, r + piece) for r in range(0, D, piece)]
    jobs = []
    for f in fsl:
        jobs += [(wg_hbm, wg_s, p, f) for p in psl] + [(wu_hbm, wu_s, p, f) for p in psl]
        jobs += [(wd_hbm, wd_s, f, p) for p in psl]
    per_chunk = len(jobs) // len(fsl)

    def copy(j):
        w_hbm, _, rows, cols = jobs[j]
        return pltpu.make_async_copy(w_hbm.at[e, rows, cols], stage.at[j % slots], sem.at[j % slots])

    def ffn(load):
        x = x_ref[...].astype(BF16)
        acc = jnp.zeros(o_ref.shape, F32)
        if load:
            for j in range(slots - 1):
                copy(j).start()
        for c, sl in enumerate(fsl):
            if load:
                for j in range(c * per_chunk, (c + 1) * per_chunk):
                    if j + slots - 1 < len(jobs):
                        copy(j + slots - 1).start()
                    copy(j).wait()
                    _, w_s, rows, cols = jobs[j]
                    w_s[rows, cols] = stage[j % slots].astype(BF16)
            g = _dot(x, wg_s[:, sl])
            u = _dot(x, wu_s[:, sl])
            acc = acc + _dot((_silu(g) * u).astype(BF16), wd_s[sl, :])
        o_ref[...] = acc

    @pl.when(jnp.logical_and(active, first))
    def _():
        ffn(load=True)

    @pl.when(jnp.logical_and(active, jnp.logical_not(first)))
    def _():
        ffn(load=False)

    @pl.when(jnp.logical_not(active))
    def _():
        o_ref[...] = jnp.zeros_like(o_ref)


def _moe_ffn(xs, meta, wg, wu, wd):
    R, D = xs.shape
    E, _, FF = wg.shape
    grid_spec = pltpu.PrefetchScalarGridSpec(
        num_scalar_prefetch=2,
        grid=(R // MOE_TM,),
        in_specs=[pl.BlockSpec((MOE_TM, D), lambda i, te, nu: (jnp.minimum(i, nu[0] - 1), 0)),
                  pl.BlockSpec(memory_space=pl.ANY), pl.BlockSpec(memory_space=pl.ANY),
                  pl.BlockSpec(memory_space=pl.ANY)],
        out_specs=pl.BlockSpec((MOE_TM, D), lambda i, te, nu: (i, 0)),
        scratch_shapes=[pltpu.VMEM((D, FF), BF16), pltpu.VMEM((D, FF), BF16), pltpu.VMEM((FF, D), BF16),
                        pltpu.VMEM((MOE_LOAD_SLOTS, MOE_FC, MOE_FC), F32),
                        pltpu.SemaphoreType.DMA((MOE_LOAD_SLOTS,))],
    )
    return pl.pallas_call(
        _moe_ffn_body,
        grid_spec=grid_spec,
        out_shape=jax.ShapeDtypeStruct((R, D), F32),
        compiler_params=_params("arbitrary"),
        name="moe_ffn",
    )(meta["tile_expert"], meta["n_used"], xs, wg, wu, wd)


def _combine_body(lo_ref, rs_ref, rl_ref, ys_hbm, x_ref, rt_ref, pc_ref, g_ref, o_ref, ybuf, sem):
    b, nb = pl.program_id(0), pl.num_programs(0)
    slot = b % 2

    def runs(blk, s, wait):
        for e in range(N_EXPERTS):
            j = blk * N_EXPERTS + e
            _run_dma(ys_hbm, ybuf.at[s], rs_ref[j], lo_ref[j], rl_ref[j], sem.at[s], wait)

    @pl.when(b == 0)
    def _():
        ybuf[...] = jnp.zeros_like(ybuf)
        runs(0, 0, False)

    @pl.when(b + 1 < nb)
    def _():
        runs(b + 1, 1 - slot, False)

    runs(b, slot, True)
    y = ybuf[slot].astype(BF16)
    pc, rt = pc_ref[...], rt_ref[...]
    lane = lax.broadcasted_iota(jnp.int32, (MOE_TB, BLOCK_LANES), 1)
    q = jnp.where(lane == pc[:, 0:1], rt[:, 2:3], jnp.where(lane == pc[:, 1:2], rt[:, 3:4], 0.0)).astype(BF16)
    o_ref[...] = _rms(x_ref[...] + _dot(q, y), g_ref[...])


def _combine(ys, meta, x, rt, g):
    T, D = x.shape
    grid_spec = pltpu.PrefetchScalarGridSpec(
        num_scalar_prefetch=3,
        grid=(T // MOE_TB,),
        in_specs=[pl.BlockSpec(memory_space=pl.ANY),
                  pl.BlockSpec((MOE_TB, D), lambda b, *_: (b, 0)),
                  pl.BlockSpec((MOE_TB, LANES), lambda b, *_: (b, 0)),
                  pl.BlockSpec((MOE_TB, TOP_K), lambda b, *_: (b, 0)),
                  pl.BlockSpec((1, D), lambda b, *_: (0, 0))],
        out_specs=pl.BlockSpec((MOE_TB, D), lambda b, *_: (b, 0)),
        scratch_shapes=[pltpu.VMEM((2, BLOCK_LANES, D), F32), pltpu.SemaphoreType.DMA((2,))],
    )
    return pl.pallas_call(
        _combine_body,
        grid_spec=grid_spec,
        out_shape=jax.ShapeDtypeStruct((T, D), F32),
        compiler_params=_params("arbitrary"),
        name="moe_combine",
    )(meta["loc_off"], meta["run_start"], meta["run_len"], ys, x, rt, meta["pos_c"], g.reshape(1, D))


def kernel(x, norm_mix_g, w_in, ret_gn_g, attn_sinks, conv_dw_w, conv_dw_b, conv_ln_g, conv_ln_b,
           conv_pw_w, w_out, norm_ffn_g, ffn_w_gate, ffn_w_up, ffn_w_down, moe_router, moe_w_gate,
           moe_w_up, moe_w_down, final_norm_g):
    B, S, D = x.shape
    T = B * S
    depth = w_in.shape[0]
    assert depth == 2 and ffn_w_gate.shape[0] == 1 and moe_router.shape[0] == 1, "dense layer then MoE layer"
    assert T % MOE_TB == 0 and S % MOE_TB == 0
    assert moe_w_gate.shape[-1] % MOE_FC == 0 and D % MOE_FC == 0

    for l in range(depth):
        rq, rk, rv, rg, sq, sk, sv, ch = _inproj(x, norm_mix_g[l], w_in[l].astype(BF16), tm=INPROJ_TM)
        y_ret_t = _retention(rq, rk, rv, rg, ret_gn_g[l], ts=RET_TS)
        y_swa_t = _swa(sq, sk, sv, attn_sinks[l], tq=SWA_TQ)
        y_conv = _conv(ch, conv_dw_w[l], conv_dw_b[l], conv_ln_g[l], conv_ln_b[l], conv_pw_w[l],
                       ts=CONV_TS, rb=CONV_RB)
        flat = lambda a: a.reshape(T, a.shape[-1])
        j = l // 2
        if l % 2 == 0:
            x = _outproj_dense_ffn(y_ret_t, y_swa_t, flat(y_conv), w_out[l].astype(BF16), flat(x),
                                   norm_ffn_g[l], ffn_w_gate[j], ffn_w_up[j], ffn_w_down[j],
                                   tm=FFN_TM, fc=FFN_FC).reshape(B, S, D)
        else:
            x2, hn, rt, rl = _outproj_route(y_ret_t, y_swa_t, flat(y_conv), w_out[l].astype(BF16), flat(x),
                                            norm_ffn_g[l], moe_router[j])
            n_rows = TOP_K * T + (T // MOE_TB) * N_EXPERTS * RUN_ALIGN + N_EXPERTS * MOE_TM
            meta = _route_meta(rt, rl, n_rows // MOE_TM)
            xs = _dispatch(hn, meta, n_rows)
            ys = _moe_ffn(xs, meta, moe_w_gate[j], moe_w_up[j], moe_w_down[j])
            x = _combine(ys, meta, x2, rt, final_norm_g).reshape(B, S, D)
    return x
```

```python
import functools

import jax
import jax.numpy as jnp
from jax import lax
from jax.experimental import pallas as pl
from jax.experimental.pallas import tpu as pltpu

F32 = jnp.float32
BF16 = jnp.bfloat16

HEAD_DIM = 64
RET_HEADS = 4
RET_W = RET_HEADS * HEAD_DIM
SWA_HEADS = 8
SWA_KV_HEADS = 2
SWA_GROUP = SWA_HEADS // SWA_KV_HEADS
SWA_W = SWA_HEADS * HEAD_DIM
SWA_KV_W = SWA_KV_HEADS * HEAD_DIM
CONV_CH = 256
CONV_GROUPS = 4
CONV_WIDTH = 31
WINDOW = 128
BLOCK = 128
RET_CHUNK = 128
ROPE_THETA = 500000.0
ROPE_DIM = HEAD_DIM // 4
RET_ROPE_THETA = 10000.0
N_EXPERTS = 8
TOP_K = 2
EPS = 1e-6
NEG_INF = -1e30

LANES = 128
CONV_HALO = 32
VMEM_LIMIT = 56 * 1024 * 1024

O_RQ, O_RK, O_RV, O_RG = 0, RET_W, 2 * RET_W, 3 * RET_W
O_SQ = 4 * RET_W
O_SK = O_SQ + SWA_W
O_SV = O_SK + SWA_KV_W
O_CA = O_SV + SWA_KV_W
O_CG = O_CA + CONV_CH
D_IN = O_CG + CONV_CH


INPROJ_TM = 1024
RET_TS = 4096
SWA_TQ = 4096
CONV_TS = 1024
CONV_RB = 256
FFN_TM = 1024
FFN_FC = 512


def _params(*sem):
    return pltpu.CompilerParams(dimension_semantics=sem, vmem_limit_bytes=VMEM_LIMIT)


def _rms(x, g):
    return x * lax.rsqrt(jnp.mean(x * x, axis=-1, keepdims=True) + EPS) * g


def _silu(x):
    return x * jax.nn.sigmoid(x)


def _dot(a, b):
    return jnp.dot(a, b, preferred_element_type=F32)


def _dot_nt(a, b):
    return lax.dot_general(a, b, (((1,), (1,)), ((), ())), preferred_element_type=F32)


def _dot_tn(a, b):
    return lax.dot_general(a, b, (((0,), (0,)), ((), ())), preferred_element_type=F32)


def _rope_tables(seq, theta, rot_dim):
    half = rot_dim // 2
    inv = 1.0 / (theta ** (jnp.arange(half, dtype=F32) / half))
    ang = jnp.arange(seq, dtype=F32)[:, None] * inv[None, :]
    cos, sin = jnp.cos(ang), jnp.sin(ang)
    rest = HEAD_DIM - rot_dim
    c = jnp.concatenate([cos, cos, jnp.ones((seq, rest), F32)], axis=1)
    s = jnp.concatenate([-sin, sin, jnp.zeros((seq, rest), F32)], axis=1)
    reps = LANES // HEAD_DIM
    return jnp.tile(c, (1, reps)), jnp.tile(s, (1, reps))


def _inproj_body(x_ref, g_ref, w_ref, wgt_ref, rc_ref, rs_ref, sc_ref, ss_ref,
                 rq_ref, rk_ref, rv_ref, rg_ref, sq_ref, sk_ref, sv_ref, ch_ref, ztbuf):
    h = _rms(x_ref[0], g_ref[...]).astype(BF16)
    for a, b in ((O_RQ, O_RK), (O_RK, O_RV), (O_RV, O_RG), (O_RG, O_SQ), (O_SQ, O_SK), (O_SK, O_CA)):
        ztbuf[a:b, :] = _dot_nt(wgt_ref[a:b, :], h)
    za = _dot(h, w_ref[:, O_CA:O_CG])
    zg = _dot(h, w_ref[:, O_CG:D_IN])

    def rope_store(o_ref, a, heads, half, c, s, scale):
        for hd in range(heads):
            lo = a + hd * HEAD_DIM
            z = ztbuf[lo:lo + HEAD_DIM, :]
            parts = [ztbuf[lo + half:lo + 2 * half, :], ztbuf[lo:lo + half, :]]
            if 2 * half < HEAD_DIM:
                parts.append(ztbuf[lo + 2 * half:lo + HEAD_DIM, :])
            r = z * c + jnp.concatenate(parts, axis=0) * s
            if scale != 1.0:
                r = r * scale
            o_ref[0, hd * HEAD_DIM:(hd + 1) * HEAD_DIM, :] = r.astype(BF16)

    rc, rs = rc_ref[...], rs_ref[...]
    sc, ss = sc_ref[...], ss_ref[...]
    scale = HEAD_DIM ** -0.5
    rope_store(rq_ref, O_RQ, RET_HEADS, HEAD_DIM // 2, rc, rs, 1.0)
    rope_store(rk_ref, O_RK, RET_HEADS, HEAD_DIM // 2, rc, rs, scale)
    rv_ref[0] = ztbuf[O_RV:O_RG, :].astype(BF16)
    rg_ref[0] = ztbuf[O_RG:O_SQ, :]
    rope_store(sq_ref, O_SQ, SWA_HEADS, ROPE_DIM // 2, sc, ss, scale)
    rope_store(sk_ref, O_SK, SWA_KV_HEADS, ROPE_DIM // 2, sc, ss, 1.0)
    sv_ref[0] = ztbuf[O_SV:O_CA, :].astype(BF16)
    ch_ref[0] = za * jax.nn.sigmoid(zg)


def _inproj(x, g, w_bf, tm):
    B, S, D = x.shape
    tm = min(tm, S)
    one_head_t = lambda t: t[:, :HEAD_DIM].T
    rc, rs = map(one_head_t, _rope_tables(S, RET_ROPE_THETA, HEAD_DIM))
    sc, ss = map(one_head_t, _rope_tables(S, ROPE_THETA, ROPE_DIM))
    tab_t = pl.BlockSpec((HEAD_DIM, tm), lambda s, b: (0, s))

    def chan(width, dtype):
        return (jax.ShapeDtypeStruct((B, width, S), dtype), pl.BlockSpec((1, width, tm), lambda s, b: (b, 0, s)))

    outs = [chan(RET_W, BF16), chan(RET_W, BF16), chan(RET_W, BF16), chan(RET_W, F32),
            chan(SWA_W, BF16), chan(SWA_KV_W, BF16), chan(SWA_KV_W, BF16),
            (jax.ShapeDtypeStruct((B, S, CONV_CH), F32), pl.BlockSpec((1, tm, CONV_CH), lambda s, b: (b, s, 0)))]
    return pl.pallas_call(
        _inproj_body,
        grid=(S // tm, B),
        in_specs=[pl.BlockSpec((1, tm, D), lambda s, b: (b, s, 0)),
                  pl.BlockSpec((1, D), lambda s, b: (0, 0)),
                  pl.BlockSpec((D, D_IN), lambda s, b: (0, 0)),
                  pl.BlockSpec((O_CA, D), lambda s, b: (0, 0)),
                  tab_t, tab_t, tab_t, tab_t],
        out_specs=[o[1] for o in outs],
        out_shape=[o[0] for o in outs],
        scratch_shapes=[pltpu.VMEM((O_CA, tm), F32)],
        compiler_params=_params("arbitrary", "arbitrary"),
        name="inproj",
    )(x, g.reshape(1, D), w_bf, w_bf[:, :O_CA].T, rc, rs, sc, ss)


def _ret_body(q_ref, k_ref, v_ref, g_ref, dm_ref, kd_ref, qd_ref, cd_ref, gn_ref, o_ref, st_ref, *, ts):
    @pl.when(pl.program_id(2) == 0)
    def _():
        st_ref[...] = jnp.zeros_like(st_ref)

    C = RET_CHUNK
    for c in range(ts // C):
        rows = slice(c * C, (c + 1) * C)
        q, k, v = q_ref[0, :, rows], k_ref[0, :, rows], v_ref[0, :, rows]
        st = st_ref[...]
        scores = _dot_tn(k, q) * dm_ref[0]
        intra = _dot(v, scores.astype(BF16))
        cross = _dot(st.astype(BF16), q) * qd_ref[0]
        kdec = (k.astype(F32) * kd_ref[0]).astype(BF16)
        st_ref[...] = st * cd_ref[0] + _dot_nt(v, kdec)
        o = intra + cross
        mu = jnp.mean(o, axis=0, keepdims=True)
        d = o - mu
        var = jnp.mean(d * d, axis=0, keepdims=True)
        on = d * lax.rsqrt(var + EPS) * gn_ref[0]
        o_ref[0, :, rows] = (_silu(g_ref[0, :, rows]) * on).astype(BF16)


def _retention(rq, rk, rv, rg_t, gn_g, ts):
    B, W, S = rq.shape
    H, D, C = RET_HEADS, HEAD_DIM, RET_CHUNK
    ts = min(ts, S)
    lg = jnp.log(1.0 - 2.0 ** (-5.0 - jnp.arange(H, dtype=F32)))
    idx = jnp.arange(C)
    rel = idx[:, None] - idx[None, :]
    dmask = jnp.where(rel[None] >= 0,
                      jnp.exp(jnp.maximum(rel, 0)[None].astype(F32) * lg[:, None, None]), 0.0)
    k_decay = jnp.exp((C - 1 - idx)[:, None].astype(F32) * lg[None, :])
    q_decay = jnp.exp((idx + 1)[:, None].astype(F32) * lg[None, :])
    chunk_decay = jnp.exp(C * lg)
    dm_t = jnp.swapaxes(dmask, 1, 2)
    kd = jnp.broadcast_to(k_decay.T[:, None, :], (H, D, C))
    qd = jnp.broadcast_to(q_decay.T[:, None, :], (H, D, C))
    cd = jnp.broadcast_to(chunk_decay[:, None, None], (H, D, D))
    gn = jnp.broadcast_to(gn_g.reshape(H, D, 1), (H, D, C))
    chan = pl.BlockSpec((1, D, ts), lambda b, h, n: (b, h, n))

    def per_head(r, c):
        return pl.BlockSpec((1, r, c), lambda b, h, n: (h, 0, 0))

    return pl.pallas_call(
        functools.partial(_ret_body, ts=ts),
        grid=(B, H, S // ts),
        in_specs=[chan, chan, chan, chan, per_head(C, C), per_head(D, C), per_head(D, C),
                  per_head(D, D), per_head(D, C)],
        out_specs=chan,
        out_shape=jax.ShapeDtypeStruct((B, W, S), BF16),
        scratch_shapes=[pltpu.VMEM((D, D), F32)],
        compiler_params=_params("arbitrary", "arbitrary", "arbitrary"),
        name="retention",
    )(rq, rk, rv, rg_t, dm_t, kd, qd, cd, gn)


def _swa_body(sink_ref, q_ref, kc_ref, kp_ref, vc_ref, vp_ref, bias_ref, o_ref, kcat, vcat, sbuf, pbuf, rbuf,
              *, tq):
    hk, n = pl.program_id(1), pl.program_id(2)
    kcat[:, 0:BLOCK], kcat[:, BLOCK:] = kp_ref[0], kc_ref[0]
    vcat[:, 0:BLOCK], vcat[:, BLOCK:] = vp_ref[0], vc_ref[0]
    width = SWA_GROUP * BLOCK
    group = lax.broadcasted_iota(jnp.int32, (1, width), 1) // BLOCK
    sink = jnp.zeros((1, width), F32)
    for g in range(SWA_GROUP):
        sink = jnp.where(group == g, sink_ref[hk * SWA_GROUP + g], sink)
    nq = tq // BLOCK
    for j in range(nq):
        q = jnp.concatenate([q_ref[0, g * HEAD_DIM:(g + 1) * HEAD_DIM, j * BLOCK:(j + 1) * BLOCK]
                             for g in range(SWA_GROUP)], axis=1)
        kb = kcat[:, j * BLOCK:(j + 2) * BLOCK]
        sbuf[j] = _dot_tn(kb, q)
    for j in range(nq):
        s = sbuf[j] + (bias_ref[0] if j > 0 else bias_ref[jnp.where(n == 0, 1, 0)])
        m = jnp.maximum(jnp.max(s, axis=0, keepdims=True), sink)
        p = jnp.exp(s - m)
        rbuf[j] = 1.0 / (jnp.sum(p, axis=0, keepdims=True) + jnp.exp(sink - m))
        pbuf[j] = p.astype(BF16)
    for j in range(nq):
        vb = vcat[:, j * BLOCK:(j + 2) * BLOCK]
        o = _dot(vb, pbuf[j]) * rbuf[j]
        for g in range(SWA_GROUP):
            o_ref[0, g * HEAD_DIM:(g + 1) * HEAD_DIM, j * BLOCK:(j + 1) * BLOCK] = (
                o[:, g * BLOCK:(g + 1) * BLOCK].astype(BF16))


def _swa(sq, sk, sv, sinks, tq):
    B, _, S = sq.shape
    D = HEAD_DIM
    tq = min(tq, S)
    r = tq // BLOCK
    qi = jnp.arange(BLOCK)[None, :] + BLOCK
    kj = jnp.arange(2 * BLOCK)[:, None]
    rel = qi - kj
    allowed = (rel >= 0) & (rel < WINDOW)
    allowed = jnp.stack([allowed, allowed & (kj >= BLOCK)])
    bias = jnp.tile(jnp.where(allowed, 0.0, NEG_INF).astype(F32), (1, 1, SWA_GROUP))
    nq, width = tq // BLOCK, SWA_GROUP * BLOCK
    cur = pl.BlockSpec((1, D, tq), lambda b, h, n: (b, h, n))
    prev = pl.BlockSpec((1, D, BLOCK), lambda b, h, n: (b, h, jnp.maximum(n * r - 1, 0)))
    return pl.pallas_call(
        functools.partial(_swa_body, tq=tq),
        grid=(B, SWA_KV_HEADS, S // tq),
        in_specs=[pl.BlockSpec(memory_space=pltpu.SMEM),
                  pl.BlockSpec((1, SWA_GROUP * D, tq), lambda b, h, n: (b, h, n)),
                  cur, prev, cur, prev,
                  pl.BlockSpec((2, 2 * BLOCK, width), lambda b, h, n: (0, 0, 0))],
        out_specs=pl.BlockSpec((1, SWA_GROUP * D, tq), lambda b, h, n: (b, h, n)),
        out_shape=jax.ShapeDtypeStruct((B, SWA_W, S), BF16),
        scratch_shapes=[pltpu.VMEM((D, BLOCK + tq), BF16), pltpu.VMEM((D, BLOCK + tq), BF16),
                        pltpu.VMEM((nq, 2 * BLOCK, width), F32), pltpu.VMEM((nq, 2 * BLOCK, width), BF16),
                        pltpu.VMEM((nq, 1, width), F32)],
        compiler_params=_params("arbitrary", "arbitrary", "arbitrary"),
        name="swa",
    )(sinks.astype(F32), sq, sk, sk, sv, sv, bias)


def _conv_body(h_ref, dw_ref, db_ref, lg_ref, lb_ref, pw_ref, o_ref, hbuf, hsh, *, ts, rb):
    s = pl.program_id(1)
    sub = 8
    span = ts + CONV_HALO - sub

    @pl.when(s == 0)
    def _():
        hbuf[0:CONV_HALO, :] = jnp.zeros((CONV_HALO, CONV_CH), F32)

    @pl.when(s > 0)
    def _():
        hbuf[0:CONV_HALO, :] = hbuf[ts:ts + CONV_HALO, :]

    hbuf[CONV_HALO:CONV_HALO + ts, :] = h_ref[0]
    for p in range(1, sub):
        hsh[p - 1] = hbuf[p:p + span, :]
    off = CONV_HALO - (CONV_WIDTH - 1)
    for r in range(ts // rb):
        base = r * rb
        acc = jnp.broadcast_to(db_ref[...], (rb, CONV_CH))
        for w in range(CONV_WIDTH):
            q, p = divmod(off + w, sub)
            lo = base + q * sub
            tap = hbuf[lo:lo + rb, :] if p == 0 else hsh[p - 1, lo:lo + rb, :]
            acc = acc + tap * dw_ref[w:w + 1, :]
        mu = jnp.mean(acc, axis=-1, keepdims=True)
        d = acc - mu
        var = jnp.mean(d * d, axis=-1, keepdims=True)
        hn = d * lax.rsqrt(var + EPS) * lg_ref[...] + lb_ref[...]
        o_ref[0, base:base + rb, :] = _dot(_silu(hn).astype(BF16), pw_ref[...]).astype(BF16)


def _conv(ch, dw_w, dw_b, ln_g, ln_b, pw_w, ts, rb):
    B, S, W = ch.shape
    ts = min(ts, S)
    rb = min(rb, ts)
    pw = jnp.zeros((W, W), F32)
    gd = W // CONV_GROUPS
    for g in range(CONV_GROUPS):
        pw = pw.at[g * gd:(g + 1) * gd, g * gd:(g + 1) * gd].set(pw_w[g])
    act = pl.BlockSpec((1, ts, W), lambda b, s: (b, s, 0))

    def const(shape):
        return pl.BlockSpec(shape, lambda b, s: (0,) * len(shape))

    return pl.pallas_call(
        functools.partial(_conv_body, ts=ts, rb=rb),
        grid=(B, S // ts),
        in_specs=[act, const((CONV_WIDTH, W)), const((1, W)), const((1, W)), const((1, W)), const((W, W))],
        out_specs=act,
        out_shape=jax.ShapeDtypeStruct((B, S, W), BF16),
        scratch_shapes=[pltpu.VMEM((CONV_HALO + ts, W), F32), pltpu.VMEM((7, CONV_HALO + ts - 8, W), F32)],
        compiler_params=_params("arbitrary", "arbitrary"),
        name="conv",
    )(ch, dw_w, dw_b.reshape(1, W), ln_g.reshape(1, W), ln_b.reshape(1, W), pw.astype(BF16))


def _mixer_residual(yr_ref, ys_ref, yc_ref, w_ref, x_ref):
    a, b = RET_W, RET_W + SWA_W
    return x_ref[...] + (_dot_tn(yr_ref[0], w_ref[0:a, :]) + _dot_tn(ys_ref[0], w_ref[a:b, :])
                         + _dot(yc_ref[...], w_ref[b:, :]))


def _mixer_specs(tm, tiles_per_seq):
    def chan(width):
        return pl.BlockSpec((1, width, tm), lambda i: (i // tiles_per_seq, 0, i % tiles_per_seq))

    return [chan(RET_W), chan(SWA_W), pl.BlockSpec((tm, CONV_CH), lambda i: (i, 0))]


def _outproj_body(yr_ref, ys_ref, yc_ref, w_ref, x_ref, g_ref, wr_ref, before_ref, xo_ref, hn_ref, rt_ref, rl_ref):
    x = _mixer_residual(yr_ref, ys_ref, yc_ref, w_ref, x_ref)
    xo_ref[...] = x
    h = _rms(x, g_ref[...]).astype(BF16)
    hn_ref[...] = h
    logits = _dot(h, wr_ref[...])
    lane = lax.broadcasted_iota(jnp.int32, logits.shape, 1).astype(F32)
    lg = jnp.where(lane < N_EXPERTS, logits, -jnp.inf)
    m1 = jnp.max(lg, axis=-1, keepdims=True)
    i1 = jnp.min(jnp.where(lg == m1, lane, float(LANES)), axis=-1, keepdims=True)
    lg2 = jnp.where(lane == i1, -jnp.inf, lg)
    m2 = jnp.max(lg2, axis=-1, keepdims=True)
    i2 = jnp.min(jnp.where(lg2 == m2, lane, float(LANES)), axis=-1, keepdims=True)
    e = jnp.exp(m2 - m1)
    w1 = 1.0 / (1.0 + e)
    w2 = e / (1.0 + e)
    hit1, hit2 = lane == i1, lane == i2
    onehot = jnp.where(jnp.logical_or(hit1, hit2), 1.0, 0.0)
    rank = _dot(before_ref[...], onehot.astype(BF16))
    count = jnp.sum(onehot, axis=0, keepdims=True).astype(jnp.int32)
    run_len = (count + (RUN_ALIGN - 1)) & ~(RUN_ALIGN - 1)
    lower = (lax.broadcasted_iota(jnp.int32, (LANES, LANES), 0) < lax.broadcasted_iota(jnp.int32, (LANES, LANES), 1))
    run_rows = jnp.broadcast_to(run_len.astype(F32), (8, LANES)).astype(BF16)
    loc_off = _dot(run_rows, jnp.where(lower, 1.0, 0.0).astype(BF16))[0:1, :]
    pos = rank + loc_off
    p1 = jnp.sum(jnp.where(hit1, pos, 0.0), axis=-1, keepdims=True)
    p2 = jnp.sum(jnp.where(hit2, pos, 0.0), axis=-1, keepdims=True)
    cols = (i1, i2, w1, w2, p1, p2)
    rt = jnp.zeros_like(logits)
    for c, v in enumerate(cols):
        rt = jnp.where(lane == c, v, rt)
    rt_ref[...] = rt
    rl_ref[0] = jnp.broadcast_to(run_len, (8, LANES))


def _outproj_route(yr, ys, yc, w_bf, x, g, w_router):
    T, D = x.shape
    tm = MOE_TB
    S = ys.shape[2]

    def row(width):
        return pl.BlockSpec((tm, width), lambda i: (i, 0))

    def const(shape):
        return pl.BlockSpec(shape, lambda i: (0,) * len(shape))

    wr = jnp.zeros((D, LANES), F32).at[:, :N_EXPERTS].set(w_router).astype(BF16)
    before = (jnp.arange(tm)[:, None] > jnp.arange(tm)[None, :]).astype(BF16)
    return pl.pallas_call(
        _outproj_body,
        grid=(T // tm,),
        in_specs=_mixer_specs(tm, S // tm) + [const((D, D)), row(D), const((1, D)), const((D, LANES)),
                                              const((tm, tm))],
        out_specs=[row(D), row(D), row(LANES), pl.BlockSpec((1, 8, LANES), lambda i: (i, 0, 0))],
        out_shape=[jax.ShapeDtypeStruct((T, D), F32), jax.ShapeDtypeStruct((T, D), BF16),
                   jax.ShapeDtypeStruct((T, LANES), F32), jax.ShapeDtypeStruct((T // tm, 8, LANES), jnp.int32)],
        compiler_params=_params("arbitrary"),
        name="outproj_route",
    )(yr, ys, yc, w_bf, x, g.reshape(1, D), wr, before)


def _dense_ffn_body(yr_ref, ys_ref, yc_ref, wo_ref, x_ref, g_ref, wg_ref, wu_ref, wd_ref, o_ref, *, chunks):
    x = _mixer_residual(yr_ref, ys_ref, yc_ref, wo_ref, x_ref)
    h = _rms(x, g_ref[...]).astype(BF16)
    acc = x
    for a, b in chunks:
        g = _dot(h, wg_ref[:, a:b])
        u = _dot(h, wu_ref[:, a:b])
        acc = acc + _dot((_silu(g) * u).astype(BF16), wd_ref[a:b, :])
    o_ref[...] = acc


def _outproj_dense_ffn(yr, ys, yc, wo_bf, x, g, wg, wu, wd, tm, fc):
    T, D = x.shape
    FF = wg.shape[1]
    S = ys.shape[2]
    tm = min(tm, S)
    chunks = tuple((a, min(a + fc, FF)) for a in range(0, FF, fc))

    def row(width):
        return pl.BlockSpec((tm, width), lambda i: (i, 0))

    def const(shape):
        return pl.BlockSpec(shape, lambda i: (0,) * len(shape), pipeline_mode=pl.Buffered(1))

    return pl.pallas_call(
        functools.partial(_dense_ffn_body, chunks=chunks),
        grid=(T // tm,),
        in_specs=_mixer_specs(tm, S // tm) + [const((D, D)), row(D), const((1, D)),
                                              const((D, FF)), const((D, FF)), const((FF, D))],
        out_specs=row(D),
        out_shape=jax.ShapeDtypeStruct((T, D), F32),
        compiler_params=_params("arbitrary"),
        name="outproj_dense_ffn",
    )(yr, ys, yc, wo_bf, x, g.reshape(1, D), wg.astype(BF16), wu.astype(BF16), wd.astype(BF16))


MOE_TB = 512
MOE_TM = 512
RUN_ALIGN = 8
RUN_SIZES = (512, 256, 128, 64, 32, 16, 8)
BLOCK_ROWS = TOP_K * MOE_TB + N_EXPERTS * RUN_ALIGN
BLOCK_LANES = 1152
MOE_FC = 512
MOE_LOAD_SLOTS = 12


def _route_meta(rt, rl, n_tiles):
    T = rt.shape[0]
    nb = T // MOE_TB
    pos = rt[:, 4:4 + TOP_K].astype(jnp.int32)
    run_len = rl[:, 0, :N_EXPERTS]
    loc_off = jnp.cumsum(run_len, axis=1) - run_len
    group = jnp.sum(run_len, axis=0)
    ptiles = (group + MOE_TM - 1) // MOE_TM
    tile_end = jnp.cumsum(ptiles)
    gstart = (tile_end - ptiles) * MOE_TM
    run_start = gstart[None, :] + jnp.cumsum(run_len, axis=0) - run_len
    tile_expert = jnp.sum(jnp.arange(n_tiles, dtype=jnp.int32)[:, None] >= tile_end[None, :], axis=1)
    i32 = lambda a: a.astype(jnp.int32)
    pos3 = pos.reshape(nb, MOE_TB, TOP_K)
    return dict(
        pos_l=i32(jnp.swapaxes(pos3, 1, 2)),
        pos_c=i32(pos3.reshape(T, TOP_K)),
        loc_off=i32(loc_off.reshape(-1)), run_start=i32(run_start.reshape(-1)), run_len=i32(run_len.reshape(-1)),
        pad_start=i32(gstart + group), pad_len=i32(ptiles * MOE_TM - group),
        tile_expert=i32(jnp.minimum(tile_expert, N_EXPERTS - 1)), n_used=i32(tile_end[-1:]))


def _run_dma(src, dst, src_off, dst_off, length, sem, wait):
    off = 0
    for k in RUN_SIZES:
        part = length & k

        @pl.when(part != 0)
        def _(off=off, k=k):
            cp = pltpu.make_async_copy(src.at[pl.ds(pl.multiple_of(src_off + off, RUN_ALIGN), k)],
                                       dst.at[pl.ds(pl.multiple_of(dst_off + off, RUN_ALIGN), k)], sem)
            if wait:
                cp.wait()
            else:
                cp.start()

        off = off + part


def _dispatch_body(lo_ref, rs_ref, rl_ref, ps_ref, pn_ref, nu_ref, pos_ref, h_ref, xs_hbm, sbuf, zbuf, sem):
    b, nb = pl.program_id(0), pl.num_programs(0)
    slot = b % 2
    n_tiles = xs_hbm.shape[0] // MOE_TM

    def zero_tile(i, wait):
        cp = pltpu.make_async_copy(zbuf, xs_hbm.at[pl.ds(pl.multiple_of(i * MOE_TM, MOE_TM), MOE_TM)], sem.at[2])
        if wait:
            cp.wait()
        else:
            cp.start()

    def runs(blk, s, wait):
        for e in range(N_EXPERTS):
            j = blk * N_EXPERTS + e
            _run_dma(sbuf.at[s], xs_hbm, lo_ref[j], rs_ref[j], rl_ref[j], sem.at[s], wait)

    @pl.when(b >= 2)
    def _():
        runs(b - 2, slot, True)

    pos = pos_ref[0]
    r = lax.broadcasted_iota(jnp.int32, (BLOCK_ROWS, MOE_TB), 0)
    onehot = jnp.logical_or(r == pos[0:1, :], r == pos[1:2, :])
    sbuf[slot] = _dot(jnp.where(onehot, 1.0, 0.0).astype(BF16), h_ref[...])
    runs(b, slot, False)

    @pl.when(b == nb - 1)
    def _():
        zbuf[...] = jnp.zeros_like(zbuf)
        for e in range(N_EXPERTS):
            _run_dma(zbuf, xs_hbm, 0, ps_ref[e], pn_ref[e], sem.at[2], False)
        lax.fori_loop(nu_ref[0], n_tiles, lambda i, c: zero_tile(i, False), None)
        for e in range(N_EXPERTS):
            _run_dma(zbuf, xs_hbm, 0, ps_ref[e], pn_ref[e], sem.at[2], True)
        lax.fori_loop(nu_ref[0], n_tiles, lambda i, c: zero_tile(i, True), None)

        @pl.when(b >= 1)
        def _():
            runs(b - 1, 1 - slot, True)

        runs(b, slot, True)


def _dispatch(hn, meta, n_rows):
    T, D = hn.shape
    nb = T // MOE_TB
    grid_spec = pltpu.PrefetchScalarGridSpec(
        num_scalar_prefetch=6,
        grid=(nb,),
        in_specs=[pl.BlockSpec((1, TOP_K, MOE_TB), lambda b, *_: (b, 0, 0)),
                  pl.BlockSpec((MOE_TB, D), lambda b, *_: (b, 0))],
        out_specs=pl.BlockSpec(memory_space=pl.ANY),
        scratch_shapes=[pltpu.VMEM((2, BLOCK_ROWS, D), F32), pltpu.VMEM((MOE_TM, D), F32),
                        pltpu.SemaphoreType.DMA((3,))],
    )
    return pl.pallas_call(
        _dispatch_body,
        grid_spec=grid_spec,
        out_shape=jax.ShapeDtypeStruct((n_rows, D), F32),
        compiler_params=pltpu.CompilerParams(dimension_semantics=("arbitrary",), vmem_limit_bytes=VMEM_LIMIT,
                                             has_side_effects=True),
        name="moe_dispatch",
    )(meta["loc_off"], meta["run_start"], meta["run_len"], meta["pad_start"], meta["pad_len"], meta["n_used"],
      meta["pos_l"], hn)


def _moe_ffn_body(te_ref, nu_ref, x_ref, wg_hbm, wu_hbm, wd_hbm, o_ref, wg_s, wu_s, wd_s, stage, sem):
    i = pl.program_id(0)
    e = te_ref[i]
    active = i < nu_ref[0]
    first = jnp.logical_or(i == 0, e != te_ref[jnp.maximum(i - 1, 0)])
    D, FF = wg_s.shape
    fsl = [slice(f * MOE_FC, (f + 1) * MOE_FC) for f in range(FF // MOE_FC)]
    slots, piece = stage.shape[0], stage.shape[1]
    psl = [slice(r, r + piece) for r in range(0, D, piece)]
    jobs = []
    for f in fsl:
        jobs += [(wg_hbm, wg_s, p, f) for p in psl] + [(wu_hbm, wu_s, p, f) for p in psl]
        jobs += [(wd_hbm, wd_s, f, p) for p in psl]
    per_chunk = len(jobs) // len(fsl)

    def copy(j):
        w_hbm, _, rows, cols = jobs[j]
        return pltpu.make_async_copy(w_hbm.at[e, rows, cols], stage.at[j % slots], sem.at[j % slots])

    def ffn(load):
        x = x_ref[...].astype(BF16)
        acc = jnp.zeros(o_ref.shape, F32)
        if load:
            for j in range(slots - 1):
                copy(j).start()
        for c, sl in enumerate(fsl):
            if load:
                for j in range(c * per_chunk, (c + 1) * per_chunk):
                    if j + slots - 1 < len(jobs):
                        copy(j + slots - 1).start()
                    copy(j).wait()
                    _, w_s, rows, cols = jobs[j]
                    w_s[rows, cols] = stage[j % slots].astype(BF16)
            g = _dot(x, wg_s[:, sl])
            u = _dot(x, wu_s[:, sl])
            acc = acc + _dot((_silu(g) * u).astype(BF16), wd_s[sl, :])
        o_ref[...] = acc

    @pl.when(jnp.logical_and(active, first))
    def _():
        ffn(load=True)

    @pl.when(jnp.logical_and(active, jnp.logical_not(first)))
    def _():
        ffn(load=False)

    @pl.when(jnp.logical_not(active))
    def _():
        o_ref[...] = jnp.zeros_like(o_ref)


def _moe_ffn(xs, meta, wg, wu, wd):
    R, D = xs.shape
    E, _, FF = wg.shape
    grid_spec = pltpu.PrefetchScalarGridSpec(
        num_scalar_prefetch=2,
        grid=(R // MOE_TM,),
        in_specs=[pl.BlockSpec((MOE_TM, D), lambda i, te, nu: (jnp.minimum(i, nu[0] - 1), 0)),
                  pl.BlockSpec(memory_space=pl.ANY), pl.BlockSpec(memory_space=pl.ANY),
                  pl.BlockSpec(memory_space=pl.ANY)],
        out_specs=pl.BlockSpec((MOE_TM, D), lambda i, te, nu: (i, 0)),
        scratch_shapes=[pltpu.VMEM((D, FF), BF16), pltpu.VMEM((D, FF), BF16), pltpu.VMEM((FF, D), BF16),
                        pltpu.VMEM((MOE_LOAD_SLOTS, MOE_FC, MOE_FC), F32),
                        pltpu.SemaphoreType.DMA((MOE_LOAD_SLOTS,))],
    )
    return pl.pallas_call(
        _moe_ffn_body,
        grid_spec=grid_spec,
        out_shape=jax.ShapeDtypeStruct((R, D), F32),
        compiler_params=_params("arbitrary"),
        name="moe_ffn",
    )(meta["tile_expert"], meta["n_used"], xs, wg, wu, wd)


def _combine_body(lo_ref, rs_ref, rl_ref, ys_hbm, x_ref, rt_ref, pc_ref, g_ref, o_ref, ybuf, sem):
    b, nb = pl.program_id(0), pl.num_programs(0)
    slot = b % 2

    def runs(blk, s, wait):
        for e in range(N_EXPERTS):
            j = blk * N_EXPERTS + e
            _run_dma(ys_hbm, ybuf.at[s], rs_ref[j], lo_ref[j], rl_ref[j], sem.at[s], wait)

    @pl.when(b == 0)
    def _():
        ybuf[...] = jnp.zeros_like(ybuf)
        runs(0, 0, False)

    @pl.when(b + 1 < nb)
    def _():
        runs(b + 1, 1 - slot, False)

    runs(b, slot, True)
    y = ybuf[slot].astype(BF16)
    pc, rt = pc_ref[...], rt_ref[...]
    lane = lax.broadcasted_iota(jnp.int32, (MOE_TB, BLOCK_LANES), 1)
    q = jnp.where(lane == pc[:, 0:1], rt[:, 2:3], jnp.where(lane == pc[:, 1:2], rt[:, 3:4], 0.0)).astype(BF16)
    o_ref[...] = _rms(x_ref[...] + _dot(q, y), g_ref[...])


def _combine(ys, meta, x, rt, g):
    T, D = x.shape
    grid_spec = pltpu.PrefetchScalarGridSpec(
        num_scalar_prefetch=3,
        grid=(T // MOE_TB,),
        in_specs=[pl.BlockSpec(memory_space=pl.ANY),
                  pl.BlockSpec((MOE_TB, D), lambda b, *_: (b, 0)),
                  pl.BlockSpec((MOE_TB, LANES), lambda b, *_: (b, 0)),
                  pl.BlockSpec((MOE_TB, TOP_K), lambda b, *_: (b, 0)),
                  pl.BlockSpec((1, D), lambda b, *_: (0, 0))],
        out_specs=pl.BlockSpec((MOE_TB, D), lambda b, *_: (b, 0)),
        scratch_shapes=[pltpu.VMEM((2, BLOCK_LANES, D), F32), pltpu.SemaphoreType.DMA((2,))],
    )
    return pl.pallas_call(
        _combine_body,
        grid_spec=grid_spec,
        out_shape=jax.ShapeDtypeStruct((T, D), F32),
        compiler_params=_params("arbitrary"),
        name="moe_combine",
    )(meta["loc_off"], meta["run_start"], meta["run_len"], ys, x, rt, meta["pos_c"], g.reshape(1, D))


def kernel(x, norm_mix_g, w_in, ret_gn_g, attn_sinks, conv_dw_w, conv_dw_b, conv_ln_g, conv_ln_b,
           conv_pw_w, w_out, norm_ffn_g, ffn_w_gate, ffn_w_up, ffn_w_down, moe_router, moe_w_gate,
           moe_w_up, moe_w_down, final_norm_g):
    B, S, D = x.shape
    T = B * S
    depth = w_in.shape[0]
    assert depth == 2 and ffn_w_gate.shape[0] == 1 and moe_router.shape[0] == 1, "dense layer then MoE layer"
    assert T % MOE_TB == 0 and S % MOE_TB == 0
    assert moe_w_gate.shape[-1] % MOE_FC == 0 and D % MOE_FC == 0

    for l in range(depth):
        rq, rk, rv, rg, sq, sk, sv, ch = _inproj(x, norm_mix_g[l], w_in[l].astype(BF16), tm=INPROJ_TM)
        y_ret_t = _retention(rq, rk, rv, rg, ret_gn_g[l], ts=RET_TS)
        y_swa_t = _swa(sq, sk, sv, attn_sinks[l], tq=SWA_TQ)
        y_conv = _conv(ch, conv_dw_w[l], conv_dw_b[l], conv_ln_g[l], conv_ln_b[l], conv_pw_w[l],
                       ts=CONV_TS, rb=CONV_RB)
        flat = lambda a: a.reshape(T, a.shape[-1])
        j = l // 2
        if l % 2 == 0:
            x = _outproj_dense_ffn(y_ret_t, y_swa_t, flat(y_conv), w_out[l].astype(BF16), flat(x),
                                   norm_ffn_g[l], ffn_w_gate[j], ffn_w_up[j], ffn_w_down[j],
                                   tm=FFN_TM, fc=FFN_FC).reshape(B, S, D)
        else:
            x2, hn, rt, rl = _outproj_route(y_ret_t, y_swa_t, flat(y_conv), w_out[l].astype(BF16), flat(x),
                                            norm_ffn_g[l], moe_router[j])
            n_rows = TOP_K * T + (T // MOE_TB) * N_EXPERTS * RUN_ALIGN + N_EXPERTS * MOE_TM
            meta = _route_meta(rt, rl, n_rows // MOE_TM)
            xs = _dispatch(hn, meta, n_rows)
            ys = _moe_ffn(xs, meta, moe_w_gate[j], moe_w_up[j], moe_w_down[j])
            x = _combine(ys, meta, x2, rt, final_norm_g).reshape(B, S, D)
    return x
```
